```python
import jax, jax.numpy as jnp
from jax import lax
import numpy as np

D_MODEL = 2048
BATCH = 1
SEQ = 8192
DEPTH = 4

HEAD_DIM = 64
MIX_WIDTH = D_MODEL
A_WIDTH = 3 * MIX_WIDTH // 8
A_HEADS = A_WIDTH // HEAD_DIM
DILATED_BRANCHES = ((128, 1), (512, 4), (2048, 16))
BAND_BLOCK = 128
B_WIDTH = MIX_WIDTH // 4
B_HEADS = B_WIDTH // HEAD_DIM
FOX_BLOCK = 128
C_VWIDTH = MIX_WIDTH - A_WIDTH - B_WIDTH
C_HEADS = 4
C_DV = C_VWIDTH // C_HEADS
C_DK = C_DV // 2
C_KWIDTH = C_HEADS * C_DK
GATE_RANK = 16
GATE_TEMP = 16.0
GLA_CHUNK = 64
RMS_EPS = 1e-6

IN_SPLITS = (A_WIDTH, A_WIDTH, A_WIDTH, A_WIDTH,
             B_WIDTH, B_WIDTH, B_WIDTH, B_WIDTH, B_HEADS,
             C_KWIDTH, C_KWIDTH, C_VWIDTH, C_VWIDTH, GATE_RANK)
IN_WIDTH = int(sum(IN_SPLITS))
SPLIT_IDX = [int(i) for i in np.cumsum(IN_SPLITS)[:-1]]

kernel_name = "hymba_style_dilated_fox_gla_hybrid"


def rms_norm(x, g):
    xf = x.astype(jnp.float32)
    y = xf * lax.rsqrt(jnp.mean(xf * xf, axis=-1, keepdims=True) + RMS_EPS)
    return (y * g.astype(jnp.float32)).astype(x.dtype)


def dilated_branch(q, k, v, window, dilation):
    bsz, s, h, hd = q.shape
    steps = window // dilation
    L = s // dilation
    nb = -(-L // BAND_BLOCK)
    lp = nb * BAND_BLOCK

    def to_blocks(t):
        t = t.reshape(bsz, L, dilation, h, hd).transpose(0, 2, 1, 3, 4)
        t = jnp.pad(t, ((0, 0), (0, 0), (0, lp - L), (0, 0), (0, 0)))
        return t.reshape(bsz, dilation, nb, BAND_BLOCK, h, hd)

    def with_prev(t):
        prev = jnp.pad(t[:, :, :-1], ((0, 0), (0, 0), (1, 0), (0, 0), (0, 0), (0, 0)))
        return jnp.concatenate([prev, t], axis=3)

    qb = to_blocks(q)
    kb = with_prev(to_blocks(k))
    vb = with_prev(to_blocks(v))
    logits = jnp.einsum('brnqhe,brnkhe->brnhqk', qb, kb,
                        preferred_element_type=jnp.float32)
    qi = BAND_BLOCK + jnp.arange(BAND_BLOCK)
    ki = jnp.arange(2 * BAND_BLOCK)
    rel = qi[:, None] - ki[None, :]
    band = (rel >= 0) & (rel <= steps)
    key_idx = jnp.arange(nb)[:, None, None] * BAND_BLOCK - BAND_BLOCK + ki[None, None, :]
    mask = band[None] & (key_idx >= 0)
    logits = jnp.where(mask[None, None, :, None], logits, -jnp.inf)
    m = jnp.max(logits, axis=-1, keepdims=True)
    p = jnp.exp(logits - m)
    den = jnp.sum(p, axis=-1)
    out = jnp.einsum('brnhqk,brnkhe->brnqhe', p, vb.astype(jnp.float32))
    out = out / jnp.swapaxes(den, -1, -2)[..., None]
    lse = m[..., 0] + jnp.log(den)
    out = out.reshape(bsz, dilation, lp, h, hd)[:, :, :L]
    out = out.transpose(0, 2, 1, 3, 4).reshape(bsz, s, h, hd)
    lse = jnp.swapaxes(lse, -1, -2).reshape(bsz, dilation, lp, h)[:, :, :L]
    lse = lse.transpose(0, 2, 1, 3).reshape(bsz, s, h)
    return out, lse


def dilated_mixture(q, k, v):
    outs, lses = [], []
    for window, dilation in DILATED_BRANCHES:
        o, l = dilated_branch(q, k, v, window, dilation)
        outs.append(o)
        lses.append(l)
    wts = jax.nn.softmax(jnp.stack(lses), axis=0)
    return jnp.einsum('nbsh,nbshe->bshe', wts, jnp.stack(outs))


def forgetting_attention(q, k, v, log_f):
    bsz, s, h, hd = q.shape
    nb = s // FOX_BLOCK
    c = jnp.cumsum(log_f, axis=1)
    c_keys = jnp.swapaxes(c, 1, 2)
    qb = q.reshape(bsz, nb, FOX_BLOCK, h, hd).swapaxes(0, 1)
    cb = c.reshape(bsz, nb, FOX_BLOCK, h).swapaxes(0, 1)
    kpos = jnp.arange(s)
    vf = v.astype(jnp.float32)

    def block(args):
        qi, ci, i = args
        logits = jnp.einsum('bqhe,bkhe->bhqk', qi, k, preferred_element_type=jnp.float32)
        logits = logits + jnp.swapaxes(ci, 1, 2)[..., None] - c_keys[:, :, None, :]
        qpos = i * FOX_BLOCK + jnp.arange(FOX_BLOCK)
        logits = jnp.where(kpos[None, :] <= qpos[:, None], logits, -jnp.inf)
        p = jax.nn.softmax(logits, axis=-1)
        return jnp.einsum('bhqk,bkhe->bqhe', p, vf)

    out = lax.map(block, (qb, cb, jnp.arange(nb)))
    return out.swapaxes(0, 1).reshape(bsz, s, h, hd)


def gla_chunked(q, k, v, log_a):
    bsz, s, h, dk = q.shape
    dv = v.shape[-1]
    n = s // GLA_CHUNK

    def chunks(t):
        return t.astype(jnp.float32).reshape(bsz, n, GLA_CHUNK, *t.shape[2:]).swapaxes(0, 1)

    tri = jnp.tril(jnp.ones((GLA_CHUNK, GLA_CHUNK), dtype=bool))

    def step(state, inp):
        qc, kc, vc, ac = inp
        bc = jnp.cumsum(ac, axis=1)
        diff = bc[:, :, None] - bc[:, None, :]
        decay = jnp.exp(jnp.where(tri[None, :, :, None, None], diff, -jnp.inf))
        attn = jnp.einsum('bthd,bshd,btshd->bhts', qc, kc, decay)
        o = (jnp.einsum('bhts,bshv->bthv', attn, vc)
             + jnp.einsum('bthd,bhdv->bthv', qc * jnp.exp(bc), state))
        b_last = bc[:, -1]
        new_state = (state * jnp.exp(b_last)[..., None]
                     + jnp.einsum('bshd,bshv->bhdv', kc * jnp.exp(b_last[:, None] - bc), vc))
        return new_state, o

    state0 = jnp.zeros((bsz, h, dk, dv), jnp.float32)
    _, out = lax.scan(step, state0, (chunks(q), chunks(k), chunks(v), chunks(log_a)))
    return out.swapaxes(0, 1).reshape(bsz, s, h, dv)


def hybrid_layer(x, norm_g, w_in, a_q_gain, a_k_gain, b_q_gain, b_k_gain, fox_bias,
                 gla_gate_up, gla_gate_bias, gla_out_gain, w_out):
    bsz, s, _ = x.shape
    h = rms_norm(x, norm_g)
    proj = jnp.einsum('bsd,de->bse', h, w_in)
    (aq, ak, av, az, bq, bk, bv, bz, bf, cq, ck, cv, cz, cr) = jnp.split(proj, SPLIT_IDX, axis=-1)
    scale = HEAD_DIM ** -0.5

    heads_a = lambda t: t.reshape(bsz, s, A_HEADS, HEAD_DIM)
    qa = rms_norm(heads_a(aq), a_q_gain) * scale
    ka = rms_norm(heads_a(ak), a_k_gain)
    out_a = dilated_mixture(qa, ka, heads_a(av)).reshape(bsz, s, A_WIDTH)

    heads_b = lambda t: t.reshape(bsz, s, B_HEADS, HEAD_DIM)
    qb = rms_norm(heads_b(bq), b_q_gain) * scale
    kb = rms_norm(heads_b(bk), b_k_gain)
    log_f = jax.nn.log_sigmoid(bf.astype(jnp.float32) + fox_bias.astype(jnp.float32))
    out_b = forgetting_attention(qb, kb, heads_b(bv), log_f).reshape(bsz, s, B_WIDTH)

    qc = cq.reshape(bsz, s, C_HEADS, C_DK) * (C_DK ** -0.5)
    kc = ck.reshape(bsz, s, C_HEADS, C_DK)
    vc = cv.reshape(bsz, s, C_HEADS, C_DV)
    gate_logit = jnp.einsum('bsr,rk->bsk', cr, gla_gate_up) + gla_gate_bias
    log_a = (jax.nn.log_sigmoid(gate_logit.astype(jnp.float32)) / GATE_TEMP)
    log_a = log_a.reshape(bsz, s, C_HEADS, C_DK)
    out_c = rms_norm(gla_chunked(qc, kc, vc, log_a), gla_out_gain).reshape(bsz, s, C_VWIDTH)

    mixed = jnp.concatenate([
        out_a.astype(h.dtype) * jax.nn.silu(az),
        out_b.astype(h.dtype) * jax.nn.silu(bz),
        out_c.astype(h.dtype) * jax.nn.silu(cz)], axis=-1)
    return x + jnp.einsum('bse,ed->bsd', mixed, w_out)


def setup_inputs(seed: int = 0) -> dict:
    key = jax.random.key(seed)
    ks = jax.random.split(key, 13)
    nrm = jax.random.normal
    f32 = jnp.float32
    x = nrm(ks[0], (BATCH, SEQ, D_MODEL), f32)
    norm_g = 1.0 + 0.02 * nrm(ks[1], (DEPTH, D_MODEL), f32)
    w_in = nrm(ks[2], (DEPTH, D_MODEL, IN_WIDTH), f32) * D_MODEL ** -0.5
    a_q_gain = 1.0 + 0.02 * nrm(ks[3], (DEPTH, HEAD_DIM), f32)
    a_k_gain = 1.0 + 0.02 * nrm(ks[4], (DEPTH, HEAD_DIM), f32)
    b_q_gain = 1.0 + 0.02 * nrm(ks[5], (DEPTH, HEAD_DIM), f32)
    b_k_gain = 1.0 + 0.02 * nrm(ks[6], (DEPTH, HEAD_DIM), f32)
    fox_bias = (jnp.linspace(0.0, 6.0, B_HEADS, dtype=f32)[None, :]
                + 0.1 * nrm(ks[7], (DEPTH, B_HEADS), f32))
    gla_gate_up = nrm(ks[8], (DEPTH, GATE_RANK, C_KWIDTH), f32) * GATE_RANK ** -0.5
    gla_gate_bias = 0.1 * nrm(ks[9], (DEPTH, C_KWIDTH), f32)
    gla_out_gain = 1.0 + 0.02 * nrm(ks[10], (DEPTH, C_DV), f32)
    w_out = nrm(ks[11], (DEPTH, MIX_WIDTH, D_MODEL), f32) * MIX_WIDTH ** -0.5
    return {"x": x, "norm_g": norm_g, "w_in": w_in, "a_q_gain": a_q_gain,
            "a_k_gain": a_k_gain, "b_q_gain": b_q_gain, "b_k_gain": b_k_gain,
            "fox_bias": fox_bias, "gla_gate_up": gla_gate_up, "gla_gate_bias": gla_gate_bias,
            "gla_out_gain": gla_out_gain, "w_out": w_out}


def reference(x, norm_g, w_in, a_q_gain, a_k_gain, b_q_gain, b_k_gain, fox_bias,
              gla_gate_up, gla_gate_bias, gla_out_gain, w_out):
    for layer in range(DEPTH):
        x = hybrid_layer(x, norm_g[layer], w_in[layer], a_q_gain[layer], a_k_gain[layer],
                         b_q_gain[layer], b_k_gain[layer], fox_bias[layer],
                         gla_gate_up[layer], gla_gate_bias[layer], gla_out_gain[layer],
                         w_out[layer])
    return x
```

```python
import functools

import jax
import jax.numpy as jnp
from jax import lax
from jax.experimental import pallas as pl
from jax.experimental.pallas import tpu as pltpu

F32 = jnp.float32
MXU_DTYPE = jnp.bfloat16

LANES = 128
HEAD_DIM = 64
A_HEADS, B_HEADS, C_HEADS = 12, 8, 4
A_CHUNKS, B_CHUNKS = A_HEADS // 2, B_HEADS // 2
C_DK, C_DV = 96, 192
C_DK_PAD, C_DV_PAD = 128, 256
GATE_RANK = 16
GATE_TEMP = 16.0
RMS_EPS = 1e-6
DILATIONS = (1, 4, 16)
BAND = 128
A_TILE = 2048
GLA_CHUNK = 64
GLA_LEVELS = (32, 16, 8, 4, 2, 1)
VMEM_LIMIT = 56 * 1024 * 1024

_NT = (((1,), (1,)), ((), ()))


def _params(sem):
    return pltpu.CompilerParams(dimension_semantics=sem, vmem_limit_bytes=VMEM_LIMIT)


def _rmsnorm_body(x_ref, g_ref, o_ref):
    x = x_ref[...]
    ms = jnp.mean(x * x, axis=-1, keepdims=True)
    o_ref[...] = (x * lax.rsqrt(ms + RMS_EPS) * g_ref[...]).astype(o_ref.dtype)


def _rmsnorm(x, g, tm=512):
    s, d = x.shape
    return pl.pallas_call(
        _rmsnorm_body,
        grid=(s // tm,),
        in_specs=[pl.BlockSpec((tm, d), lambda i: (i, 0)), pl.BlockSpec((1, d), lambda i: (0, 0))],
        out_specs=pl.BlockSpec((tm, d), lambda i: (i, 0)),
        out_shape=jax.ShapeDtypeStruct((s, d), MXU_DTYPE),
        compiler_params=_params(("parallel",)),
        name="rmsnorm",
    )(x, g.reshape(1, d))


def _proj_body(h_ref, w_ref, aux_ref, o_ref, *, kind):
    acc = jnp.dot(h_ref[...], w_ref[...], preferred_element_type=F32)
    if kind == "f32":
        o_ref[...] = acc
        return
    for c in range(acc.shape[1] // LANES):
        y = acc[:, c * LANES:(c + 1) * LANES]
        a = aux_ref[:, c * LANES:(c + 1) * LANES]
        if kind == "headnorm":
            first = lax.broadcasted_iota(jnp.int32, y.shape, 1) < HEAD_DIM
            y2 = y * y
            s0 = jnp.sum(jnp.where(first, y2, 0.0), axis=-1, keepdims=True)
            s1 = jnp.sum(jnp.where(first, 0.0, y2), axis=-1, keepdims=True)
            ms = jnp.where(first, s0, s1) * (1.0 / HEAD_DIM)
            y = y * lax.rsqrt(ms + RMS_EPS) * a
        elif kind == "scale":
            y = y * a
        elif kind == "silu":
            y = y * jax.nn.sigmoid(y)
        o_ref[c] = y.astype(o_ref.dtype)


def _proj(h, w, aux, kind, tm=1024, tn=256):
    s, d = h.shape
    n = w.shape[1]
    if kind == "f32":
        tn = n
        out_shape = jax.ShapeDtypeStruct((s, n), F32)
        out_spec = pl.BlockSpec((tm, tn), lambda i, j: (i, j))
    else:
        out_shape = jax.ShapeDtypeStruct((n // LANES, s, LANES), MXU_DTYPE)
        out_spec = pl.BlockSpec((tn // LANES, tm, LANES), lambda i, j: (j, i, 0))
    return pl.pallas_call(
        functools.partial(_proj_body, kind=kind),
        grid=(s // tm, n // tn),
        in_specs=[pl.BlockSpec((tm, d), lambda i, j: (i, 0)),
                  pl.BlockSpec((d, tn), lambda i, j: (0, j)),
                  pl.BlockSpec((1, tn), lambda i, j: (0, j))],
        out_specs=out_spec,
        out_shape=out_shape,
        compiler_params=_params(("parallel", "arbitrary")),
        name="proj_" + kind,
    )(h, w, aux)


def _log_sigmoid(x):
    return jnp.minimum(x, 0.0) - jnp.log1p(jnp.exp(-jnp.abs(x)))


def _fox_cumsum_body(x_ref, b_ref, o_ref):
    x = _log_sigmoid(x_ref[...] + b_ref[...])
    idx = lax.broadcasted_iota(jnp.int32, x.shape, 1)
    shift = 1
    while shift < x.shape[1]:
        x = x + jnp.where(idx >= shift, pltpu.roll(x, shift, axis=1), 0.0)
        shift *= 2
    o_ref[...] = x


def _fox_cumsum(logit_t, bias):
    nh, s = logit_t.shape
    return pl.pallas_call(
        _fox_cumsum_body,
        out_shape=jax.ShapeDtypeStruct((nh, s), F32),
        compiler_params=pltpu.CompilerParams(vmem_limit_bytes=VMEM_LIMIT),
        name="fox_cumsum",
    )(logit_t, bias.reshape(nh, 1))


def _band_attention(q, k2, v2, first_key_idx):
    row = lax.broadcasted_iota(jnp.int32, (BAND, 2 * BAND), 0)
    col = lax.broadcasted_iota(jnp.int32, (BAND, 2 * BAND), 1)
    mask = (col >= row) & (col <= row + BAND) & (col + first_key_idx >= 0)
    first = lax.broadcasted_iota(jnp.int32, (BAND, LANES), 1) < HEAD_DIM
    zero = jnp.zeros_like(q)
    outs, lses = [], []
    for qm in (jnp.where(first, q, zero), jnp.where(first, zero, q)):
        s = lax.dot_general(qm, k2, _NT, preferred_element_type=F32)
        s = jnp.where(mask, s, -jnp.inf)
        m = jnp.max(s, axis=-1, keepdims=True)
        p = jnp.exp(s - m)
        den = jnp.sum(p, axis=-1, keepdims=True)
        outs.append(jnp.dot(p.astype(MXU_DTYPE), v2, preferred_element_type=F32) / den)
        lses.append(m + jnp.log(den))
    return jnp.where(first, outs[0], outs[1]), jnp.where(first, lses[0], lses[1])


def _dilated_body(*refs):
    ins, z_ref, o_ref, scr = refs[:15], refs[15], refs[16], refs[17:]
    kv_scr, o_scr, l_scr = scr[:6], scr[6], scr[7]
    n = pl.program_id(1)
    for bi, d in enumerate(DILATIONS):
        q_ref, kc, kp, vc, vp = ins[5 * bi:5 * bi + 5]
        kf, vf = kv_scr[2 * bi:2 * bi + 2]
        rows = A_TILE // d
        nb = rows // BAND
        for cur, prev, full in ((kc, kp, kf), (vc, vp, vf)):
            full[0:BAND, :] = prev[0, rows - BAND:rows, :]
            full[BAND:BAND + rows, :] = cur[0]
        for r in range(d):
            cols = slice(r * LANES, (r + 1) * LANES)

            def block(j, carry, q_ref=q_ref, kf=kf, vf=vf, cols=cols, r=r, d=d, nb=nb, bi=bi):
                row0 = pl.multiple_of(j * BAND, BAND)
                q = q_ref[0, pl.ds(row0, BAND), cols]
                k2 = kf[pl.ds(row0, 2 * BAND), cols]
                v2 = vf[pl.ds(row0, 2 * BAND), cols]
                o, lse = _band_attention(q, k2, v2, (n * nb + j - 1) * BAND)
                dst = pl.ds(row0 * d + r, BAND, stride=d)
                o_scr[bi, dst, :] = o
                l_scr[bi, dst, :] = lse
                return carry

            lax.fori_loop(0, nb, block, 0)
    lse = l_scr[...]
    w = jnp.exp(lse - jnp.max(lse, axis=0, keepdims=True))
    mixed = jnp.sum(w * o_scr[...], axis=0) / jnp.sum(w, axis=0)
    o_ref[...] = (mixed * z_ref[0].astype(F32)).astype(o_ref.dtype)


def _dilated_mixture(qk, v, z):
    s = qk.shape[1]
    operands, in_specs, scratch = [], [], []
    for d in DILATIONS:
        rows, width = A_TILE // d, d * LANES
        qk_view = qk.reshape(qk.shape[0], s // d, width)
        v_view = v.reshape(v.shape[0], s // d, width)
        blk = (1, rows, width)
        operands += [qk_view, qk_view, qk_view, v_view, v_view]
        in_specs += [
            pl.BlockSpec(blk, lambda c, n: (c, n, 0)),
            pl.BlockSpec(blk, lambda c, n: (A_CHUNKS + c, n, 0)),
            pl.BlockSpec(blk, lambda c, n: (A_CHUNKS + c, jnp.maximum(n - 1, 0), 0)),
            pl.BlockSpec(blk, lambda c, n: (c, n, 0)),
            pl.BlockSpec(blk, lambda c, n: (c, jnp.maximum(n - 1, 0), 0)),
        ]
        scratch += [pltpu.VMEM((rows + BAND, width), MXU_DTYPE)] * 2
    scratch += [pltpu.VMEM((len(DILATIONS), A_TILE, LANES), F32)] * 2
    operands.append(z)
    in_specs.append(pl.BlockSpec((1, A_TILE, LANES), lambda c, n: (c, n, 0)))
    return pl.pallas_call(
        _dilated_body,
        grid=(A_CHUNKS, s // A_TILE),
        in_specs=in_specs,
        out_specs=pl.BlockSpec((A_TILE, LANES), lambda c, n: (n, c)),
        out_shape=jax.ShapeDtypeStruct((s, A_CHUNKS * LANES), MXU_DTYPE),
        scratch_shapes=scratch,
        compiler_params=_params(("parallel", "arbitrary")),
        name="dilated_mixture",
    )(*operands)


def _fox_body(q_ref, k_ref, v_ref, ccol_ref, crow_ref, z_ref, o_ref, *, blk):
    qi = pl.program_id(1)
    q = q_ref[0]
    first = lax.broadcasted_iota(jnp.int32, (blk, LANES), 1) < HEAD_DIM
    causal = (lax.broadcasted_iota(jnp.int32, (blk, blk), 1)
              <= lax.broadcasted_iota(jnp.int32, (blk, blk), 0))
    zero = jnp.zeros_like(q)
    outs = []
    for h, qm in enumerate((jnp.where(first, q, zero), jnp.where(first, zero, q))):
        cq = ccol_ref[0, :, h:h + 1]

        def step(kb, carry, diagonal, qm=qm, cq=cq, h=h):
            m, l, acc = carry
            k0 = pl.multiple_of(kb * blk, blk)
            k2 = k_ref[0, pl.ds(k0, blk), :]
            v2 = v_ref[0, pl.ds(k0, blk), :]
            ck = crow_ref[0, kb, h:h + 1, :]
            s = lax.dot_general(qm, k2, _NT, preferred_element_type=F32) + (cq - ck)
            if diagonal:
                s = jnp.where(causal, s, -jnp.inf)
            m_new = jnp.maximum(m, jnp.max(s, axis=-1, keepdims=True))
            alpha = jnp.exp(m - m_new)
            p = jnp.exp(s - m_new)
            l = alpha * l + jnp.sum(p, axis=-1, keepdims=True)
            acc = alpha * acc + jnp.dot(p.astype(MXU_DTYPE), v2, preferred_element_type=F32)
            return m_new, l, acc

        init = (jnp.full((blk, 1), -jnp.inf, F32), jnp.zeros((blk, 1), F32),
                jnp.zeros((blk, LANES), F32))
        carry = lax.fori_loop(0, qi, functools.partial(step, diagonal=False), init)
        _, l, acc = step(qi, carry, diagonal=True)
        outs.append(acc / l)
    o = jnp.where(first, outs[0], outs[1])
    o_ref[...] = (o * z_ref[0].astype(F32)).astype(o_ref.dtype)


def _forgetting_attention(qk, v, z, c, blk=256):
    s = qk.shape[1]
    nblk = s // blk
    c_col = c.reshape(B_CHUNKS, 2, s).transpose(0, 2, 1)
    c_row = c.reshape(B_CHUNKS, 2, nblk, blk).transpose(0, 2, 1, 3)
    q0, k0 = 2 * A_CHUNKS, 2 * A_CHUNKS + B_CHUNKS
    return pl.pallas_call(
        functools.partial(_fox_body, blk=blk),
        grid=(B_CHUNKS, nblk),
        in_specs=[
            pl.BlockSpec((1, blk, LANES), lambda c_, i: (q0 + c_, i, 0)),
            pl.BlockSpec((1, s, LANES), lambda c_, i: (k0 + c_, 0, 0)),
            pl.BlockSpec((1, s, LANES), lambda c_, i: (A_CHUNKS + c_, 0, 0)),
            pl.BlockSpec((1, blk, 2), lambda c_, i: (c_, i, 0)),
            pl.BlockSpec((1, nblk, 2, blk), lambda c_, i: (c_, 0, 0, 0)),
            pl.BlockSpec((1, blk, LANES), lambda c_, i: (A_CHUNKS + c_, i, 0)),
        ],
        out_specs=pl.BlockSpec((blk, LANES), lambda c_, i: (i, c_)),
        out_shape=jax.ShapeDtypeStruct((s, B_CHUNKS * LANES), MXU_DTYPE),
        compiler_params=_params(("parallel", "arbitrary")),
        name="forgetting_attention",
    )(qk, qk, v, c_col, c_row, z)


def _gla_level_matrix():
    import numpy as np
    c = GLA_CHUNK
    mats = []
    for b in GLA_LEVELS:
        m = np.zeros((c, c), np.float32)
        for i in range(c):
            pivot = (i // (2 * b)) * 2 * b + b - 1
            if i > pivot:
                m[i, pivot + 1:i + 1] = 1.0
            else:
                m[i, i + 1:pivot + 1] = 1.0
        mats.append(m)
    mats.append(np.tril(np.ones((c, c), np.float32)))
    return np.concatenate(mats, axis=0)


def _gla_body(q_ref, k_ref, v_ref, g_ref, z_ref, gu_ref, gb_ref, og_ref, lvl_ref, o_ref,
              la_scr, st_scr, *, tile):
    c = GLA_CHUNK
    nlev = len(GLA_LEVELS)

    @pl.when(pl.program_id(1) == 0)
    def _():
        st_scr[...] = jnp.zeros_like(st_scr)

    logit = jnp.dot(g_ref[:, 0:GATE_RANK], gu_ref[0], preferred_element_type=F32,
                    precision=lax.Precision.HIGHEST) + gb_ref[0]
    la_scr[...] = _log_sigmoid(logit) * (1.0 / GATE_TEMP)

    row = lax.broadcasted_iota(jnp.int32, (c, c), 0)
    col = lax.broadcasted_iota(jnp.int32, (c, c), 1)
    pair_masks = [((row & -(2 * b)) == (col & -(2 * b))) & ((row & b) != 0) & ((col & b) == 0)
                  for b in GLA_LEVELS]
    diag_mask = row == col
    lvl = lvl_ref[...]
    out_gain = og_ref[...]

    def chunk(ci, carry):
        r0 = pl.multiple_of(ci * c, c)
        q = q_ref[0, pl.ds(r0, c), :].astype(F32)
        k = k_ref[0, pl.ds(r0, c), :].astype(F32)
        v = jnp.concatenate([v_ref[0, pl.ds(r0, c), :], v_ref[1, pl.ds(r0, c), :]], axis=-1)
        z = jnp.concatenate([z_ref[0, pl.ds(r0, c), :], z_ref[1, pl.ds(r0, c), :]], axis=-1)
        la = la_scr[pl.ds(r0, c), :]
        la_hi = la.astype(MXU_DTYPE)
        la_lo = (la - la_hi.astype(F32)).astype(MXU_DTYPE)
        e = (jnp.dot(lvl, la_hi, preferred_element_type=F32)
             + jnp.dot(lvl, la_lo, preferred_element_type=F32))
        bc = e[nlev * c:(nlev + 1) * c]
        f = jnp.exp(e[0:nlev * c])
        a0 = lax.dot_general(q.astype(MXU_DTYPE), k.astype(MXU_DTYPE), _NT,
                             preferred_element_type=F32)
        attn = jnp.where(diag_mask, a0, 0.0)
        for li in range(nlev):
            fl = f[li * c:(li + 1) * c]
            a = lax.dot_general((q * fl).astype(MXU_DTYPE), (k * fl).astype(MXU_DTYPE), _NT,
                                preferred_element_type=F32)
            attn = attn + jnp.where(pair_masks[li], a, 0.0)
        state = st_scr[...]
        o = jnp.dot(attn.astype(MXU_DTYPE), v, preferred_element_type=F32)
        o = o + lax.dot_general((q * jnp.exp(bc)).astype(MXU_DTYPE), state.astype(MXU_DTYPE),
                                _NT, preferred_element_type=F32)
        b_last = bc[c - 1:c, :]
        k_dec = (k * jnp.exp(b_last - bc)).astype(MXU_DTYPE)
        v_t = v.astype(F32).T.astype(MXU_DTYPE)
        st_scr[...] = state * jnp.exp(b_last) + jnp.dot(v_t, k_dec, preferred_element_type=F32)
        ms = jnp.sum(o * o, axis=-1, keepdims=True) * (1.0 / C_DV)
        y = o * lax.rsqrt(ms + RMS_EPS) * out_gain * z.astype(F32)
        o_ref[pl.ds(r0, c), :] = y.astype(o_ref.dtype)
        return carry

    lax.fori_loop(0, tile // c, chunk, 0)


def _gated_linear_attention(v_arr, z_arr, g, gate_up, gate_bias, out_gain, tile=512):
    s = v_arr.shape[1]
    q0 = A_CHUNKS + B_CHUNKS
    k0 = q0 + C_HEADS
    v0 = (k0 + C_HEADS) // 2
    z0 = (A_CHUNKS + B_CHUNKS) // 2
    lvl = jnp.asarray(_gla_level_matrix(), MXU_DTYPE)
    return pl.pallas_call(
        functools.partial(_gla_body, tile=tile),
        grid=(C_HEADS, s // tile),
        in_specs=[
            pl.BlockSpec((1, tile, LANES), lambda h, t: (q0 + h, t, 0)),
            pl.BlockSpec((1, tile, LANES), lambda h, t: (k0 + h, t, 0)),
            pl.BlockSpec((2, tile, LANES), lambda h, t: (v0 + h, t, 0)),
            pl.BlockSpec((tile, LANES), lambda h, t: (t, 0)),
            pl.BlockSpec((2, tile, LANES), lambda h, t: (z0 + h, t, 0)),
            pl.BlockSpec((1, GATE_RANK, C_DK_PAD), lambda h, t: (h, 0, 0)),
            pl.BlockSpec((1, 1, C_DK_PAD), lambda h, t: (h, 0, 0)),
            pl.BlockSpec((1, C_DV_PAD), lambda h, t: (0, 0)),
            pl.BlockSpec(lvl.shape, lambda h, t: (0, 0)),
        ],
        out_specs=pl.BlockSpec((tile, C_DV_PAD), lambda h, t: (t, h)),
        out_shape=jax.ShapeDtypeStruct((s, C_HEADS * C_DV_PAD), MXU_DTYPE),
        scratch_shapes=[pltpu.VMEM((tile, C_DK_PAD), F32), pltpu.VMEM((C_DV_PAD, C_DK_PAD), F32)],
        compiler_params=_params(("parallel", "arbitrary")),
        name="gated_linear_attention",
    )(v_arr, v_arr, v_arr, g, z_arr, gate_up, gate_bias, out_gain, lvl)


def _out_body(x_ref, a_ref, b_ref, c_ref, wa_ref, wb_ref, wc_ref, o_ref):
    acc = jnp.dot(a_ref[...], wa_ref[...], preferred_element_type=F32)
    acc = acc + jnp.dot(b_ref[...], wb_ref[...], preferred_element_type=F32)
    acc = acc + jnp.dot(c_ref[...], wc_ref[...], preferred_element_type=F32)
    o_ref[...] = x_ref[...] + acc


def _out_proj(x, ma, mb, mc, wa, wb, wc, tm=1024, tn=512):
    s, d = x.shape
    row_blk = lambda m: pl.BlockSpec((tm, m.shape[1]), lambda i, j: (i, 0))
    col_blk = lambda w: pl.BlockSpec((w.shape[0], tn), lambda i, j: (0, j))
    return pl.pallas_call(
        _out_body,
        grid=(s // tm, d // tn),
        in_specs=[pl.BlockSpec((tm, tn), lambda i, j: (i, j)),
                  row_blk(ma), row_blk(mb), row_blk(mc), col_blk(wa), col_blk(wb), col_blk(wc)],
        out_specs=pl.BlockSpec((tm, tn), lambda i, j: (i, j)),
        out_shape=jax.ShapeDtypeStruct((s, d), F32),
        compiler_params=_params(("parallel", "arbitrary")),
        name="out_proj",
    )(x, ma, mb, mc, wa, wb, wc)


def _pad_heads(w, heads, width, padded):
    lead = w.shape[:-1]
    w = w.reshape(*lead, heads, width)
    w = jnp.pad(w, [(0, 0)] * len(lead) + [(0, 0), (0, padded - width)])
    return w.reshape(*lead, heads * padded)


def _split_in_proj(w_in):
    aw, bw = A_HEADS * HEAD_DIM, B_HEADS * HEAD_DIM
    ck, cv = C_HEADS * C_DK, C_HEADS * C_DV
    sizes = (aw, aw, aw, aw, bw, bw, bw, bw, B_HEADS, ck, ck, cv, cv, GATE_RANK)
    parts, start = [], 0
    for n in sizes:
        parts.append(w_in[:, start:start + n])
        start += n
    return parts


def _layer(x, norm_g, w_in, a_q_gain, a_k_gain, b_q_gain, b_k_gain, fox_bias,
           gla_gate_up, gla_gate_bias, gla_out_gain, w_out):
    d = x.shape[1]
    (aq, ak, av, az, bq, bk, bv, bz, bf, cq, ck, cv, cz, cr) = _split_in_proj(w_in)
    pad_k = lambda w: _pad_heads(w, C_HEADS, C_DK, C_DK_PAD)
    pad_v = lambda w: _pad_heads(w, C_HEADS, C_DV, C_DV_PAD)
    w_qk = jnp.concatenate([aq, ak, bq, bk], axis=1).astype(MXU_DTYPE)
    w_v = jnp.concatenate([av, bv, pad_k(cq), pad_k(ck), pad_v(cv)], axis=1).astype(MXU_DTYPE)
    w_z = jnp.concatenate([az, bz, pad_v(cz)], axis=1).astype(MXU_DTYPE)
    w_g = jnp.concatenate([cr, bf, jnp.zeros((d, LANES - GATE_RANK - B_HEADS), F32)],
                          axis=1).astype(MXU_DTYPE)

    q_scale = HEAD_DIM ** -0.5
    gain_qk = jnp.concatenate([jnp.tile(a_q_gain * q_scale, A_HEADS), jnp.tile(a_k_gain, A_HEADS),
                               jnp.tile(b_q_gain * q_scale, B_HEADS), jnp.tile(b_k_gain, B_HEADS)])
    scale_v = jnp.concatenate([jnp.ones((A_HEADS + B_HEADS) * HEAD_DIM, F32),
                               jnp.full((C_HEADS * C_DK_PAD,), C_DK ** -0.5, F32),
                               jnp.ones((C_HEADS * (C_DK_PAD + C_DV_PAD),), F32)])

    h = _rmsnorm(x, norm_g)
    qk = _proj(h, w_qk, gain_qk.reshape(1, -1), "headnorm")
    v_arr = _proj(h, w_v, scale_v.reshape(1, -1), "scale")
    z_arr = _proj(h, w_z, jnp.zeros((1, w_z.shape[1]), F32), "silu")
    g = _proj(h, w_g, jnp.zeros((1, LANES), F32), "f32")

    c = _fox_cumsum(g[:, GATE_RANK:GATE_RANK + B_HEADS].T, fox_bias)

    mixed_a = _dilated_mixture(qk, v_arr, z_arr)
    mixed_b = _forgetting_attention(qk, v_arr, z_arr, c)
    gate_up = _pad_heads(gla_gate_up, C_HEADS, C_DK, C_DK_PAD)
    gate_up = gate_up.reshape(GATE_RANK, C_HEADS, C_DK_PAD).transpose(1, 0, 2)
    gate_bias = _pad_heads(gla_gate_bias, C_HEADS, C_DK, C_DK_PAD).reshape(C_HEADS, 1, C_DK_PAD)
    out_gain = jnp.pad(gla_out_gain, (0, C_DV_PAD - C_DV)).reshape(1, C_DV_PAD)
    mixed_c = _gated_linear_attention(v_arr, z_arr, g, gate_up, gate_bias, out_gain)

    aw, bw = A_HEADS * HEAD_DIM, B_HEADS * HEAD_DIM
    wo_a = w_out[:aw].astype(MXU_DTYPE)
    wo_b = w_out[aw:aw + bw].astype(MXU_DTYPE)
    wo_c = w_out[aw + bw:].reshape(C_HEADS, C_DV, d)
    wo_c = jnp.pad(wo_c, ((0, 0), (0, C_DV_PAD - C_DV), (0, 0))).reshape(C_HEADS * C_DV_PAD, d)
    return _out_proj(x, mixed_a, mixed_b, mixed_c, wo_a, wo_b, wo_c.astype(MXU_DTYPE))


@jax.jit
def kernel(x, norm_g, w_in, a_q_gain, a_k_gain, b_q_gain, b_k_gain, fox_bias, gla_gate_up,
           gla_gate_bias, gla_out_gain, w_out):
    bsz, s, d = x.shape
    y = x.reshape(bsz * s, d) if bsz == 1 else None
    assert y is not None, "batch size 1 only"
    for layer in range(norm_g.shape[0]):
        y = _layer(y, norm_g[layer], w_in[layer], a_q_gain[layer], a_k_gain[layer],
                   b_q_gain[layer], b_k_gain[layer], fox_bias[layer], gla_gate_up[layer],
                   gla_gate_bias[layer], gla_out_gain[layer], w_out[layer])
    return y.reshape(bsz, s, d)
```

```python
import functools

import jax
import jax.numpy as jnp
from jax import lax
from jax.experimental import pallas as pl
from jax.experimental.pallas import tpu as pltpu

F32 = jnp.float32
MXU_DTYPE = jnp.bfloat16

LANES = 128
HEAD_DIM = 64
A_HEADS, B_HEADS, C_HEADS = 12, 8, 4
A_CHUNKS, B_CHUNKS = A_HEADS // 2, B_HEADS // 2
C_DK, C_DV = 96, 192
C_DK_PAD, C_DV_PAD = 128, 256
GATE_RANK = 16
GATE_TEMP = 16.0
RMS_EPS = 1e-6
LOG2E = 1.4426950408889634
DILATIONS = (1, 4, 16)
BAND = 128
A_TILE = 2048
GLA_CHUNK = 64
GLA_LEVELS = (32, 16, 8, 4, 2, 1)
VMEM_LIMIT = 56 * 1024 * 1024

_NT = (((1,), (1,)), ((), ()))


def _params(sem):
    return pltpu.CompilerParams(dimension_semantics=sem, vmem_limit_bytes=VMEM_LIMIT)


def _rmsnorm_body(x_ref, g_ref, o_ref):
    x = x_ref[...]
    ms = jnp.mean(x * x, axis=-1, keepdims=True)
    o_ref[...] = (x * lax.rsqrt(ms + RMS_EPS) * g_ref[...]).astype(o_ref.dtype)


def _rmsnorm(x, g, tm=512):
    s, d = x.shape
    return pl.pallas_call(
        _rmsnorm_body,
        grid=(s // tm,),
        in_specs=[pl.BlockSpec((tm, d), lambda i: (i, 0)), pl.BlockSpec((1, d), lambda i: (0, 0))],
        out_specs=pl.BlockSpec((tm, d), lambda i: (i, 0)),
        out_shape=jax.ShapeDtypeStruct((s, d), MXU_DTYPE),
        compiler_params=_params(("parallel",)),
        name="rmsnorm",
    )(x, g.reshape(1, d))


def _proj_body(h_ref, w_ref, aux_ref, o_ref, *rest, kind, view_tiles):
    acc = jnp.dot(h_ref[...], w_ref[...], preferred_element_type=F32)
    if kind == "f32":
        o_ref[...] = acc
        return
    tm = acc.shape[0]
    for c in range(acc.shape[1] // LANES):
        y = acc[:, c * LANES:(c + 1) * LANES]
        a = aux_ref[:, c * LANES:(c + 1) * LANES]
        if kind == "headnorm":
            first = lax.broadcasted_iota(jnp.int32, y.shape, 1) < HEAD_DIM
            y2 = y * y
            s0 = jnp.sum(jnp.where(first, y2, 0.0), axis=-1, keepdims=True)
            s1 = jnp.sum(jnp.where(first, 0.0, y2), axis=-1, keepdims=True)
            ms = jnp.where(first, s0, s1) * (1.0 / HEAD_DIM)
            y = y * lax.rsqrt(ms + RMS_EPS) * a
        elif kind == "scale":
            y = y * a
        elif kind == "silu":
            y = y * jax.nn.sigmoid(y)
        o_ref[c] = y.astype(o_ref.dtype)
        if view_tiles:
            *view_refs, stage = rest

            @pl.when(pl.program_id(1) < view_tiles)
            def _(y=y, c=c):
                stage[...] = y
                for view, dil in zip(view_refs, DILATIONS[1:]):
                    for r in range(dil):
                        rows = stage[pl.ds(r, tm // dil, stride=dil), :]
                        view[c, :, r * LANES:(r + 1) * LANES] = rows.astype(view.dtype)


def _proj(h, w, aux, kind, view_tiles=0, tm=1024, tn=256):
    s, d = h.shape
    n = w.shape[1]
    chunks = tn // LANES
    scratch = []
    if kind == "f32":
        tn = n
        out_shape = jax.ShapeDtypeStruct((s, n), F32)
        out_spec = pl.BlockSpec((tm, tn), lambda i, j: (i, j))
    else:
        out_shape = jax.ShapeDtypeStruct((n // LANES, s, LANES), MXU_DTYPE)
        out_spec = pl.BlockSpec((chunks, tm, LANES), lambda i, j: (j, i, 0))
    if view_tiles:
        out_shape, out_spec = [out_shape], [out_spec]
        for dil in DILATIONS[1:]:
            out_shape.append(
                jax.ShapeDtypeStruct((view_tiles * chunks, s // dil, dil * LANES), MXU_DTYPE))
            out_spec.append(pl.BlockSpec((chunks, tm // dil, dil * LANES),
                                         lambda i, j: (jnp.minimum(j, view_tiles - 1), i, 0)))
        scratch = [pltpu.VMEM((tm, LANES), F32)]
    return pl.pallas_call(
        functools.partial(_proj_body, kind=kind, view_tiles=view_tiles),
        grid=(s // tm, n // tn),
        in_specs=[pl.BlockSpec((tm, d), lambda i, j: (i, 0)),
                  pl.BlockSpec((d, tn), lambda i, j: (0, j)),
                  pl.BlockSpec((1, tn), lambda i, j: (0, j))],
        out_specs=out_spec,
        out_shape=out_shape,
        scratch_shapes=scratch,
        compiler_params=_params(("parallel", "arbitrary")),
        name="proj_" + kind,
    )(h, w, aux)


def _log_sigmoid(x):
    return jnp.minimum(x, 0.0) - jnp.log1p(jnp.exp(-jnp.abs(x)))


def _fox_cumsum_body(x_ref, b_ref, o_ref):
    x = _log_sigmoid(x_ref[...] + b_ref[...])
    idx = lax.broadcasted_iota(jnp.int32, x.shape, 1)
    shift = 1
    while shift < x.shape[1]:
        x = x + jnp.where(idx >= shift, pltpu.roll(x, shift, axis=1), 0.0)
        shift *= 2
    o_ref[...] = x * LOG2E


def _fox_cumsum(logit_t, bias):
    nh, s = logit_t.shape
    return pl.pallas_call(
        _fox_cumsum_body,
        out_shape=jax.ShapeDtypeStruct((nh, s), F32),
        compiler_params=pltpu.CompilerParams(vmem_limit_bytes=VMEM_LIMIT),
        name="fox_cumsum",
    )(logit_t, bias.reshape(nh, 1))


A_GROUP = 4


def _band_attention(blocks):
    row = lax.broadcasted_iota(jnp.int32, (BAND, 2 * BAND), 0)
    col = lax.broadcasted_iota(jnp.int32, (BAND, 2 * BAND), 1)
    band = (col >= row) & (col <= row + BAND)
    first = lax.broadcasted_iota(jnp.int32, (BAND, LANES), 1) < HEAD_DIM
    first_kv = lax.broadcasted_iota(jnp.int32, (2 * BAND, LANES), 1) < HEAD_DIM
    results = []
    for q, k2, v2, first_key_idx in blocks:
        mask = band & (col + first_key_idx >= 0)
        zero, one = jnp.zeros_like(q), jnp.ones_like(v2)
        res, maxes = [], []
        for h in range(2):
            qm = jnp.where(first, q, zero) if h == 0 else jnp.where(first, zero, q)
            vh = jnp.where(first_kv, v2, one) if h == 0 else jnp.where(first_kv, one, v2)
            s = lax.dot_general(qm, k2, _NT, preferred_element_type=F32)
            s = jnp.where(mask, s, -jnp.inf)
            m = jnp.max(s, axis=-1, keepdims=True)
            p = jnp.exp2(s - m).astype(MXU_DTYPE)
            res.append(jnp.dot(p, vh, preferred_element_type=F32))
            maxes.append(m)
        num = jnp.where(first, res[0], res[1])
        den = pltpu.roll(jnp.where(first, res[1], res[0]), HEAD_DIM, axis=1)
        results.append((num / den, jnp.where(first, maxes[0], maxes[1]) + jnp.log2(den)))
    return results


def _dilated_body(*refs):
    ins, z_ref, o_ref, scr = refs[:15], refs[15], refs[16], refs[17:]
    kv_scr, o_scr, l_scr = scr[:6], scr[6], scr[7]
    n = pl.program_id(1)
    for bi, d in enumerate(DILATIONS):
        q_ref, kc, kp, vc, vp = ins[5 * bi:5 * bi + 5]
        kf, vf = kv_scr[2 * bi:2 * bi + 2]
        rows = A_TILE // d
        nb = rows // BAND
        for cur, prev, full in ((kc, kp, kf), (vc, vp, vf)):
            full[0:BAND, :] = prev[0, rows - BAND:rows, :]
            full[BAND:BAND + rows, :] = cur[0]

        def group(members, q_ref=q_ref, kf=kf, vf=vf, d=d, nb=nb, bi=bi):
            blocks = []
            for j, r in members:
                cols = slice(r * LANES, (r + 1) * LANES)
                row0 = j * BAND if isinstance(j, int) else pl.multiple_of(j * BAND, BAND)
                blocks.append((q_ref[0, pl.ds(row0, BAND), cols], kf[pl.ds(row0, 2 * BAND), cols],
                               vf[pl.ds(row0, 2 * BAND), cols], (n * nb + j - 1) * BAND))
            for (j, r), (o, lse) in zip(members, _band_attention(blocks)):
                dst = pl.ds(j * BAND * d + r, BAND, stride=d)
                o_scr[bi, dst, :] = o
                l_scr[bi, dst, :] = lse

        if nb >= A_GROUP:
            for r in range(d):
                def body(g, carry, r=r, group=group):
                    j0 = pl.multiple_of(g * A_GROUP, A_GROUP)
                    group([(j0 + i, r) for i in range(A_GROUP)])
                    return carry
                lax.fori_loop(0, nb // A_GROUP, body, 0)
        else:
            for r0 in range(0, d, A_GROUP // nb):
                group([(j, r0 + i) for i in range(A_GROUP // nb) for j in range(nb)])
    lse = l_scr[...]
    w = jnp.exp2(lse - jnp.max(lse, axis=0, keepdims=True))
    mixed = jnp.sum(w * o_scr[...], axis=0) / jnp.sum(w, axis=0)
    o_ref[...] = (mixed * z_ref[0].astype(F32)).astype(o_ref.dtype)


def _dilated_mixture(qk_views, v_views, z):
    s = z.shape[1]
    operands, in_specs, scratch = [], [], []
    for d, qk_view, v_view in zip(DILATIONS, qk_views, v_views):
        rows, width = A_TILE // d, d * LANES
        blk = (1, rows, width)
        operands += [qk_view, qk_view, qk_view, v_view, v_view]
        in_specs += [
            pl.BlockSpec(blk, lambda c, n: (c, n, 0)),
            pl.BlockSpec(blk, lambda c, n: (A_CHUNKS + c, n, 0)),
            pl.BlockSpec(blk, lambda c, n: (A_CHUNKS + c, jnp.maximum(n - 1, 0), 0)),
            pl.BlockSpec(blk, lambda c, n: (c, n, 0)),
            pl.BlockSpec(blk, lambda c, n: (c, jnp.maximum(n - 1, 0), 0)),
        ]
        scratch += [pltpu.VMEM((rows + BAND, width), MXU_DTYPE)] * 2
    scratch += [pltpu.VMEM((len(DILATIONS), A_TILE, LANES), F32)] * 2
    operands.append(z)
    in_specs.append(pl.BlockSpec((1, A_TILE, LANES), lambda c, n: (c, n, 0)))
    return pl.pallas_call(
        _dilated_body,
        grid=(A_CHUNKS, s // A_TILE),
        in_specs=in_specs,
        out_specs=pl.BlockSpec((A_TILE, LANES), lambda c, n: (n, c)),
        out_shape=jax.ShapeDtypeStruct((s, A_CHUNKS * LANES), MXU_DTYPE),
        scratch_shapes=scratch,
        compiler_params=_params(("parallel", "arbitrary")),
        name="dilated_mixture",
    )(*operands)


FOX_AUG = 3
FOX_VROWS = HEAD_DIM + 16


def _split3(c):
    hi = c.astype(jnp.bfloat16).astype(F32)
    r = c - hi
    mid = r.astype(jnp.bfloat16).astype(F32)
    return hi, mid, r - mid


def _fox_prep_body(q_ref, k_ref, v_ref, ccol_ref, crow_ref, qt_ref, ka_ref, vt_ref):
    tm = q_ref.shape[1]
    q_t = q_ref[0].astype(F32).T
    v_t = v_ref[0].astype(F32).T
    k = k_ref[0]
    sub = lax.broadcasted_iota(jnp.int32, (HEAD_DIM, tm), 0)
    lane = lax.broadcasted_iota(jnp.int32, (tm, LANES), 1)
    v_tail = (lax.broadcasted_iota(jnp.int32, (FOX_VROWS - HEAD_DIM, tm), 0) == 0).astype(F32)
    for h in range(2):
        hi, mid, lo = _split3(crow_ref[0, h:h + 1, :])
        aug_q = jnp.where(sub == 0, hi, jnp.where(sub == 1, mid, jnp.where(
            sub == 2, lo, jnp.where(sub < 2 * FOX_AUG, 1.0, 0.0))))
        q_h = q_t[h * HEAD_DIM:(h + 1) * HEAD_DIM]
        parts = [q_h, aug_q] if h == 0 else [aug_q, q_h]
        qt_ref[h] = jnp.concatenate(parts, axis=0).astype(qt_ref.dtype)
        hi, mid, lo = _split3(ccol_ref[0, :, h:h + 1])
        a0 = (1 - h) * HEAD_DIM
        aug_k = jnp.where(lane == a0 + FOX_AUG, -hi, jnp.where(lane == a0 + FOX_AUG + 1, -mid, jnp.where(
            lane == a0 + FOX_AUG + 2, -lo, jnp.where((lane >= a0) & (lane < a0 + FOX_AUG), 1.0, 0.0))))
        own = (lane < HEAD_DIM) if h == 0 else (lane >= HEAD_DIM)
        ka_ref[h] = jnp.where(own, k, aug_k.astype(k.dtype))
        vt_ref[h, 0] = jnp.concatenate([v_t[h * HEAD_DIM:(h + 1) * HEAD_DIM], v_tail],
                                       axis=0).astype(vt_ref.dtype)


def _fox_prep(qk, v, c, tm):
    s = qk.shape[1]
    c_col = c.reshape(B_CHUNKS, 2, s).transpose(0, 2, 1)
    c_row = c.reshape(B_CHUNKS, 2, s)
    q0, k0 = 2 * A_CHUNKS, 2 * A_CHUNKS + B_CHUNKS
    return pl.pallas_call(
        _fox_prep_body,
        grid=(B_CHUNKS, s // tm),
        in_specs=[
            pl.BlockSpec((1, tm, LANES), lambda p, i: (q0 + p, i, 0)),
            pl.BlockSpec((1, tm, LANES), lambda p, i: (k0 + p, i, 0)),
            pl.BlockSpec((1, tm, LANES), lambda p, i: (A_CHUNKS + p, i, 0)),
            pl.BlockSpec((1, tm, 2), lambda p, i: (p, i, 0)),
            pl.BlockSpec((1, 2, tm), lambda p, i: (p, 0, i)),
        ],
        out_specs=[
            pl.BlockSpec((2, LANES, tm), lambda p, i: (p, 0, i)),
            pl.BlockSpec((2, tm, LANES), lambda p, i: (p, i, 0)),
            pl.BlockSpec((2, 1, FOX_VROWS, tm), lambda p, i: (p, i, 0, 0)),
        ],
        out_shape=[
            jax.ShapeDtypeStruct((B_HEADS, LANES, s), MXU_DTYPE),
            jax.ShapeDtypeStruct((B_HEADS, s, LANES), MXU_DTYPE),
            jax.ShapeDtypeStruct((B_HEADS, s // tm, FOX_VROWS, tm), MXU_DTYPE),
        ],
        compiler_params=_params(("parallel", "parallel")),
        name="fox_prep",
    )(qk, qk, v, c_col, c_row)


def _fox_body(qt_ref, ka_ref, vt_ref, z_ref, o_ref, s0_scr, s1_scr, m_scr, acc_scr, *, blk):
    qi = pl.program_id(1)
    causal = (lax.broadcasted_iota(jnp.int32, (blk, blk), 0)
              <= lax.broadcasted_iota(jnp.int32, (blk, blk), 1))
    q_t = (qt_ref[0], qt_ref[1])

    def logits(kb, dst):
        k0 = pl.multiple_of(kb * blk, blk)
        for h in range(2):
            dst[h] = jnp.dot(ka_ref[h, pl.ds(k0, blk), :], q_t[h], preferred_element_type=F32)

    def accumulate(kb, src, diagonal=False):
        for h in range(2):
            s = src[h]
            if diagonal:
                s = jnp.where(causal, s, -jnp.inf)
            m = m_scr[h]
            m_new = jnp.maximum(m, jnp.max(s, axis=0, keepdims=True))
            p = jnp.exp2(s - m_new).astype(MXU_DTYPE)
            acc_scr[h] = (jnp.exp2(m - m_new) * acc_scr[h]
                          + jnp.dot(vt_ref[h, kb], p, preferred_element_type=F32))
            m_scr[h] = m_new

    m_scr[...] = jnp.full(m_scr.shape, -jnp.inf, F32)
    acc_scr[...] = jnp.zeros(acc_scr.shape, F32)
    logits(0, s0_scr)

    def pair(j, carry):
        logits(2 * j + 1, s1_scr)
        accumulate(2 * j, s0_scr)
        logits(2 * j + 2, s0_scr)
        accumulate(2 * j + 1, s1_scr)
        return carry

    lax.fori_loop(0, qi // 2, pair, 0)

    @pl.when(qi % 2 == 0)
    def _():
        accumulate(qi, s0_scr, diagonal=True)

    @pl.when(qi % 2 == 1)
    def _():
        logits(qi, s1_scr)
        accumulate(qi - 1, s0_scr)
        accumulate(qi, s1_scr, diagonal=True)

    outs = [acc_scr[h, 0:HEAD_DIM] / acc_scr[h, HEAD_DIM:HEAD_DIM + 1] for h in range(2)]
    o = jnp.concatenate(outs, axis=0).T
    o_ref[...] = (o * z_ref[0].astype(F32)).astype(o_ref.dtype)


def _forgetting_attention(qk, v, z, c, blk=512):
    s = qk.shape[1]
    qt, ka, vt = _fox_prep(qk, v, c, blk)
    return pl.pallas_call(
        functools.partial(_fox_body, blk=blk),
        grid=(B_CHUNKS, s // blk),
        in_specs=[
            pl.BlockSpec((2, LANES, blk), lambda p, i: (p, 0, i)),
            pl.BlockSpec((2, s, LANES), lambda p, i: (p, 0, 0)),
            pl.BlockSpec((2, s // blk, FOX_VROWS, blk), lambda p, i: (p, 0, 0, 0)),
            pl.BlockSpec((1, blk, LANES), lambda p, i: (A_CHUNKS + p, i, 0)),
        ],
        out_specs=pl.BlockSpec((blk, LANES), lambda p, i: (i, p)),
        out_shape=jax.ShapeDtypeStruct((s, B_CHUNKS * LANES), MXU_DTYPE),
        scratch_shapes=[pltpu.VMEM((2, blk, blk), F32), pltpu.VMEM((2, blk, blk), F32),
                        pltpu.VMEM((2, 1, blk), F32), pltpu.VMEM((2, FOX_VROWS, blk), F32)],
        compiler_params=_params(("parallel", "arbitrary")),
        name="forgetting_attention",
    )(qt, ka, vt, z)


def _gla_level_matrix():
    import numpy as np
    c = GLA_CHUNK
    mats = []
    for b in GLA_LEVELS:
        m = np.zeros((c, c), np.float32)
        for i in range(c):
            pivot = (i // (2 * b)) * 2 * b + b - 1
            if i > pivot:
                m[i, pivot + 1:i + 1] = 1.0
            else:
                m[i, i + 1:pivot + 1] = 1.0
        mats.append(m)
    mats.append(np.tril(np.ones((c, c), np.float32)))
    return np.concatenate(mats, axis=0)


def _gla_body(q_ref, k_ref, v_ref, g_ref, z_ref, gu_ref, gb_ref, og_ref, lvl_ref, o_ref,
              la_scr, st_scr, *, tile):
    c = GLA_CHUNK
    nlev = len(GLA_LEVELS)

    @pl.when(pl.program_id(1) == 0)
    def _():
        st_scr[...] = jnp.zeros_like(st_scr)

    logit = jnp.dot(g_ref[:, 0:GATE_RANK], gu_ref[0], preferred_element_type=F32,
                    precision=lax.Precision.HIGHEST) + gb_ref[0]
    la_scr[...] = _log_sigmoid(logit) * (1.0 / GATE_TEMP)

    row = lax.broadcasted_iota(jnp.int32, (c, c), 0)
    col = lax.broadcasted_iota(jnp.int32, (c, c), 1)
    pair_masks = [((row & -(2 * b)) == (col & -(2 * b))) & ((row & b) != 0) & ((col & b) == 0)
                  for b in GLA_LEVELS]
    diag_mask = row == col
    lvl = lvl_ref[...]
    out_gain = og_ref[...]

    def chunk(ci, carry):
        r0 = pl.multiple_of(ci * c, c)
        q = q_ref[0, pl.ds(r0, c), :].astype(F32)
        k = k_ref[0, pl.ds(r0, c), :].astype(F32)
        v = jnp.concatenate([v_ref[0, pl.ds(r0, c), :], v_ref[1, pl.ds(r0, c), :]], axis=-1)
        z = jnp.concatenate([z_ref[0, pl.ds(r0, c), :], z_ref[1, pl.ds(r0, c), :]], axis=-1)
        la = la_scr[pl.ds(r0, c), :]
        la_hi = la.astype(MXU_DTYPE)
        la_lo = (la - la_hi.astype(F32)).astype(MXU_DTYPE)
        e = (jnp.dot(lvl, la_hi, preferred_element_type=F32)
             + jnp.dot(lvl, la_lo, preferred_element_type=F32))
        bc = e[nlev * c:(nlev + 1) * c]
        f = jnp.exp(e[0:nlev * c])
        a0 = lax.dot_general(q.astype(MXU_DTYPE), k.astype(MXU_DTYPE), _NT,
                             preferred_element_type=F32)
        attn = jnp.where(diag_mask, a0, 0.0)
        for li in range(nlev):
            fl = f[li * c:(li + 1) * c]
            a = lax.dot_general((q * fl).astype(MXU_DTYPE), (k * fl).astype(MXU_DTYPE), _NT,
                                preferred_element_type=F32)
            attn = attn + jnp.where(pair_masks[li], a, 0.0)
        state = st_scr[...]
        o = jnp.dot(attn.astype(MXU_DTYPE), v, preferred_element_type=F32)
        o = o + lax.dot_general((q * jnp.exp(bc)).astype(MXU_DTYPE), state.astype(MXU_DTYPE),
                                _NT, preferred_element_type=F32)
        b_last = bc[c - 1:c, :]
        k_dec = (k * jnp.exp(b_last - bc)).astype(MXU_DTYPE)
        v_t = v.astype(F32).T.astype(MXU_DTYPE)
        st_scr[...] = state * jnp.exp(b_last) + jnp.dot(v_t, k_dec, preferred_element_type=F32)
        ms = jnp.sum(o * o, axis=-1, keepdims=True) * (1.0 / C_DV)
        y = o * lax.rsqrt(ms + RMS_EPS) * out_gain * z.astype(F32)
        o_ref[pl.ds(r0, c), :] = y.astype(o_ref.dtype)
        return carry

    lax.fori_loop(0, tile // c, chunk, 0)


def _gated_linear_attention(v_arr, z_arr, g, gate_up, gate_bias, out_gain, tile=512):
    s = v_arr.shape[1]
    q0 = A_CHUNKS + B_CHUNKS
    k0 = q0 + C_HEADS
    v0 = (k0 + C_HEADS) // 2
    z0 = (A_CHUNKS + B_CHUNKS) // 2
    lvl = jnp.asarray(_gla_level_matrix(), MXU_DTYPE)
    return pl.pallas_call(
        functools.partial(_gla_body, tile=tile),
        grid=(C_HEADS, s // tile),
        in_specs=[
            pl.BlockSpec((1, tile, LANES), lambda h, t: (q0 + h, t, 0)),
            pl.BlockSpec((1, tile, LANES), lambda h, t: (k0 + h, t, 0)),
            pl.BlockSpec((2, tile, LANES), lambda h, t: (v0 + h, t, 0)),
            pl.BlockSpec((tile, LANES), lambda h, t: (t, 0)),
            pl.BlockSpec((2, tile, LANES), lambda h, t: (z0 + h, t, 0)),
            pl.BlockSpec((1, GATE_RANK, C_DK_PAD), lambda h, t: (h, 0, 0)),
            pl.BlockSpec((1, 1, C_DK_PAD), lambda h, t: (h, 0, 0)),
            pl.BlockSpec((1, C_DV_PAD), lambda h, t: (0, 0)),
            pl.BlockSpec(lvl.shape, lambda h, t: (0, 0)),
        ],
        out_specs=pl.BlockSpec((tile, C_DV_PAD), lambda h, t: (t, h)),
        out_shape=jax.ShapeDtypeStruct((s, C_HEADS * C_DV_PAD), MXU_DTYPE),
        scratch_shapes=[pltpu.VMEM((tile, C_DK_PAD), F32), pltpu.VMEM((C_DV_PAD, C_DK_PAD), F32)],
        compiler_params=_params(("parallel", "arbitrary")),
        name="gated_linear_attention",
    )(v_arr, v_arr, v_arr, g, z_arr, gate_up, gate_bias, out_gain, lvl)


def _out_body(x_ref, a_ref, b_ref, c_ref, wa_ref, wb_ref, wc_ref, o_ref):
    acc = jnp.dot(a_ref[...], wa_ref[...], preferred_element_type=F32)
    acc = acc + jnp.dot(b_ref[...], wb_ref[...], preferred_element_type=F32)
    acc = acc + jnp.dot(c_ref[...], wc_ref[...], preferred_element_type=F32)
    o_ref[...] = x_ref[...] + acc


def _out_proj(x, ma, mb, mc, wa, wb, wc, tm=1024, tn=512):
    s, d = x.shape
    row_blk = lambda m: pl.BlockSpec((tm, m.shape[1]), lambda i, j: (i, 0))
    col_blk = lambda w: pl.BlockSpec((w.shape[0], tn), lambda i, j: (0, j))
    return pl.pallas_call(
        _out_body,
        grid=(s // tm, d // tn),
        in_specs=[pl.BlockSpec((tm, tn), lambda i, j: (i, j)),
                  row_blk(ma), row_blk(mb), row_blk(mc), col_blk(wa), col_blk(wb), col_blk(wc)],
        out_specs=pl.BlockSpec((tm, tn), lambda i, j: (i, j)),
        out_shape=jax.ShapeDtypeStruct((s, d), F32),
        compiler_params=_params(("parallel", "arbitrary")),
        name="out_proj",
    )(x, ma, mb, mc, wa, wb, wc)


def _pad_heads(w, heads, width, padded):
    lead = w.shape[:-1]
    w = w.reshape(*lead, heads, width)
    w = jnp.pad(w, [(0, 0)] * len(lead) + [(0, 0), (0, padded - width)])
    return w.reshape(*lead, heads * padded)


def _split_in_proj(w_in):
    aw, bw = A_HEADS * HEAD_DIM, B_HEADS * HEAD_DIM
    ck, cv = C_HEADS * C_DK, C_HEADS * C_DV
    sizes = (aw, aw, aw, aw, bw, bw, bw, bw, B_HEADS, ck, ck, cv, cv, GATE_RANK)
    parts, start = [], 0
    for n in sizes:
        parts.append(w_in[:, start:start + n])
        start += n
    return parts


def _layer(x, norm_g, w_in, a_q_gain, a_k_gain, b_q_gain, b_k_gain, fox_bias,
           gla_gate_up, gla_gate_bias, gla_out_gain, w_out):
    d = x.shape[1]
    (aq, ak, av, az, bq, bk, bv, bz, bf, cq, ck, cv, cz, cr) = _split_in_proj(w_in)
    pad_k = lambda w: _pad_heads(w, C_HEADS, C_DK, C_DK_PAD)
    pad_v = lambda w: _pad_heads(w, C_HEADS, C_DV, C_DV_PAD)
    w_qk = jnp.concatenate([aq, ak, bq, bk], axis=1).astype(MXU_DTYPE)
    w_v = jnp.concatenate([av, bv, pad_k(cq), pad_k(ck), pad_v(cv)], axis=1).astype(MXU_DTYPE)
    w_z = jnp.concatenate([az, bz, pad_v(cz)], axis=1).astype(MXU_DTYPE)
    w_g = jnp.concatenate([cr, bf, jnp.zeros((d, LANES - GATE_RANK - B_HEADS), F32)],
                          axis=1).astype(MXU_DTYPE)

    q_scale = HEAD_DIM ** -0.5 * LOG2E
    gain_qk = jnp.concatenate([jnp.tile(a_q_gain * q_scale, A_HEADS), jnp.tile(a_k_gain, A_HEADS),
                               jnp.tile(b_q_gain * q_scale, B_HEADS), jnp.tile(b_k_gain, B_HEADS)])
    scale_v = jnp.concatenate([jnp.ones((A_HEADS + B_HEADS) * HEAD_DIM, F32),
                               jnp.full((C_HEADS * C_DK_PAD,), C_DK ** -0.5, F32),
                               jnp.ones((C_HEADS * (C_DK_PAD + C_DV_PAD),), F32)])

    h = _rmsnorm(x, norm_g)
    a_tiles = A_CHUNKS * LANES // 256
    qk, qk4, qk16 = _proj(h, w_qk, gain_qk.reshape(1, -1), "headnorm", view_tiles=2 * a_tiles)
    v_arr, v4, v16 = _proj(h, w_v, scale_v.reshape(1, -1), "scale", view_tiles=a_tiles)
    z_arr = _proj(h, w_z, jnp.zeros((1, w_z.shape[1]), F32), "silu")
    g = _proj(h, w_g, jnp.zeros((1, LANES), F32), "f32")

    c = _fox_cumsum(g[:, GATE_RANK:GATE_RANK + B_HEADS].T, fox_bias)

    mixed_a = _dilated_mixture((qk, qk4, qk16), (v_arr, v4, v16), z_arr)
    mixed_b = _forgetting_attention(qk, v_arr, z_arr, c)
    gate_up = _pad_heads(gla_gate_up, C_HEADS, C_DK, C_DK_PAD)
    gate_up = gate_up.reshape(GATE_RANK, C_HEADS, C_DK_PAD).transpose(1, 0, 2)
    gate_bias = _pad_heads(gla_gate_bias, C_HEADS, C_DK, C_DK_PAD).reshape(C_HEADS, 1, C_DK_PAD)
    out_gain = jnp.pad(gla_out_gain, (0, C_DV_PAD - C_DV)).reshape(1, C_DV_PAD)
    mixed_c = _gated_linear_attention(v_arr, z_arr, g, gate_up, gate_bias, out_gain)

    aw, bw = A_HEADS * HEAD_DIM, B_HEADS * HEAD_DIM
    wo_a = w_out[:aw].astype(MXU_DTYPE)
    wo_b = w_out[aw:aw + bw].astype(MXU_DTYPE)
    wo_c = w_out[aw + bw:].reshape(C_HEADS, C_DV, d)
    wo_c = jnp.pad(wo_c, ((0, 0), (0, C_DV_PAD - C_DV), (0, 0))).reshape(C_HEADS * C_DV_PAD, d)
    return _out_proj(x, mixed_a, mixed_b, mixed_c, wo_a, wo_b, wo_c.astype(MXU_DTYPE))


@jax.jit
def kernel(x, norm_g, w_in, a_q_gain, a_k_gain, b_q_gain, b_k_gain, fox_bias, gla_gate_up,
           gla_gate_bias, gla_out_gain, w_out):
    bsz, s, d = x.shape
    y = x.reshape(bsz * s, d) if bsz == 1 else None
    assert y is not None, "batch size 1 only"
    for layer in range(norm_g.shape[0]):
        y = _layer(y, norm_g[layer], w_in[layer], a_q_gain[layer], a_k_gain[layer],
                   b_q_gain[layer], b_k_gain[layer], fox_bias[layer], gla_gate_up[layer],
                   gla_gate_bias[layer], gla_out_gain[layer], w_out[layer])
    return y.reshape(bsz, s, d)
```

```python
import functools

import numpy as np
import jax
import jax.numpy as jnp
from jax import lax
from jax.experimental import pallas as pl
from jax.experimental.pallas import tpu as pltpu

F32 = jnp.float32
MXU_DTYPE = jnp.bfloat16

LANES = 128
MXU_COLS = 256
HEAD_DIM = 64
A_HEADS, B_HEADS, C_HEADS = 12, 8, 4
A_CHUNKS, B_CHUNKS = A_HEADS // 2, B_HEADS // 2
C_DK, C_DV = 96, 192
C_DK_PAD, C_DV_PAD = 128, 256
GATE_RANK = 16
GATE_TEMP = 16.0
RMS_EPS = 1e-6
LOG2E = 1.4426950408889634
DILATIONS = (1, 4, 16)
BAND = 128
A_TILE = 2048
GLA_CHUNK = 64
GLA_LEVELS = (32, 16, 8, 4, 2, 1)
VMEM_LIMIT = 56 * 1024 * 1024

AQ, AK, AV, AZ = 0, 6, 12, 18
BQ, BK, BV, BZ = 24, 28, 32, 36
CQ, CK, CV, CZ = 40, 44, 48, 56
N_CHUNKS = 64
VIEW_CHUNKS = AZ
A_WIDTH, B_WIDTH = A_HEADS * HEAD_DIM, B_HEADS * HEAD_DIM

_NT = (((1,), (1,)), ((), ()))
_TN = (((0,), (0,)), ((), ()))


def _params(sem):
    return pltpu.CompilerParams(dimension_semantics=sem, vmem_limit_bytes=VMEM_LIMIT)


def _log_sigmoid(x):
    return jnp.minimum(x, 0.0) - jnp.log1p(jnp.exp(-jnp.abs(x)))


def _tile_in(j, lo_chunk, hi_chunk):
    chunks = MXU_COLS // LANES
    return (j >= lo_chunk // chunks) & (j < hi_chunk // chunks)


def _proj_body(x_ref, g_ref, w_ref, aux_ref, o_ref, gate_ref, v4_ref, v16_ref, h_scr, stage):
    j = pl.program_id(1)

    @pl.when(j == 0)
    def _():
        x = x_ref[...]
        ms = jnp.mean(x * x, axis=-1, keepdims=True)
        h_scr[...] = (x * lax.rsqrt(ms + RMS_EPS) * g_ref[0]).astype(h_scr.dtype)

    acc = jnp.dot(h_scr[...], w_ref[0], preferred_element_type=F32)
    tm = acc.shape[0]

    def epilogue(kind):
        for c in range(MXU_COLS // LANES):
            y = acc[:, c * LANES:(c + 1) * LANES]
            if kind == "headnorm":
                first = lax.broadcasted_iota(jnp.int32, y.shape, 1) < HEAD_DIM
                y2 = y * y
                s0 = jnp.sum(jnp.where(first, y2, 0.0), axis=-1, keepdims=True)
                s1 = jnp.sum(jnp.where(first, 0.0, y2), axis=-1, keepdims=True)
                ms = jnp.where(first, s0, s1) * (1.0 / HEAD_DIM)
                y = y * lax.rsqrt(ms + RMS_EPS) * aux_ref[0, :, c * LANES:(c + 1) * LANES]
            elif kind == "scale":
                y = y * aux_ref[0, :, c * LANES:(c + 1) * LANES]
            elif kind == "silu":
                y = y * jax.nn.sigmoid(y)
            o_ref[c] = y.astype(o_ref.dtype)
            if kind == "silu":
                continue

            @pl.when(_tile_in(j, 0, VIEW_CHUNKS))
            def _(y=y, c=c):
                stage[...] = y
                for view, dil in zip((v4_ref, v16_ref), DILATIONS[1:]):
                    for r in range(dil):
                        rows = stage[pl.ds(r, tm // dil, stride=dil), :]
                        view[c, :, r * LANES:(r + 1) * LANES] = rows.astype(view.dtype)

    is_norm = _tile_in(j, AQ, AV) | _tile_in(j, BQ, BV)
    is_silu = _tile_in(j, AZ, BQ) | _tile_in(j, BZ, CQ) | _tile_in(j, CZ, N_CHUNKS)
    is_gate = j == N_CHUNKS * LANES // MXU_COLS
    pl.when(is_norm)(functools.partial(epilogue, "headnorm"))
    pl.when(is_silu)(functools.partial(epilogue, "silu"))
    pl.when(jnp.logical_not(is_norm | is_silu | is_gate))(functools.partial(epilogue, "scale"))

    @pl.when(is_gate)
    def _():
        gate_ref[...] = acc[:, 0:LANES]


def _in_proj(x, norm_g, w_all, aux_all, layer, tm=1024):
    s, d = x.shape
    n_tiles = w_all.shape[2] // MXU_COLS
    chunks = MXU_COLS // LANES
    last_tile = N_CHUNKS // chunks - 1
    last_view = VIEW_CHUNKS // chunks - 1
    view_shapes, view_specs = [], []
    for dil in DILATIONS[1:]:
        view_shapes.append(jax.ShapeDtypeStruct((VIEW_CHUNKS, s // dil, dil * LANES), MXU_DTYPE))
        view_specs.append(pl.BlockSpec((chunks, tm // dil, dil * LANES),
                                       lambda i, j: (jnp.minimum(j, last_view), i, 0)))
    return pl.pallas_call(
        _proj_body,
        grid=(s // tm, n_tiles),
        in_specs=[pl.BlockSpec((tm, d), lambda i, j: (i, 0)),
                  pl.BlockSpec((1, 1, d), lambda i, j: (layer, 0, 0)),
                  pl.BlockSpec((1, d, MXU_COLS), lambda i, j: (layer, 0, j)),
                  pl.BlockSpec((1, 1, MXU_COLS), lambda i, j: (layer, 0, j))],
        out_specs=[pl.BlockSpec((chunks, tm, LANES), lambda i, j: (jnp.minimum(j, last_tile), i, 0)),
                   pl.BlockSpec((tm, LANES), lambda i, j: (i, 0))] + view_specs,
        out_shape=[jax.ShapeDtypeStruct((N_CHUNKS, s, LANES), MXU_DTYPE),
                   jax.ShapeDtypeStruct((s, LANES), F32)] + view_shapes,
        scratch_shapes=[pltpu.VMEM((tm, d), MXU_DTYPE), pltpu.VMEM((tm, LANES), F32)],
        compiler_params=_params(("parallel", "arbitrary")),
        name="in_proj",
    )(x, norm_g, w_all, aux_all)


def _fox_cumsum_body(x_ref, b_ref, o_ref):
    x = _log_sigmoid(x_ref[...] + b_ref[...])
    idx = lax.broadcasted_iota(jnp.int32, x.shape, 1)
    shift = 1
    while shift < x.shape[1]:
        x = x + jnp.where(idx >= shift, pltpu.roll(x, shift, axis=1), 0.0)
        shift *= 2
    o_ref[...] = x * LOG2E


def _fox_cumsum(logit_t, bias):
    nh, s = logit_t.shape
    return pl.pallas_call(
        _fox_cumsum_body,
        out_shape=jax.ShapeDtypeStruct((nh, s), F32),
        compiler_params=pltpu.CompilerParams(vmem_limit_bytes=VMEM_LIMIT),
        name="fox_cumsum",
    )(logit_t, bias.reshape(nh, 1))


A_GROUP = 4


def _band_attention(blocks):
    row = lax.broadcasted_iota(jnp.int32, (BAND, 2 * BAND), 0)
    col = lax.broadcasted_iota(jnp.int32, (BAND, 2 * BAND), 1)
    band = (col >= row) & (col <= row + BAND)
    first = lax.broadcasted_iota(jnp.int32, (BAND, LANES), 1) < HEAD_DIM
    first_kv = lax.broadcasted_iota(jnp.int32, (2 * BAND, LANES), 1) < HEAD_DIM
    results = []
    for q, k2, v2, first_key_idx in blocks:
        mask = band & (col + first_key_idx >= 0)
        zero, one = jnp.zeros_like(q), jnp.ones_like(v2)
        res, maxes = [], []
        for h in range(2):
            qm = jnp.where(first, q, zero) if h == 0 else jnp.where(first, zero, q)
            vh = jnp.where(first_kv, v2, one) if h == 0 else jnp.where(first_kv, one, v2)
            s = lax.dot_general(qm, k2, _NT, preferred_element_type=F32)
            s = jnp.where(mask, s, -jnp.inf)
            m = jnp.max(s, axis=-1, keepdims=True)
            p = jnp.exp2(s - m).astype(MXU_DTYPE)
            res.append(jnp.dot(p, vh, preferred_element_type=F32))
            maxes.append(m)
        num = jnp.where(first, res[0], res[1])
        den = pltpu.roll(jnp.where(first, res[1], res[0]), HEAD_DIM, axis=1)
        results.append((num / den, jnp.where(first, maxes[0], maxes[1]) + jnp.log2(den)))
    return results


def _dilated_body(*refs):
    ins, z_ref, o_ref, scr = refs[:15], refs[15], refs[16], refs[17:]
    kv_scr, o_scr, l_scr = scr[:6], scr[6], scr[7]
    n = pl.program_id(1)
    for bi, d in enumerate(DILATIONS):
        q_ref, kc, kp, vc, vp = ins[5 * bi:5 * bi + 5]
        kf, vf = kv_scr[2 * bi:2 * bi + 2]
        rows = A_TILE // d
        nb = rows // BAND
        for cur, prev, full in ((kc, kp, kf), (vc, vp, vf)):
            full[0:BAND, :] = prev[0, rows - BAND:rows, :]
            full[BAND:BAND + rows, :] = cur[0]

        def group(members, q_ref=q_ref, kf=kf, vf=vf, d=d, nb=nb, bi=bi):
            blocks = []
            for j, r in members:
                cols = slice(r * LANES, (r + 1) * LANES)
                row0 = j * BAND if isinstance(j, int) else pl.multiple_of(j * BAND, BAND)
                blocks.append((q_ref[0, pl.ds(row0, BAND), cols], kf[pl.ds(row0, 2 * BAND), cols],
                               vf[pl.ds(row0, 2 * BAND), cols], (n * nb + j - 1) * BAND))
            for (j, r), (o, lse) in zip(members, _band_attention(blocks)):
                dst = pl.ds(j * BAND * d + r, BAND, stride=d)
                o_scr[bi, dst, :] = o
                l_scr[bi, dst, :] = lse

        if nb >= A_GROUP:
            for r in range(d):
                def body(g, carry, r=r, group=group):
                    j0 = pl.multiple_of(g * A_GROUP, A_GROUP)
                    group([(j0 + i, r) for i in range(A_GROUP)])
                    return carry
                lax.fori_loop(0, nb // A_GROUP, body, 0)
        else:
            for r0 in range(0, d, A_GROUP // nb):
                group([(j, r0 + i) for i in range(A_GROUP // nb) for j in range(nb)])
    lse = l_scr[...]
    w = jnp.exp2(lse - jnp.max(lse, axis=0, keepdims=True))
    mixed = jnp.sum(w * o_scr[...], axis=0) / jnp.sum(w, axis=0)
    o_ref[...] = (mixed * z_ref[0].astype(F32)).astype(o_ref.dtype)


def _dilated_mixture(proj, views):
    s = proj.shape[1]
    operands, in_specs, scratch = [], [], []
    for d, view in zip(DILATIONS, (proj,) + tuple(views)):
        rows, width = A_TILE // d, d * LANES
        blk = (1, rows, width)
        operands += [view] * 5
        in_specs += [
            pl.BlockSpec(blk, lambda c, n: (AQ + c, n, 0)),
            pl.BlockSpec(blk, lambda c, n: (AK + c, n, 0)),
            pl.BlockSpec(blk, lambda c, n: (AK + c, jnp.maximum(n - 1, 0), 0)),
            pl.BlockSpec(blk, lambda c, n: (AV + c, n, 0)),
            pl.BlockSpec(blk, lambda c, n: (AV + c, jnp.maximum(n - 1, 0), 0)),
        ]
        scratch += [pltpu.VMEM((rows + BAND, width), MXU_DTYPE)] * 2
    scratch += [pltpu.VMEM((len(DILATIONS), A_TILE, LANES), F32)] * 2
    operands.append(proj)
    in_specs.append(pl.BlockSpec((1, A_TILE, LANES), lambda c, n: (AZ + c, n, 0)))
    return pl.pallas_call(
        _dilated_body,
        grid=(A_CHUNKS, s // A_TILE),
        in_specs=in_specs,
        out_specs=pl.BlockSpec((A_TILE, LANES), lambda c, n: (n, c)),
        out_shape=jax.ShapeDtypeStruct((s, A_WIDTH), MXU_DTYPE),
        scratch_shapes=scratch,
        compiler_params=_params(("parallel", "arbitrary")),
        name="dilated_mixture",
    )(*operands)


FOX_AUG = 3
FOX_VROWS = HEAD_DIM + 16


def _split3(c):
    hi = c.astype(jnp.bfloat16).astype(F32)
    r = c - hi
    mid = r.astype(jnp.bfloat16).astype(F32)
    return hi, mid, r - mid


def _fox_prep_body(q_ref, k_ref, v_ref, ccol_ref, crow_ref, qt_ref, ka_ref, vt_ref):
    tm = q_ref.shape[1]
    q_t = q_ref[0].astype(F32).T
    v_t = v_ref[0].astype(F32).T
    k = k_ref[0]
    sub = lax.broadcasted_iota(jnp.int32, (HEAD_DIM, tm), 0)
    lane = lax.broadcasted_iota(jnp.int32, (tm, LANES), 1)
    v_tail = (lax.broadcasted_iota(jnp.int32, (FOX_VROWS - HEAD_DIM, tm), 0) == 0).astype(F32)
    for h in range(2):
        hi, mid, lo = _split3(crow_ref[0, h:h + 1, :])
        aug_q = jnp.where(sub == 0, hi, jnp.where(sub == 1, mid, jnp.where(
            sub == 2, lo, jnp.where(sub < 2 * FOX_AUG, 1.0, 0.0))))
        q_h = q_t[h * HEAD_DIM:(h + 1) * HEAD_DIM]
        parts = [q_h, aug_q] if h == 0 else [aug_q, q_h]
        qt_ref[h] = jnp.concatenate(parts, axis=0).astype(qt_ref.dtype)
        hi, mid, lo = _split3(ccol_ref[0, :, h:h + 1])
        a0 = (1 - h) * HEAD_DIM
        aug_k = jnp.where(lane == a0 + FOX_AUG, -hi, jnp.where(lane == a0 + FOX_AUG + 1, -mid, jnp.where(
            lane == a0 + FOX_AUG + 2, -lo, jnp.where((lane >= a0) & (lane < a0 + FOX_AUG), 1.0, 0.0))))
        own = (lane < HEAD_DIM) if h == 0 else (lane >= HEAD_DIM)
        ka_ref[h] = jnp.where(own, k, aug_k.astype(k.dtype))
        vt_ref[h, 0] = jnp.concatenate([v_t[h * HEAD_DIM:(h + 1) * HEAD_DIM], v_tail],
                                       axis=0).astype(vt_ref.dtype)


def _fox_prep(proj, c, tm):
    s = proj.shape[1]
    c_col = c.reshape(B_CHUNKS, 2, s).transpose(0, 2, 1)
    c_row = c.reshape(B_CHUNKS, 2, s)
    return pl.pallas_call(
        _fox_prep_body,
        grid=(B_CHUNKS, s // tm),
        in_specs=[
            pl.BlockSpec((1, tm, LANES), lambda p, i: (BQ + p, i, 0)),
            pl.BlockSpec((1, tm, LANES), lambda p, i: (BK + p, i, 0)),
            pl.BlockSpec((1, tm, LANES), lambda p, i: (BV + p, i, 0)),
            pl.BlockSpec((1, tm, 2), lambda p, i: (p, i, 0)),
            pl.BlockSpec((1, 2, tm), lambda p, i: (p, 0, i)),
        ],
        out_specs=[
            pl.BlockSpec((2, LANES, tm), lambda p, i: (p, 0, i)),
            pl.BlockSpec((2, tm, LANES), lambda p, i: (p, i, 0)),
            pl.BlockSpec((2, 1, FOX_VROWS, tm), lambda p, i: (p, i, 0, 0)),
        ],
        out_shape=[
            jax.ShapeDtypeStruct((B_HEADS, LANES, s), MXU_DTYPE),
            jax.ShapeDtypeStruct((B_HEADS, s, LANES), MXU_DTYPE),
            jax.ShapeDtypeStruct((B_HEADS, s // tm, FOX_VROWS, tm), MXU_DTYPE),
        ],
        compiler_params=_params(("parallel", "parallel")),
        name="fox_prep",
    )(proj, proj, proj, c_col, c_row)


def _fox_body(qt_ref, ka_ref, vt_ref, z_ref, o_ref, s0_scr, s1_scr, m_scr, acc_scr, *, blk):
    qi = pl.program_id(1)
    causal = (lax.broadcasted_iota(jnp.int32, (blk, blk), 0)
              <= lax.broadcasted_iota(jnp.int32, (blk, blk), 1))
    q_t = (qt_ref[0], qt_ref[1])

    def logits(kb, dst):
        k0 = pl.multiple_of(kb * blk, blk)
        for h in range(2):
            dst[h] = jnp.dot(ka_ref[h, pl.ds(k0, blk), :], q_t[h], preferred_element_type=F32)

    def accumulate(kb, src, diagonal=False):
        for h in range(2):
            s = src[h]
            if diagonal:
                s = jnp.where(causal, s, -jnp.inf)
            m = m_scr[h]
            m_new = jnp.maximum(m, jnp.max(s, axis=0, keepdims=True))
            p = jnp.exp2(s - m_new).astype(MXU_DTYPE)
            acc_scr[h] = (jnp.exp2(m - m_new) * acc_scr[h]
                          + jnp.dot(vt_ref[h, kb], p, preferred_element_type=F32))
            m_scr[h] = m_new

    m_scr[...] = jnp.full(m_scr.shape, -jnp.inf, F32)
    acc_scr[...] = jnp.zeros(acc_scr.shape, F32)
    logits(0, s0_scr)

    def pair(j, carry):
        logits(2 * j + 1, s1_scr)
        accumulate(2 * j, s0_scr)
        logits(2 * j + 2, s0_scr)
        accumulate(2 * j + 1, s1_scr)
        return carry

    lax.fori_loop(0, qi // 2, pair, 0)

    @pl.when(qi % 2 == 0)
    def _():
        accumulate(qi, s0_scr, diagonal=True)

    @pl.when(qi % 2 == 1)
    def _():
        logits(qi, s1_scr)
        accumulate(qi - 1, s0_scr)
        accumulate(qi, s1_scr, diagonal=True)

    outs = [acc_scr[h, 0:HEAD_DIM] / acc_scr[h, HEAD_DIM:HEAD_DIM + 1] for h in range(2)]
    o = jnp.concatenate(outs, axis=0).T
    o_ref[...] = (o * z_ref[0].astype(F32)).astype(o_ref.dtype)


def _forgetting_attention(proj, c, blk=512):
    s = proj.shape[1]
    qt, ka, vt = _fox_prep(proj, c, blk)
    return pl.pallas_call(
        functools.partial(_fox_body, blk=blk),
        grid=(B_CHUNKS, s // blk),
        in_specs=[
            pl.BlockSpec((2, LANES, blk), lambda p, i: (p, 0, i)),
            pl.BlockSpec((2, s, LANES), lambda p, i: (p, 0, 0)),
            pl.BlockSpec((2, s // blk, FOX_VROWS, blk), lambda p, i: (p, 0, 0, 0)),
            pl.BlockSpec((1, blk, LANES), lambda p, i: (BZ + p, i, 0)),
        ],
        out_specs=pl.BlockSpec((blk, LANES), lambda p, i: (i, p)),
        out_shape=jax.ShapeDtypeStruct((s, B_WIDTH), MXU_DTYPE),
        scratch_shapes=[pltpu.VMEM((2, blk, blk), F32), pltpu.VMEM((2, blk, blk), F32),
                        pltpu.VMEM((2, 1, blk), F32), pltpu.VMEM((2, FOX_VROWS, blk), F32)],
        compiler_params=_params(("parallel", "arbitrary")),
        name="forgetting_attention",
    )(qt, ka, vt, proj)


GLA_GROUP = 4


def _gla_level_matrix():
    c = GLA_CHUNK
    mats = []
    for b in GLA_LEVELS:
        m = np.zeros((c, c), np.float32)
        for i in range(c):
            pivot = (i // (2 * b)) * 2 * b + b - 1
            if i > pivot:
                m[i, pivot + 1:i + 1] = 1.0
            else:
                m[i, i + 1:pivot + 1] = 1.0
        mats.append(m)
    mats.append(np.tril(np.ones((c, c), np.float32)))
    return np.concatenate(mats, axis=0)


def _gla_body(q_ref, k_ref, v_ref, g_ref, z_ref, gu_ref, gb_ref, og_ref, lvl_ref, o_ref,
              st_scr, *, tile):
    c = GLA_CHUNK
    nlev = len(GLA_LEVELS)
    nch = tile // c
    grp = GLA_GROUP * c

    @pl.when(pl.program_id(1) == 0)
    def _():
        st_scr[...] = jnp.zeros_like(st_scr)

    gate_in = g_ref[:, 0:GATE_RANK]
    gate_hi = gate_in.astype(MXU_DTYPE)
    gate_lo = (gate_in - gate_hi.astype(F32)).astype(MXU_DTYPE)
    gate_w = gu_ref[0].astype(MXU_DTYPE)
    logit = (jnp.dot(gate_hi, gate_w, preferred_element_type=F32)
             + jnp.dot(gate_lo, gate_w, preferred_element_type=F32) + gb_ref[0])
    la = _log_sigmoid(logit) * (1.0 / GATE_TEMP)

    la_cat = jnp.concatenate([la[ci * c:(ci + 1) * c] for ci in range(nch)], axis=1)
    la_hi = la_cat.astype(MXU_DTYPE)
    la_lo = (la_cat - la_hi.astype(F32)).astype(MXU_DTYPE)
    f = jnp.exp(jnp.dot(lvl_ref[0:nlev * c, :], la_hi, preferred_element_type=F32))
    tri = lvl_ref[nlev * c:(nlev + 1) * c, :]
    bc = (jnp.dot(tri, la_hi, preferred_element_type=F32)
          + jnp.dot(tri, la_lo, preferred_element_type=F32))

    row = lax.broadcasted_iota(jnp.int32, (grp, grp), 0)
    col = lax.broadcasted_iota(jnp.int32, (grp, grp), 1)
    diff = (row ^ col) & (c - 1)
    level = jnp.full((grp, grp), nlev, jnp.int32)
    for li, b in reversed(list(enumerate(GLA_LEVELS))):
        level = jnp.where(diff >= b, li, level)
    level = jnp.where(((row & -c) == (col & -c)) & (col <= row), level, -1)
    out_gain = og_ref[...]

    state = st_scr[...]
    for g0 in range(0, nch, GLA_GROUP):
        rows = slice(g0 * c, g0 * c + grp)
        q = q_ref[0, rows, :].astype(F32)
        k = k_ref[0, rows, :].astype(F32)
        v = jnp.concatenate([v_ref[0, rows, :], v_ref[1, rows, :]], axis=-1)
        z = jnp.concatenate([z_ref[0, rows, :], z_ref[1, rows, :]], axis=-1)
        a = lax.dot_general(q.astype(MXU_DTYPE), k.astype(MXU_DTYPE), _NT,
                            preferred_element_type=F32)
        attn = jnp.where(level == nlev, a, 0.0)
        for li in range(nlev):
            fl = jnp.concatenate([f[li * c:(li + 1) * c, ch * LANES:(ch + 1) * LANES]
                                  for ch in range(g0, g0 + GLA_GROUP)], axis=0)
            a = lax.dot_general((q * fl).astype(MXU_DTYPE), (k * fl).astype(MXU_DTYPE), _NT,
                                preferred_element_type=F32)
            attn = jnp.where(level == li, a, attn)
        o_intra = jnp.dot(attn.astype(MXU_DTYPE), v, preferred_element_type=F32)

        for ci in range(GLA_GROUP):
            ch = g0 + ci
            r = slice(ci * c, (ci + 1) * c)
            bc_c = bc[:, ch * LANES:(ch + 1) * LANES]
            b_last = bc_c[c - 1:c, :]
            q_dec = (q[r] * jnp.exp(bc_c)).astype(MXU_DTYPE)
            o = o_intra[r] + jnp.dot(q_dec, state.astype(MXU_DTYPE), preferred_element_type=F32)
            k_dec = (k[r] * jnp.exp(b_last - bc_c)).astype(MXU_DTYPE)
            decay = jnp.broadcast_to(jnp.exp(b_last), (LANES, LANES)).T
            state = (state * jnp.concatenate([decay, decay], axis=1)
                     + lax.dot_general(k_dec, v[r], _TN, preferred_element_type=F32))
            ms = jnp.sum(o * o, axis=-1, keepdims=True) * (1.0 / C_DV)
            y = o * lax.rsqrt(ms + RMS_EPS) * out_gain * z[r].astype(F32)
            o_ref[pl.ds(ch * c, c), :] = y.astype(o_ref.dtype)
    st_scr[...] = state


def _gated_linear_attention(proj, gate, gate_up, gate_bias, out_gain, tile=512):
    s = proj.shape[1]
    lvl = jnp.asarray(_gla_level_matrix(), MXU_DTYPE)
    return pl.pallas_call(
        functools.partial(_gla_body, tile=tile),
        grid=(C_HEADS, s // tile),
        in_specs=[
            pl.BlockSpec((1, tile, LANES), lambda h, t: (CQ + h, t, 0)),
            pl.BlockSpec((1, tile, LANES), lambda h, t: (CK + h, t, 0)),
            pl.BlockSpec((2, tile, LANES), lambda h, t: (CV // 2 + h, t, 0)),
            pl.BlockSpec((tile, LANES), lambda h, t: (t, 0)),
            pl.BlockSpec((2, tile, LANES), lambda h, t: (CZ // 2 + h, t, 0)),
            pl.BlockSpec((1, GATE_RANK, C_DK_PAD), lambda h, t: (h, 0, 0)),
            pl.BlockSpec((1, 1, C_DK_PAD), lambda h, t: (h, 0, 0)),
            pl.BlockSpec((1, C_DV_PAD), lambda h, t: (0, 0)),
            pl.BlockSpec(lvl.shape, lambda h, t: (0, 0)),
        ],
        out_specs=pl.BlockSpec((tile, C_DV_PAD), lambda h, t: (t, h)),
        out_shape=jax.ShapeDtypeStruct((s, C_HEADS * C_DV_PAD), MXU_DTYPE),
        scratch_shapes=[pltpu.VMEM((C_DK_PAD, C_DV_PAD), F32)],
        compiler_params=_params(("parallel", "arbitrary")),
        name="gated_linear_attention",
    )(proj, proj, proj, gate, proj, gate_up, gate_bias, out_gain, lvl)


def _out_body(x_ref, a_ref, b_ref, c_ref, wa_ref, wb_ref, wc_ref, o_ref):
    acc = jnp.dot(a_ref[...], wa_ref[0], preferred_element_type=F32)
    acc = acc + jnp.dot(b_ref[...], wb_ref[0], preferred_element_type=F32)
    acc = acc + jnp.dot(c_ref[...], wc_ref[0], preferred_element_type=F32)
    o_ref[...] = x_ref[...] + acc


def _out_proj(x, ma, mb, mc, wo_all, layer, tm=1024, tn=512):
    s, d = x.shape
    row_blk = lambda m: pl.BlockSpec((tm, m.shape[1]), lambda i, j: (i, 0))
    c_rows, b_rows, a_rows = mc.shape[1], mb.shape[1], ma.shape[1]
    assert c_rows % b_rows == 0 and (c_rows + b_rows) % a_rows == 0
    return pl.pallas_call(
        _out_body,
        grid=(s // tm, d // tn),
        in_specs=[pl.BlockSpec((tm, tn), lambda i, j: (i, j)),
                  row_blk(ma), row_blk(mb), row_blk(mc),
                  pl.BlockSpec((1, a_rows, tn), lambda i, j: (layer, (c_rows + b_rows) // a_rows, j)),
                  pl.BlockSpec((1, b_rows, tn), lambda i, j: (layer, c_rows // b_rows, j)),
                  pl.BlockSpec((1, c_rows, tn), lambda i, j: (layer, 0, j))],
        out_specs=pl.BlockSpec((tm, tn), lambda i, j: (i, j)),
        out_shape=jax.ShapeDtypeStruct((s, d), F32),
        compiler_params=_params(("parallel", "arbitrary")),
        name="out_proj",
    )(x, ma, mb, mc, wo_all, wo_all, wo_all)


def _pad_heads(w, heads, width, padded):
    lead = w.shape[:-1]
    w = w.reshape(*lead, heads, width)
    w = jnp.pad(w, [(0, 0)] * len(lead) + [(0, 0), (0, padded - width)])
    return w.reshape(*lead, heads * padded)


def _prepare(w_in, w_out, a_q_gain, a_k_gain, b_q_gain, b_k_gain):
    layers, d, _ = w_in.shape
    ck, cv = C_HEADS * C_DK, C_HEADS * C_DV
    ab = 4 * (A_WIDTH + B_WIDTH)
    bounds = np.cumsum([ab, B_HEADS, ck, ck, cv, cv, GATE_RANK])
    w_ab, bf, cq, ckk, cvv, cz, cr = [w_in[:, :, lo:hi] for lo, hi in
                                      zip(np.concatenate([[0], bounds[:-1]]), bounds)]
    pad_k = lambda w: _pad_heads(w, C_HEADS, C_DK, C_DK_PAD)
    pad_v = lambda w: _pad_heads(w, C_HEADS, C_DV, C_DV_PAD)
    tail = jnp.zeros((layers, d, MXU_COLS - GATE_RANK - B_HEADS), F32)
    w_all = jnp.concatenate([w_ab, pad_k(cq), pad_k(ckk), pad_v(cvv), pad_v(cz), cr, bf, tail],
                            axis=-1).astype(MXU_DTYPE)

    q_scale = HEAD_DIM ** -0.5 * LOG2E
    tile_heads = lambda g, n: jnp.tile(g, (1, n))
    ones = lambda n: jnp.ones((layers, n), F32)
    aux_all = jnp.concatenate([
        tile_heads(a_q_gain * q_scale, A_HEADS), tile_heads(a_k_gain, A_HEADS), ones(2 * A_WIDTH),
        tile_heads(b_q_gain * q_scale, B_HEADS), tile_heads(b_k_gain, B_HEADS), ones(2 * B_WIDTH),
        jnp.full((layers, C_HEADS * C_DK_PAD), C_DK ** -0.5, F32),
        ones(C_HEADS * (C_DK_PAD + 2 * C_DV_PAD) + MXU_COLS)], axis=-1)[:, None, :]

    wo_c = w_out[:, A_WIDTH + B_WIDTH:].reshape(layers, C_HEADS, C_DV, d)
    wo_c = jnp.pad(wo_c, ((0, 0), (0, 0), (0, C_DV_PAD - C_DV), (0, 0)))
    wo_all = jnp.concatenate([wo_c.reshape(layers, C_HEADS * C_DV_PAD, d),
                              w_out[:, A_WIDTH:A_WIDTH + B_WIDTH], w_out[:, :A_WIDTH]],
                             axis=1).astype(MXU_DTYPE)
    return w_all, aux_all, wo_all


def _layer(x, layer, norm_g, w_all, aux_all, wo_all, fox_bias, gla_gate_up, gla_gate_bias,
           gla_out_gain):
    proj, gate, v4, v16 = _in_proj(x, norm_g, w_all, aux_all, layer)
    c = _fox_cumsum(gate[:, GATE_RANK:GATE_RANK + B_HEADS].T, fox_bias)
    mixed_a = _dilated_mixture(proj, (v4, v16))
    mixed_b = _forgetting_attention(proj, c)
    gate_up = _pad_heads(gla_gate_up, C_HEADS, C_DK, C_DK_PAD)
    gate_up = gate_up.reshape(GATE_RANK, C_HEADS, C_DK_PAD).transpose(1, 0, 2)
    gate_bias = _pad_heads(gla_gate_bias, C_HEADS, C_DK, C_DK_PAD).reshape(C_HEADS, 1, C_DK_PAD)
    out_gain = jnp.pad(gla_out_gain, (0, C_DV_PAD - C_DV)).reshape(1, C_DV_PAD)
    mixed_c = _gated_linear_attention(proj, gate, gate_up, gate_bias, out_gain)
    return _out_proj(x, mixed_a, mixed_b, mixed_c, wo_all, layer)


@jax.jit
def kernel(x, norm_g, w_in, a_q_gain, a_k_gain, b_q_gain, b_k_gain, fox_bias, gla_gate_up,
           gla_gate_bias, gla_out_gain, w_out):
    bsz, s, d = x.shape
    assert bsz == 1, "batch size 1 only"
    w_all, aux_all, wo_all = _prepare(w_in, w_out, a_q_gain, a_k_gain, b_q_gain, b_k_gain)
    norm_g = norm_g[:, None, :]
    y = x.reshape(s, d)
    for layer in range(w_in.shape[0]):
        y = _layer(y, layer, norm_g, w_all, aux_all, wo_all, fox_bias[layer], gla_gate_up[layer],
                   gla_gate_bias[layer], gla_out_gain[layer])
    return y.reshape(bsz, s, d)
```

```python
import functools

import numpy as np
import jax
import jax.numpy as jnp
from jax import lax
from jax.experimental import pallas as pl
from jax.experimental.pallas import tpu as pltpu

F32 = jnp.float32
MXU_DTYPE = jnp.bfloat16

LANES = 128
MXU_COLS = 256
HEAD_DIM = 64
A_HEADS, B_HEADS, C_HEADS = 12, 8, 4
A_CHUNKS, B_CHUNKS = A_HEADS // 2, B_HEADS // 2
C_DK, C_DV = 96, 192
C_DK_PAD, C_DV_PAD = 128, 256
GATE_RANK = 16
GATE_TEMP = 16.0
RMS_EPS = 1e-6
LOG2E = 1.4426950408889634
DILATIONS = (1, 4, 16)
BAND = 128
A_TILE = 2048
GLA_CHUNK = 64
GLA_LEVELS = (32, 16, 8, 4, 2, 1)
VMEM_LIMIT = 56 * 1024 * 1024

AQ, AK, AV, AZ = 0, 6, 12, 18
BQ, BK, BV, BZ = 24, 28, 32, 36
CQ, CK, CV, CZ = 0, 4, 8, 16
VIEW_TILES = AZ * LANES // MXU_COLS
A_WIDTH, B_WIDTH = A_HEADS * HEAD_DIM, B_HEADS * HEAD_DIM

_NT = (((1,), (1,)), ((), ()))
_TN = (((0,), (0,)), ((), ()))


def _params(sem):
    return pltpu.CompilerParams(dimension_semantics=sem, vmem_limit_bytes=VMEM_LIMIT)


def _log_sigmoid(x):
    return jnp.minimum(x, 0.0) - jnp.log1p(jnp.exp(-jnp.abs(x)))


AB_KINDS = (("headnorm",) * (2 * A_CHUNKS) + ("scale",) * A_CHUNKS + ("silu",) * A_CHUNKS
            + ("headnorm",) * (2 * B_CHUNKS) + ("scale",) * B_CHUNKS + ("silu",) * B_CHUNKS)[::2]
C_KINDS = (("scale",) * (4 * C_HEADS) + ("silu",) * (2 * C_HEADS))[::2] + ("gate",)


def _kind_ranges(kinds, kind):
    runs, start = [], None
    for t, k in enumerate(kinds + (None,)):
        if k == kind and start is None:
            start = t
        elif k != kind and start is not None:
            runs.append((start, t))
            start = None
    return runs


def _proj_body(x_ref, g_ref, w_ref, aux_ref, *rest, kinds, view_tiles):
    n_tiles = len(kinds)
    o_ref, rest = rest[0], rest[1:]
    gate_ref = None
    if kinds[-1] == "gate":
        gate_ref, rest = rest[0], rest[1:]
    if view_tiles:
        view_refs, rest = rest[:len(DILATIONS) - 1], rest[len(DILATIONS) - 1:]
        stage, rest = rest[-1], rest[:-1]
    h_scr, acc_scr = rest[0], rest[1:3]
    j = pl.program_id(1)
    tm = h_scr.shape[0]

    def matmul(dst):
        dst[...] = jnp.dot(h_scr[...], w_ref[0].astype(MXU_DTYPE), preferred_element_type=F32)

    def epilogue(src, kind):
        if kind == "gate":
            gate_ref[...] = src[:, 0:LANES]
            return
        for c in range(MXU_COLS // LANES):
            y = src[:, c * LANES:(c + 1) * LANES]
            if kind == "headnorm":
                first = lax.broadcasted_iota(jnp.int32, y.shape, 1) < HEAD_DIM
                y2 = y * y
                s0 = jnp.sum(jnp.where(first, y2, 0.0), axis=-1, keepdims=True)
                s1 = jnp.sum(jnp.where(first, 0.0, y2), axis=-1, keepdims=True)
                ms = jnp.where(first, s0, s1) * (1.0 / HEAD_DIM)
                y = y * lax.rsqrt(ms + RMS_EPS) * aux_ref[0, :, c * LANES:(c + 1) * LANES]
            elif kind == "scale":
                y = y * aux_ref[0, :, c * LANES:(c + 1) * LANES]
            elif kind == "silu":
                y = y * jax.nn.sigmoid(y)
            o_ref[c] = y.astype(o_ref.dtype)
            if not view_tiles or kind == "silu":
                continue

            @pl.when(j - 1 < view_tiles)
            def _(y=y, c=c):
                stage[...] = y
                for view, dil in zip(view_refs, DILATIONS[1:]):
                    for r in range(dil):
                        rows = stage[pl.ds(r, tm // dil, stride=dil), :]
                        view[c, :, r * LANES:(r + 1) * LANES] = rows.astype(view.dtype)

    @pl.when(j == 0)
    def _():
        x = x_ref[...]
        ms = jnp.mean(x * x, axis=-1, keepdims=True)
        h_scr[...] = (x * lax.rsqrt(ms + RMS_EPS) * g_ref[0]).astype(h_scr.dtype)
        matmul(acc_scr[0])

    for parity in range(2):
        for kind in sorted(set(kinds[:-1])):
            in_kind = functools.reduce(
                jnp.logical_or, [(j - 1 >= lo) & (j - 1 < min(hi, n_tiles - 1))
                                 for lo, hi in _kind_ranges(kinds, kind) if lo < n_tiles - 1])

            @pl.when((j >= 1) & (j < n_tiles) & (j % 2 == parity) & in_kind)
            def _(parity=parity, kind=kind):
                matmul(acc_scr[parity])
                epilogue(acc_scr[1 - parity], kind)

    @pl.when(j == n_tiles)
    def _():
        epilogue(acc_scr[(n_tiles - 1) % 2], kinds[-1])


def _in_proj(x, norm_g, w, aux, layer, kinds, name, view_tiles=0, tm=1024):
    s, d = x.shape
    n_tiles = len(kinds)
    chunks = MXU_COLS // LANES
    has_gate = kinds[-1] == "gate"
    out_tiles = n_tiles - 1 if has_gate else n_tiles
    prev = lambda j, n: jnp.clip(j - 1, 0, n - 1)
    out_shape = [jax.ShapeDtypeStruct((out_tiles * chunks, s, LANES), MXU_DTYPE)]
    out_specs = [pl.BlockSpec((chunks, tm, LANES), lambda i, j: (prev(j, out_tiles), i, 0))]
    scratch = [pltpu.VMEM((tm, d), MXU_DTYPE), pltpu.VMEM((tm, MXU_COLS), F32),
               pltpu.VMEM((tm, MXU_COLS), F32)]
    if has_gate:
        out_shape.append(jax.ShapeDtypeStruct((s, LANES), F32))
        out_specs.append(pl.BlockSpec((tm, LANES), lambda i, j: (i, 0)))
    if view_tiles:
        for dil in DILATIONS[1:]:
            out_shape.append(
                jax.ShapeDtypeStruct((view_tiles * chunks, s // dil, dil * LANES), MXU_DTYPE))
            out_specs.append(pl.BlockSpec((chunks, tm // dil, dil * LANES),
                                          lambda i, j: (prev(j, view_tiles), i, 0)))
        scratch.append(pltpu.VMEM((tm, LANES), F32))
    return pl.pallas_call(
        functools.partial(_proj_body, kinds=kinds, view_tiles=view_tiles),
        grid=(s // tm, n_tiles + 1),
        in_specs=[pl.BlockSpec((tm, d), lambda i, j: (i, 0)),
                  pl.BlockSpec((1, 1, d), lambda i, j: (layer, 0, 0)),
                  pl.BlockSpec((1, d, MXU_COLS),
                               lambda i, j: (layer, 0, jnp.minimum(j, n_tiles - 1))),
                  pl.BlockSpec((1, 1, MXU_COLS), lambda i, j: (layer, 0, prev(j, n_tiles)))],
        out_specs=out_specs,
        out_shape=out_shape,
        scratch_shapes=scratch,
        compiler_params=_params(("parallel", "arbitrary")),
        name=name,
    )(x, norm_g, w, aux)


def _fox_cumsum_body(x_ref, b_ref, o_ref):
    x = _log_sigmoid(x_ref[...] + b_ref[...])
    idx = lax.broadcasted_iota(jnp.int32, x.shape, 1)
    shift = 1
    while shift < x.shape[1]:
        x = x + jnp.where(idx >= shift, pltpu.roll(x, shift, axis=1), 0.0)
        shift *= 2
    o_ref[...] = x * LOG2E


def _fox_cumsum(logit_t, bias):
    nh, s = logit_t.shape
    return pl.pallas_call(
        _fox_cumsum_body,
        out_shape=jax.ShapeDtypeStruct((nh, s), F32),
        compiler_params=pltpu.CompilerParams(vmem_limit_bytes=VMEM_LIMIT),
        name="fox_cumsum",
    )(logit_t, bias.reshape(nh, 1))


A_GROUP = 4


def _band_attention(blocks):
    row = lax.broadcasted_iota(jnp.int32, (BAND, 2 * BAND), 0)
    col = lax.broadcasted_iota(jnp.int32, (BAND, 2 * BAND), 1)
    band = (col >= row) & (col <= row + BAND)
    first = lax.broadcasted_iota(jnp.int32, (BAND, LANES), 1) < HEAD_DIM
    first_kv = lax.broadcasted_iota(jnp.int32, (2 * BAND, LANES), 1) < HEAD_DIM
    results = []
    for q, k2, v2, first_key_idx in blocks:
        mask = band & (col + first_key_idx >= 0)
        zero, one = jnp.zeros_like(q), jnp.ones_like(v2)
        res, maxes = [], []
        for h in range(2):
            qm = jnp.where(first, q, zero) if h == 0 else jnp.where(first, zero, q)
            vh = jnp.where(first_kv, v2, one) if h == 0 else jnp.where(first_kv, one, v2)
            s = lax.dot_general(qm, k2, _NT, preferred_element_type=F32)
            s = jnp.where(mask, s, -jnp.inf)
            m = jnp.max(s, axis=-1, keepdims=True)
            p = jnp.exp2(s - m).astype(MXU_DTYPE)
            res.append(jnp.dot(p, vh, preferred_element_type=F32))
            maxes.append(m)
        num = jnp.where(first, res[0], res[1])
        den = pltpu.roll(jnp.where(first, res[1], res[0]), HEAD_DIM, axis=1)
        results.append((num / den, jnp.where(first, maxes[0], maxes[1]) + jnp.log2(den)))
    return results


def _dilated_body(*refs):
    ins, z_ref, o_ref, scr = refs[:15], refs[15], refs[16], refs[17:]
    kv_scr, o_scr, l_scr = scr[:6], scr[6], scr[7]
    n = pl.program_id(1)
    for bi, d in enumerate(DILATIONS):
        q_ref, kc, kp, vc, vp = ins[5 * bi:5 * bi + 5]
        kf, vf = kv_scr[2 * bi:2 * bi + 2]
        rows = A_TILE // d
        nb = rows // BAND
        for cur, prev, full in ((kc, kp, kf), (vc, vp, vf)):
            full[0:BAND, :] = prev[0, rows - BAND:rows, :]
            full[BAND:BAND + rows, :] = cur[0]

        def group(members, q_ref=q_ref, kf=kf, vf=vf, d=d, nb=nb, bi=bi):
            blocks = []
            for j, r in members:
                cols = slice(r * LANES, (r + 1) * LANES)
                row0 = j * BAND if isinstance(j, int) else pl.multiple_of(j * BAND, BAND)
                blocks.append((q_ref[0, pl.ds(row0, BAND), cols], kf[pl.ds(row0, 2 * BAND), cols],
                               vf[pl.ds(row0, 2 * BAND), cols], (n * nb + j - 1) * BAND))
            for (j, r), (o, lse) in zip(members, _band_attention(blocks)):
                dst = pl.ds(j * BAND * d + r, BAND, stride=d)
                o_scr[bi, dst, :] = o
                l_scr[bi, dst, :] = lse

        if nb >= A_GROUP:
            for r in range(d):
                def body(g, carry, r=r, group=group):
                    j0 = pl.multiple_of(g * A_GROUP, A_GROUP)
                    group([(j0 + i, r) for i in range(A_GROUP)])
                    return carry
                lax.fori_loop(0, nb // A_GROUP, body, 0)
        else:
            for r0 in range(0, d, A_GROUP // nb):
                group([(j, r0 + i) for i in range(A_GROUP // nb) for j in range(nb)])
    lse = l_scr[...]
    w = jnp.exp2(lse - jnp.max(lse, axis=0, keepdims=True))
    mixed = jnp.sum(w * o_scr[...], axis=0) / jnp.sum(w, axis=0)
    o_ref[...] = (mixed * z_ref[0].astype(F32)).astype(o_ref.dtype)


def _dilated_mixture(proj, views):
    s = proj.shape[1]
    operands, in_specs, scratch = [], [], []
    for d, view in zip(DILATIONS, (proj,) + tuple(views)):
        rows, width = A_TILE // d, d * LANES
        blk = (1, rows, width)
        operands += [view] * 5
        in_specs += [
            pl.BlockSpec(blk, lambda c, n: (AQ + c, n, 0)),
            pl.BlockSpec(blk, lambda c, n: (AK + c, n, 0)),
            pl.BlockSpec(blk, lambda c, n: (AK + c, jnp.maximum(n - 1, 0), 0)),
            pl.BlockSpec(blk, lambda c, n: (AV + c, n, 0)),
            pl.BlockSpec(blk, lambda c, n: (AV + c, jnp.maximum(n - 1, 0), 0)),
        ]
        scratch += [pltpu.VMEM((rows + BAND, width), MXU_DTYPE)] * 2
    scratch += [pltpu.VMEM((len(DILATIONS), A_TILE, LANES), F32)] * 2
    operands.append(proj)
    in_specs.append(pl.BlockSpec((1, A_TILE, LANES), lambda c, n: (AZ + c, n, 0)))
    return pl.pallas_call(
        _dilated_body,
        grid=(A_CHUNKS, s // A_TILE),
        in_specs=in_specs,
        out_specs=pl.BlockSpec((A_TILE, LANES), lambda c, n: (n, c)),
        out_shape=jax.ShapeDtypeStruct((s, A_WIDTH), MXU_DTYPE),
        scratch_shapes=scratch,
        compiler_params=_params(("parallel", "arbitrary")),
        name="dilated_mixture",
    )(*operands)


FOX_AUG = 3
FOX_VROWS = HEAD_DIM + 16


def _split3(c):
    hi = c.astype(jnp.bfloat16).astype(F32)
    r = c - hi
    mid = r.astype(jnp.bfloat16).astype(F32)
    return hi, mid, r - mid


def _fox_prep_body(q_ref, k_ref, v_ref, ccol_ref, crow_ref, qt_ref, ka_ref, vt_ref):
    tm = q_ref.shape[1]
    q_t = q_ref[0].astype(F32).T
    v_t = v_ref[0].astype(F32).T
    k = k_ref[0]
    sub = lax.broadcasted_iota(jnp.int32, (HEAD_DIM, tm), 0)
    lane = lax.broadcasted_iota(jnp.int32, (tm, LANES), 1)
    v_tail = (lax.broadcasted_iota(jnp.int32, (FOX_VROWS - HEAD_DIM, tm), 0) == 0).astype(F32)
    for h in range(2):
        hi, mid, lo = _split3(crow_ref[0, h:h + 1, :])
        aug_q = jnp.where(sub == 0, hi, jnp.where(sub == 1, mid, jnp.where(
            sub == 2, lo, jnp.where(sub < 2 * FOX_AUG, 1.0, 0.0))))
        q_h = q_t[h * HEAD_DIM:(h + 1) * HEAD_DIM]
        parts = [q_h, aug_q] if h == 0 else [aug_q, q_h]
        qt_ref[h] = jnp.concatenate(parts, axis=0).astype(qt_ref.dtype)
        hi, mid, lo = _split3(ccol_ref[0, :, h:h + 1])
        a0 = (1 - h) * HEAD_DIM
        aug_k = jnp.where(lane == a0 + FOX_AUG, -hi, jnp.where(lane == a0 + FOX_AUG + 1, -mid, jnp.where(
            lane == a0 + FOX_AUG + 2, -lo, jnp.where((lane >= a0) & (lane < a0 + FOX_AUG), 1.0, 0.0))))
        own = (lane < HEAD_DIM) if h == 0 else (lane >= HEAD_DIM)
        ka_ref[h] = jnp.where(own, k, aug_k.astype(k.dtype))
        vt_ref[h, 0] = jnp.concatenate([v_t[h * HEAD_DIM:(h + 1) * HEAD_DIM], v_tail],
                                       axis=0).astype(vt_ref.dtype)


def _fox_prep(proj, c, tm):
    s = proj.shape[1]
    c_col = c.reshape(B_CHUNKS, 2, s).transpose(0, 2, 1)
    c_row = c.reshape(B_CHUNKS, 2, s)
    return pl.pallas_call(
        _fox_prep_body,
        grid=(B_CHUNKS, s // tm),
        in_specs=[
            pl.BlockSpec((1, tm, LANES), lambda p, i: (BQ + p, i, 0)),
            pl.BlockSpec((1, tm, LANES), lambda p, i: (BK + p, i, 0)),
            pl.BlockSpec((1, tm, LANES), lambda p, i: (BV + p, i, 0)),
            pl.BlockSpec((1, tm, 2), lambda p, i: (p, i, 0)),
            pl.BlockSpec((1, 2, tm), lambda p, i: (p, 0, i)),
        ],
        out_specs=[
            pl.BlockSpec((2, LANES, tm), lambda p, i: (p, 0, i)),
            pl.BlockSpec((2, tm, LANES), lambda p, i: (p, i, 0)),
            pl.BlockSpec((2, 1, FOX_VROWS, tm), lambda p, i: (p, i, 0, 0)),
        ],
        out_shape=[
            jax.ShapeDtypeStruct((B_HEADS, LANES, s), MXU_DTYPE),
            jax.ShapeDtypeStruct((B_HEADS, s, LANES), MXU_DTYPE),
            jax.ShapeDtypeStruct((B_HEADS, s // tm, FOX_VROWS, tm), MXU_DTYPE),
        ],
        compiler_params=_params(("parallel", "parallel")),
        name="fox_prep",
    )(proj, proj, proj, c_col, c_row)


def _fox_body(qt_ref, ka_ref, vt_ref, z_ref, o_ref, s0_scr, s1_scr, m_scr, acc_scr, *, blk):
    qi = pl.program_id(1)
    causal = (lax.broadcasted_iota(jnp.int32, (blk, blk), 0)
              <= lax.broadcasted_iota(jnp.int32, (blk, blk), 1))
    q_t = (qt_ref[0], qt_ref[1])

    def logits(kb, dst):
        k0 = pl.multiple_of(kb * blk, blk)
        for h in range(2):
            dst[h] = jnp.dot(ka_ref[h, pl.ds(k0, blk), :], q_t[h], preferred_element_type=F32)

    def accumulate(kb, src, diagonal=False):
        for h in range(2):
            s = src[h]
            if diagonal:
                s = jnp.where(causal, s, -jnp.inf)
            m = m_scr[h]
            m_new = jnp.maximum(m, jnp.max(s, axis=0, keepdims=True))
            p = jnp.exp2(s - m_new).astype(MXU_DTYPE)
            acc_scr[h] = (jnp.exp2(m - m_new) * acc_scr[h]
                          + jnp.dot(vt_ref[h, kb], p, preferred_element_type=F32))
            m_scr[h] = m_new

    m_scr[...] = jnp.full(m_scr.shape, -jnp.inf, F32)
    acc_scr[...] = jnp.zeros(acc_scr.shape, F32)
    logits(0, s0_scr)

    def pair(j, carry):
        logits(2 * j + 1, s1_scr)
        accumulate(2 * j, s0_scr)
        logits(2 * j + 2, s0_scr)
        accumulate(2 * j + 1, s1_scr)
        return carry

    lax.fori_loop(0, qi // 2, pair, 0)

    @pl.when(qi % 2 == 0)
    def _():
        accumulate(qi, s0_scr, diagonal=True)

    @pl.when(qi % 2 == 1)
    def _():
        logits(qi, s1_scr)
        accumulate(qi - 1, s0_scr)
        accumulate(qi, s1_scr, diagonal=True)

    outs = [acc_scr[h, 0:HEAD_DIM] / acc_scr[h, HEAD_DIM:HEAD_DIM + 1] for h in range(2)]
    o = jnp.concatenate(outs, axis=0).T
    o_ref[...] = (o * z_ref[0].astype(F32)).astype(o_ref.dtype)


def _forgetting_attention(proj, c, blk=512):
    s = proj.shape[1]
    qt, ka, vt = _fox_prep(proj, c, blk)
    return pl.pallas_call(
        functools.partial(_fox_body, blk=blk),
        grid=(B_CHUNKS, s // blk),
        in_specs=[
            pl.BlockSpec((2, LANES, blk), lambda p, i: (p, 0, i)),
            pl.BlockSpec((2, s, LANES), lambda p, i: (p, 0, 0)),
            pl.BlockSpec((2, s // blk, FOX_VROWS, blk), lambda p, i: (p, 0, 0, 0)),
            pl.BlockSpec((1, blk, LANES), lambda p, i: (BZ + p, i, 0)),
        ],
        out_specs=pl.BlockSpec((blk, LANES), lambda p, i: (i, p)),
        out_shape=jax.ShapeDtypeStruct((s, B_WIDTH), MXU_DTYPE),
        scratch_shapes=[pltpu.VMEM((2, blk, blk), F32), pltpu.VMEM((2, blk, blk), F32),
                        pltpu.VMEM((2, 1, blk), F32), pltpu.VMEM((2, FOX_VROWS, blk), F32)],
        compiler_params=_params(("parallel", "arbitrary")),
        name="forgetting_attention",
    )(qt, ka, vt, proj)


GLA_GROUP = 4


def _gla_level_matrix():
    c = GLA_CHUNK
    mats = []
    for b in GLA_LEVELS:
        m = np.zeros((c, c), np.float32)
        for i in range(c):
            pivot = (i // (2 * b)) * 2 * b + b - 1
            if i > pivot:
                m[i, pivot + 1:i + 1] = 1.0
            else:
                m[i, i + 1:pivot + 1] = 1.0
        mats.append(m)
    mats.append(np.tril(np.ones((c, c), np.float32)))
    return np.concatenate(mats, axis=0)


def _gla_body(q_ref, k_ref, v_ref, g_ref, z_ref, gu_ref, gb_ref, og_ref, lvl_ref, o_ref,
              st_scr, *, tile):
    c = GLA_CHUNK
    nlev = len(GLA_LEVELS)
    nch = tile // c
    grp = GLA_GROUP * c

    @pl.when(pl.program_id(1) == 0)
    def _():
        st_scr[...] = jnp.zeros_like(st_scr)

    gate_in = g_ref[:, 0:GATE_RANK]
    gate_hi = gate_in.astype(MXU_DTYPE)
    gate_lo = (gate_in - gate_hi.astype(F32)).astype(MXU_DTYPE)
    gate_w = gu_ref[0].astype(MXU_DTYPE)
    logit = (jnp.dot(gate_hi, gate_w, preferred_element_type=F32)
             + jnp.dot(gate_lo, gate_w, preferred_element_type=F32) + gb_ref[0])
    la = _log_sigmoid(logit) * (1.0 / GATE_TEMP)

    la_cat = jnp.concatenate([la[ci * c:(ci + 1) * c] for ci in range(nch)], axis=1)
    la_hi = la_cat.astype(MXU_DTYPE)
    la_lo = (la_cat - la_hi.astype(F32)).astype(MXU_DTYPE)
    f = jnp.exp(jnp.dot(lvl_ref[0:nlev * c, :], la_hi, preferred_element_type=F32))
    tri = lvl_ref[nlev * c:(nlev + 1) * c, :]
    bc = (jnp.dot(tri, la_hi, preferred_element_type=F32)
          + jnp.dot(tri, la_lo, preferred_element_type=F32))

    row = lax.broadcasted_iota(jnp.int32, (grp, grp), 0)
    col = lax.broadcasted_iota(jnp.int32, (grp, grp), 1)
    diff = (row ^ col) & (c - 1)
    level = jnp.full((grp, grp), nlev, jnp.int32)
    for li, b in reversed(list(enumerate(GLA_LEVELS))):
        level = jnp.where(diff >= b, li, level)
    level = jnp.where(((row & -c) == (col & -c)) & (col <= row), level, -1)
    out_gain = og_ref[...]

    state = st_scr[...]
    for g0 in range(0, nch, GLA_GROUP):
        rows = slice(g0 * c, g0 * c + grp)
        q = q_ref[0, rows, :].astype(F32)
        k = k_ref[0, rows, :].astype(F32)
        v = jnp.concatenate([v_ref[0, rows, :], v_ref[1, rows, :]], axis=-1)
        z = jnp.concatenate([z_ref[0, rows, :], z_ref[1, rows, :]], axis=-1)
        a = lax.dot_general(q.astype(MXU_DTYPE), k.astype(MXU_DTYPE), _NT,
                            preferred_element_type=F32)
        attn = jnp.where(level == nlev, a, 0.0)
        for li in range(nlev):
            fl = jnp.concatenate([f[li * c:(li + 1) * c, ch * LANES:(ch + 1) * LANES]
                                  for ch in range(g0, g0 + GLA_GROUP)], axis=0)
            a = lax.dot_general((q * fl).astype(MXU_DTYPE), (k * fl).astype(MXU_DTYPE), _NT,
                                preferred_element_type=F32)
            attn = jnp.where(level == li, a, attn)
        o_intra = jnp.dot(attn.astype(MXU_DTYPE), v, preferred_element_type=F32)

        for ci in range(GLA_GROUP):
            ch = g0 + ci
            r = slice(ci * c, (ci + 1) * c)
            bc_c = bc[:, ch * LANES:(ch + 1) * LANES]
            b_last = bc_c[c - 1:c, :]
            q_dec = (q[r] * jnp.exp(bc_c)).astype(MXU_DTYPE)
            o = o_intra[r] + jnp.dot(q_dec, state.astype(MXU_DTYPE), preferred_element_type=F32)
            k_dec = (k[r] * jnp.exp(b_last - bc_c)).astype(MXU_DTYPE)
            decay = jnp.broadcast_to(jnp.exp(b_last), (LANES, LANES)).T
            state = (state * jnp.concatenate([decay, decay], axis=1)
                     + lax.dot_general(k_dec, v[r], _TN, preferred_element_type=F32))
            ms = jnp.sum(o * o, axis=-1, keepdims=True) * (1.0 / C_DV)
            y = o * lax.rsqrt(ms + RMS_EPS) * out_gain * z[r].astype(F32)
            o_ref[pl.ds(ch * c, c), :] = y.astype(o_ref.dtype)
    st_scr[...] = state


def _gated_linear_attention(proj, gate, gate_up, gate_bias, out_gain, tile=512):
    s = proj.shape[1]
    lvl = jnp.asarray(_gla_level_matrix(), MXU_DTYPE)
    return pl.pallas_call(
        functools.partial(_gla_body, tile=tile),
        grid=(C_HEADS, s // tile),
        in_specs=[
            pl.BlockSpec((1, tile, LANES), lambda h, t: (CQ + h, t, 0)),
            pl.BlockSpec((1, tile, LANES), lambda h, t: (CK + h, t, 0)),
            pl.BlockSpec((2, tile, LANES), lambda h, t: (CV // 2 + h, t, 0)),
            pl.BlockSpec((tile, LANES), lambda h, t: (t, 0)),
            pl.BlockSpec((2, tile, LANES), lambda h, t: (CZ // 2 + h, t, 0)),
            pl.BlockSpec((1, GATE_RANK, C_DK_PAD), lambda h, t: (h, 0, 0)),
            pl.BlockSpec((1, 1, C_DK_PAD), lambda h, t: (h, 0, 0)),
            pl.BlockSpec((1, C_DV_PAD), lambda h, t: (0, 0)),
            pl.BlockSpec(lvl.shape, lambda h, t: (0, 0)),
        ],
        out_specs=pl.BlockSpec((tile, C_DV_PAD), lambda h, t: (t, h)),
        out_shape=jax.ShapeDtypeStruct((s, C_HEADS * C_DV_PAD), MXU_DTYPE),
        scratch_shapes=[pltpu.VMEM((C_DK_PAD, C_DV_PAD), F32)],
        compiler_params=_params(("parallel", "arbitrary")),
        name="gated_linear_attention",
    )(proj, proj, proj, gate, proj, gate_up, gate_bias, out_gain, lvl)


def _out_body(x_ref, a_ref, b_ref, c_ref, wab_ref, wc_ref, o_ref):
    wa = wab_ref[0, 0:A_WIDTH, :].astype(MXU_DTYPE)
    wb = wab_ref[0, A_WIDTH:A_WIDTH + B_WIDTH, :].astype(MXU_DTYPE)
    acc = jnp.dot(a_ref[...], wa, preferred_element_type=F32)
    acc = acc + jnp.dot(b_ref[...], wb, preferred_element_type=F32)
    acc = acc + jnp.dot(c_ref[...], wc_ref[0], preferred_element_type=F32)
    o_ref[...] = x_ref[...] + acc


def _out_proj(x, ma, mb, mc, w_out, wo_c, layer, tm=1024, tn=512):
    s, d = x.shape
    row_blk = lambda m: pl.BlockSpec((tm, m.shape[1]), lambda i, j: (i, 0))
    return pl.pallas_call(
        _out_body,
        grid=(s // tm, d // tn),
        in_specs=[pl.BlockSpec((tm, tn), lambda i, j: (i, j)),
                  row_blk(ma), row_blk(mb), row_blk(mc),
                  pl.BlockSpec((1, A_WIDTH + B_WIDTH, tn), lambda i, j: (layer, 0, j)),
                  pl.BlockSpec((1, mc.shape[1], tn), lambda i, j: (layer, 0, j))],
        out_specs=pl.BlockSpec((tm, tn), lambda i, j: (i, j)),
        out_shape=jax.ShapeDtypeStruct((s, d), F32),
        compiler_params=_params(("parallel", "arbitrary")),
        name="out_proj",
    )(x, ma, mb, mc, w_out, wo_c)


def _pad_heads(w, heads, width, padded):
    lead = w.shape[:-1]
    w = w.reshape(*lead, heads, width)
    w = jnp.pad(w, [(0, 0)] * len(lead) + [(0, 0), (0, padded - width)])
    return w.reshape(*lead, heads * padded)


def _prepare(w_in, w_out, a_q_gain, a_k_gain, b_q_gain, b_k_gain):
    layers, d, _ = w_in.shape
    ck, cv = C_HEADS * C_DK, C_HEADS * C_DV
    ab = 4 * (A_WIDTH + B_WIDTH)
    bounds = np.cumsum([ab, B_HEADS, ck, ck, cv, cv, GATE_RANK])
    bf, cq, ckk, cvv, cz, cr = [w_in[:, :, lo:hi] for lo, hi in zip(bounds[:-1], bounds[1:])]
    pad_k = lambda w: _pad_heads(w, C_HEADS, C_DK, C_DK_PAD)
    pad_v = lambda w: _pad_heads(w, C_HEADS, C_DV, C_DV_PAD)
    tail = jnp.zeros((layers, d, MXU_COLS - GATE_RANK - B_HEADS), F32)
    w_c = jnp.concatenate([pad_k(cq), pad_k(ckk), pad_v(cvv), pad_v(cz), cr, bf, tail],
                          axis=-1).astype(MXU_DTYPE)

    q_scale = HEAD_DIM ** -0.5 * LOG2E
    tile_heads = lambda g, n: jnp.tile(g, (1, n))
    ones = lambda n: jnp.ones((layers, n), F32)
    aux_ab = jnp.concatenate([
        tile_heads(a_q_gain * q_scale, A_HEADS), tile_heads(a_k_gain, A_HEADS), ones(2 * A_WIDTH),
        tile_heads(b_q_gain * q_scale, B_HEADS), tile_heads(b_k_gain, B_HEADS), ones(2 * B_WIDTH)],
        axis=-1)[:, None, :]
    aux_c = jnp.concatenate([
        jnp.full((layers, C_HEADS * C_DK_PAD), C_DK ** -0.5, F32),
        ones(C_HEADS * (C_DK_PAD + 2 * C_DV_PAD) + MXU_COLS)], axis=-1)[:, None, :]

    wo_c = w_out[:, A_WIDTH + B_WIDTH:].reshape(layers, C_HEADS, C_DV, d)
    wo_c = jnp.pad(wo_c, ((0, 0), (0, 0), (0, C_DV_PAD - C_DV), (0, 0)))
    wo_c = wo_c.reshape(layers, C_HEADS * C_DV_PAD, d).astype(MXU_DTYPE)
    return w_c, aux_ab, aux_c, wo_c


def _layer(x, layer, norm_g, w_in, w_c, aux_ab, aux_c, w_out, wo_c, fox_bias, gla_gate_up,
           gla_gate_bias, gla_out_gain):
    proj_ab, v4, v16 = _in_proj(x, norm_g, w_in, aux_ab, layer, AB_KINDS, "in_proj_ab",
                                view_tiles=VIEW_TILES)
    proj_c, gate = _in_proj(x, norm_g, w_c, aux_c, layer, C_KINDS, "in_proj_c")
    c = _fox_cumsum(gate[:, GATE_RANK:GATE_RANK + B_HEADS].T, fox_bias)
    mixed_a = _dilated_mixture(proj_ab, (v4, v16))
    mixed_b = _forgetting_attention(proj_ab, c)
    gate_up = _pad_heads(gla_gate_up, C_HEADS, C_DK, C_DK_PAD)
    gate_up = gate_up.reshape(GATE_RANK, C_HEADS, C_DK_PAD).transpose(1, 0, 2)
    gate_bias = _pad_heads(gla_gate_bias, C_HEADS, C_DK, C_DK_PAD).reshape(C_HEADS, 1, C_DK_PAD)
    out_gain = jnp.pad(gla_out_gain, (0, C_DV_PAD - C_DV)).reshape(1, C_DV_PAD)
    mixed_c = _gated_linear_attention(proj_c, gate, gate_up, gate_bias, out_gain)
    return _out_proj(x, mixed_a, mixed_b, mixed_c, w_out, wo_c, layer)


@jax.jit
def kernel(x, norm_g, w_in, a_q_gain, a_k_gain, b_q_gain, b_k_gain, fox_bias, gla_gate_up,
           gla_gate_bias, gla_out_gain, w_out):
    bsz, s, d = x.shape
    assert bsz == 1, "batch size 1 only"
    w_c, aux_ab, aux_c, wo_c = _prepare(w_in, w_out, a_q_gain, a_k_gain, b_q_gain, b_k_gain)
    norm_g = norm_g[:, None, :]
    y = x.reshape(s, d)
    for layer in range(w_in.shape[0]):
        y = _layer(y, layer, norm_g, w_in, w_c, aux_ab, aux_c, w_out, wo_c, fox_bias[layer],
                   gla_gate_up[layer], gla_gate_bias[layer], gla_out_gain[layer])
    return y.reshape(bsz, s, d)
```

```python
import functools

import numpy as np
import jax
import jax.numpy as jnp
from jax import lax
from jax.experimental import pallas as pl
from jax.experimental.pallas import tpu as pltpu

F32 = jnp.float32
MXU_DTYPE = jnp.bfloat16

LANES = 128
MXU_COLS = 256
HEAD_DIM = 64
A_HEADS, B_HEADS, C_HEADS = 12, 8, 4
A_CHUNKS, B_CHUNKS = A_HEADS // 2, B_HEADS // 2
C_DK, C_DV = 96, 192
C_DK_PAD, C_DV_PAD = 128, 256
GATE_RANK = 16
GATE_TEMP = 16.0
RMS_EPS = 1e-6
LOG2E = 1.4426950408889634
DILATIONS = (1, 4, 16)
BAND = 128
A_TILE = 2048
GLA_CHUNK = 64
GLA_LEVELS = (32, 16, 8, 4, 2, 1)
VMEM_LIMIT = 56 * 1024 * 1024

AQ, AK, AV, AZ = 0, 6, 12, 18
BQ, BK, BV, BZ = 24, 28, 32, 36
CQ, CK, CV, CZ = 0, 4, 8, 16
A_WIDTH, B_WIDTH = A_HEADS * HEAD_DIM, B_HEADS * HEAD_DIM

_NT = (((1,), (1,)), ((), ()))
_TN = (((0,), (0,)), ((), ()))


def _params(sem):
    return pltpu.CompilerParams(dimension_semantics=sem, vmem_limit_bytes=VMEM_LIMIT)


def _log_sigmoid(x):
    return jnp.minimum(x, 0.0) - jnp.log1p(jnp.exp(-jnp.abs(x)))


AB_KINDS = (("headnorm+views",) * (2 * A_CHUNKS) + ("scale+views",) * A_CHUNKS
            + ("silu",) * A_CHUNKS
            + ("headnorm",) * (2 * B_CHUNKS) + ("scale",) * B_CHUNKS + ("silu",) * B_CHUNKS)[::2]
C_KINDS = (("scale",) * (4 * C_HEADS) + ("silu",) * (2 * C_HEADS))[::2] + ("gate",)


def _kind_ranges(kinds, kind):
    runs, start = [], None
    for t, k in enumerate(kinds + (None,)):
        if k == kind and start is None:
            start = t
        elif k != kind and start is not None:
            runs.append((start, t))
            start = None
    return runs


def _view_tiles(kinds):
    n = sum(k.endswith("+views") for k in kinds)
    assert all(k.endswith("+views") for k in kinds[:n])
    return n


def _proj_body(x_ref, g_ref, w_ref, aux_ref, *rest, kinds):
    view_tiles = _view_tiles(kinds)
    n_tiles = len(kinds)
    o_ref, rest = rest[0], rest[1:]
    gate_ref = None
    if kinds[-1] == "gate":
        gate_ref, rest = rest[0], rest[1:]
    if view_tiles:
        view_refs, rest = rest[:len(DILATIONS) - 1], rest[len(DILATIONS) - 1:]
        n_stage = MXU_COLS // LANES
        stage, rest = rest[-n_stage:], rest[:-n_stage]
    h_scr, acc_scr = rest[0], rest[1:3]
    j = pl.program_id(1)
    tm = h_scr.shape[0]

    half = tm // 2

    def matmul(dst, rows):
        dst[rows, :] = lax.dot_general(h_scr[rows, :], w_ref[0].astype(MXU_DTYPE), _NT,
                                       preferred_element_type=F32)

    def epilogue(src, kind, rows):
        r0, nr = rows.start, rows.stop - rows.start
        if kind == "gate":
            gate_ref[rows, :] = src[rows, 0:LANES]
            return
        base = kind.split("+")[0]
        for c in range(MXU_COLS // LANES):
            cols = slice(c * LANES, (c + 1) * LANES)
            y = src[rows, cols]
            if base == "headnorm":
                first = lax.broadcasted_iota(jnp.int32, y.shape, 1) < HEAD_DIM
                y2 = y * y
                s0 = jnp.sum(jnp.where(first, y2, 0.0), axis=-1, keepdims=True)
                s1 = jnp.sum(jnp.where(first, 0.0, y2), axis=-1, keepdims=True)
                ms = jnp.where(first, s0, s1) * (1.0 / HEAD_DIM)
                y = y * lax.rsqrt(ms + RMS_EPS) * aux_ref[0, :, cols]
            elif base == "scale":
                y = y * aux_ref[0, :, cols]
            elif base == "silu":
                y = y * jax.nn.sigmoid(y)
            o_ref[c, rows, :] = y.astype(o_ref.dtype)
            if not kind.endswith("+views"):
                continue
            stage[c][rows, :] = y
            for view, dil in zip(view_refs, DILATIONS[1:]):
                for r in range(dil):
                    picked = stage[c][pl.ds(r0 + r, nr // dil, stride=dil), :]
                    view[c, r0 // dil:(r0 + nr) // dil, r * LANES:(r + 1) * LANES] = (
                        picked.astype(view.dtype))

    halves = (slice(0, half), slice(half, tm))

    @pl.when(j == 0)
    def _():
        x = x_ref[...]
        ms = jnp.mean(x * x, axis=-1, keepdims=True)
        h_scr[...] = (x * lax.rsqrt(ms + RMS_EPS) * g_ref[0]).astype(h_scr.dtype)
        for rows in halves:
            matmul(acc_scr[0], rows)

    for parity in range(2):
        for kind in sorted(set(kinds[:-1])):
            in_kind = functools.reduce(
                jnp.logical_or, [(j - 1 >= lo) & (j - 1 < min(hi, n_tiles - 1))
                                 for lo, hi in _kind_ranges(kinds, kind) if lo < n_tiles - 1])

            @pl.when((j >= 1) & (j < n_tiles) & (j % 2 == parity) & in_kind)
            def _(parity=parity, kind=kind):
                for rows in halves:
                    matmul(acc_scr[parity], rows)
                    epilogue(acc_scr[1 - parity], kind, rows)

    @pl.when(j == n_tiles)
    def _():
        for rows in halves:
            epilogue(acc_scr[(n_tiles - 1) % 2], kinds[-1], rows)


def _in_proj(x, norm_g, w_t, aux, layer, kinds, name, tm=1024):
    s, d = x.shape
    n_tiles = len(kinds)
    view_tiles = _view_tiles(kinds)
    chunks = MXU_COLS // LANES
    has_gate = kinds[-1] == "gate"
    out_tiles = n_tiles - 1 if has_gate else n_tiles
    prev = lambda j, n: jnp.clip(j - 1, 0, n - 1)
    out_shape = [jax.ShapeDtypeStruct((out_tiles * chunks, s, LANES), MXU_DTYPE)]
    out_specs = [pl.BlockSpec((chunks, tm, LANES), lambda i, j: (prev(j, out_tiles), i, 0))]
    scratch = [pltpu.VMEM((tm, d), MXU_DTYPE), pltpu.VMEM((tm, MXU_COLS), F32),
               pltpu.VMEM((tm, MXU_COLS), F32)]
    if has_gate:
        out_shape.append(jax.ShapeDtypeStruct((s, LANES), F32))
        out_specs.append(pl.BlockSpec((tm, LANES), lambda i, j: (i, 0)))
    if view_tiles:
        for dil in DILATIONS[1:]:
            out_shape.append(
                jax.ShapeDtypeStruct((view_tiles * chunks, s // dil, dil * LANES), MXU_DTYPE))
            out_specs.append(pl.BlockSpec((chunks, tm // dil, dil * LANES),
                                          lambda i, j: (prev(j, view_tiles), i, 0)))
        scratch += [pltpu.VMEM((tm, LANES), F32)] * chunks
    return pl.pallas_call(
        functools.partial(_proj_body, kinds=kinds),
        grid=(s // tm, n_tiles + 1),
        in_specs=[pl.BlockSpec((tm, d), lambda i, j: (i, 0)),
                  pl.BlockSpec((1, 1, d), lambda i, j: (layer, 0, 0)),
                  pl.BlockSpec((1, MXU_COLS, d),
                               lambda i, j: (layer, jnp.minimum(j, n_tiles - 1), 0)),
                  pl.BlockSpec((1, 1, MXU_COLS), lambda i, j: (layer, 0, prev(j, n_tiles)))],
        out_specs=out_specs,
        out_shape=out_shape,
        scratch_shapes=scratch,
        compiler_params=_params(("parallel", "arbitrary")),
        name=name,
    )(x, norm_g, w_t, aux)


def _fox_cumsum_body(x_ref, b_ref, o_ref):
    x = _log_sigmoid(x_ref[...] + b_ref[...])
    idx = lax.broadcasted_iota(jnp.int32, x.shape, 1)
    shift = 1
    while shift < x.shape[1]:
        x = x + jnp.where(idx >= shift, pltpu.roll(x, shift, axis=1), 0.0)
        shift *= 2
    o_ref[...] = x * LOG2E


def _fox_cumsum(logit_t, bias):
    nh, s = logit_t.shape
    return pl.pallas_call(
        _fox_cumsum_body,
        out_shape=jax.ShapeDtypeStruct((nh, s), F32),
        compiler_params=pltpu.CompilerParams(vmem_limit_bytes=VMEM_LIMIT),
        name="fox_cumsum",
    )(logit_t, bias.reshape(nh, 1))


A_GROUP = 4


def _band_attention(blocks):
    row = lax.broadcasted_iota(jnp.int32, (BAND, 2 * BAND), 0)
    col = lax.broadcasted_iota(jnp.int32, (BAND, 2 * BAND), 1)
    band = (col >= row) & (col <= row + BAND)
    first = lax.broadcasted_iota(jnp.int32, (BAND, LANES), 1) < HEAD_DIM
    first_kv = lax.broadcasted_iota(jnp.int32, (2 * BAND, LANES), 1) < HEAD_DIM
    chains = [(b, h) for b in range(len(blocks)) for h in range(2)]
    logits = []
    for b, h in chains:
        q, k2 = blocks[b][0], blocks[b][1]
        zero = jnp.zeros_like(q)
        qm = jnp.where(first, q, zero) if h == 0 else jnp.where(first, zero, q)
        logits.append(lax.dot_general(qm, k2, _NT, preferred_element_type=F32))
    masks = [band & (col + blk[3] >= 0) for blk in blocks]
    logits = [jnp.where(masks[b], s, -jnp.inf) for (b, h), s in zip(chains, logits)]
    maxes = [jnp.max(s, axis=-1, keepdims=True) for s in logits]
    probs = [jnp.exp2(s - m).astype(MXU_DTYPE) for s, m in zip(logits, maxes)]
    res = []
    for (b, h), p in zip(chains, probs):
        v2 = blocks[b][2]
        one = jnp.ones_like(v2)
        vh = jnp.where(first_kv, v2, one) if h == 0 else jnp.where(first_kv, one, v2)
        res.append(jnp.dot(p, vh, preferred_element_type=F32))
    results = []
    for b in range(len(blocks)):
        r0, r1 = res[2 * b], res[2 * b + 1]
        num = jnp.where(first, r0, r1)
        den = pltpu.roll(jnp.where(first, r1, r0), HEAD_DIM, axis=1)
        m = jnp.where(first, maxes[2 * b], maxes[2 * b + 1])
        results.append((num / den, m + jnp.log2(den)))
    return results


def _dilated_body(*refs):
    ins, z_ref, o_ref, scr = refs[:15], refs[15], refs[16], refs[17:]
    kv_scr, o_scr, l_scr = scr[:6], scr[6], scr[7]
    n = pl.program_id(1)
    for bi, d in enumerate(DILATIONS):
        q_ref, kc, kp, vc, vp = ins[5 * bi:5 * bi + 5]
        kf, vf = kv_scr[2 * bi:2 * bi + 2]
        rows = A_TILE // d
        nb = rows // BAND
        for cur, prev, full in ((kc, kp, kf), (vc, vp, vf)):
            full[0:BAND, :] = prev[0, rows - BAND:rows, :]
            full[BAND:BAND + rows, :] = cur[0]

        def group(members, q_ref=q_ref, kf=kf, vf=vf, d=d, nb=nb, bi=bi):
            blocks = []
            for j, r in members:
                cols = slice(r * LANES, (r + 1) * LANES)
                row0 = j * BAND if isinstance(j, int) else pl.multiple_of(j * BAND, BAND)
                blocks.append((q_ref[0, pl.ds(row0, BAND), cols], kf[pl.ds(row0, 2 * BAND), cols],
                               vf[pl.ds(row0, 2 * BAND), cols], (n * nb + j - 1) * BAND))
            for (j, r), (o, lse) in zip(members, _band_attention(blocks)):
                dst = pl.ds(j * BAND * d + r, BAND, stride=d)
                o_scr[bi, dst, :] = o
                l_scr[bi, dst, :] = lse

        if nb >= A_GROUP:
            for r in range(d):
                def body(g, carry, r=r, group=group):
                    j0 = pl.multiple_of(g * A_GROUP, A_GROUP)
                    group([(j0 + i, r) for i in range(A_GROUP)])
                    return carry
                lax.fori_loop(0, nb // A_GROUP, body, 0)
        else:
            for r0 in range(0, d, A_GROUP // nb):
                group([(j, r0 + i) for i in range(A_GROUP // nb) for j in range(nb)])
    lse = l_scr[...]
    w = jnp.exp2(lse - jnp.max(lse, axis=0, keepdims=True))
    mixed = jnp.sum(w * o_scr[...], axis=0) / jnp.sum(w, axis=0)
    o_ref[...] = (mixed * z_ref[0].astype(F32)).astype(o_ref.dtype)


def _dilated_mixture(proj, views):
    s = proj.shape[1]
    operands, in_specs, scratch = [], [], []
    for d, view in zip(DILATIONS, (proj,) + tuple(views)):
        rows, width = A_TILE // d, d * LANES
        blk = (1, rows, width)
        operands += [view] * 5
        in_specs += [
            pl.BlockSpec(blk, lambda c, n: (AQ + c, n, 0)),
            pl.BlockSpec(blk, lambda c, n: (AK + c, n, 0)),
            pl.BlockSpec(blk, lambda c, n: (AK + c, jnp.maximum(n - 1, 0), 0)),
            pl.BlockSpec(blk, lambda c, n: (AV + c, n, 0)),
            pl.BlockSpec(blk, lambda c, n: (AV + c, jnp.maximum(n - 1, 0), 0)),
        ]
        scratch += [pltpu.VMEM((rows + BAND, width), MXU_DTYPE)] * 2
    scratch += [pltpu.VMEM((len(DILATIONS), A_TILE, LANES), F32)] * 2
    operands.append(proj)
    in_specs.append(pl.BlockSpec((1, A_TILE, LANES), lambda c, n: (AZ + c, n, 0)))
    return pl.pallas_call(
        _dilated_body,
        grid=(A_CHUNKS, s // A_TILE),
        in_specs=in_specs,
        out_specs=pl.BlockSpec((A_TILE, LANES), lambda c, n: (n, c)),
        out_shape=jax.ShapeDtypeStruct((s, A_WIDTH), MXU_DTYPE),
        scratch_shapes=scratch,
        compiler_params=_params(("parallel", "arbitrary")),
        name="dilated_mixture",
    )(*operands)


FOX_AUG = 3
FOX_VROWS = HEAD_DIM + 16


def _split3(c):
    hi = c.astype(jnp.bfloat16).astype(F32)
    r = c - hi
    mid = r.astype(jnp.bfloat16).astype(F32)
    return hi, mid, r - mid


def _fox_prep_body(q_ref, k_ref, v_ref, ccol_ref, crow_ref, qt_ref, ka_ref, vt_ref):
    tm = q_ref.shape[1]
    q_t = q_ref[0].astype(F32).T
    v_t = v_ref[0].astype(F32).T
    k = k_ref[0]
    sub = lax.broadcasted_iota(jnp.int32, (HEAD_DIM, tm), 0)
    lane = lax.broadcasted_iota(jnp.int32, (tm, LANES), 1)
    v_tail = (lax.broadcasted_iota(jnp.int32, (FOX_VROWS - HEAD_DIM, tm), 0) == 0).astype(F32)
    for h in range(2):
        hi, mid, lo = _split3(crow_ref[0, h:h + 1, :])
        aug_q = jnp.where(sub == 0, hi, jnp.where(sub == 1, mid, jnp.where(
            sub == 2, lo, jnp.where(sub < 2 * FOX_AUG, 1.0, 0.0))))
        q_h = q_t[h * HEAD_DIM:(h + 1) * HEAD_DIM]
        parts = [q_h, aug_q] if h == 0 else [aug_q, q_h]
        qt_ref[h] = jnp.concatenate(parts, axis=0).astype(qt_ref.dtype)
        hi, mid, lo = _split3(ccol_ref[0, :, h:h + 1])
        a0 = (1 - h) * HEAD_DIM
        aug_k = jnp.where(lane == a0 + FOX_AUG, -hi, jnp.where(lane == a0 + FOX_AUG + 1, -mid, jnp.where(
            lane == a0 + FOX_AUG + 2, -lo, jnp.where((lane >= a0) & (lane < a0 + FOX_AUG), 1.0, 0.0))))
        own = (lane < HEAD_DIM) if h == 0 else (lane >= HEAD_DIM)
        ka_ref[h] = jnp.where(own, k, aug_k.astype(k.dtype))
        vt_ref[h, 0] = jnp.concatenate([v_t[h * HEAD_DIM:(h + 1) * HEAD_DIM], v_tail],
                                       axis=0).astype(vt_ref.dtype)


def _fox_prep(proj, c, tm):
    s = proj.shape[1]
    c_col = c.reshape(B_CHUNKS, 2, s).transpose(0, 2, 1)
    c_row = c.reshape(B_CHUNKS, 2, s)
    return pl.pallas_call(
        _fox_prep_body,
        grid=(B_CHUNKS, s // tm),
        in_specs=[
            pl.BlockSpec((1, tm, LANES), lambda p, i: (BQ + p, i, 0)),
            pl.BlockSpec((1, tm, LANES), lambda p, i: (BK + p, i, 0)),
            pl.BlockSpec((1, tm, LANES), lambda p, i: (BV + p, i, 0)),
            pl.BlockSpec((1, tm, 2), lambda p, i: (p, i, 0)),
            pl.BlockSpec((1, 2, tm), lambda p, i: (p, 0, i)),
        ],
        out_specs=[
            pl.BlockSpec((2, LANES, tm), lambda p, i: (p, 0, i)),
            pl.BlockSpec((2, tm, LANES), lambda p, i: (p, i, 0)),
            pl.BlockSpec((2, 1, FOX_VROWS, tm), lambda p, i: (p, i, 0, 0)),
        ],
        out_shape=[
            jax.ShapeDtypeStruct((B_HEADS, LANES, s), MXU_DTYPE),
            jax.ShapeDtypeStruct((B_HEADS, s, LANES), MXU_DTYPE),
            jax.ShapeDtypeStruct((B_HEADS, s // tm, FOX_VROWS, tm), MXU_DTYPE),
        ],
        compiler_params=_params(("parallel", "parallel")),
        name="fox_prep",
    )(proj, proj, proj, c_col, c_row)


def _fox_body(qt_ref, ka_ref, vt_ref, z_ref, o_ref, s0_scr, s1_scr, m_scr, acc_scr, *, blk):
    qi = pl.program_id(1)
    causal = (lax.broadcasted_iota(jnp.int32, (blk, blk), 0)
              <= lax.broadcasted_iota(jnp.int32, (blk, blk), 1))
    q_t = (qt_ref[0], qt_ref[1])

    def logits(kb, dst):
        k0 = pl.multiple_of(kb * blk, blk)
        for h in range(2):
            dst[h] = jnp.dot(ka_ref[h, pl.ds(k0, blk), :], q_t[h], preferred_element_type=F32)

    def accumulate(kb, src, diagonal=False):
        for h in range(2):
            s = src[h]
            if diagonal:
                s = jnp.where(causal, s, -jnp.inf)
            m = m_scr[h]
            m_new = jnp.maximum(m, jnp.max(s, axis=0, keepdims=True))
            p = jnp.exp2(s - m_new).astype(MXU_DTYPE)
            acc_scr[h] = (jnp.exp2(m - m_new) * acc_scr[h]
                          + jnp.dot(vt_ref[h, kb], p, preferred_element_type=F32))
            m_scr[h] = m_new

    m_scr[...] = jnp.full(m_scr.shape, -jnp.inf, F32)
    acc_scr[...] = jnp.zeros(acc_scr.shape, F32)
    logits(0, s0_scr)

    def pair(j, carry):
        logits(2 * j + 1, s1_scr)
        accumulate(2 * j, s0_scr)
        logits(2 * j + 2, s0_scr)
        accumulate(2 * j + 1, s1_scr)
        return carry

    lax.fori_loop(0, qi // 2, pair, 0)

    @pl.when(qi % 2 == 0)
    def _():
        accumulate(qi, s0_scr, diagonal=True)

    @pl.when(qi % 2 == 1)
    def _():
        logits(qi, s1_scr)
        accumulate(qi - 1, s0_scr)
        accumulate(qi, s1_scr, diagonal=True)

    outs = [acc_scr[h, 0:HEAD_DIM] / acc_scr[h, HEAD_DIM:HEAD_DIM + 1] for h in range(2)]
    o = jnp.concatenate(outs, axis=0).T
    o_ref[...] = (o * z_ref[0].astype(F32)).astype(o_ref.dtype)


def _forgetting_attention(proj, c, blk=512):
    s = proj.shape[1]
    qt, ka, vt = _fox_prep(proj, c, blk)
    return pl.pallas_call(
        functools.partial(_fox_body, blk=blk),
        grid=(B_CHUNKS, s // blk),
        in_specs=[
            pl.BlockSpec((2, LANES, blk), lambda p, i: (p, 0, i)),
            pl.BlockSpec((2, s, LANES), lambda p, i: (p, 0, 0)),
            pl.BlockSpec((2, s // blk, FOX_VROWS, blk), lambda p, i: (p, 0, 0, 0)),
            pl.BlockSpec((1, blk, LANES), lambda p, i: (BZ + p, i, 0)),
        ],
        out_specs=pl.BlockSpec((blk, LANES), lambda p, i: (i, p)),
        out_shape=jax.ShapeDtypeStruct((s, B_WIDTH), MXU_DTYPE),
        scratch_shapes=[pltpu.VMEM((2, blk, blk), F32), pltpu.VMEM((2, blk, blk), F32),
                        pltpu.VMEM((2, 1, blk), F32), pltpu.VMEM((2, FOX_VROWS, blk), F32)],
        compiler_params=_params(("parallel", "arbitrary")),
        name="forgetting_attention",
    )(qt, ka, vt, proj)


GLA_GROUP = 4


def _gla_level_matrix():
    c = GLA_CHUNK
    mats = []
    for b in GLA_LEVELS:
        m = np.zeros((c, c), np.float32)
        for i in range(c):
            pivot = (i // (2 * b)) * 2 * b + b - 1
            if i > pivot:
                m[i, pivot + 1:i + 1] = 1.0
            else:
                m[i, i + 1:pivot + 1] = 1.0
        mats.append(m)
    mats.append(np.tril(np.ones((c, c), np.float32)))
    return np.concatenate(mats, axis=0)


def _gla_body(q_ref, k_ref, v_ref, g_ref, z_ref, gu_ref, gb_ref, og_ref, lvl_ref, o_ref,
              st_scr, *, tile):
    c = GLA_CHUNK
    nlev = len(GLA_LEVELS)
    nch = tile // c
    grp = GLA_GROUP * c

    @pl.when(pl.program_id(1) == 0)
    def _():
        st_scr[...] = jnp.zeros_like(st_scr)

    gate_in = g_ref[:, 0:GATE_RANK]
    gate_hi = gate_in.astype(MXU_DTYPE)
    gate_lo = (gate_in - gate_hi.astype(F32)).astype(MXU_DTYPE)
    gate_w = gu_ref[0].astype(MXU_DTYPE)
    logit = (jnp.dot(gate_hi, gate_w, preferred_element_type=F32)
             + jnp.dot(gate_lo, gate_w, preferred_element_type=F32) + gb_ref[0])
    la = _log_sigmoid(logit) * (1.0 / GATE_TEMP)

    la_cat = jnp.concatenate([la[ci * c:(ci + 1) * c] for ci in range(nch)], axis=1)
    la_hi = la_cat.astype(MXU_DTYPE)
    la_lo = (la_cat - la_hi.astype(F32)).astype(MXU_DTYPE)
    f = jnp.exp(jnp.dot(lvl_ref[0:nlev * c, :], la_hi, preferred_element_type=F32))
    tri = lvl_ref[nlev * c:(nlev + 1) * c, :]
    bc = (jnp.dot(tri, la_hi, preferred_element_type=F32)
          + jnp.dot(tri, la_lo, preferred_element_type=F32))

    row = lax.broadcasted_iota(jnp.int32, (grp, grp), 0)
    col = lax.broadcasted_iota(jnp.int32, (grp, grp), 1)
    diff = (row ^ col) & (c - 1)
    level = jnp.full((grp, grp), nlev, jnp.int32)
    for li, b in reversed(list(enumerate(GLA_LEVELS))):
        level = jnp.where(diff >= b, li, level)
    level = jnp.where(((row & -c) == (col & -c)) & (col <= row), level, -1)
    out_gain = og_ref[...]

    state = st_scr[...]
    for g0 in range(0, nch, GLA_GROUP):
        rows = slice(g0 * c, g0 * c + grp)
        q = q_ref[0, rows, :].astype(F32)
        k = k_ref[0, rows, :].astype(F32)
        v = jnp.concatenate([v_ref[0, rows, :], v_ref[1, rows, :]], axis=-1)
        z = jnp.concatenate([z_ref[0, rows, :], z_ref[1, rows, :]], axis=-1)
        a = lax.dot_general(q.astype(MXU_DTYPE), k.astype(MXU_DTYPE), _NT,
                            preferred_element_type=F32)
        attn = jnp.where(level == nlev, a, 0.0)
        for li in range(nlev):
            fl = jnp.concatenate([f[li * c:(li + 1) * c, ch * LANES:(ch + 1) * LANES]
                                  for ch in range(g0, g0 + GLA_GROUP)], axis=0)
            a = lax.dot_general((q * fl).astype(MXU_DTYPE), (k * fl).astype(MXU_DTYPE), _NT,
                                preferred_element_type=F32)
            attn = jnp.where(level == li, a, attn)
        o_intra = jnp.dot(attn.astype(MXU_DTYPE), v, preferred_element_type=F32)

        for ci in range(GLA_GROUP):
            ch = g0 + ci
            r = slice(ci * c, (ci + 1) * c)
            bc_c = bc[:, ch * LANES:(ch + 1) * LANES]
            b_last = bc_c[c - 1:c, :]
            q_dec = (q[r] * jnp.exp(bc_c)).astype(MXU_DTYPE)
            o = o_intra[r] + jnp.dot(q_dec, state.astype(MXU_DTYPE), preferred_element_type=F32)
            k_dec = (k[r] * jnp.exp(b_last - bc_c)).astype(MXU_DTYPE)
            decay = jnp.broadcast_to(jnp.exp(b_last), (LANES, LANES)).T
            state = (state * jnp.concatenate([decay, decay], axis=1)
                     + lax.dot_general(k_dec, v[r], _TN, preferred_element_type=F32))
            ms = jnp.sum(o * o, axis=-1, keepdims=True) * (1.0 / C_DV)
            y = o * lax.rsqrt(ms + RMS_EPS) * out_gain * z[r].astype(F32)
            o_ref[pl.ds(ch * c, c), :] = y.astype(o_ref.dtype)
    st_scr[...] = state


def _gated_linear_attention(proj, gate, gate_up, gate_bias, out_gain, tile=512):
    s = proj.shape[1]
    lvl = jnp.asarray(_gla_level_matrix(), MXU_DTYPE)
    return pl.pallas_call(
        functools.partial(_gla_body, tile=tile),
        grid=(C_HEADS, s // tile),
        in_specs=[
            pl.BlockSpec((1, tile, LANES), lambda h, t: (CQ + h, t, 0)),
            pl.BlockSpec((1, tile, LANES), lambda h, t: (CK + h, t, 0)),
            pl.BlockSpec((2, tile, LANES), lambda h, t: (CV // 2 + h, t, 0)),
            pl.BlockSpec((tile, LANES), lambda h, t: (t, 0)),
            pl.BlockSpec((2, tile, LANES), lambda h, t: (CZ // 2 + h, t, 0)),
            pl.BlockSpec((1, GATE_RANK, C_DK_PAD), lambda h, t: (h, 0, 0)),
            pl.BlockSpec((1, 1, C_DK_PAD), lambda h, t: (h, 0, 0)),
            pl.BlockSpec((1, C_DV_PAD), lambda h, t: (0, 0)),
            pl.BlockSpec(lvl.shape, lambda h, t: (0, 0)),
        ],
        out_specs=pl.BlockSpec((tile, C_DV_PAD), lambda h, t: (t, h)),
        out_shape=jax.ShapeDtypeStruct((s, C_HEADS * C_DV_PAD), MXU_DTYPE),
        scratch_shapes=[pltpu.VMEM((C_DK_PAD, C_DV_PAD), F32)],
        compiler_params=_params(("parallel", "arbitrary")),
        name="gated_linear_attention",
    )(proj, proj, proj, gate, proj, gate_up, gate_bias, out_gain, lvl)


def _out_body(x_ref, a_ref, b_ref, c_ref, wab_ref, wc_ref, o_ref):
    wa = wab_ref[0, 0:A_WIDTH, :].astype(MXU_DTYPE)
    wb = wab_ref[0, A_WIDTH:A_WIDTH + B_WIDTH, :].astype(MXU_DTYPE)
    acc = jnp.dot(a_ref[...], wa, preferred_element_type=F32)
    acc = acc + jnp.dot(b_ref[...], wb, preferred_element_type=F32)
    acc = acc + jnp.dot(c_ref[...], wc_ref[0], preferred_element_type=F32)
    o_ref[...] = x_ref[...] + acc


def _out_proj(x, ma, mb, mc, w_out, wo_c, layer, tm=1024, tn=512):
    s, d = x.shape
    row_blk = lambda m: pl.BlockSpec((tm, m.shape[1]), lambda i, j: (i, 0))
    return pl.pallas_call(
        _out_body,
        grid=(s // tm, d // tn),
        in_specs=[pl.BlockSpec((tm, tn), lambda i, j: (i, j)),
                  row_blk(ma), row_blk(mb), row_blk(mc),
                  pl.BlockSpec((1, A_WIDTH + B_WIDTH, tn), lambda i, j: (layer, 0, j)),
                  pl.BlockSpec((1, mc.shape[1], tn), lambda i, j: (layer, 0, j))],
        out_specs=pl.BlockSpec((tm, tn), lambda i, j: (i, j)),
        out_shape=jax.ShapeDtypeStruct((s, d), F32),
        compiler_params=_params(("parallel", "arbitrary")),
        name="out_proj",
    )(x, ma, mb, mc, w_out, wo_c)


def _pad_heads(w, heads, width, padded):
    lead = w.shape[:-1]
    w = w.reshape(*lead, heads, width)
    w = jnp.pad(w, [(0, 0)] * len(lead) + [(0, 0), (0, padded - width)])
    return w.reshape(*lead, heads * padded)


def _prepare(w_in, w_out, a_q_gain, a_k_gain, b_q_gain, b_k_gain):
    layers, d, _ = w_in.shape
    w_t = jnp.swapaxes(w_in, 1, 2)
    ck, cv = C_HEADS * C_DK, C_HEADS * C_DV
    ab = 4 * (A_WIDTH + B_WIDTH)
    bounds = np.cumsum([ab, B_HEADS, ck, ck, cv, cv, GATE_RANK])
    bf, cq, ckk, cvv, cz, cr = [w_t[:, lo:hi] for lo, hi in zip(bounds[:-1], bounds[1:])]

    def pad_rows(w, width, padded):
        w = w.reshape(layers, C_HEADS, width, d)
        w = jnp.pad(w, ((0, 0), (0, 0), (0, padded - width), (0, 0)))
        return w.reshape(layers, C_HEADS * padded, d)

    pad_k = lambda w: pad_rows(w, C_DK, C_DK_PAD)
    pad_v = lambda w: pad_rows(w, C_DV, C_DV_PAD)
    tail = jnp.zeros((layers, MXU_COLS - GATE_RANK - B_HEADS, d), F32)
    w_c_t = jnp.concatenate([pad_k(cq), pad_k(ckk), pad_v(cvv), pad_v(cz), cr, bf, tail], axis=1)

    q_scale = HEAD_DIM ** -0.5 * LOG2E
    tile_heads = lambda g, n: jnp.tile(g, (1, n))
    ones = lambda n: jnp.ones((layers, n), F32)
    aux_ab = jnp.concatenate([
        tile_heads(a_q_gain * q_scale, A_HEADS), tile_heads(a_k_gain, A_HEADS), ones(2 * A_WIDTH),
        tile_heads(b_q_gain * q_scale, B_HEADS), tile_heads(b_k_gain, B_HEADS), ones(2 * B_WIDTH)],
        axis=-1)[:, None, :]
    aux_c = jnp.concatenate([
        jnp.full((layers, C_HEADS * C_DK_PAD), C_DK ** -0.5, F32),
        ones(C_HEADS * (C_DK_PAD + 2 * C_DV_PAD) + MXU_COLS)], axis=-1)[:, None, :]

    wo_c = w_out[:, A_WIDTH + B_WIDTH:].reshape(layers, C_HEADS, C_DV, d)
    wo_c = jnp.pad(wo_c, ((0, 0), (0, 0), (0, C_DV_PAD - C_DV), (0, 0)))
    wo_c = wo_c.reshape(layers, C_HEADS * C_DV_PAD, d).astype(MXU_DTYPE)
    return w_t, w_c_t, aux_ab, aux_c, wo_c


def _layer(x, layer, norm_g, w_in, w_c, aux_ab, aux_c, w_out, wo_c, fox_bias, gla_gate_up,
           gla_gate_bias, gla_out_gain):
    proj_ab, v4, v16 = _in_proj(x, norm_g, w_in, aux_ab, layer, AB_KINDS, "in_proj_ab")
    proj_c, gate = _in_proj(x, norm_g, w_c, aux_c, layer, C_KINDS, "in_proj_c")
    c = _fox_cumsum(gate[:, GATE_RANK:GATE_RANK + B_HEADS].T, fox_bias)
    mixed_a = _dilated_mixture(proj_ab, (v4, v16))
    mixed_b = _forgetting_attention(proj_ab, c)
    gate_up = _pad_heads(gla_gate_up, C_HEADS, C_DK, C_DK_PAD)
    gate_up = gate_up.reshape(GATE_RANK, C_HEADS, C_DK_PAD).transpose(1, 0, 2)
    gate_bias = _pad_heads(gla_gate_bias, C_HEADS, C_DK, C_DK_PAD).reshape(C_HEADS, 1, C_DK_PAD)
    out_gain = jnp.pad(gla_out_gain, (0, C_DV_PAD - C_DV)).reshape(1, C_DV_PAD)
    mixed_c = _gated_linear_attention(proj_c, gate, gate_up, gate_bias, out_gain)
    return _out_proj(x, mixed_a, mixed_b, mixed_c, w_out, wo_c, layer)


@jax.jit
def kernel(x, norm_g, w_in, a_q_gain, a_k_gain, b_q_gain, b_k_gain, fox_bias, gla_gate_up,
           gla_gate_bias, gla_out_gain, w_out):
    bsz, s, d = x.shape
    assert bsz == 1, "batch size 1 only"
    w_t, w_c, aux_ab, aux_c, wo_c = _prepare(w_in, w_out, a_q_gain, a_k_gain, b_q_gain, b_k_gain)
    norm_g = norm_g[:, None, :]
    y = x.reshape(s, d)
    for layer in range(w_in.shape[0]):
        y = _layer(y, layer, norm_g, w_t, w_c, aux_ab, aux_c, w_out, wo_c, fox_bias[layer],
                   gla_gate_up[layer], gla_gate_bias[layer], gla_out_gain[layer])
    return y.reshape(bsz, s, d)
```

```python
import functools

import numpy as np
import jax
import jax.numpy as jnp
from jax import lax
from jax.experimental import pallas as pl
from jax.experimental.pallas import tpu as pltpu

F32 = jnp.float32
MXU_DTYPE = jnp.bfloat16

LANES = 128
MXU_COLS = 256
HEAD_DIM = 64
A_HEADS, B_HEADS, C_HEADS = 12, 8, 4
A_CHUNKS, B_CHUNKS = A_HEADS // 2, B_HEADS // 2
C_DK, C_DV = 96, 192
C_DK_PAD, C_DV_PAD = 128, 256
GATE_RANK = 16
GATE_TEMP = 16.0
RMS_EPS = 1e-6
LOG2E = 1.4426950408889634
DILATIONS = (1, 4, 16)
BAND = 128
A_TILE = 2048
GLA_CHUNK = 64
GLA_LEVELS = (32, 16, 8, 4, 2, 1)
VMEM_LIMIT = 56 * 1024 * 1024

AQ, AK, AV, AZ = 0, 6, 12, 18
BQ, BK, BV, BZ = 24, 28, 32, 36
CQ, CK, CV, CZ = 0, 4, 8, 16
A_WIDTH, B_WIDTH = A_HEADS * HEAD_DIM, B_HEADS * HEAD_DIM

_NT = (((1,), (1,)), ((), ()))
_TN = (((0,), (0,)), ((), ()))


def _params(sem):
    return pltpu.CompilerParams(dimension_semantics=sem, vmem_limit_bytes=VMEM_LIMIT)


def _log_sigmoid(x):
    return jnp.minimum(x, 0.0) - jnp.log1p(jnp.exp(-jnp.abs(x)))


AB_KINDS = (("headnorm+views",) * (2 * A_CHUNKS) + ("scale+views",) * A_CHUNKS
            + ("silu",) * A_CHUNKS
            + ("headnorm",) * (2 * B_CHUNKS) + ("scale",) * B_CHUNKS + ("silu",) * B_CHUNKS)[::2]
C_KINDS = (("scale",) * (4 * C_HEADS) + ("silu",) * (2 * C_HEADS))[::2] + ("gate",)


def _kind_ranges(kinds, kind):
    runs, start = [], None
    for t, k in enumerate(kinds + (None,)):
        if k == kind and start is None:
            start = t
        elif k != kind and start is not None:
            runs.append((start, t))
            start = None
    return runs


def _view_tiles(kinds):
    n = sum(k.endswith("+views") for k in kinds)
    assert all(k.endswith("+views") for k in kinds[:n])
    return n


def _proj_body(x_ref, g_ref, w_ref, aux_ref, *rest, kinds):
    view_tiles = _view_tiles(kinds)
    n_tiles = len(kinds)
    o_ref, rest = rest[0], rest[1:]
    gate_ref = None
    if kinds[-1] == "gate":
        gate_ref, rest = rest[0], rest[1:]
    if view_tiles:
        view_refs, rest = rest[:len(DILATIONS) - 1], rest[len(DILATIONS) - 1:]
        n_stage = MXU_COLS // LANES
        stage, rest = rest[-n_stage:], rest[:-n_stage]
    h_scr, acc_scr = rest[0], rest[1:3]
    j = pl.program_id(1)
    tm = h_scr.shape[0]

    half = tm // 2

    def matmul(dst, rows):
        dst[rows, :] = lax.dot_general(h_scr[rows, :], w_ref[0].astype(MXU_DTYPE), _NT,
                                       preferred_element_type=F32)

    def epilogue(src, kind, rows):
        r0, nr = rows.start, rows.stop - rows.start
        if kind == "gate":
            gate_ref[rows, :] = src[rows, 0:LANES]
            return
        base = kind.split("+")[0]
        for c in range(MXU_COLS // LANES):
            cols = slice(c * LANES, (c + 1) * LANES)
            y = src[rows, cols]
            if base == "headnorm":
                first = lax.broadcasted_iota(jnp.int32, y.shape, 1) < HEAD_DIM
                y2 = y * y
                s0 = jnp.sum(jnp.where(first, y2, 0.0), axis=-1, keepdims=True)
                s1 = jnp.sum(jnp.where(first, 0.0, y2), axis=-1, keepdims=True)
                ms = jnp.where(first, s0, s1) * (1.0 / HEAD_DIM)
                y = y * lax.rsqrt(ms + RMS_EPS) * aux_ref[0, :, cols]
            elif base == "scale":
                y = y * aux_ref[0, :, cols]
            elif base == "silu":
                y = y * jax.nn.sigmoid(y)
            o_ref[c, rows, :] = y.astype(o_ref.dtype)
            if not kind.endswith("+views"):
                continue
            stage[c][rows, :] = y
            for view, dil in zip(view_refs, DILATIONS[1:]):
                for r in range(dil):
                    picked = stage[c][pl.ds(r0 + r, nr // dil, stride=dil), :]
                    view[c, r0 // dil:(r0 + nr) // dil, r * LANES:(r + 1) * LANES] = (
                        picked.astype(view.dtype))

    halves = (slice(0, half), slice(half, tm))

    @pl.when(j == 0)
    def _():
        x = x_ref[...]
        ms = jnp.mean(x * x, axis=-1, keepdims=True)
        h_scr[...] = (x * lax.rsqrt(ms + RMS_EPS) * g_ref[0]).astype(h_scr.dtype)
        for rows in halves:
            matmul(acc_scr[0], rows)

    for parity in range(2):
        for kind in sorted(set(kinds[:-1])):
            in_kind = functools.reduce(
                jnp.logical_or, [(j - 1 >= lo) & (j - 1 < min(hi, n_tiles - 1))
                                 for lo, hi in _kind_ranges(kinds, kind) if lo < n_tiles - 1])

            @pl.when((j >= 1) & (j < n_tiles) & (j % 2 == parity) & in_kind)
            def _(parity=parity, kind=kind):
                for rows in halves:
                    matmul(acc_scr[parity], rows)
                    epilogue(acc_scr[1 - parity], kind, rows)

    @pl.when(j == n_tiles)
    def _():
        for rows in halves:
            epilogue(acc_scr[(n_tiles - 1) % 2], kinds[-1], rows)


def _in_proj(x, norm_g, w_t, aux, layer, kinds, name, tm=1024):
    s, d = x.shape
    n_tiles = len(kinds)
    view_tiles = _view_tiles(kinds)
    chunks = MXU_COLS // LANES
    has_gate = kinds[-1] == "gate"
    out_tiles = n_tiles - 1 if has_gate else n_tiles
    prev = lambda j, n: jnp.clip(j - 1, 0, n - 1)
    out_shape = [jax.ShapeDtypeStruct((out_tiles * chunks, s, LANES), MXU_DTYPE)]
    out_specs = [pl.BlockSpec((chunks, tm, LANES), lambda i, j: (prev(j, out_tiles), i, 0))]
    scratch = [pltpu.VMEM((tm, d), MXU_DTYPE), pltpu.VMEM((tm, MXU_COLS), F32),
               pltpu.VMEM((tm, MXU_COLS), F32)]
    if has_gate:
        out_shape.append(jax.ShapeDtypeStruct((s, LANES), F32))
        out_specs.append(pl.BlockSpec((tm, LANES), lambda i, j: (i, 0)))
    if view_tiles:
        for dil in DILATIONS[1:]:
            out_shape.append(
                jax.ShapeDtypeStruct((view_tiles * chunks, s // dil, dil * LANES), MXU_DTYPE))
            out_specs.append(pl.BlockSpec((chunks, tm // dil, dil * LANES),
                                          lambda i, j: (prev(j, view_tiles), i, 0)))
        scratch += [pltpu.VMEM((tm, LANES), F32)] * chunks
    return pl.pallas_call(
        functools.partial(_proj_body, kinds=kinds),
        grid=(s // tm, n_tiles + 1),
        in_specs=[pl.BlockSpec((tm, d), lambda i, j: (i, 0)),
                  pl.BlockSpec((1, 1, d), lambda i, j: (layer, 0, 0)),
                  pl.BlockSpec((1, MXU_COLS, d),
                               lambda i, j: (layer, jnp.minimum(j, n_tiles - 1), 0)),
                  pl.BlockSpec((1, 1, MXU_COLS), lambda i, j: (layer, 0, prev(j, n_tiles)))],
        out_specs=out_specs,
        out_shape=out_shape,
        scratch_shapes=scratch,
        compiler_params=_params(("parallel", "arbitrary")),
        name=name,
    )(x, norm_g, w_t, aux)


def _fox_cumsum_body(x_ref, b_ref, o_ref):
    x = _log_sigmoid(x_ref[...] + b_ref[...])
    idx = lax.broadcasted_iota(jnp.int32, x.shape, 1)
    shift = 1
    while shift < x.shape[1]:
        x = x + jnp.where(idx >= shift, pltpu.roll(x, shift, axis=1), 0.0)
        shift *= 2
    o_ref[...] = x * LOG2E


def _fox_cumsum(logit_t, bias):
    nh, s = logit_t.shape
    return pl.pallas_call(
        _fox_cumsum_body,
        out_shape=jax.ShapeDtypeStruct((nh, s), F32),
        compiler_params=pltpu.CompilerParams(vmem_limit_bytes=VMEM_LIMIT),
        name="fox_cumsum",
    )(logit_t, bias.reshape(nh, 1))


A_GROUP = 4


def _band_attention(blocks):
    row = lax.broadcasted_iota(jnp.int32, (BAND, 2 * BAND), 0)
    col = lax.broadcasted_iota(jnp.int32, (BAND, 2 * BAND), 1)
    band = (col >= row) & (col <= row + BAND)
    first = lax.broadcasted_iota(jnp.int32, (BAND, LANES), 1) < HEAD_DIM
    first_kv = lax.broadcasted_iota(jnp.int32, (2 * BAND, LANES), 1) < HEAD_DIM
    chains = [(b, h) for b in range(len(blocks)) for h in range(2)]
    logits = []
    for b, h in chains:
        q, k2 = blocks[b][0], blocks[b][1]
        zero = jnp.zeros_like(q)
        qm = jnp.where(first, q, zero) if h == 0 else jnp.where(first, zero, q)
        logits.append(lax.dot_general(qm, k2, _NT, preferred_element_type=F32))
    masks = [band & (col + blk[3] >= 0) for blk in blocks]
    logits = [jnp.where(masks[b], s, -jnp.inf) for (b, h), s in zip(chains, logits)]
    maxes = [jnp.max(s, axis=-1, keepdims=True) for s in logits]
    probs = [jnp.exp2(s - m).astype(MXU_DTYPE) for s, m in zip(logits, maxes)]
    res = []
    for (b, h), p in zip(chains, probs):
        v2 = blocks[b][2]
        one = jnp.ones_like(v2)
        vh = jnp.where(first_kv, v2, one) if h == 0 else jnp.where(first_kv, one, v2)
        res.append(jnp.dot(p, vh, preferred_element_type=F32))
    results = []
    for b in range(len(blocks)):
        r0, r1 = res[2 * b], res[2 * b + 1]
        num = jnp.where(first, r0, r1)
        den = pltpu.roll(jnp.where(first, r1, r0), HEAD_DIM, axis=1)
        m = jnp.where(first, maxes[2 * b], maxes[2 * b + 1])
        results.append((num / den, m + jnp.log2(den)))
    return results


def _dilated_body(*refs):
    ins, z_ref, o_ref, scr = refs[:15], refs[15], refs[16], refs[17:]
    kv_scr, o_scr, l_scr = scr[:6], scr[6], scr[7]
    n = pl.program_id(1)
    for bi, d in enumerate(DILATIONS):
        q_ref, kc, kp, vc, vp = ins[5 * bi:5 * bi + 5]
        kf, vf = kv_scr[2 * bi:2 * bi + 2]
        rows = A_TILE // d
        nb = rows // BAND
        for cur, prev, full in ((kc, kp, kf), (vc, vp, vf)):
            full[0:BAND, :] = prev[0, rows - BAND:rows, :]
            full[BAND:BAND + rows, :] = cur[0]

        def group(members, q_ref=q_ref, kf=kf, vf=vf, d=d, nb=nb, bi=bi):
            blocks = []
            for j, r in members:
                cols = slice(r * LANES, (r + 1) * LANES)
                row0 = j * BAND if isinstance(j, int) else pl.multiple_of(j * BAND, BAND)
                blocks.append((q_ref[0, pl.ds(row0, BAND), cols], kf[pl.ds(row0, 2 * BAND), cols],
                               vf[pl.ds(row0, 2 * BAND), cols], (n * nb + j - 1) * BAND))
            for (j, r), (o, lse) in zip(members, _band_attention(blocks)):
                dst = pl.ds(j * BAND * d + r, BAND, stride=d)
                o_scr[bi, dst, :] = o
                l_scr[bi, dst, :] = lse

        if nb >= A_GROUP:
            for r in range(d):
                def body(g, carry, r=r, group=group):
                    j0 = pl.multiple_of(g * A_GROUP, A_GROUP)
                    group([(j0 + i, r) for i in range(A_GROUP)])
                    return carry
                lax.fori_loop(0, nb // A_GROUP, body, 0)
        else:
            for r0 in range(0, d, A_GROUP // nb):
                group([(j, r0 + i) for i in range(A_GROUP // nb) for j in range(nb)])
    lse = l_scr[...]
    w = jnp.exp2(lse - jnp.max(lse, axis=0, keepdims=True))
    mixed = jnp.sum(w * o_scr[...], axis=0) / jnp.sum(w, axis=0)
    o_ref[...] = (mixed * z_ref[0].astype(F32)).astype(o_ref.dtype)


def _dilated_mixture(proj, views):
    s = proj.shape[1]
    operands, in_specs, scratch = [], [], []
    for d, view in zip(DILATIONS, (proj,) + tuple(views)):
        rows, width = A_TILE // d, d * LANES
        blk = (1, rows, width)
        operands += [view] * 5
        in_specs += [
            pl.BlockSpec(blk, lambda c, n: (AQ + c, n, 0)),
            pl.BlockSpec(blk, lambda c, n: (AK + c, n, 0)),
            pl.BlockSpec(blk, lambda c, n: (AK + c, jnp.maximum(n - 1, 0), 0)),
            pl.BlockSpec(blk, lambda c, n: (AV + c, n, 0)),
            pl.BlockSpec(blk, lambda c, n: (AV + c, jnp.maximum(n - 1, 0), 0)),
        ]
        scratch += [pltpu.VMEM((rows + BAND, width), MXU_DTYPE)] * 2
    scratch += [pltpu.VMEM((len(DILATIONS), A_TILE, LANES), F32)] * 2
    operands.append(proj)
    in_specs.append(pl.BlockSpec((1, A_TILE, LANES), lambda c, n: (AZ + c, n, 0)))
    return pl.pallas_call(
        _dilated_body,
        grid=(A_CHUNKS, s // A_TILE),
        in_specs=in_specs,
        out_specs=pl.BlockSpec((A_TILE, LANES), lambda c, n: (n, c)),
        out_shape=jax.ShapeDtypeStruct((s, A_WIDTH), MXU_DTYPE),
        scratch_shapes=scratch,
        compiler_params=_params(("parallel", "arbitrary")),
        name="dilated_mixture",
    )(*operands)


FOX_AUG = 3
FOX_VROWS = HEAD_DIM + 16
FOX_DEN_MIN, FOX_DEN_MAX = 2.0 ** -90, 2.0 ** 100


def _split3(c):
    hi = c.astype(jnp.bfloat16).astype(F32)
    r = c - hi
    mid = r.astype(jnp.bfloat16).astype(F32)
    return hi, mid, r - mid


def _fox_prep_body(q_ref, k_ref, v_ref, ccol_ref, crow_ref, qt_ref, ka_ref, vt_ref):
    tm = q_ref.shape[1]
    q_t = q_ref[0].astype(F32).T
    v_t = v_ref[0].astype(F32).T
    k = k_ref[0]
    sub = lax.broadcasted_iota(jnp.int32, (HEAD_DIM, tm), 0)
    lane = lax.broadcasted_iota(jnp.int32, (tm, LANES), 1)
    v_tail = (lax.broadcasted_iota(jnp.int32, (FOX_VROWS - HEAD_DIM, tm), 0) == 0).astype(F32)
    for h in range(2):
        hi, mid, lo = _split3(crow_ref[0, h:h + 1, :])
        aug_q = jnp.where(sub == 0, hi, jnp.where(sub == 1, mid, jnp.where(
            sub == 2, lo, jnp.where(sub < 2 * FOX_AUG, 1.0, 0.0))))
        q_h = q_t[h * HEAD_DIM:(h + 1) * HEAD_DIM]
        parts = [q_h, aug_q] if h == 0 else [aug_q, q_h]
        qt_ref[h] = jnp.concatenate(parts, axis=0).astype(qt_ref.dtype)
        hi, mid, lo = _split3(ccol_ref[0, :, h:h + 1])
        a0 = (1 - h) * HEAD_DIM
        aug_k = jnp.where(lane == a0 + FOX_AUG, -hi, jnp.where(lane == a0 + FOX_AUG + 1, -mid, jnp.where(
            lane == a0 + FOX_AUG + 2, -lo, jnp.where((lane >= a0) & (lane < a0 + FOX_AUG), 1.0, 0.0))))
        own = (lane < HEAD_DIM) if h == 0 else (lane >= HEAD_DIM)
        ka_ref[h] = jnp.where(own, k, aug_k.astype(k.dtype))
        vt_ref[h, 0] = jnp.concatenate([v_t[h * HEAD_DIM:(h + 1) * HEAD_DIM], v_tail],
                                       axis=0).astype(vt_ref.dtype)


def _fox_prep(proj, c, shift, tm):
    s = proj.shape[1]
    c_col = c.reshape(B_CHUNKS, 2, s).transpose(0, 2, 1)
    c_row = c.reshape(B_CHUNKS, 2, s) - shift
    return pl.pallas_call(
        _fox_prep_body,
        grid=(B_CHUNKS, s // tm),
        in_specs=[
            pl.BlockSpec((1, tm, LANES), lambda p, i: (BQ + p, i, 0)),
            pl.BlockSpec((1, tm, LANES), lambda p, i: (BK + p, i, 0)),
            pl.BlockSpec((1, tm, LANES), lambda p, i: (BV + p, i, 0)),
            pl.BlockSpec((1, tm, 2), lambda p, i: (p, i, 0)),
            pl.BlockSpec((1, 2, tm), lambda p, i: (p, 0, i)),
        ],
        out_specs=[
            pl.BlockSpec((2, LANES, tm), lambda p, i: (p, 0, i)),
            pl.BlockSpec((2, tm, LANES), lambda p, i: (p, i, 0)),
            pl.BlockSpec((2, 1, FOX_VROWS, tm), lambda p, i: (p, i, 0, 0)),
        ],
        out_shape=[
            jax.ShapeDtypeStruct((B_HEADS, LANES, s), MXU_DTYPE),
            jax.ShapeDtypeStruct((B_HEADS, s, LANES), MXU_DTYPE),
            jax.ShapeDtypeStruct((B_HEADS, s // tm, FOX_VROWS, tm), MXU_DTYPE),
        ],
        compiler_params=_params(("parallel", "parallel")),
        name="fox_prep",
    )(proj, proj, proj, c_col, c_row)


def _fox_body(qt_ref, ka_ref, vt_ref, z_ref, o_ref, s0_scr, s1_scr, m_scr, acc_scr, *, blk):
    qi = pl.program_id(1)
    causal = (lax.broadcasted_iota(jnp.int32, (blk, blk), 0)
              <= lax.broadcasted_iota(jnp.int32, (blk, blk), 1))
    q_t = (qt_ref[0], qt_ref[1])

    def logits(kb, dst):
        k0 = pl.multiple_of(kb * blk, blk)
        for h in range(2):
            dst[h] = jnp.dot(ka_ref[h, pl.ds(k0, blk), :], q_t[h], preferred_element_type=F32)

    def accumulate_shifted(kb, src, diagonal=False):
        for h in range(2):
            s = src[h]
            if diagonal:
                s = jnp.where(causal, s, -jnp.inf)
            acc_scr[h] += jnp.dot(vt_ref[h, kb], jnp.exp2(s).astype(MXU_DTYPE),
                                  preferred_element_type=F32)

    def accumulate_online(kb, src, diagonal=False):
        for h in range(2):
            s = src[h]
            if diagonal:
                s = jnp.where(causal, s, -jnp.inf)
            m = m_scr[h]
            m_new = jnp.maximum(m, jnp.max(s, axis=0, keepdims=True))
            p = jnp.exp2(s - m_new).astype(MXU_DTYPE)
            acc_scr[h] = (jnp.exp2(m - m_new) * acc_scr[h]
                          + jnp.dot(vt_ref[h, kb], p, preferred_element_type=F32))
            m_scr[h] = m_new

    def attend(accumulate):
        acc_scr[...] = jnp.zeros(acc_scr.shape, F32)
        logits(0, s0_scr)

        def pair(j, carry):
            logits(2 * j + 1, s1_scr)
            accumulate(2 * j, s0_scr)
            logits(2 * j + 2, s0_scr)
            accumulate(2 * j + 1, s1_scr)
            return carry

        lax.fori_loop(0, qi // 2, pair, 0)

        @pl.when(qi % 2 == 0)
        def _():
            accumulate(qi, s0_scr, diagonal=True)

        @pl.when(qi % 2 == 1)
        def _():
            logits(qi, s1_scr)
            accumulate(qi - 1, s0_scr)
            accumulate(qi, s1_scr, diagonal=True)

        outs = [acc_scr[h, 0:HEAD_DIM] / acc_scr[h, HEAD_DIM:HEAD_DIM + 1] for h in range(2)]
        o = jnp.concatenate(outs, axis=0).T
        o_ref[...] = (o * z_ref[0].astype(F32)).astype(o_ref.dtype)

    attend(accumulate_shifted)
    den = jnp.concatenate([acc_scr[h, HEAD_DIM:HEAD_DIM + 1] for h in range(2)], axis=0)
    in_range = (jnp.min(den) >= FOX_DEN_MIN) & (jnp.max(den) <= FOX_DEN_MAX)

    @pl.when(jnp.logical_not(in_range))
    def _():
        m_scr[...] = jnp.full(m_scr.shape, -jnp.inf, F32)
        attend(accumulate_online)


def _forgetting_attention(proj, c, shift, blk=512):
    s = proj.shape[1]
    qt, ka, vt = _fox_prep(proj, c, shift, blk)
    return pl.pallas_call(
        functools.partial(_fox_body, blk=blk),
        grid=(B_CHUNKS, s // blk),
        in_specs=[
            pl.BlockSpec((2, LANES, blk), lambda p, i: (p, 0, i)),
            pl.BlockSpec((2, s, LANES), lambda p, i: (p, 0, 0)),
            pl.BlockSpec((2, s // blk, FOX_VROWS, blk), lambda p, i: (p, 0, 0, 0)),
            pl.BlockSpec((1, blk, LANES), lambda p, i: (BZ + p, i, 0)),
        ],
        out_specs=pl.BlockSpec((blk, LANES), lambda p, i: (i, p)),
        out_shape=jax.ShapeDtypeStruct((s, B_WIDTH), MXU_DTYPE),
        scratch_shapes=[pltpu.VMEM((2, blk, blk), F32), pltpu.VMEM((2, blk, blk), F32),
                        pltpu.VMEM((2, 1, blk), F32), pltpu.VMEM((2, FOX_VROWS, blk), F32)],
        compiler_params=_params(("parallel", "arbitrary")),
        name="forgetting_attention",
    )(qt, ka, vt, proj)


GLA_GROUP = 4


def _gla_level_matrix():
    c = GLA_CHUNK
    mats = []
    for b in GLA_LEVELS:
        m = np.zeros((c, c), np.float32)
        for i in range(c):
            pivot = (i // (2 * b)) * 2 * b + b - 1
            if i > pivot:
                m[i, pivot + 1:i + 1] = 1.0
            else:
                m[i, i + 1:pivot + 1] = 1.0
        mats.append(m)
    mats.append(np.tril(np.ones((c, c), np.float32)))
    return np.concatenate(mats, axis=0)


def _gla_body(q_ref, k_ref, v_ref, g_ref, z_ref, gu_ref, gb_ref, og_ref, lvl_ref, o_ref,
              st_scr, *, tile):
    c = GLA_CHUNK
    nlev = len(GLA_LEVELS)
    nch = tile // c
    grp = GLA_GROUP * c

    @pl.when(pl.program_id(1) == 0)
    def _():
        st_scr[...] = jnp.zeros_like(st_scr)

    gate_in = g_ref[:, 0:GATE_RANK]
    gate_hi = gate_in.astype(MXU_DTYPE)
    gate_lo = (gate_in - gate_hi.astype(F32)).astype(MXU_DTYPE)
    gate_w = gu_ref[0].astype(MXU_DTYPE)
    logit = (jnp.dot(gate_hi, gate_w, preferred_element_type=F32)
             + jnp.dot(gate_lo, gate_w, preferred_element_type=F32) + gb_ref[0])
    la = _log_sigmoid(logit) * (1.0 / GATE_TEMP)

    la_cat = jnp.concatenate([la[ci * c:(ci + 1) * c] for ci in range(nch)], axis=1)
    la_hi = la_cat.astype(MXU_DTYPE)
    la_lo = (la_cat - la_hi.astype(F32)).astype(MXU_DTYPE)
    f = jnp.exp(jnp.dot(lvl_ref[0:nlev * c, :], la_hi, preferred_element_type=F32))
    tri = lvl_ref[nlev * c:(nlev + 1) * c, :]
    bc = (jnp.dot(tri, la_hi, preferred_element_type=F32)
          + jnp.dot(tri, la_lo, preferred_element_type=F32))

    row = lax.broadcasted_iota(jnp.int32, (grp, grp), 0)
    col = lax.broadcasted_iota(jnp.int32, (grp, grp), 1)
    diff = (row ^ col) & (c - 1)
    level = jnp.full((grp, grp), nlev, jnp.int32)
    for li, b in reversed(list(enumerate(GLA_LEVELS))):
        level = jnp.where(diff >= b, li, level)
    level = jnp.where(((row & -c) == (col & -c)) & (col <= row), level, -1)
    out_gain = og_ref[...]

    state = st_scr[...]
    for g0 in range(0, nch, GLA_GROUP):
        rows = slice(g0 * c, g0 * c + grp)
        q = q_ref[0, rows, :].astype(F32)
        k = k_ref[0, rows, :].astype(F32)
        v = jnp.concatenate([v_ref[0, rows, :], v_ref[1, rows, :]], axis=-1)
        z = jnp.concatenate([z_ref[0, rows, :], z_ref[1, rows, :]], axis=-1)
        a = lax.dot_general(q.astype(MXU_DTYPE), k.astype(MXU_DTYPE), _NT,
                            preferred_element_type=F32)
        attn = jnp.where(level == nlev, a, 0.0)
        for li in range(nlev):
            fl = jnp.concatenate([f[li * c:(li + 1) * c, ch * LANES:(ch + 1) * LANES]
                                  for ch in range(g0, g0 + GLA_GROUP)], axis=0)
            a = lax.dot_general((q * fl).astype(MXU_DTYPE), (k * fl).astype(MXU_DTYPE), _NT,
                                preferred_element_type=F32)
            attn = jnp.where(level == li, a, attn)
        o_intra = jnp.dot(attn.astype(MXU_DTYPE), v, preferred_element_type=F32)

        for ci in range(GLA_GROUP):
            ch = g0 + ci
            r = slice(ci * c, (ci + 1) * c)
            bc_c = bc[:, ch * LANES:(ch + 1) * LANES]
            b_last = bc_c[c - 1:c, :]
            q_dec = (q[r] * jnp.exp(bc_c)).astype(MXU_DTYPE)
            o = o_intra[r] + jnp.dot(q_dec, state.astype(MXU_DTYPE), preferred_element_type=F32)
            k_dec = (k[r] * jnp.exp(b_last - bc_c)).astype(MXU_DTYPE)
            decay = jnp.broadcast_to(jnp.exp(b_last), (LANES, LANES)).T
            state = (state * jnp.concatenate([decay, decay], axis=1)
                     + lax.dot_general(k_dec, v[r], _TN, preferred_element_type=F32))
            ms = jnp.sum(o * o, axis=-1, keepdims=True) * (1.0 / C_DV)
            y = o * lax.rsqrt(ms + RMS_EPS) * out_gain * z[r].astype(F32)
            o_ref[pl.ds(ch * c, c), :] = y.astype(o_ref.dtype)
    st_scr[...] = state


def _gated_linear_attention(proj, gate, gate_up, gate_bias, out_gain, tile=512):
    s = proj.shape[1]
    lvl = jnp.asarray(_gla_level_matrix(), MXU_DTYPE)
    return pl.pallas_call(
        functools.partial(_gla_body, tile=tile),
        grid=(C_HEADS, s // tile),
        in_specs=[
            pl.BlockSpec((1, tile, LANES), lambda h, t: (CQ + h, t, 0)),
            pl.BlockSpec((1, tile, LANES), lambda h, t: (CK + h, t, 0)),
            pl.BlockSpec((2, tile, LANES), lambda h, t: (CV // 2 + h, t, 0)),
            pl.BlockSpec((tile, LANES), lambda h, t: (t, 0)),
            pl.BlockSpec((2, tile, LANES), lambda h, t: (CZ // 2 + h, t, 0)),
            pl.BlockSpec((1, GATE_RANK, C_DK_PAD), lambda h, t: (h, 0, 0)),
            pl.BlockSpec((1, 1, C_DK_PAD), lambda h, t: (h, 0, 0)),
            pl.BlockSpec((1, C_DV_PAD), lambda h, t: (0, 0)),
            pl.BlockSpec(lvl.shape, lambda h, t: (0, 0)),
        ],
        out_specs=pl.BlockSpec((tile, C_DV_PAD), lambda h, t: (t, h)),
        out_shape=jax.ShapeDtypeStruct((s, C_HEADS * C_DV_PAD), MXU_DTYPE),
        scratch_shapes=[pltpu.VMEM((C_DK_PAD, C_DV_PAD), F32)],
        compiler_params=_params(("parallel", "arbitrary")),
        name="gated_linear_attention",
    )(proj, proj, proj, gate, proj, gate_up, gate_bias, out_gain, lvl)


def _out_body(x_ref, a_ref, b_ref, c_ref, wab_ref, wc_ref, o_ref):
    wa = wab_ref[0, 0:A_WIDTH, :].astype(MXU_DTYPE)
    wb = wab_ref[0, A_WIDTH:A_WIDTH + B_WIDTH, :].astype(MXU_DTYPE)
    acc = jnp.dot(a_ref[...], wa, preferred_element_type=F32)
    acc = acc + jnp.dot(b_ref[...], wb, preferred_element_type=F32)
    acc = acc + jnp.dot(c_ref[...], wc_ref[0], preferred_element_type=F32)
    o_ref[...] = x_ref[...] + acc


def _out_proj(x, ma, mb, mc, w_out, wo_c, layer, tm=1024, tn=512):
    s, d = x.shape
    row_blk = lambda m: pl.BlockSpec((tm, m.shape[1]), lambda i, j: (i, 0))
    return pl.pallas_call(
        _out_body,
        grid=(s // tm, d // tn),
        in_specs=[pl.BlockSpec((tm, tn), lambda i, j: (i, j)),
                  row_blk(ma), row_blk(mb), row_blk(mc),
                  pl.BlockSpec((1, A_WIDTH + B_WIDTH, tn), lambda i, j: (layer, 0, j)),
                  pl.BlockSpec((1, mc.shape[1], tn), lambda i, j: (layer, 0, j))],
        out_specs=pl.BlockSpec((tm, tn), lambda i, j: (i, j)),
        out_shape=jax.ShapeDtypeStruct((s, d), F32),
        compiler_params=_params(("parallel", "arbitrary")),
        name="out_proj",
    )(x, ma, mb, mc, w_out, wo_c)


def _pad_heads(w, heads, width, padded):
    lead = w.shape[:-1]
    w = w.reshape(*lead, heads, width)
    w = jnp.pad(w, [(0, 0)] * len(lead) + [(0, 0), (0, padded - width)])
    return w.reshape(*lead, heads * padded)


def _prepare(w_in, w_out, a_q_gain, a_k_gain, b_q_gain, b_k_gain):
    layers, d, _ = w_in.shape
    w_t = jnp.swapaxes(w_in, 1, 2)
    ck, cv = C_HEADS * C_DK, C_HEADS * C_DV
    ab = 4 * (A_WIDTH + B_WIDTH)
    bounds = np.cumsum([ab, B_HEADS, ck, ck, cv, cv, GATE_RANK])
    bf, cq, ckk, cvv, cz, cr = [w_t[:, lo:hi] for lo, hi in zip(bounds[:-1], bounds[1:])]

    def pad_rows(w, width, padded):
        w = w.reshape(layers, C_HEADS, width, d)
        w = jnp.pad(w, ((0, 0), (0, 0), (0, padded - width), (0, 0)))
        return w.reshape(layers, C_HEADS * padded, d)

    pad_k = lambda w: pad_rows(w, C_DK, C_DK_PAD)
    pad_v = lambda w: pad_rows(w, C_DV, C_DV_PAD)
    tail = jnp.zeros((layers, MXU_COLS - GATE_RANK - B_HEADS, d), F32)
    w_c_t = jnp.concatenate([pad_k(cq), pad_k(ckk), pad_v(cvv), pad_v(cz), cr, bf, tail], axis=1)

    q_scale = HEAD_DIM ** -0.5 * LOG2E
    tile_heads = lambda g, n: jnp.tile(g, (1, n))
    ones = lambda n: jnp.ones((layers, n), F32)
    aux_ab = jnp.concatenate([
        tile_heads(a_q_gain * q_scale, A_HEADS), tile_heads(a_k_gain, A_HEADS), ones(2 * A_WIDTH),
        tile_heads(b_q_gain * q_scale, B_HEADS), tile_heads(b_k_gain, B_HEADS), ones(2 * B_WIDTH)],
        axis=-1)[:, None, :]
    aux_c = jnp.concatenate([
        jnp.full((layers, C_HEADS * C_DK_PAD), C_DK ** -0.5, F32),
        ones(C_HEADS * (C_DK_PAD + 2 * C_DV_PAD) + MXU_COLS)], axis=-1)[:, None, :]

    wo_c = w_out[:, A_WIDTH + B_WIDTH:].reshape(layers, C_HEADS, C_DV, d)
    wo_c = jnp.pad(wo_c, ((0, 0), (0, 0), (0, C_DV_PAD - C_DV), (0, 0)))
    wo_c = wo_c.reshape(layers, C_HEADS * C_DV_PAD, d).astype(MXU_DTYPE)
    return w_t, w_c_t, aux_ab, aux_c, wo_c


def _layer(x, layer, norm_g, w_in, w_c, aux_ab, aux_c, w_out, wo_c, fox_bias, fox_shift,
           gla_gate_up, gla_gate_bias, gla_out_gain):
    proj_ab, v4, v16 = _in_proj(x, norm_g, w_in, aux_ab, layer, AB_KINDS, "in_proj_ab")
    proj_c, gate = _in_proj(x, norm_g, w_c, aux_c, layer, C_KINDS, "in_proj_c")
    c = _fox_cumsum(gate[:, GATE_RANK:GATE_RANK + B_HEADS].T, fox_bias)
    mixed_a = _dilated_mixture(proj_ab, (v4, v16))
    mixed_b = _forgetting_attention(proj_ab, c, fox_shift)
    gate_up = _pad_heads(gla_gate_up, C_HEADS, C_DK, C_DK_PAD)
    gate_up = gate_up.reshape(GATE_RANK, C_HEADS, C_DK_PAD).transpose(1, 0, 2)
    gate_bias = _pad_heads(gla_gate_bias, C_HEADS, C_DK, C_DK_PAD).reshape(C_HEADS, 1, C_DK_PAD)
    out_gain = jnp.pad(gla_out_gain, (0, C_DV_PAD - C_DV)).reshape(1, C_DV_PAD)
    mixed_c = _gated_linear_attention(proj_c, gate, gate_up, gate_bias, out_gain)
    return _out_proj(x, mixed_a, mixed_b, mixed_c, w_out, wo_c, layer)


@jax.jit
def kernel(x, norm_g, w_in, a_q_gain, a_k_gain, b_q_gain, b_k_gain, fox_bias, gla_gate_up,
           gla_gate_bias, gla_out_gain, w_out):
    bsz, s, d = x.shape
    assert bsz == 1, "batch size 1 only"
    w_t, w_c, aux_ab, aux_c, wo_c = _prepare(w_in, w_out, a_q_gain, a_k_gain, b_q_gain, b_k_gain)
    norm_g = norm_g[:, None, :]
    fox_shift = (HEAD_DIM ** 0.5 * LOG2E * jnp.max(jnp.abs(b_q_gain), axis=-1)
                 * jnp.max(jnp.abs(b_k_gain), axis=-1))
    y = x.reshape(s, d)
    for layer in range(w_in.shape[0]):
        y = _layer(y, layer, norm_g, w_t, w_c, aux_ab, aux_c, w_out, wo_c, fox_bias[layer],
                   fox_shift[layer], gla_gate_up[layer], gla_gate_bias[layer], gla_out_gain[layer])
    return y.reshape(bsz, s, d)
```

```python
import functools

import numpy as np
import jax
import jax.numpy as jnp
from jax import lax
from jax.experimental import pallas as pl
from jax.experimental.pallas import tpu as pltpu

F32 = jnp.float32
MXU_DTYPE = jnp.bfloat16

LANES = 128
MXU_COLS = 256
HEAD_DIM = 64
A_HEADS, B_HEADS, C_HEADS = 12, 8, 4
A_CHUNKS, B_CHUNKS = A_HEADS // 2, B_HEADS // 2
C_DK, C_DV = 96, 192
C_DK_PAD, C_DV_PAD = 128, 256
GATE_RANK = 16
GATE_TEMP = 16.0
RMS_EPS = 1e-6
LOG2E = 1.4426950408889634
DILATIONS = (1, 4, 16)
BAND = 128
A_TILE = 2048
GLA_CHUNK = 64
GLA_LEVELS = (32, 16, 8, 4, 2, 1)
VMEM_LIMIT = 56 * 1024 * 1024

AQ, AK, AV, AZ = 0, 6, 12, 18
BQ, BK, BV, BZ = 24, 28, 32, 36
CQ, CK, CV, CZ = 0, 4, 8, 16
A_WIDTH, B_WIDTH = A_HEADS * HEAD_DIM, B_HEADS * HEAD_DIM

_NT = (((1,), (1,)), ((), ()))
_TN = (((0,), (0,)), ((), ()))


def _params(sem):
    return pltpu.CompilerParams(dimension_semantics=sem, vmem_limit_bytes=VMEM_LIMIT)


def _log_sigmoid(x):
    return jnp.minimum(x, 0.0) - jnp.log1p(jnp.exp(-jnp.abs(x)))


AB_KINDS = (("headnorm+views",) * (2 * A_CHUNKS) + ("scale+views",) * A_CHUNKS
            + ("silu",) * A_CHUNKS
            + ("headnorm",) * (2 * B_CHUNKS) + ("scale",) * B_CHUNKS + ("silu",) * B_CHUNKS)[::2]
C_KINDS = (("scale",) * (4 * C_HEADS) + ("silu",) * (2 * C_HEADS))[::2] + ("gate",)


def _kind_ranges(kinds, kind):
    runs, start = [], None
    for t, k in enumerate(kinds + (None,)):
        if k == kind and start is None:
            start = t
        elif k != kind and start is not None:
            runs.append((start, t))
            start = None
    return runs


def _view_tiles(kinds):
    n = sum(k.endswith("+views") for k in kinds)
    assert all(k.endswith("+views") for k in kinds[:n])
    return n


def _rmsnorm_body(x_ref, g_ref, o_ref):
    x = x_ref[...]
    ms = jnp.mean(x * x, axis=-1, keepdims=True)
    o_ref[...] = (x * lax.rsqrt(ms + RMS_EPS) * g_ref[0]).astype(o_ref.dtype)


def _rmsnorm(x, norm_g, layer, tm=512):
    s, d = x.shape
    return pl.pallas_call(
        _rmsnorm_body,
        grid=(s // tm,),
        in_specs=[pl.BlockSpec((tm, d), lambda i: (i, 0)),
                  pl.BlockSpec((1, 1, d), lambda i: (layer, 0, 0))],
        out_specs=pl.BlockSpec((tm, d), lambda i: (i, 0)),
        out_shape=jax.ShapeDtypeStruct((s, d), MXU_DTYPE),
        compiler_params=_params(("parallel",)),
        name="rmsnorm",
    )(x, norm_g)


def _proj_body(h_ref, w_ref, aux_ref, *rest, kinds):
    view_tiles = _view_tiles(kinds)
    n_tiles = len(kinds)
    o_ref, rest = rest[0], rest[1:]
    gate_ref = None
    if kinds[-1] == "gate":
        gate_ref, rest = rest[0], rest[1:]
    if view_tiles:
        view_refs, rest = rest[:len(DILATIONS) - 1], rest[len(DILATIONS) - 1:]
        n_stage = MXU_COLS // LANES
        stage, rest = rest[-n_stage:], rest[:-n_stage]
    acc_scr = rest[0:2]
    j = pl.program_id(1)
    tm = h_ref.shape[0]

    def matmul(dst, rows):
        dst[rows, :] = lax.dot_general(h_ref[rows, :], w_ref[0].astype(MXU_DTYPE), _NT,
                                       preferred_element_type=F32)

    def epilogue(src, kind, rows):
        r0, nr = rows.start, rows.stop - rows.start
        if kind == "gate":
            gate_ref[rows, :] = src[rows, 0:LANES]
            return
        base = kind.split("+")[0]
        for c in range(MXU_COLS // LANES):
            cols = slice(c * LANES, (c + 1) * LANES)
            y = src[rows, cols]
            if base == "headnorm":
                first = lax.broadcasted_iota(jnp.int32, y.shape, 1) < HEAD_DIM
                y2 = y * y
                s0 = jnp.sum(jnp.where(first, y2, 0.0), axis=-1, keepdims=True)
                s1 = jnp.sum(jnp.where(first, 0.0, y2), axis=-1, keepdims=True)
                ms = jnp.where(first, s0, s1) * (1.0 / HEAD_DIM)
                y = y * lax.rsqrt(ms + RMS_EPS) * aux_ref[0, :, cols]
            elif base == "scale":
                y = y * aux_ref[0, :, cols]
            elif base == "silu":
                y = y * jax.nn.sigmoid(y)
            o_ref[c, rows, :] = y.astype(o_ref.dtype)
            if not kind.endswith("+views"):
                continue
            stage[c][rows, :] = y
            for view, dil in zip(view_refs, DILATIONS[1:]):
                for r in range(dil):
                    picked = stage[c][pl.ds(r0 + r, nr // dil, stride=dil), :]
                    view[c, r0 // dil:(r0 + nr) // dil, r * LANES:(r + 1) * LANES] = (
                        picked.astype(view.dtype))

    halves = tuple(slice(r, r + tm // 4) for r in range(0, tm, tm // 4))

    @pl.when(j == 0)
    def _():
        for rows in halves:
            matmul(acc_scr[0], rows)

    for parity in range(2):
        for kind in sorted(set(kinds[:-1])):
            in_kind = functools.reduce(
                jnp.logical_or, [(j - 1 >= lo) & (j - 1 < min(hi, n_tiles - 1))
                                 for lo, hi in _kind_ranges(kinds, kind) if lo < n_tiles - 1])

            @pl.when((j >= 1) & (j < n_tiles) & (j % 2 == parity) & in_kind)
            def _(parity=parity, kind=kind):
                for rows in halves:
                    epilogue(acc_scr[1 - parity], kind, rows)
                    matmul(acc_scr[parity], rows)

    @pl.when(j == n_tiles)
    def _():
        for rows in halves:
            epilogue(acc_scr[(n_tiles - 1) % 2], kinds[-1], rows)


def _in_proj(h, w_t, aux, layer, kinds, name, tm=2048):
    s, d = h.shape
    n_tiles = len(kinds)
    view_tiles = _view_tiles(kinds)
    chunks = MXU_COLS // LANES
    has_gate = kinds[-1] == "gate"
    out_tiles = n_tiles - 1 if has_gate else n_tiles
    prev = lambda j, n: jnp.clip(j - 1, 0, n - 1)
    out_shape = [jax.ShapeDtypeStruct((out_tiles * chunks, s, LANES), MXU_DTYPE)]
    out_specs = [pl.BlockSpec((chunks, tm, LANES), lambda i, j: (prev(j, out_tiles), i, 0))]
    scratch = [pltpu.VMEM((tm, MXU_COLS), F32), pltpu.VMEM((tm, MXU_COLS), F32)]
    if has_gate:
        out_shape.append(jax.ShapeDtypeStruct((s, LANES), F32))
        out_specs.append(pl.BlockSpec((tm, LANES), lambda i, j: (i, 0)))
    if view_tiles:
        for dil in DILATIONS[1:]:
            out_shape.append(
                jax.ShapeDtypeStruct((view_tiles * chunks, s // dil, dil * LANES), MXU_DTYPE))
            out_specs.append(pl.BlockSpec((chunks, tm // dil, dil * LANES),
                                          lambda i, j: (prev(j, view_tiles), i, 0)))
        scratch += [pltpu.VMEM((tm, LANES), F32)] * chunks
    return pl.pallas_call(
        functools.partial(_proj_body, kinds=kinds),
        grid=(s // tm, n_tiles + 1),
        in_specs=[pl.BlockSpec((tm, d), lambda i, j: (i, 0)),
                  pl.BlockSpec((1, MXU_COLS, d),
                               lambda i, j: (layer, jnp.minimum(j, n_tiles - 1), 0)),
                  pl.BlockSpec((1, 1, MXU_COLS), lambda i, j: (layer, 0, prev(j, n_tiles)))],
        out_specs=out_specs,
        out_shape=out_shape,
        scratch_shapes=scratch,
        compiler_params=_params(("parallel", "arbitrary")),
        name=name,
    )(h, w_t, aux)


def _fox_cumsum_body(x_ref, b_ref, o_ref):
    x = _log_sigmoid(x_ref[...] + b_ref[...])
    idx = lax.broadcasted_iota(jnp.int32, x.shape, 1)
    shift = 1
    while shift < x.shape[1]:
        x = x + jnp.where(idx >= shift, pltpu.roll(x, shift, axis=1), 0.0)
        shift *= 2
    o_ref[...] = x * LOG2E


def _fox_cumsum(logit_t, bias):
    nh, s = logit_t.shape
    return pl.pallas_call(
        _fox_cumsum_body,
        out_shape=jax.ShapeDtypeStruct((nh, s), F32),
        compiler_params=pltpu.CompilerParams(vmem_limit_bytes=VMEM_LIMIT),
        name="fox_cumsum",
    )(logit_t, bias.reshape(nh, 1))


A_GROUP = 4


def _band_attention(blocks):
    row = lax.broadcasted_iota(jnp.int32, (BAND, 2 * BAND), 0)
    col = lax.broadcasted_iota(jnp.int32, (BAND, 2 * BAND), 1)
    band = (col >= row) & (col <= row + BAND)
    first = lax.broadcasted_iota(jnp.int32, (BAND, LANES), 1) < HEAD_DIM
    first_kv = lax.broadcasted_iota(jnp.int32, (2 * BAND, LANES), 1) < HEAD_DIM
    chains = [(b, h) for b in range(len(blocks)) for h in range(2)]
    logits = []
    for b, h in chains:
        q, k2 = blocks[b][0], blocks[b][1]
        zero = jnp.zeros_like(q)
        qm = jnp.where(first, q, zero) if h == 0 else jnp.where(first, zero, q)
        logits.append(lax.dot_general(qm, k2, _NT, preferred_element_type=F32))
    masks = [band & (col + blk[3] >= 0) for blk in blocks]
    logits = [jnp.where(masks[b], s, -jnp.inf) for (b, h), s in zip(chains, logits)]
    maxes = [jnp.max(s, axis=-1, keepdims=True) for s in logits]
    probs = [jnp.exp2(s - m).astype(MXU_DTYPE) for s, m in zip(logits, maxes)]
    res = []
    for (b, h), p in zip(chains, probs):
        v2 = blocks[b][2]
        one = jnp.ones_like(v2)
        vh = jnp.where(first_kv, v2, one) if h == 0 else jnp.where(first_kv, one, v2)
        res.append(jnp.dot(p, vh, preferred_element_type=F32))
    results = []
    for b in range(len(blocks)):
        r0, r1 = res[2 * b], res[2 * b + 1]
        num = jnp.where(first, r0, r1)
        den = pltpu.roll(jnp.where(first, r1, r0), HEAD_DIM, axis=1)
        m = jnp.where(first, maxes[2 * b], maxes[2 * b + 1])
        results.append((num / den, m + jnp.log2(den)))
    return results


def _dilated_body(*refs):
    ins, z_ref, o_ref, scr = refs[:15], refs[15], refs[16], refs[17:]
    kv_scr, o_scr, l_scr = scr[:6], scr[6], scr[7]
    n = pl.program_id(1)
    for bi, d in enumerate(DILATIONS):
        q_ref, kc, kp, vc, vp = ins[5 * bi:5 * bi + 5]
        kf, vf = kv_scr[2 * bi:2 * bi + 2]
        rows = A_TILE // d
        nb = rows // BAND
        for cur, prev, full in ((kc, kp, kf), (vc, vp, vf)):
            full[0:BAND, :] = prev[0, rows - BAND:rows, :]
            full[BAND:BAND + rows, :] = cur[0]

        def group(members, q_ref=q_ref, kf=kf, vf=vf, d=d, nb=nb, bi=bi):
            blocks = []
            for j, r in members:
                cols = slice(r * LANES, (r + 1) * LANES)
                row0 = j * BAND if isinstance(j, int) else pl.multiple_of(j * BAND, BAND)
                blocks.append((q_ref[0, pl.ds(row0, BAND), cols], kf[pl.ds(row0, 2 * BAND), cols],
                               vf[pl.ds(row0, 2 * BAND), cols], (n * nb + j - 1) * BAND))
            for (j, r), (o, lse) in zip(members, _band_attention(blocks)):
                dst = pl.ds(j * BAND * d + r, BAND, stride=d)
                o_scr[bi, dst, :] = o
                l_scr[bi, dst, :] = lse

        if nb >= A_GROUP:
            for r in range(d):
                def body(g, carry, r=r, group=group):
                    j0 = pl.multiple_of(g * A_GROUP, A_GROUP)
                    group([(j0 + i, r) for i in range(A_GROUP)])
                    return carry
                lax.fori_loop(0, nb // A_GROUP, body, 0)
        else:
            for r0 in range(0, d, A_GROUP // nb):
                group([(j, r0 + i) for i in range(A_GROUP // nb) for j in range(nb)])
    lse = l_scr[...]
    w = jnp.exp2(lse - jnp.max(lse, axis=0, keepdims=True))
    mixed = jnp.sum(w * o_scr[...], axis=0) / jnp.sum(w, axis=0)
    o_ref[...] = (mixed * z_ref[0].astype(F32)).astype(o_ref.dtype)


def _dilated_mixture(proj, views):
    s = proj.shape[1]
    operands, in_specs, scratch = [], [], []
    for d, view in zip(DILATIONS, (proj,) + tuple(views)):
        rows, width = A_TILE // d, d * LANES
        blk = (1, rows, width)
        operands += [view] * 5
        in_specs += [
            pl.BlockSpec(blk, lambda c, n: (AQ + c, n, 0)),
            pl.BlockSpec(blk, lambda c, n: (AK + c, n, 0)),
            pl.BlockSpec(blk, lambda c, n: (AK + c, jnp.maximum(n - 1, 0), 0)),
            pl.BlockSpec(blk, lambda c, n: (AV + c, n, 0)),
            pl.BlockSpec(blk, lambda c, n: (AV + c, jnp.maximum(n - 1, 0), 0)),
        ]
        scratch += [pltpu.VMEM((rows + BAND, width), MXU_DTYPE)] * 2
    scratch += [pltpu.VMEM((len(DILATIONS), A_TILE, LANES), F32)] * 2
    operands.append(proj)
    in_specs.append(pl.BlockSpec((1, A_TILE, LANES), lambda c, n: (AZ + c, n, 0)))
    return pl.pallas_call(
        _dilated_body,
        grid=(A_CHUNKS, s // A_TILE),
        in_specs=in_specs,
        out_specs=pl.BlockSpec((A_TILE, LANES), lambda c, n: (n, c)),
        out_shape=jax.ShapeDtypeStruct((s, A_WIDTH), MXU_DTYPE),
        scratch_shapes=scratch,
        compiler_params=_params(("parallel", "arbitrary")),
        name="dilated_mixture",
    )(*operands)


FOX_AUG = 3
FOX_VROWS = HEAD_DIM + 16
FOX_DEN_MIN, FOX_DEN_MAX = 2.0 ** -90, 2.0 ** 100


def _split3(c):
    hi = c.astype(jnp.bfloat16).astype(F32)
    r = c - hi
    mid = r.astype(jnp.bfloat16).astype(F32)
    return hi, mid, r - mid


def _fox_prep_body(q_ref, k_ref, v_ref, ccol_ref, crow_ref, qt_ref, ka_ref, vt_ref):
    tm = q_ref.shape[1]
    q_t = q_ref[0].astype(F32).T
    v_t = v_ref[0].astype(F32).T
    k = k_ref[0]
    sub = lax.broadcasted_iota(jnp.int32, (HEAD_DIM, tm), 0)
    lane = lax.broadcasted_iota(jnp.int32, (tm, LANES), 1)
    v_tail = (lax.broadcasted_iota(jnp.int32, (FOX_VROWS - HEAD_DIM, tm), 0) == 0).astype(F32)
    for h in range(2):
        hi, mid, lo = _split3(crow_ref[0, h:h + 1, :])
        aug_q = jnp.where(sub == 0, hi, jnp.where(sub == 1, mid, jnp.where(
            sub == 2, lo, jnp.where(sub < 2 * FOX_AUG, 1.0, 0.0))))
        q_h = q_t[h * HEAD_DIM:(h + 1) * HEAD_DIM]
        parts = [q_h, aug_q] if h == 0 else [aug_q, q_h]
        qt_ref[h] = jnp.concatenate(parts, axis=0).astype(qt_ref.dtype)
        hi, mid, lo = _split3(ccol_ref[0, :, h:h + 1])
        a0 = (1 - h) * HEAD_DIM
        aug_k = jnp.where(lane == a0 + FOX_AUG, -hi, jnp.where(lane == a0 + FOX_AUG + 1, -mid, jnp.where(
            lane == a0 + FOX_AUG + 2, -lo, jnp.where((lane >= a0) & (lane < a0 + FOX_AUG), 1.0, 0.0))))
        own = (lane < HEAD_DIM) if h == 0 else (lane >= HEAD_DIM)
        ka_ref[h] = jnp.where(own, k, aug_k.astype(k.dtype))
        vt_ref[h, 0] = jnp.concatenate([v_t[h * HEAD_DIM:(h + 1) * HEAD_DIM], v_tail],
                                       axis=0).astype(vt_ref.dtype)


def _fox_prep(proj, c, shift, tm):
    s = proj.shape[1]
    c_col = c.reshape(B_CHUNKS, 2, s).transpose(0, 2, 1)
    c_row = c.reshape(B_CHUNKS, 2, s) - shift
    return pl.pallas_call(
        _fox_prep_body,
        grid=(B_CHUNKS, s // tm),
        in_specs=[
            pl.BlockSpec((1, tm, LANES), lambda p, i: (BQ + p, i, 0)),
            pl.BlockSpec((1, tm, LANES), lambda p, i: (BK + p, i, 0)),
            pl.BlockSpec((1, tm, LANES), lambda p, i: (BV + p, i, 0)),
            pl.BlockSpec((1, tm, 2), lambda p, i: (p, i, 0)),
            pl.BlockSpec((1, 2, tm), lambda p, i: (p, 0, i)),
        ],
        out_specs=[
            pl.BlockSpec((2, LANES, tm), lambda p, i: (p, 0, i)),
            pl.BlockSpec((2, tm, LANES), lambda p, i: (p, i, 0)),
            pl.BlockSpec((2, 1, FOX_VROWS, tm), lambda p, i: (p, i, 0, 0)),
        ],
        out_shape=[
            jax.ShapeDtypeStruct((B_HEADS, LANES, s), MXU_DTYPE),
            jax.ShapeDtypeStruct((B_HEADS, s, LANES), MXU_DTYPE),
            jax.ShapeDtypeStruct((B_HEADS, s // tm, FOX_VROWS, tm), MXU_DTYPE),
        ],
        compiler_params=_params(("parallel", "parallel")),
        name="fox_prep",
    )(proj, proj, proj, c_col, c_row)


def _fox_body(qt_ref, ka_ref, vt_ref, z_ref, o_ref, s0_scr, s1_scr, m_scr, acc_scr, *, blk):
    qi = pl.program_id(1)
    causal = (lax.broadcasted_iota(jnp.int32, (blk, blk), 0)
              <= lax.broadcasted_iota(jnp.int32, (blk, blk), 1))
    q_t = (qt_ref[0], qt_ref[1])

    def logits(kb, dst):
        k0 = pl.multiple_of(kb * blk, blk)
        for h in range(2):
            dst[h] = jnp.dot(ka_ref[h, pl.ds(k0, blk), :], q_t[h], preferred_element_type=F32)

    def accumulate_shifted(kb, src, diagonal=False):
        for h in range(2):
            s = src[h]
            if diagonal:
                s = jnp.where(causal, s, -jnp.inf)
            acc_scr[h] += jnp.dot(vt_ref[h, kb], jnp.exp2(s).astype(MXU_DTYPE),
                                  preferred_element_type=F32)

    def accumulate_online(kb, src, diagonal=False):
        for h in range(2):
            s = src[h]
            if diagonal:
                s = jnp.where(causal, s, -jnp.inf)
            m = m_scr[h]
            m_new = jnp.maximum(m, jnp.max(s, axis=0, keepdims=True))
            p = jnp.exp2(s - m_new).astype(MXU_DTYPE)
            acc_scr[h] = (jnp.exp2(m - m_new) * acc_scr[h]
                          + jnp.dot(vt_ref[h, kb], p, preferred_element_type=F32))
            m_scr[h] = m_new

    def attend(accumulate):
        acc_scr[...] = jnp.zeros(acc_scr.shape, F32)
        logits(0, s0_scr)

        def pair(j, carry):
            logits(2 * j + 1, s1_scr)
            accumulate(2 * j, s0_scr)
            logits(2 * j + 2, s0_scr)
            accumulate(2 * j + 1, s1_scr)
            return carry

        lax.fori_loop(0, qi // 2, pair, 0)

        @pl.when(qi % 2 == 0)
        def _():
            accumulate(qi, s0_scr, diagonal=True)

        @pl.when(qi % 2 == 1)
        def _():
            logits(qi, s1_scr)
            accumulate(qi - 1, s0_scr)
            accumulate(qi, s1_scr, diagonal=True)

        outs = [acc_scr[h, 0:HEAD_DIM] / acc_scr[h, HEAD_DIM:HEAD_DIM + 1] for h in range(2)]
        o = jnp.concatenate(outs, axis=0).T
        o_ref[...] = (o * z_ref[0].astype(F32)).astype(o_ref.dtype)

    attend(accumulate_shifted)
    den = jnp.concatenate([acc_scr[h, HEAD_DIM:HEAD_DIM + 1] for h in range(2)], axis=0)
    in_range = (jnp.min(den) >= FOX_DEN_MIN) & (jnp.max(den) <= FOX_DEN_MAX)

    @pl.when(jnp.logical_not(in_range))
    def _():
        m_scr[...] = jnp.full(m_scr.shape, -jnp.inf, F32)
        attend(accumulate_online)


def _forgetting_attention(proj, c, shift, blk=512):
    s = proj.shape[1]
    qt, ka, vt = _fox_prep(proj, c, shift, blk)
    return pl.pallas_call(
        functools.partial(_fox_body, blk=blk),
        grid=(B_CHUNKS, s // blk),
        in_specs=[
            pl.BlockSpec((2, LANES, blk), lambda p, i: (p, 0, i)),
            pl.BlockSpec((2, s, LANES), lambda p, i: (p, 0, 0)),
            pl.BlockSpec((2, s // blk, FOX_VROWS, blk), lambda p, i: (p, 0, 0, 0)),
            pl.BlockSpec((1, blk, LANES), lambda p, i: (BZ + p, i, 0)),
        ],
        out_specs=pl.BlockSpec((blk, LANES), lambda p, i: (i, p)),
        out_shape=jax.ShapeDtypeStruct((s, B_WIDTH), MXU_DTYPE),
        scratch_shapes=[pltpu.VMEM((2, blk, blk), F32), pltpu.VMEM((2, blk, blk), F32),
                        pltpu.VMEM((2, 1, blk), F32), pltpu.VMEM((2, FOX_VROWS, blk), F32)],
        compiler_params=_params(("parallel", "arbitrary")),
        name="forgetting_attention",
    )(qt, ka, vt, proj)


GLA_GROUP = 4


def _gla_level_matrix():
    c = GLA_CHUNK
    mats = []
    for b in GLA_LEVELS:
        m = np.zeros((c, c), np.float32)
        for i in range(c):
            pivot = (i // (2 * b)) * 2 * b + b - 1
            if i > pivot:
                m[i, pivot + 1:i + 1] = 1.0
            else:
                m[i, i + 1:pivot + 1] = 1.0
        mats.append(m)
    mats.append(np.tril(np.ones((c, c), np.float32)))
    return np.concatenate(mats, axis=0)


def _gla_body(q_ref, k_ref, v_ref, g_ref, z_ref, gu_ref, gb_ref, og_ref, lvl_ref, o_ref,
              st_scr, *, tile):
    c = GLA_CHUNK
    nlev = len(GLA_LEVELS)
    nch = tile // c
    grp = GLA_GROUP * c

    @pl.when(pl.program_id(1) == 0)
    def _():
        st_scr[...] = jnp.zeros_like(st_scr)

    gate_in = g_ref[:, 0:GATE_RANK]
    gate_hi = gate_in.astype(MXU_DTYPE)
    gate_lo = (gate_in - gate_hi.astype(F32)).astype(MXU_DTYPE)
    gate_w = gu_ref[0].astype(MXU_DTYPE)
    logit = (jnp.dot(gate_hi, gate_w, preferred_element_type=F32)
             + jnp.dot(gate_lo, gate_w, preferred_element_type=F32) + gb_ref[0])
    la = _log_sigmoid(logit) * (1.0 / GATE_TEMP)

    la_cat = jnp.concatenate([la[ci * c:(ci + 1) * c] for ci in range(nch)], axis=1)
    la_hi = la_cat.astype(MXU_DTYPE)
    la_lo = (la_cat - la_hi.astype(F32)).astype(MXU_DTYPE)
    f = jnp.exp(jnp.dot(lvl_ref[0:nlev * c, :], la_hi, preferred_element_type=F32))
    tri = lvl_ref[nlev * c:(nlev + 1) * c, :]
    bc = (jnp.dot(tri, la_hi, preferred_element_type=F32)
          + jnp.dot(tri, la_lo, preferred_element_type=F32))

    row = lax.broadcasted_iota(jnp.int32, (grp, grp), 0)
    col = lax.broadcasted_iota(jnp.int32, (grp, grp), 1)
    diff = (row ^ col) & (c - 1)
    level = jnp.full((grp, grp), nlev, jnp.int32)
    for li, b in reversed(list(enumerate(GLA_LEVELS))):
        level = jnp.where(diff >= b, li, level)
    level = jnp.where(((row & -c) == (col & -c)) & (col <= row), level, -1)
    out_gain = og_ref[...]

    state = st_scr[...]
    for g0 in range(0, nch, GLA_GROUP):
        rows = slice(g0 * c, g0 * c + grp)
        q = q_ref[0, rows, :].astype(F32)
        k = k_ref[0, rows, :].astype(F32)
        v = jnp.concatenate([v_ref[0, rows, :], v_ref[1, rows, :]], axis=-1)
        z = jnp.concatenate([z_ref[0, rows, :], z_ref[1, rows, :]], axis=-1)
        a = lax.dot_general(q.astype(MXU_DTYPE), k.astype(MXU_DTYPE), _NT,
                            preferred_element_type=F32)
        attn = jnp.where(level == nlev, a, 0.0)
        for li in range(nlev):
            fl = jnp.concatenate([f[li * c:(li + 1) * c, ch * LANES:(ch + 1) * LANES]
                                  for ch in range(g0, g0 + GLA_GROUP)], axis=0)
            a = lax.dot_general((q * fl).astype(MXU_DTYPE), (k * fl).astype(MXU_DTYPE), _NT,
                                preferred_element_type=F32)
            attn = jnp.where(level == li, a, attn)
        o_intra = jnp.dot(attn.astype(MXU_DTYPE), v, preferred_element_type=F32)

        for ci in range(GLA_GROUP):
            ch = g0 + ci
            r = slice(ci * c, (ci + 1) * c)
            bc_c = bc[:, ch * LANES:(ch + 1) * LANES]
            b_last = bc_c[c - 1:c, :]
            q_dec = (q[r] * jnp.exp(bc_c)).astype(MXU_DTYPE)
            o = o_intra[r] + jnp.dot(q_dec, state.astype(MXU_DTYPE), preferred_element_type=F32)
            k_dec = (k[r] * jnp.exp(b_last - bc_c)).astype(MXU_DTYPE)
            decay = jnp.broadcast_to(jnp.exp(b_last), (LANES, LANES)).T
            state = (state * jnp.concatenate([decay, decay], axis=1)
                     + lax.dot_general(k_dec, v[r], _TN, preferred_element_type=F32))
            ms = jnp.sum(o * o, axis=-1, keepdims=True) * (1.0 / C_DV)
            y = o * lax.rsqrt(ms + RMS_EPS) * out_gain * z[r].astype(F32)
            o_ref[pl.ds(ch * c, c), :] = y.astype(o_ref.dtype)
    st_scr[...] = state


def _gated_linear_attention(proj, gate, gate_up, gate_bias, out_gain, tile=512):
    s = proj.shape[1]
    lvl = jnp.asarray(_gla_level_matrix(), MXU_DTYPE)
    return pl.pallas_call(
        functools.partial(_gla_body, tile=tile),
        grid=(C_HEADS, s // tile),
        in_specs=[
            pl.BlockSpec((1, tile, LANES), lambda h, t: (CQ + h, t, 0)),
            pl.BlockSpec((1, tile, LANES), lambda h, t: (CK + h, t, 0)),
            pl.BlockSpec((2, tile, LANES), lambda h, t: (CV // 2 + h, t, 0)),
            pl.BlockSpec((tile, LANES), lambda h, t: (t, 0)),
            pl.BlockSpec((2, tile, LANES), lambda h, t: (CZ // 2 + h, t, 0)),
            pl.BlockSpec((1, GATE_RANK, C_DK_PAD), lambda h, t: (h, 0, 0)),
            pl.BlockSpec((1, 1, C_DK_PAD), lambda h, t: (h, 0, 0)),
            pl.BlockSpec((1, C_DV_PAD), lambda h, t: (0, 0)),
            pl.BlockSpec(lvl.shape, lambda h, t: (0, 0)),
        ],
        out_specs=pl.BlockSpec((tile, C_DV_PAD), lambda h, t: (t, h)),
        out_shape=jax.ShapeDtypeStruct((s, C_HEADS * C_DV_PAD), MXU_DTYPE),
        scratch_shapes=[pltpu.VMEM((C_DK_PAD, C_DV_PAD), F32)],
        compiler_params=_params(("parallel", "arbitrary")),
        name="gated_linear_attention",
    )(proj, proj, proj, gate, proj, gate_up, gate_bias, out_gain, lvl)


def _out_body(x_ref, a_ref, b_ref, c_ref, wab_ref, wc_ref, o_ref):
    wa = wab_ref[0, 0:A_WIDTH, :].astype(MXU_DTYPE)
    wb = wab_ref[0, A_WIDTH:A_WIDTH + B_WIDTH, :].astype(MXU_DTYPE)
    acc = jnp.dot(a_ref[...], wa, preferred_element_type=F32)
    acc = acc + jnp.dot(b_ref[...], wb, preferred_element_type=F32)
    acc = acc + jnp.dot(c_ref[...], wc_ref[0], preferred_element_type=F32)
    o_ref[...] = x_ref[...] + acc


def _out_proj(x, ma, mb, mc, w_out, wo_c, layer, tm=1024, tn=512):
    s, d = x.shape
    row_blk = lambda m: pl.BlockSpec((tm, m.shape[1]), lambda i, j: (i, 0))
    return pl.pallas_call(
        _out_body,
        grid=(s // tm, d // tn),
        in_specs=[pl.BlockSpec((tm, tn), lambda i, j: (i, j)),
                  row_blk(ma), row_blk(mb), row_blk(mc),
                  pl.BlockSpec((1, A_WIDTH + B_WIDTH, tn), lambda i, j: (layer, 0, j)),
                  pl.BlockSpec((1, mc.shape[1], tn), lambda i, j: (layer, 0, j))],
        out_specs=pl.BlockSpec((tm, tn), lambda i, j: (i, j)),
        out_shape=jax.ShapeDtypeStruct((s, d), F32),
        compiler_params=_params(("parallel", "arbitrary")),
        name="out_proj",
    )(x, ma, mb, mc, w_out, wo_c)


def _pad_heads(w, heads, width, padded):
    lead = w.shape[:-1]
    w = w.reshape(*lead, heads, width)
    w = jnp.pad(w, [(0, 0)] * len(lead) + [(0, 0), (0, padded - width)])
    return w.reshape(*lead, heads * padded)


def _prepare(w_in, w_out, a_q_gain, a_k_gain, b_q_gain, b_k_gain):
    layers, d, _ = w_in.shape
    w_t = jnp.swapaxes(w_in, 1, 2)
    ck, cv = C_HEADS * C_DK, C_HEADS * C_DV
    ab = 4 * (A_WIDTH + B_WIDTH)
    bounds = np.cumsum([ab, B_HEADS, ck, ck, cv, cv, GATE_RANK])
    bf, cq, ckk, cvv, cz, cr = [w_t[:, lo:hi] for lo, hi in zip(bounds[:-1], bounds[1:])]

    def pad_rows(w, width, padded):
        w = w.reshape(layers, C_HEADS, width, d)
        w = jnp.pad(w, ((0, 0), (0, 0), (0, padded - width), (0, 0)))
        return w.reshape(layers, C_HEADS * padded, d)

    pad_k = lambda w: pad_rows(w, C_DK, C_DK_PAD)
    pad_v = lambda w: pad_rows(w, C_DV, C_DV_PAD)
    tail = jnp.zeros((layers, MXU_COLS - GATE_RANK - B_HEADS, d), F32)
    w_c_t = jnp.concatenate([pad_k(cq), pad_k(ckk), pad_v(cvv), pad_v(cz), cr, bf, tail], axis=1)

    q_scale = HEAD_DIM ** -0.5 * LOG2E
    tile_heads = lambda g, n: jnp.tile(g, (1, n))
    ones = lambda n: jnp.ones((layers, n), F32)
    aux_ab = jnp.concatenate([
        tile_heads(a_q_gain * q_scale, A_HEADS), tile_heads(a_k_gain, A_HEADS), ones(2 * A_WIDTH),
        tile_heads(b_q_gain * q_scale, B_HEADS), tile_heads(b_k_gain, B_HEADS), ones(2 * B_WIDTH)],
        axis=-1)[:, None, :]
    aux_c = jnp.concatenate([
        jnp.full((layers, C_HEADS * C_DK_PAD), C_DK ** -0.5, F32),
        ones(C_HEADS * (C_DK_PAD + 2 * C_DV_PAD) + MXU_COLS)], axis=-1)[:, None, :]

    wo_c = w_out[:, A_WIDTH + B_WIDTH:].reshape(layers, C_HEADS, C_DV, d)
    wo_c = jnp.pad(wo_c, ((0, 0), (0, 0), (0, C_DV_PAD - C_DV), (0, 0)))
    wo_c = wo_c.reshape(layers, C_HEADS * C_DV_PAD, d).astype(MXU_DTYPE)
    return w_t, w_c_t, aux_ab, aux_c, wo_c


def _layer(x, layer, norm_g, w_in, w_c, aux_ab, aux_c, w_out, wo_c, fox_bias, fox_shift,
           gla_gate_up, gla_gate_bias, gla_out_gain):
    h = _rmsnorm(x, norm_g, layer)
    proj_ab, v4, v16 = _in_proj(h, w_in, aux_ab, layer, AB_KINDS, "in_proj_ab")
    proj_c, gate = _in_proj(h, w_c, aux_c, layer, C_KINDS, "in_proj_c")
    c = _fox_cumsum(gate[:, GATE_RANK:GATE_RANK + B_HEADS].T, fox_bias)
    mixed_a = _dilated_mixture(proj_ab, (v4, v16))
    mixed_b = _forgetting_attention(proj_ab, c, fox_shift)
    gate_up = _pad_heads(gla_gate_up, C_HEADS, C_DK, C_DK_PAD)
    gate_up = gate_up.reshape(GATE_RANK, C_HEADS, C_DK_PAD).transpose(1, 0, 2)
    gate_bias = _pad_heads(gla_gate_bias, C_HEADS, C_DK, C_DK_PAD).reshape(C_HEADS, 1, C_DK_PAD)
    out_gain = jnp.pad(gla_out_gain, (0, C_DV_PAD - C_DV)).reshape(1, C_DV_PAD)
    mixed_c = _gated_linear_attention(proj_c, gate, gate_up, gate_bias, out_gain)
    return _out_proj(x, mixed_a, mixed_b, mixed_c, w_out, wo_c, layer)


@jax.jit
def kernel(x, norm_g, w_in, a_q_gain, a_k_gain, b_q_gain, b_k_gain, fox_bias, gla_gate_up,
           gla_gate_bias, gla_out_gain, w_out):
    bsz, s, d = x.shape
    assert bsz == 1, "batch size 1 only"
    w_t, w_c, aux_ab, aux_c, wo_c = _prepare(w_in, w_out, a_q_gain, a_k_gain, b_q_gain, b_k_gain)
    norm_g = norm_g[:, None, :]
    fox_shift = (HEAD_DIM ** 0.5 * LOG2E * jnp.max(jnp.abs(b_q_gain), axis=-1)
                 * jnp.max(jnp.abs(b_k_gain), axis=-1))
    y = x.reshape(s, d)
    for layer in range(w_in.shape[0]):
        y = _layer(y, layer, norm_g, w_t, w_c, aux_ab, aux_c, w_out, wo_c, fox_bias[layer],
                   fox_shift[layer], gla_gate_up[layer], gla_gate_bias[layer], gla_out_gain[layer])
    return y.reshape(bsz, s, d)
```

```python
import functools

import numpy as np
import jax
import jax.numpy as jnp
from jax import lax
from jax.experimental import pallas as pl
from jax.experimental.pallas import tpu as pltpu

F32 = jnp.float32
MXU_DTYPE = jnp.bfloat16

LANES = 128
MXU_COLS = 256
HEAD_DIM = 64
A_HEADS, B_HEADS, C_HEADS = 12, 8, 4
A_CHUNKS, B_CHUNKS = A_HEADS // 2, B_HEADS // 2
C_DK, C_DV = 96, 192
C_DK_PAD, C_DV_PAD = 128, 256
GATE_RANK = 16
GATE_TEMP = 16.0
RMS_EPS = 1e-6
LOG2E = 1.4426950408889634
DILATIONS = (1, 4, 16)
BAND = 128
A_TILE = 2048
GLA_CHUNK = 64
GLA_LEVELS = (32, 16, 8, 4, 2, 1)
VMEM_LIMIT = 56 * 1024 * 1024

AQ, AK, AV, AZ = 0, 6, 12, 18
BQ, BK, BV, BZ = 24, 28, 32, 36
CQ, CK, CV, CZ = 0, 4, 8, 16
A_WIDTH, B_WIDTH = A_HEADS * HEAD_DIM, B_HEADS * HEAD_DIM

_NT = (((1,), (1,)), ((), ()))
_TN = (((0,), (0,)), ((), ()))


def _params(sem):
    return pltpu.CompilerParams(dimension_semantics=sem, vmem_limit_bytes=VMEM_LIMIT)


def _log_sigmoid(x):
    return jnp.minimum(x, 0.0) - jnp.log1p(jnp.exp(-jnp.abs(x)))


AB_KINDS = (("headnorm+views",) * (2 * A_CHUNKS) + ("scale+views",) * A_CHUNKS
            + ("silu",) * A_CHUNKS
            + ("headnorm",) * (2 * B_CHUNKS) + ("scale",) * B_CHUNKS + ("silu",) * B_CHUNKS)[::2]
C_KINDS = (("scale",) * (4 * C_HEADS) + ("silu",) * (2 * C_HEADS))[::2] + ("gate",)


def _kind_ranges(kinds, kind):
    runs, start = [], None
    for t, k in enumerate(kinds + (None,)):
        if k == kind and start is None:
            start = t
        elif k != kind and start is not None:
            runs.append((start, t))
            start = None
    return runs


def _view_tiles(kinds):
    n = sum(k.endswith("+views") for k in kinds)
    assert all(k.endswith("+views") for k in kinds[:n])
    return n


def _rmsnorm_body(x_ref, g_ref, o_ref):
    x = x_ref[...]
    ms = jnp.mean(x * x, axis=-1, keepdims=True)
    o_ref[...] = (x * lax.rsqrt(ms + RMS_EPS) * g_ref[0]).astype(o_ref.dtype)


def _rmsnorm(x, norm_g, layer, tm=512):
    s, d = x.shape
    return pl.pallas_call(
        _rmsnorm_body,
        grid=(s // tm,),
        in_specs=[pl.BlockSpec((tm, d), lambda i: (i, 0)),
                  pl.BlockSpec((1, 1, d), lambda i: (layer, 0, 0))],
        out_specs=pl.BlockSpec((tm, d), lambda i: (i, 0)),
        out_shape=jax.ShapeDtypeStruct((s, d), MXU_DTYPE),
        compiler_params=_params(("parallel",)),
        name="rmsnorm",
    )(x, norm_g)


def _proj_body(h_ref, w_ref, aux_ref, *rest, kinds):
    view_tiles = _view_tiles(kinds)
    n_tiles = len(kinds)
    o_ref, rest = rest[0], rest[1:]
    gate_ref = None
    if kinds[-1] == "gate":
        gate_ref, rest = rest[0], rest[1:]
    if view_tiles:
        view_refs, rest = rest[:len(DILATIONS) - 1], rest[len(DILATIONS) - 1:]
        n_stage = MXU_COLS // LANES
        stage, rest = rest[-n_stage:], rest[:-n_stage]
    acc_scr = rest[0:2]
    j = pl.program_id(1)
    tm = h_ref.shape[0]

    def matmul(dst, rows):
        dst[rows, :] = lax.dot_general(h_ref[rows, :], w_ref[0].astype(MXU_DTYPE), _NT,
                                       preferred_element_type=F32)

    def epilogue(src, kind, rows):
        r0, nr = rows.start, rows.stop - rows.start
        if kind == "gate":
            gate_ref[rows, :] = src[rows, 0:LANES]
            return
        base = kind.split("+")[0]
        for c in range(MXU_COLS // LANES):
            cols = slice(c * LANES, (c + 1) * LANES)
            y = src[rows, cols]
            if base == "headnorm":
                first = lax.broadcasted_iota(jnp.int32, y.shape, 1) < HEAD_DIM
                y2 = y * y
                s0 = jnp.sum(jnp.where(first, y2, 0.0), axis=-1, keepdims=True)
                s1 = jnp.sum(jnp.where(first, 0.0, y2), axis=-1, keepdims=True)
                ms = jnp.where(first, s0, s1) * (1.0 / HEAD_DIM)
                y = y * lax.rsqrt(ms + RMS_EPS) * aux_ref[0, :, cols]
            elif base == "scale":
                y = y * aux_ref[0, :, cols]
            elif base == "silu":
                y = y * jax.nn.sigmoid(y)
            o_ref[c, rows, :] = y.astype(o_ref.dtype)
            if not kind.endswith("+views"):
                continue
            stage[c][rows, :] = y
            for view, dil in zip(view_refs, DILATIONS[1:]):
                for r in range(dil):
                    picked = stage[c][pl.ds(r0 + r, nr // dil, stride=dil), :]
                    view[c, r0 // dil:(r0 + nr) // dil, r * LANES:(r + 1) * LANES] = (
                        picked.astype(view.dtype))

    halves = tuple(slice(r, r + tm // 4) for r in range(0, tm, tm // 4))

    @pl.when(j == 0)
    def _():
        for rows in halves:
            matmul(acc_scr[0], rows)

    for parity in range(2):
        for kind in sorted(set(kinds[:-1])):
            in_kind = functools.reduce(
                jnp.logical_or, [(j - 1 >= lo) & (j - 1 < min(hi, n_tiles - 1))
                                 for lo, hi in _kind_ranges(kinds, kind) if lo < n_tiles - 1])

            @pl.when((j >= 1) & (j < n_tiles) & (j % 2 == parity) & in_kind)
            def _(parity=parity, kind=kind):
                for rows in halves:
                    epilogue(acc_scr[1 - parity], kind, rows)
                    matmul(acc_scr[parity], rows)

    @pl.when(j == n_tiles)
    def _():
        for rows in halves:
            epilogue(acc_scr[(n_tiles - 1) % 2], kinds[-1], rows)


def _in_proj(h, w_t, aux, layer, kinds, name, tm=2048):
    s, d = h.shape
    n_tiles = len(kinds)
    view_tiles = _view_tiles(kinds)
    chunks = MXU_COLS // LANES
    has_gate = kinds[-1] == "gate"
    out_tiles = n_tiles - 1 if has_gate else n_tiles
    prev = lambda j, n: jnp.clip(j - 1, 0, n - 1)
    out_shape = [jax.ShapeDtypeStruct((out_tiles * chunks, s, LANES), MXU_DTYPE)]
    out_specs = [pl.BlockSpec((chunks, tm, LANES), lambda i, j: (prev(j, out_tiles), i, 0))]
    scratch = [pltpu.VMEM((tm, MXU_COLS), F32), pltpu.VMEM((tm, MXU_COLS), F32)]
    if has_gate:
        out_shape.append(jax.ShapeDtypeStruct((s, LANES), F32))
        out_specs.append(pl.BlockSpec((tm, LANES), lambda i, j: (i, 0)))
    if view_tiles:
        for dil in DILATIONS[1:]:
            out_shape.append(
                jax.ShapeDtypeStruct((view_tiles * chunks, s // dil, dil * LANES), MXU_DTYPE))
            out_specs.append(pl.BlockSpec((chunks, tm // dil, dil * LANES),
                                          lambda i, j: (prev(j, view_tiles), i, 0)))
        scratch += [pltpu.VMEM((tm, LANES), F32)] * chunks
    return pl.pallas_call(
        functools.partial(_proj_body, kinds=kinds),
        grid=(s // tm, n_tiles + 1),
        in_specs=[pl.BlockSpec((tm, d), lambda i, j: (i, 0)),
                  pl.BlockSpec((1, MXU_COLS, d),
                               lambda i, j: (layer, jnp.minimum(j, n_tiles - 1), 0)),
                  pl.BlockSpec((1, 1, MXU_COLS), lambda i, j: (layer, 0, prev(j, n_tiles)))],
        out_specs=out_specs,
        out_shape=out_shape,
        scratch_shapes=scratch,
        compiler_params=_params(("parallel", "arbitrary")),
        name=name,
    )(h, w_t, aux)


def _fox_cumsum_body(x_ref, b_ref, o_ref):
    x = _log_sigmoid(x_ref[...] + b_ref[...])
    idx = lax.broadcasted_iota(jnp.int32, x.shape, 1)
    shift = 1
    while shift < x.shape[1]:
        x = x + jnp.where(idx >= shift, pltpu.roll(x, shift, axis=1), 0.0)
        shift *= 2
    o_ref[...] = x * LOG2E


def _fox_cumsum(logit_t, bias):
    nh, s = logit_t.shape
    return pl.pallas_call(
        _fox_cumsum_body,
        out_shape=jax.ShapeDtypeStruct((nh, s), F32),
        compiler_params=pltpu.CompilerParams(vmem_limit_bytes=VMEM_LIMIT),
        name="fox_cumsum",
    )(logit_t, bias.reshape(nh, 1))


A_GROUP = 4


A_DEN_MIN, A_DEN_MAX = 2.0 ** -90, 2.0 ** 100


def _band_attention(blocks, shift):
    first = lax.broadcasted_iota(jnp.int32, (BAND, LANES), 1) < HEAD_DIM
    first_kv = lax.broadcasted_iota(jnp.int32, (2 * BAND, LANES), 1) < HEAD_DIM
    chains = [(b, h) for b in range(len(blocks)) for h in range(2)]
    logits = []
    for b, h in chains:
        q, k2 = blocks[b][0], blocks[b][1]
        zero = jnp.zeros_like(q)
        qm = jnp.where(first, q, zero) if h == 0 else jnp.where(first, zero, q)
        logits.append(lax.dot_general(qm, k2, _NT, preferred_element_type=F32))
    logits = [s + blocks[b][3] for (b, h), s in zip(chains, logits)]
    if shift is None:
        maxes = [jnp.max(s, axis=-1, keepdims=True) for s in logits]
        probs = [jnp.exp2(s - m).astype(MXU_DTYPE) for s, m in zip(logits, maxes)]
    else:
        maxes = [shift] * len(chains)
        probs = [jnp.exp2(s).astype(MXU_DTYPE) for s in logits]
    res = []
    for (b, h), p in zip(chains, probs):
        v2 = blocks[b][2]
        one = jnp.ones_like(v2)
        vh = jnp.where(first_kv, v2, one) if h == 0 else jnp.where(first_kv, one, v2)
        res.append(jnp.dot(p, vh, preferred_element_type=F32))
    results = []
    for b in range(len(blocks)):
        r0, r1 = res[2 * b], res[2 * b + 1]
        num = jnp.where(first, r0, r1)
        den = pltpu.roll(jnp.where(first, r1, r0), HEAD_DIM, axis=1)
        m = maxes[2 * b] if shift is not None else jnp.where(first, maxes[2 * b], maxes[2 * b + 1])
        results.append((num / den, m + jnp.log2(den), den))
    return results


def _dilated_body(*refs):
    ins, z_ref, u_ref, o_ref, scr = refs[:15], refs[15], refs[16], refs[17], refs[18:]
    kv_scr, o_scr, l_scr, den_scr = scr[:6], scr[6], scr[7], scr[8]
    n = pl.program_id(1)
    for bi, d in enumerate(DILATIONS):
        kc, kp, vc, vp = ins[5 * bi + 1:5 * bi + 5]
        kf, vf = kv_scr[2 * bi:2 * bi + 2]
        rows = A_TILE // d
        for cur, prev, full in ((kc, kp, kf), (vc, vp, vf)):
            full[0:BAND, :] = prev[0, rows - BAND:rows, :]
            full[BAND:BAND + rows, :] = cur[0]

    row = lax.broadcasted_iota(jnp.int32, (BAND, 2 * BAND), 0)
    col = lax.broadcasted_iota(jnp.int32, (BAND, 2 * BAND), 1)
    band = (col >= row) & (col <= row + BAND)
    shift = u_ref[0:1, 0:1]

    def attend(shifted):
        offset = shift if shifted else 0.0
        bias_band = jnp.where(band, 0.0, -jnp.inf) - offset
        bias_first = jnp.where(band & (col >= BAND), 0.0, -jnp.inf) - offset
        for bi, d in enumerate(DILATIONS):
            q_ref = ins[5 * bi]
            kf, vf = kv_scr[2 * bi:2 * bi + 2]
            nb = A_TILE // d // BAND

            def group(members, q_ref=q_ref, kf=kf, vf=vf, d=d, nb=nb, bi=bi):
                blocks = []
                for j, r, maybe_first in members:
                    cols = slice(r * LANES, (r + 1) * LANES)
                    row0 = j * BAND if isinstance(j, int) else pl.multiple_of(j * BAND, BAND)
                    bias = bias_band
                    if maybe_first:
                        bias = jnp.where(n * nb + j >= 1, bias_band, bias_first)
                    blocks.append((q_ref[0, pl.ds(row0, BAND), cols],
                                   kf[pl.ds(row0, 2 * BAND), cols],
                                   vf[pl.ds(row0, 2 * BAND), cols], bias))
                results = _band_attention(blocks, shift if shifted else None)
                for (j, r, _), (o, lse, den) in zip(members, results):
                    dst = pl.ds(j * BAND * d + r, BAND, stride=d)
                    o_scr[bi, dst, :] = o
                    l_scr[bi, dst, :] = lse
                    if shifted:
                        den_scr[0] = jnp.minimum(den_scr[0], den)
                        den_scr[1] = jnp.maximum(den_scr[1], den)

            if nb >= A_GROUP:
                for r in range(d):
                    def body(g, carry, r=r, group=group):
                        j0 = pl.multiple_of(g * A_GROUP, A_GROUP)
                        group([(j0 + i, r, i == 0) for i in range(A_GROUP)])
                        return carry
                    lax.fori_loop(0, nb // A_GROUP, body, 0)
            else:
                for r0 in range(0, d, A_GROUP // nb):
                    group([(j, r0 + i, j == 0) for i in range(A_GROUP // nb) for j in range(nb)])

    den_scr[0] = jnp.full((BAND, LANES), jnp.inf, F32)
    den_scr[1] = jnp.zeros((BAND, LANES), F32)
    attend(True)
    in_range = (jnp.min(den_scr[0]) >= A_DEN_MIN) & (jnp.max(den_scr[1]) <= A_DEN_MAX)
    pl.when(jnp.logical_not(in_range))(functools.partial(attend, False))
    lse = l_scr[...]
    w = jnp.exp2(lse - jnp.max(lse, axis=0, keepdims=True))
    mixed = jnp.sum(w * o_scr[...], axis=0) / jnp.sum(w, axis=0)
    o_ref[...] = (mixed * z_ref[0].astype(F32)).astype(o_ref.dtype)


def _dilated_mixture(proj, views, shift):
    s = proj.shape[1]
    operands, in_specs, scratch = [], [], []
    for d, view in zip(DILATIONS, (proj,) + tuple(views)):
        rows, width = A_TILE // d, d * LANES
        blk = (1, rows, width)
        operands += [view] * 5
        in_specs += [
            pl.BlockSpec(blk, lambda c, n: (AQ + c, n, 0)),
            pl.BlockSpec(blk, lambda c, n: (AK + c, n, 0)),
            pl.BlockSpec(blk, lambda c, n: (AK + c, jnp.maximum(n - 1, 0), 0)),
            pl.BlockSpec(blk, lambda c, n: (AV + c, n, 0)),
            pl.BlockSpec(blk, lambda c, n: (AV + c, jnp.maximum(n - 1, 0), 0)),
        ]
        scratch += [pltpu.VMEM((rows + BAND, width), MXU_DTYPE)] * 2
    scratch += [pltpu.VMEM((len(DILATIONS), A_TILE, LANES), F32)] * 2
    scratch.append(pltpu.VMEM((2, BAND, LANES), F32))
    operands += [proj, jnp.full((8, LANES), shift, F32)]
    in_specs += [pl.BlockSpec((1, A_TILE, LANES), lambda c, n: (AZ + c, n, 0)),
                 pl.BlockSpec((8, LANES), lambda c, n: (0, 0))]
    return pl.pallas_call(
        _dilated_body,
        grid=(A_CHUNKS, s // A_TILE),
        in_specs=in_specs,
        out_specs=pl.BlockSpec((A_TILE, LANES), lambda c, n: (n, c)),
        out_shape=jax.ShapeDtypeStruct((s, A_WIDTH), MXU_DTYPE),
        scratch_shapes=scratch,
        compiler_params=_params(("parallel", "arbitrary")),
        name="dilated_mixture",
    )(*operands)


FOX_AUG = 3
FOX_VROWS = HEAD_DIM + 16
FOX_DEN_MIN, FOX_DEN_MAX = 2.0 ** -90, 2.0 ** 100


def _split3(c):
    hi = c.astype(jnp.bfloat16).astype(F32)
    r = c - hi
    mid = r.astype(jnp.bfloat16).astype(F32)
    return hi, mid, r - mid


def _fox_prep_body(q_ref, k_ref, v_ref, ccol_ref, crow_ref, qt_ref, ka_ref, vt_ref):
    tm = q_ref.shape[1]
    q_t = q_ref[0].astype(F32).T
    v_t = v_ref[0].astype(F32).T
    k = k_ref[0]
    sub = lax.broadcasted_iota(jnp.int32, (HEAD_DIM, tm), 0)
    lane = lax.broadcasted_iota(jnp.int32, (tm, LANES), 1)
    v_tail = (lax.broadcasted_iota(jnp.int32, (FOX_VROWS - HEAD_DIM, tm), 0) == 0).astype(F32)
    for h in range(2):
        hi, mid, lo = _split3(crow_ref[0, h:h + 1, :])
        aug_q = jnp.where(sub == 0, hi, jnp.where(sub == 1, mid, jnp.where(
            sub == 2, lo, jnp.where(sub < 2 * FOX_AUG, 1.0, 0.0))))
        q_h = q_t[h * HEAD_DIM:(h + 1) * HEAD_DIM]
        parts = [q_h, aug_q] if h == 0 else [aug_q, q_h]
        qt_ref[h] = jnp.concatenate(parts, axis=0).astype(qt_ref.dtype)
        hi, mid, lo = _split3(ccol_ref[0, :, h:h + 1])
        a0 = (1 - h) * HEAD_DIM
        aug_k = jnp.where(lane == a0 + FOX_AUG, -hi, jnp.where(lane == a0 + FOX_AUG + 1, -mid, jnp.where(
            lane == a0 + FOX_AUG + 2, -lo, jnp.where((lane >= a0) & (lane < a0 + FOX_AUG), 1.0, 0.0))))
        own = (lane < HEAD_DIM) if h == 0 else (lane >= HEAD_DIM)
        ka_ref[h] = jnp.where(own, k, aug_k.astype(k.dtype))
        vt_ref[h, 0] = jnp.concatenate([v_t[h * HEAD_DIM:(h + 1) * HEAD_DIM], v_tail],
                                       axis=0).astype(vt_ref.dtype)


def _fox_prep(proj, c, shift, tm):
    s = proj.shape[1]
    c_col = c.reshape(B_CHUNKS, 2, s).transpose(0, 2, 1)
    c_row = c.reshape(B_CHUNKS, 2, s) - shift
    return pl.pallas_call(
        _fox_prep_body,
        grid=(B_CHUNKS, s // tm),
        in_specs=[
            pl.BlockSpec((1, tm, LANES), lambda p, i: (BQ + p, i, 0)),
            pl.BlockSpec((1, tm, LANES), lambda p, i: (BK + p, i, 0)),
            pl.BlockSpec((1, tm, LANES), lambda p, i: (BV + p, i, 0)),
            pl.BlockSpec((1, tm, 2), lambda p, i: (p, i, 0)),
            pl.BlockSpec((1, 2, tm), lambda p, i: (p, 0, i)),
        ],
        out_specs=[
            pl.BlockSpec((2, LANES, tm), lambda p, i: (p, 0, i)),
            pl.BlockSpec((2, tm, LANES), lambda p, i: (p, i, 0)),
            pl.BlockSpec((2, 1, FOX_VROWS, tm), lambda p, i: (p, i, 0, 0)),
        ],
        out_shape=[
            jax.ShapeDtypeStruct((B_HEADS, LANES, s), MXU_DTYPE),
            jax.ShapeDtypeStruct((B_HEADS, s, LANES), MXU_DTYPE),
            jax.ShapeDtypeStruct((B_HEADS, s // tm, FOX_VROWS, tm), MXU_DTYPE),
        ],
        compiler_params=_params(("parallel", "parallel")),
        name="fox_prep",
    )(proj, proj, proj, c_col, c_row)


def _fox_body(qt_ref, ka_ref, vt_ref, z_ref, o_ref, s0_scr, s1_scr, m_scr, acc_scr, *, blk):
    qi = pl.program_id(1)
    causal = (lax.broadcasted_iota(jnp.int32, (blk, blk), 0)
              <= lax.broadcasted_iota(jnp.int32, (blk, blk), 1))
    q_t = (qt_ref[0], qt_ref[1])

    def logits(kb, dst):
        k0 = pl.multiple_of(kb * blk, blk)
        for h in range(2):
            dst[h] = jnp.dot(ka_ref[h, pl.ds(k0, blk), :], q_t[h], preferred_element_type=F32)

    def accumulate_shifted(kb, src, diagonal=False):
        for h in range(2):
            s = src[h]
            if diagonal:
                s = jnp.where(causal, s, -jnp.inf)
            acc_scr[h] += jnp.dot(vt_ref[h, kb], jnp.exp2(s).astype(MXU_DTYPE),
                                  preferred_element_type=F32)

    def accumulate_online(kb, src, diagonal=False):
        for h in range(2):
            s = src[h]
            if diagonal:
                s = jnp.where(causal, s, -jnp.inf)
            m = m_scr[h]
            m_new = jnp.maximum(m, jnp.max(s, axis=0, keepdims=True))
            p = jnp.exp2(s - m_new).astype(MXU_DTYPE)
            acc_scr[h] = (jnp.exp2(m - m_new) * acc_scr[h]
                          + jnp.dot(vt_ref[h, kb], p, preferred_element_type=F32))
            m_scr[h] = m_new

    def attend(accumulate):
        acc_scr[...] = jnp.zeros(acc_scr.shape, F32)
        logits(0, s0_scr)

        def pair(j, carry):
            logits(2 * j + 1, s1_scr)
            accumulate(2 * j, s0_scr)
            logits(2 * j + 2, s0_scr)
            accumulate(2 * j + 1, s1_scr)
            return carry

        lax.fori_loop(0, qi // 2, pair, 0)

        @pl.when(qi % 2 == 0)
        def _():
            accumulate(qi, s0_scr, diagonal=True)

        @pl.when(qi % 2 == 1)
        def _():
            logits(qi, s1_scr)
            accumulate(qi - 1, s0_scr)
            accumulate(qi, s1_scr, diagonal=True)

        outs = [acc_scr[h, 0:HEAD_DIM] / acc_scr[h, HEAD_DIM:HEAD_DIM + 1] for h in range(2)]
        o = jnp.concatenate(outs, axis=0).T
        o_ref[...] = (o * z_ref[0].astype(F32)).astype(o_ref.dtype)

    attend(accumulate_shifted)
    den = jnp.concatenate([acc_scr[h, HEAD_DIM:HEAD_DIM + 1] for h in range(2)], axis=0)
    in_range = (jnp.min(den) >= FOX_DEN_MIN) & (jnp.max(den) <= FOX_DEN_MAX)

    @pl.when(jnp.logical_not(in_range))
    def _():
        m_scr[...] = jnp.full(m_scr.shape, -jnp.inf, F32)
        attend(accumulate_online)


def _forgetting_attention(proj, c, shift, blk=512):
    s = proj.shape[1]
    qt, ka, vt = _fox_prep(proj, c, shift, blk)
    return pl.pallas_call(
        functools.partial(_fox_body, blk=blk),
        grid=(B_CHUNKS, s // blk),
        in_specs=[
            pl.BlockSpec((2, LANES, blk), lambda p, i: (p, 0, i)),
            pl.BlockSpec((2, s, LANES), lambda p, i: (p, 0, 0)),
            pl.BlockSpec((2, s // blk, FOX_VROWS, blk), lambda p, i: (p, 0, 0, 0)),
            pl.BlockSpec((1, blk, LANES), lambda p, i: (BZ + p, i, 0)),
        ],
        out_specs=pl.BlockSpec((blk, LANES), lambda p, i: (i, p)),
        out_shape=jax.ShapeDtypeStruct((s, B_WIDTH), MXU_DTYPE),
        scratch_shapes=[pltpu.VMEM((2, blk, blk), F32), pltpu.VMEM((2, blk, blk), F32),
                        pltpu.VMEM((2, 1, blk), F32), pltpu.VMEM((2, FOX_VROWS, blk), F32)],
        compiler_params=_params(("parallel", "arbitrary")),
        name="forgetting_attention",
    )(qt, ka, vt, proj)


GLA_GROUP = 4


def _gla_level_matrix():
    c = GLA_CHUNK
    mats = []
    for b in GLA_LEVELS:
        m = np.zeros((c, c), np.float32)
        for i in range(c):
            pivot = (i // (2 * b)) * 2 * b + b - 1
            if i > pivot:
                m[i, pivot + 1:i + 1] = 1.0
            else:
                m[i, i + 1:pivot + 1] = 1.0
        mats.append(m)
    mats.append(np.tril(np.ones((c, c), np.float32)))
    return np.concatenate(mats, axis=0)


def _gla_body(q_ref, k_ref, v_ref, g_ref, z_ref, gu_ref, gb_ref, og_ref, lvl_ref, o_ref,
              st_scr, *, tile):
    c = GLA_CHUNK
    nlev = len(GLA_LEVELS)
    nch = tile // c
    grp = GLA_GROUP * c

    @pl.when(pl.program_id(1) == 0)
    def _():
        st_scr[...] = jnp.zeros_like(st_scr)

    gate_in = g_ref[:, 0:GATE_RANK]
    gate_hi = gate_in.astype(MXU_DTYPE)
    gate_lo = (gate_in - gate_hi.astype(F32)).astype(MXU_DTYPE)
    gate_w = gu_ref[0].astype(MXU_DTYPE)
    logit = (jnp.dot(gate_hi, gate_w, preferred_element_type=F32)
             + jnp.dot(gate_lo, gate_w, preferred_element_type=F32) + gb_ref[0])
    la = _log_sigmoid(logit) * (1.0 / GATE_TEMP)

    la_cat = jnp.concatenate([la[ci * c:(ci + 1) * c] for ci in range(nch)], axis=1)
    la_hi = la_cat.astype(MXU_DTYPE)
    la_lo = (la_cat - la_hi.astype(F32)).astype(MXU_DTYPE)
    f = jnp.exp(jnp.dot(lvl_ref[0:nlev * c, :], la_hi, preferred_element_type=F32))
    tri = lvl_ref[nlev * c:(nlev + 1) * c, :]
    bc = (jnp.dot(tri, la_hi, preferred_element_type=F32)
          + jnp.dot(tri, la_lo, preferred_element_type=F32))

    row = lax.broadcasted_iota(jnp.int32, (grp, grp), 0)
    col = lax.broadcasted_iota(jnp.int32, (grp, grp), 1)
    diff = (row ^ col) & (c - 1)
    level = jnp.full((grp, grp), nlev, jnp.int32)
    for li, b in reversed(list(enumerate(GLA_LEVELS))):
        level = jnp.where(diff >= b, li, level)
    level = jnp.where(((row & -c) == (col & -c)) & (col <= row), level, -1)
    out_gain = og_ref[...]

    state = st_scr[...]
    for g0 in range(0, nch, GLA_GROUP):
        rows = slice(g0 * c, g0 * c + grp)
        q = q_ref[0, rows, :].astype(F32)
        k = k_ref[0, rows, :].astype(F32)
        v = jnp.concatenate([v_ref[0, rows, :], v_ref[1, rows, :]], axis=-1)
        z = jnp.concatenate([z_ref[0, rows, :], z_ref[1, rows, :]], axis=-1)
        a = lax.dot_general(q.astype(MXU_DTYPE), k.astype(MXU_DTYPE), _NT,
                            preferred_element_type=F32)
        attn = jnp.where(level == nlev, a, 0.0)
        for li in range(nlev):
            fl = jnp.concatenate([f[li * c:(li + 1) * c, ch * LANES:(ch + 1) * LANES]
                                  for ch in range(g0, g0 + GLA_GROUP)], axis=0)
            a = lax.dot_general((q * fl).astype(MXU_DTYPE), (k * fl).astype(MXU_DTYPE), _NT,
                                preferred_element_type=F32)
            attn = jnp.where(level == li, a, attn)
        o_intra = jnp.dot(attn.astype(MXU_DTYPE), v, preferred_element_type=F32)

        for ci in range(GLA_GROUP):
            ch = g0 + ci
            r = slice(ci * c, (ci + 1) * c)
            bc_c = bc[:, ch * LANES:(ch + 1) * LANES]
            b_last = bc_c[c - 1:c, :]
            q_dec = (q[r] * jnp.exp(bc_c)).astype(MXU_DTYPE)
            o = o_intra[r] + jnp.dot(q_dec, state.astype(MXU_DTYPE), preferred_element_type=F32)
            k_dec = (k[r] * jnp.exp(b_last - bc_c)).astype(MXU_DTYPE)
            decay = jnp.broadcast_to(jnp.exp(b_last), (LANES, LANES)).T
            state = (state * jnp.concatenate([decay, decay], axis=1)
                     + lax.dot_general(k_dec, v[r], _TN, preferred_element_type=F32))
            ms = jnp.sum(o * o, axis=-1, keepdims=True) * (1.0 / C_DV)
            y = o * lax.rsqrt(ms + RMS_EPS) * out_gain * z[r].astype(F32)
            o_ref[pl.ds(ch * c, c), :] = y.astype(o_ref.dtype)
    st_scr[...] = state


def _gated_linear_attention(proj, gate, gate_up, gate_bias, out_gain, tile=512):
    s = proj.shape[1]
    lvl = jnp.asarray(_gla_level_matrix(), MXU_DTYPE)
    return pl.pallas_call(
        functools.partial(_gla_body, tile=tile),
        grid=(C_HEADS, s // tile),
        in_specs=[
            pl.BlockSpec((1, tile, LANES), lambda h, t: (CQ + h, t, 0)),
            pl.BlockSpec((1, tile, LANES), lambda h, t: (CK + h, t, 0)),
            pl.BlockSpec((2, tile, LANES), lambda h, t: (CV // 2 + h, t, 0)),
            pl.BlockSpec((tile, LANES), lambda h, t: (t, 0)),
            pl.BlockSpec((2, tile, LANES), lambda h, t: (CZ // 2 + h, t, 0)),
            pl.BlockSpec((1, GATE_RANK, C_DK_PAD), lambda h, t: (h, 0, 0)),
            pl.BlockSpec((1, 1, C_DK_PAD), lambda h, t: (h, 0, 0)),
            pl.BlockSpec((1, C_DV_PAD), lambda h, t: (0, 0)),
            pl.BlockSpec(lvl.shape, lambda h, t: (0, 0)),
        ],
        out_specs=pl.BlockSpec((tile, C_DV_PAD), lambda h, t: (t, h)),
        out_shape=jax.ShapeDtypeStruct((s, C_HEADS * C_DV_PAD), MXU_DTYPE),
        scratch_shapes=[pltpu.VMEM((C_DK_PAD, C_DV_PAD), F32)],
        compiler_params=_params(("parallel", "arbitrary")),
        name="gated_linear_attention",
    )(proj, proj, proj, gate, proj, gate_up, gate_bias, out_gain, lvl)


def _out_body(x_ref, a_ref, b_ref, c_ref, wab_ref, wc_ref, o_ref):
    wa = wab_ref[0, 0:A_WIDTH, :].astype(MXU_DTYPE)
    wb = wab_ref[0, A_WIDTH:A_WIDTH + B_WIDTH, :].astype(MXU_DTYPE)
    acc = jnp.dot(a_ref[...], wa, preferred_element_type=F32)
    acc = acc + jnp.dot(b_ref[...], wb, preferred_element_type=F32)
    acc = acc + jnp.dot(c_ref[...], wc_ref[0], preferred_element_type=F32)
    o_ref[...] = x_ref[...] + acc


def _out_proj(x, ma, mb, mc, w_out, wo_c, layer, tm=1024, tn=512):
    s, d = x.shape
    row_blk = lambda m: pl.BlockSpec((tm, m.shape[1]), lambda i, j: (i, 0))
    return pl.pallas_call(
        _out_body,
        grid=(s // tm, d // tn),
        in_specs=[pl.BlockSpec((tm, tn), lambda i, j: (i, j)),
                  row_blk(ma), row_blk(mb), row_blk(mc),
                  pl.BlockSpec((1, A_WIDTH + B_WIDTH, tn), lambda i, j: (layer, 0, j)),
                  pl.BlockSpec((1, mc.shape[1], tn), lambda i, j: (layer, 0, j))],
        out_specs=pl.BlockSpec((tm, tn), lambda i, j: (i, j)),
        out_shape=jax.ShapeDtypeStruct((s, d), F32),
        compiler_params=_params(("parallel", "arbitrary")),
        name="out_proj",
    )(x, ma, mb, mc, w_out, wo_c)


def _pad_heads(w, heads, width, padded):
    lead = w.shape[:-1]
    w = w.reshape(*lead, heads, width)
    w = jnp.pad(w, [(0, 0)] * len(lead) + [(0, 0), (0, padded - width)])
    return w.reshape(*lead, heads * padded)


def _prepare(w_in, w_out, a_q_gain, a_k_gain, b_q_gain, b_k_gain):
    layers, d, _ = w_in.shape
    w_t = jnp.swapaxes(w_in, 1, 2)
    ck, cv = C_HEADS * C_DK, C_HEADS * C_DV
    ab = 4 * (A_WIDTH + B_WIDTH)
    bounds = np.cumsum([ab, B_HEADS, ck, ck, cv, cv, GATE_RANK])
    bf, cq, ckk, cvv, cz, cr = [w_t[:, lo:hi] for lo, hi in zip(bounds[:-1], bounds[1:])]

    def pad_rows(w, width, padded):
        w = w.reshape(layers, C_HEADS, width, d)
        w = jnp.pad(w, ((0, 0), (0, 0), (0, padded - width), (0, 0)))
        return w.reshape(layers, C_HEADS * padded, d)

    pad_k = lambda w: pad_rows(w, C_DK, C_DK_PAD)
    pad_v = lambda w: pad_rows(w, C_DV, C_DV_PAD)
    tail = jnp.zeros((layers, MXU_COLS - GATE_RANK - B_HEADS, d), F32)
    w_c_t = jnp.concatenate([pad_k(cq), pad_k(ckk), pad_v(cvv), pad_v(cz), cr, bf, tail], axis=1)

    q_scale = HEAD_DIM ** -0.5 * LOG2E
    tile_heads = lambda g, n: jnp.tile(g, (1, n))
    ones = lambda n: jnp.ones((layers, n), F32)
    aux_ab = jnp.concatenate([
        tile_heads(a_q_gain * q_scale, A_HEADS), tile_heads(a_k_gain, A_HEADS), ones(2 * A_WIDTH),
        tile_heads(b_q_gain * q_scale, B_HEADS), tile_heads(b_k_gain, B_HEADS), ones(2 * B_WIDTH)],
        axis=-1)[:, None, :]
    aux_c = jnp.concatenate([
        jnp.full((layers, C_HEADS * C_DK_PAD), C_DK ** -0.5, F32),
        ones(C_HEADS * (C_DK_PAD + 2 * C_DV_PAD) + MXU_COLS)], axis=-1)[:, None, :]

    wo_c = w_out[:, A_WIDTH + B_WIDTH:].reshape(layers, C_HEADS, C_DV, d)
    wo_c = jnp.pad(wo_c, ((0, 0), (0, 0), (0, C_DV_PAD - C_DV), (0, 0)))
    wo_c = wo_c.reshape(layers, C_HEADS * C_DV_PAD, d).astype(MXU_DTYPE)
    return w_t, w_c_t, aux_ab, aux_c, wo_c


def _layer(x, layer, norm_g, w_in, w_c, aux_ab, aux_c, w_out, wo_c, fox_bias, a_shift, fox_shift,
           gla_gate_up, gla_gate_bias, gla_out_gain):
    h = _rmsnorm(x, norm_g, layer)
    proj_ab, v4, v16 = _in_proj(h, w_in, aux_ab, layer, AB_KINDS, "in_proj_ab")
    proj_c, gate = _in_proj(h, w_c, aux_c, layer, C_KINDS, "in_proj_c")
    c = _fox_cumsum(gate[:, GATE_RANK:GATE_RANK + B_HEADS].T, fox_bias)
    mixed_a = _dilated_mixture(proj_ab, (v4, v16), a_shift)
    mixed_b = _forgetting_attention(proj_ab, c, fox_shift)
    gate_up = _pad_heads(gla_gate_up, C_HEADS, C_DK, C_DK_PAD)
    gate_up = gate_up.reshape(GATE_RANK, C_HEADS, C_DK_PAD).transpose(1, 0, 2)
    gate_bias = _pad_heads(gla_gate_bias, C_HEADS, C_DK, C_DK_PAD).reshape(C_HEADS, 1, C_DK_PAD)
    out_gain = jnp.pad(gla_out_gain, (0, C_DV_PAD - C_DV)).reshape(1, C_DV_PAD)
    mixed_c = _gated_linear_attention(proj_c, gate, gate_up, gate_bias, out_gain)
    return _out_proj(x, mixed_a, mixed_b, mixed_c, w_out, wo_c, layer)


@jax.jit
def kernel(x, norm_g, w_in, a_q_gain, a_k_gain, b_q_gain, b_k_gain, fox_bias, gla_gate_up,
           gla_gate_bias, gla_out_gain, w_out):
    bsz, s, d = x.shape
    assert bsz == 1, "batch size 1 only"
    w_t, w_c, aux_ab, aux_c, wo_c = _prepare(w_in, w_out, a_q_gain, a_k_gain, b_q_gain, b_k_gain)
    norm_g = norm_g[:, None, :]
    bound = lambda gq, gk: (HEAD_DIM ** 0.5 * LOG2E * jnp.max(jnp.abs(gq), axis=-1)
                            * jnp.max(jnp.abs(gk), axis=-1))
    a_shift, fox_shift = bound(a_q_gain, a_k_gain), bound(b_q_gain, b_k_gain)
    y = x.reshape(s, d)
    for layer in range(w_in.shape[0]):
        y = _layer(y, layer, norm_g, w_t, w_c, aux_ab, aux_c, w_out, wo_c, fox_bias[layer],
                   a_shift[layer], fox_shift[layer], gla_gate_up[layer], gla_gate_bias[layer],
                   gla_out_gain[layer])
    return y.reshape(bsz, s, d)
```

```python
import functools

import numpy as np
import jax
import jax.numpy as jnp
from jax import lax
from jax.experimental import pallas as pl
from jax.experimental.pallas import tpu as pltpu

F32 = jnp.float32
MXU_DTYPE = jnp.bfloat16

LANES = 128
MXU_COLS = 256
HEAD_DIM = 64
A_HEADS, B_HEADS, C_HEADS = 12, 8, 4
A_CHUNKS, B_CHUNKS = A_HEADS // 2, B_HEADS // 2
C_DK, C_DV = 96, 192
C_DK_PAD, C_DV_PAD = 128, 256
GATE_RANK = 16
GATE_TEMP = 16.0
RMS_EPS = 1e-6
LOG2E = 1.4426950408889634
DILATIONS = (1, 4, 16)
BAND = 128
A_TILE = 2048
GLA_CHUNK = 64
GLA_LEVELS = (32, 16, 8, 4, 2, 1)
VMEM_LIMIT = 56 * 1024 * 1024

AQ, AK, AV, AZ = 0, 6, 12, 18
BQ, BK, BV, BZ = 24, 28, 32, 36
CQ, CK, CV, CZ = 0, 4, 8, 16
A_WIDTH, B_WIDTH = A_HEADS * HEAD_DIM, B_HEADS * HEAD_DIM

_NT = (((1,), (1,)), ((), ()))
_TN = (((0,), (0,)), ((), ()))


def _params(sem):
    return pltpu.CompilerParams(dimension_semantics=sem, vmem_limit_bytes=VMEM_LIMIT)


def _log_sigmoid(x):
    return jnp.minimum(x, 0.0) - jnp.log1p(jnp.exp(-jnp.abs(x)))


AB_KINDS = (("headnorm+views",) * (2 * A_CHUNKS) + ("scale+views",) * A_CHUNKS
            + ("silu",) * A_CHUNKS
            + ("headnorm",) * (2 * B_CHUNKS) + ("scale",) * B_CHUNKS + ("silu",) * B_CHUNKS)[::2]
C_KINDS = (("scale",) * (4 * C_HEADS) + ("silu",) * (2 * C_HEADS))[::2] + ("gate",)


def _kind_ranges(kinds, kind):
    runs, start = [], None
    for t, k in enumerate(kinds + (None,)):
        if k == kind and start is None:
            start = t
        elif k != kind and start is not None:
            runs.append((start, t))
            start = None
    return runs


def _view_tiles(kinds):
    n = sum(k.endswith("+views") for k in kinds)
    assert all(k.endswith("+views") for k in kinds[:n])
    return n


def _rmsnorm_body(x_ref, g_ref, o_ref):
    x = x_ref[...]
    ms = jnp.mean(x * x, axis=-1, keepdims=True)
    o_ref[...] = (x * lax.rsqrt(ms + RMS_EPS) * g_ref[0]).astype(o_ref.dtype)


def _rmsnorm(x, norm_g, layer, tm=512):
    s, d = x.shape
    return pl.pallas_call(
        _rmsnorm_body,
        grid=(s // tm,),
        in_specs=[pl.BlockSpec((tm, d), lambda i: (i, 0)),
                  pl.BlockSpec((1, 1, d), lambda i: (layer, 0, 0))],
        out_specs=pl.BlockSpec((tm, d), lambda i: (i, 0)),
        out_shape=jax.ShapeDtypeStruct((s, d), MXU_DTYPE),
        compiler_params=_params(("parallel",)),
        name="rmsnorm",
    )(x, norm_g)


def _proj_body(h_ref, w_ref, aux_ref, *rest, kinds):
    view_tiles = _view_tiles(kinds)
    n_tiles = len(kinds)
    o_ref, rest = rest[0], rest[1:]
    gate_ref = None
    if kinds[-1] == "gate":
        gate_ref, rest = rest[0], rest[1:]
    if view_tiles:
        view_refs, rest = rest[:len(DILATIONS) - 1], rest[len(DILATIONS) - 1:]
        n_stage = MXU_COLS // LANES
        stage, rest = rest[-n_stage:], rest[:-n_stage]
    acc_scr = rest[0:2]
    j = pl.program_id(1)
    tm = h_ref.shape[0]

    def matmul(dst, rows):
        dst[rows, :] = lax.dot_general(h_ref[rows, :], w_ref[0].astype(MXU_DTYPE), _NT,
                                       preferred_element_type=F32)

    def epilogue(src, kind, rows):
        r0, nr = rows.start, rows.stop - rows.start
        if kind == "gate":
            gate_ref[rows, :] = src[rows, 0:LANES]
            return
        base = kind.split("+")[0]
        for c in range(MXU_COLS // LANES):
            cols = slice(c * LANES, (c + 1) * LANES)
            y = src[rows, cols]
            if base == "headnorm":
                first = lax.broadcasted_iota(jnp.int32, y.shape, 1) < HEAD_DIM
                y2 = y * y
                s0 = jnp.sum(jnp.where(first, y2, 0.0), axis=-1, keepdims=True)
                s1 = jnp.sum(jnp.where(first, 0.0, y2), axis=-1, keepdims=True)
                ms = jnp.where(first, s0, s1) * (1.0 / HEAD_DIM)
                y = y * lax.rsqrt(ms + RMS_EPS) * aux_ref[0, :, cols]
            elif base == "scale":
                y = y * aux_ref[0, :, cols]
            elif base == "silu":
                y = y * jax.nn.sigmoid(y)
            o_ref[c, rows, :] = y.astype(o_ref.dtype)
            if not kind.endswith("+views"):
                continue
            stage[c][rows, :] = y
            for view, dil in zip(view_refs, DILATIONS[1:]):
                for r in range(dil):
                    picked = stage[c][pl.ds(r0 + r, nr // dil, stride=dil), :]
                    view[c, r0 // dil:(r0 + nr) // dil, r * LANES:(r + 1) * LANES] = (
                        picked.astype(view.dtype))

    halves = tuple(slice(r, r + tm // 4) for r in range(0, tm, tm // 4))

    @pl.when(j == 0)
    def _():
        for rows in halves:
            matmul(acc_scr[0], rows)

    for parity in range(2):
        for kind in sorted(set(kinds[:-1])):
            in_kind = functools.reduce(
                jnp.logical_or, [(j - 1 >= lo) & (j - 1 < min(hi, n_tiles - 1))
                                 for lo, hi in _kind_ranges(kinds, kind) if lo < n_tiles - 1])

            @pl.when((j >= 1) & (j < n_tiles) & (j % 2 == parity) & in_kind)
            def _(parity=parity, kind=kind):
                for rows in halves:
                    epilogue(acc_scr[1 - parity], kind, rows)
                    matmul(acc_scr[parity], rows)

    @pl.when(j == n_tiles)
    def _():
        for rows in halves:
            epilogue(acc_scr[(n_tiles - 1) % 2], kinds[-1], rows)


def _in_proj(h, w_t, aux, layer, kinds, name, tm=2048):
    s, d = h.shape
    n_tiles = len(kinds)
    view_tiles = _view_tiles(kinds)
    chunks = MXU_COLS // LANES
    has_gate = kinds[-1] == "gate"
    out_tiles = n_tiles - 1 if has_gate else n_tiles
    prev = lambda j, n: jnp.clip(j - 1, 0, n - 1)
    out_shape = [jax.ShapeDtypeStruct((out_tiles * chunks, s, LANES), MXU_DTYPE)]
    out_specs = [pl.BlockSpec((chunks, tm, LANES), lambda i, j: (prev(j, out_tiles), i, 0))]
    scratch = [pltpu.VMEM((tm, MXU_COLS), F32), pltpu.VMEM((tm, MXU_COLS), F32)]
    if has_gate:
        out_shape.append(jax.ShapeDtypeStruct((s, LANES), F32))
        out_specs.append(pl.BlockSpec((tm, LANES), lambda i, j: (i, 0)))
    if view_tiles:
        for dil in DILATIONS[1:]:
            out_shape.append(
                jax.ShapeDtypeStruct((view_tiles * chunks, s // dil, dil * LANES), MXU_DTYPE))
            out_specs.append(pl.BlockSpec((chunks, tm // dil, dil * LANES),
                                          lambda i, j: (prev(j, view_tiles), i, 0)))
        scratch += [pltpu.VMEM((tm, LANES), F32)] * chunks
    return pl.pallas_call(
        functools.partial(_proj_body, kinds=kinds),
        grid=(s // tm, n_tiles + 1),
        in_specs=[pl.BlockSpec((tm, d), lambda i, j: (i, 0)),
                  pl.BlockSpec((1, MXU_COLS, d),
                               lambda i, j: (layer, jnp.minimum(j, n_tiles - 1), 0)),
                  pl.BlockSpec((1, 1, MXU_COLS), lambda i, j: (layer, 0, prev(j, n_tiles)))],
        out_specs=out_specs,
        out_shape=out_shape,
        scratch_shapes=scratch,
        compiler_params=_params(("parallel", "arbitrary")),
        name=name,
    )(h, w_t, aux)


def _fox_cumsum_body(x_ref, b_ref, o_ref):
    x = _log_sigmoid(x_ref[...] + b_ref[...])
    idx = lax.broadcasted_iota(jnp.int32, x.shape, 1)
    shift = 1
    while shift < x.shape[1]:
        x = x + jnp.where(idx >= shift, pltpu.roll(x, shift, axis=1), 0.0)
        shift *= 2
    o_ref[...] = x * LOG2E


def _fox_cumsum(logit_t, bias):
    nh, s = logit_t.shape
    return pl.pallas_call(
        _fox_cumsum_body,
        out_shape=jax.ShapeDtypeStruct((nh, s), F32),
        compiler_params=pltpu.CompilerParams(vmem_limit_bytes=VMEM_LIMIT),
        name="fox_cumsum",
    )(logit_t, bias.reshape(nh, 1))


A_GROUP = 4


A_DEN_MIN, A_DEN_MAX = 2.0 ** -90, 2.0 ** 100


def _band_attention(blocks, shift):
    first = lax.broadcasted_iota(jnp.int32, (BAND, LANES), 1) < HEAD_DIM
    first_kv = lax.broadcasted_iota(jnp.int32, (2 * BAND, LANES), 1) < HEAD_DIM
    chains = [(b, h) for b in range(len(blocks)) for h in range(2)]
    logits = []
    for b, h in chains:
        q, k2 = blocks[b][0], blocks[b][1]
        zero = jnp.zeros_like(q)
        qm = jnp.where(first, q, zero) if h == 0 else jnp.where(first, zero, q)
        logits.append(lax.dot_general(qm, k2, _NT, preferred_element_type=F32))
    logits = [s + blocks[b][3] for (b, h), s in zip(chains, logits)]
    if shift is None:
        maxes = [jnp.max(s, axis=-1, keepdims=True) for s in logits]
        probs = [jnp.exp2(s - m).astype(MXU_DTYPE) for s, m in zip(logits, maxes)]
    else:
        maxes = [shift] * len(chains)
        probs = [jnp.exp2(s).astype(MXU_DTYPE) for s in logits]
    res = []
    for (b, h), p in zip(chains, probs):
        v2 = blocks[b][2]
        one = jnp.ones_like(v2)
        vh = jnp.where(first_kv, v2, one) if h == 0 else jnp.where(first_kv, one, v2)
        res.append(jnp.dot(p, vh, preferred_element_type=F32))
    results = []
    for b in range(len(blocks)):
        r0, r1 = res[2 * b], res[2 * b + 1]
        num = jnp.where(first, r0, r1)
        den = pltpu.roll(jnp.where(first, r1, r0), HEAD_DIM, axis=1)
        m = maxes[2 * b] if shift is not None else jnp.where(first, maxes[2 * b], maxes[2 * b + 1])
        results.append((num / den, m + jnp.log2(den), den))
    return results


def _dilated_body(*refs):
    ins, z_ref, u_ref, o_ref, scr = refs[:15], refs[15], refs[16], refs[17], refs[18:]
    kv_scr, o_scr, l_scr, den_scr = scr[:6], scr[6], scr[7], scr[8]
    n = pl.program_id(1)
    for bi, d in enumerate(DILATIONS):
        kc, kp, vc, vp = ins[5 * bi + 1:5 * bi + 5]
        kf, vf = kv_scr[2 * bi:2 * bi + 2]
        rows = A_TILE // d
        for cur, prev, full in ((kc, kp, kf), (vc, vp, vf)):
            full[0:BAND, :] = prev[0, rows - BAND:rows, :]
            full[BAND:BAND + rows, :] = cur[0]

    row = lax.broadcasted_iota(jnp.int32, (BAND, 2 * BAND), 0)
    col = lax.broadcasted_iota(jnp.int32, (BAND, 2 * BAND), 1)
    band = (col >= row) & (col <= row + BAND)
    shift = u_ref[0:1, 0:1]

    def attend(shifted):
        offset = shift if shifted else 0.0
        bias_band = jnp.where(band, 0.0, -jnp.inf) - offset
        bias_first = jnp.where(band & (col >= BAND), 0.0, -jnp.inf) - offset
        for bi, d in enumerate(DILATIONS):
            q_ref = ins[5 * bi]
            kf, vf = kv_scr[2 * bi:2 * bi + 2]
            nb = A_TILE // d // BAND

            def group(members, q_ref=q_ref, kf=kf, vf=vf, d=d, nb=nb, bi=bi):
                blocks = []
                for j, r, maybe_first in members:
                    cols = slice(r * LANES, (r + 1) * LANES)
                    row0 = j * BAND if isinstance(j, int) else pl.multiple_of(j * BAND, BAND)
                    bias = bias_band
                    if maybe_first:
                        bias = jnp.where(n * nb + j >= 1, bias_band, bias_first)
                    blocks.append((q_ref[0, pl.ds(row0, BAND), cols],
                                   kf[pl.ds(row0, 2 * BAND), cols],
                                   vf[pl.ds(row0, 2 * BAND), cols], bias))
                results = _band_attention(blocks, shift if shifted else None)
                for (j, r, _), (o, lse, den) in zip(members, results):
                    dst = pl.ds(j * BAND * d + r, BAND, stride=d)
                    o_scr[bi, dst, :] = o
                    l_scr[bi, dst, :] = lse
                    if shifted:
                        den_scr[0] = jnp.minimum(den_scr[0], den)
                        den_scr[1] = jnp.maximum(den_scr[1], den)

            if nb >= A_GROUP:
                for r in range(d):
                    def body(g, carry, r=r, group=group):
                        j0 = pl.multiple_of(g * A_GROUP, A_GROUP)
                        group([(j0 + i, r, i == 0) for i in range(A_GROUP)])
                        return carry
                    lax.fori_loop(0, nb // A_GROUP, body, 0)
            else:
                for r0 in range(0, d, A_GROUP // nb):
                    group([(j, r0 + i, j == 0) for i in range(A_GROUP // nb) for j in range(nb)])

    den_scr[0] = jnp.full((BAND, LANES), jnp.inf, F32)
    den_scr[1] = jnp.zeros((BAND, LANES), F32)
    attend(True)
    in_range = (jnp.min(den_scr[0]) >= A_DEN_MIN) & (jnp.max(den_scr[1]) <= A_DEN_MAX)
    pl.when(jnp.logical_not(in_range))(functools.partial(attend, False))
    lse = l_scr[...]
    w = jnp.exp2(lse - jnp.max(lse, axis=0, keepdims=True))
    mixed = jnp.sum(w * o_scr[...], axis=0) / jnp.sum(w, axis=0)
    o_ref[...] = (mixed * z_ref[0].astype(F32)).astype(o_ref.dtype)


def _dilated_mixture(proj, views, shift):
    s = proj.shape[1]
    operands, in_specs, scratch = [], [], []
    for d, view in zip(DILATIONS, (proj,) + tuple(views)):
        rows, width = A_TILE // d, d * LANES
        blk = (1, rows, width)
        operands += [view] * 5
        in_specs += [
            pl.BlockSpec(blk, lambda c, n: (AQ + c, n, 0)),
            pl.BlockSpec(blk, lambda c, n: (AK + c, n, 0)),
            pl.BlockSpec(blk, lambda c, n: (AK + c, jnp.maximum(n - 1, 0), 0)),
            pl.BlockSpec(blk, lambda c, n: (AV + c, n, 0)),
            pl.BlockSpec(blk, lambda c, n: (AV + c, jnp.maximum(n - 1, 0), 0)),
        ]
        scratch += [pltpu.VMEM((rows + BAND, width), MXU_DTYPE)] * 2
    scratch += [pltpu.VMEM((len(DILATIONS), A_TILE, LANES), F32)] * 2
    scratch.append(pltpu.VMEM((2, BAND, LANES), F32))
    operands += [proj, jnp.full((8, LANES), shift, F32)]
    in_specs += [pl.BlockSpec((1, A_TILE, LANES), lambda c, n: (AZ + c, n, 0)),
                 pl.BlockSpec((8, LANES), lambda c, n: (0, 0))]
    return pl.pallas_call(
        _dilated_body,
        grid=(A_CHUNKS, s // A_TILE),
        in_specs=in_specs,
        out_specs=pl.BlockSpec((A_TILE, LANES), lambda c, n: (n, c)),
        out_shape=jax.ShapeDtypeStruct((s, A_WIDTH), MXU_DTYPE),
        scratch_shapes=scratch,
        compiler_params=_params(("parallel", "arbitrary")),
        name="dilated_mixture",
    )(*operands)


FOX_AUG = 3
FOX_VROWS = HEAD_DIM + 16
FOX_DEN_MIN, FOX_DEN_MAX = 2.0 ** -90, 2.0 ** 100
FOX_ZERO_EXP = 160.0


def _split3(c):
    hi = c.astype(jnp.bfloat16).astype(F32)
    r = c - hi
    mid = r.astype(jnp.bfloat16).astype(F32)
    return hi, mid, r - mid


def _fox_prep_body(q_ref, k_ref, v_ref, ccol_ref, crow_ref, qt_ref, ka_ref, vt_ref):
    tm = q_ref.shape[1]
    q_t = q_ref[0].astype(F32).T
    v_t = v_ref[0].astype(F32).T
    k = k_ref[0]
    sub = lax.broadcasted_iota(jnp.int32, (HEAD_DIM, tm), 0)
    lane = lax.broadcasted_iota(jnp.int32, (tm, LANES), 1)
    v_tail = (lax.broadcasted_iota(jnp.int32, (FOX_VROWS - HEAD_DIM, tm), 0) == 0).astype(F32)
    for h in range(2):
        hi, mid, lo = _split3(crow_ref[0, h:h + 1, :])
        aug_q = jnp.where(sub == 0, hi, jnp.where(sub == 1, mid, jnp.where(
            sub == 2, lo, jnp.where(sub < 2 * FOX_AUG, 1.0, 0.0))))
        q_h = q_t[h * HEAD_DIM:(h + 1) * HEAD_DIM]
        parts = [q_h, aug_q] if h == 0 else [aug_q, q_h]
        qt_ref[h] = jnp.concatenate(parts, axis=0).astype(qt_ref.dtype)
        hi, mid, lo = _split3(ccol_ref[0, :, h:h + 1])
        a0 = (1 - h) * HEAD_DIM
        aug_k = jnp.where(lane == a0 + FOX_AUG, -hi, jnp.where(lane == a0 + FOX_AUG + 1, -mid, jnp.where(
            lane == a0 + FOX_AUG + 2, -lo, jnp.where((lane >= a0) & (lane < a0 + FOX_AUG), 1.0, 0.0))))
        own = (lane < HEAD_DIM) if h == 0 else (lane >= HEAD_DIM)
        ka_ref[h] = jnp.where(own, k, aug_k.astype(k.dtype))
        vt_ref[h, 0] = jnp.concatenate([v_t[h * HEAD_DIM:(h + 1) * HEAD_DIM], v_tail],
                                       axis=0).astype(vt_ref.dtype)


def _fox_prep(proj, c, shift, tm):
    s = proj.shape[1]
    c_col = c.reshape(B_CHUNKS, 2, s).transpose(0, 2, 1)
    c_row = c.reshape(B_CHUNKS, 2, s) - shift
    return pl.pallas_call(
        _fox_prep_body,
        grid=(B_CHUNKS, s // tm),
        in_specs=[
            pl.BlockSpec((1, tm, LANES), lambda p, i: (BQ + p, i, 0)),
            pl.BlockSpec((1, tm, LANES), lambda p, i: (BK + p, i, 0)),
            pl.BlockSpec((1, tm, LANES), lambda p, i: (BV + p, i, 0)),
            pl.BlockSpec((1, tm, 2), lambda p, i: (p, i, 0)),
            pl.BlockSpec((1, 2, tm), lambda p, i: (p, 0, i)),
        ],
        out_specs=[
            pl.BlockSpec((2, LANES, tm), lambda p, i: (p, 0, i)),
            pl.BlockSpec((2, tm, LANES), lambda p, i: (p, i, 0)),
            pl.BlockSpec((2, 1, FOX_VROWS, tm), lambda p, i: (p, i, 0, 0)),
        ],
        out_shape=[
            jax.ShapeDtypeStruct((B_HEADS, LANES, s), MXU_DTYPE),
            jax.ShapeDtypeStruct((B_HEADS, s, LANES), MXU_DTYPE),
            jax.ShapeDtypeStruct((B_HEADS, s // tm, FOX_VROWS, tm), MXU_DTYPE),
        ],
        compiler_params=_params(("parallel", "parallel")),
        name="fox_prep",
    )(proj, proj, proj, c_col, c_row)


def _fox_body(lo_ref, qt_ref, ka_ref, vt_ref, z_ref, o_ref, s0_scr, s1_scr, m_scr, acc_scr, *,
              blk):
    qi = pl.program_id(1)
    causal = (lax.broadcasted_iota(jnp.int32, (blk, blk), 0)
              <= lax.broadcasted_iota(jnp.int32, (blk, blk), 1))
    q_t = (qt_ref[0], qt_ref[1])

    def logits(kb, dst):
        k0 = pl.multiple_of(kb * blk, blk)
        for h in range(2):
            dst[h] = jnp.dot(ka_ref[h, pl.ds(k0, blk), :], q_t[h], preferred_element_type=F32)

    def accumulate_shifted(kb, src, diagonal=False):
        for h in range(2):
            s = src[h]
            if diagonal:
                s = jnp.where(causal, s, -jnp.inf)
            acc_scr[h] += jnp.dot(vt_ref[h, kb], jnp.exp2(s).astype(MXU_DTYPE),
                                  preferred_element_type=F32)

    def accumulate_online(kb, src, diagonal=False):
        for h in range(2):
            s = src[h]
            if diagonal:
                s = jnp.where(causal, s, -jnp.inf)
            m = m_scr[h]
            m_new = jnp.maximum(m, jnp.max(s, axis=0, keepdims=True))
            p = jnp.exp2(s - m_new).astype(MXU_DTYPE)
            acc_scr[h] = (jnp.exp2(m - m_new) * acc_scr[h]
                          + jnp.dot(vt_ref[h, kb], p, preferred_element_type=F32))
            m_scr[h] = m_new

    def attend(accumulate, lo):
        acc_scr[...] = jnp.zeros(acc_scr.shape, F32)
        logits(lo, s0_scr)

        def pair(j, carry):
            kb = lo + 2 * j
            logits(kb + 1, s1_scr)
            accumulate(kb, s0_scr)
            logits(kb + 2, s0_scr)
            accumulate(kb + 1, s1_scr)
            return carry

        lax.fori_loop(0, (qi - lo) // 2, pair, 0)

        @pl.when((qi - lo) % 2 == 0)
        def _():
            accumulate(qi, s0_scr, diagonal=True)

        @pl.when((qi - lo) % 2 == 1)
        def _():
            logits(qi, s1_scr)
            accumulate(qi - 1, s0_scr)
            accumulate(qi, s1_scr, diagonal=True)

        outs = [acc_scr[h, 0:HEAD_DIM] / acc_scr[h, HEAD_DIM:HEAD_DIM + 1] for h in range(2)]
        o = jnp.concatenate(outs, axis=0).T
        o_ref[...] = (o * z_ref[0].astype(F32)).astype(o_ref.dtype)

    attend(accumulate_shifted, lo_ref[pl.program_id(0), qi])
    den = jnp.concatenate([acc_scr[h, HEAD_DIM:HEAD_DIM + 1] for h in range(2)], axis=0)
    in_range = (jnp.min(den) >= FOX_DEN_MIN) & (jnp.max(den) <= FOX_DEN_MAX)

    @pl.when(jnp.logical_not(in_range))
    def _():
        m_scr[...] = jnp.full(m_scr.shape, -jnp.inf, F32)
        attend(accumulate_online, 0)


def _fox_first_blocks(c, shift, blk):
    heads, s = c.shape
    nblk = s // blk
    c_first = c[:, ::blk]
    c_last = c[:, blk - 1::blk]
    dead = (c_first[:, :, None] - c_last[:, None, :]) <= -(FOX_ZERO_EXP + 0.02 * shift)
    dead &= jnp.arange(nblk)[None, None, :] < jnp.arange(nblk)[None, :, None]
    prefix = jnp.cumprod(dead.astype(jnp.int32), axis=-1)
    lo = jnp.sum(prefix, axis=-1)
    return jnp.min(lo.reshape(heads // 2, 2, nblk), axis=1).astype(jnp.int32)


def _forgetting_attention(proj, c, shift, blk=512):
    s = proj.shape[1]
    qt, ka, vt = _fox_prep(proj, c, shift, blk)
    lo = _fox_first_blocks(c, shift, blk)
    grid_spec = pltpu.PrefetchScalarGridSpec(
        num_scalar_prefetch=1,
        grid=(B_CHUNKS, s // blk),
        in_specs=[
            pl.BlockSpec((2, LANES, blk), lambda p, i, lo: (p, 0, i)),
            pl.BlockSpec((2, s, LANES), lambda p, i, lo: (p, 0, 0)),
            pl.BlockSpec((2, s // blk, FOX_VROWS, blk), lambda p, i, lo: (p, 0, 0, 0)),
            pl.BlockSpec((1, blk, LANES), lambda p, i, lo: (BZ + p, i, 0)),
        ],
        out_specs=pl.BlockSpec((blk, LANES), lambda p, i, lo: (i, p)),
        scratch_shapes=[pltpu.VMEM((2, blk, blk), F32), pltpu.VMEM((2, blk, blk), F32),
                        pltpu.VMEM((2, 1, blk), F32), pltpu.VMEM((2, FOX_VROWS, blk), F32)],
    )
    return pl.pallas_call(
        functools.partial(_fox_body, blk=blk),
        grid_spec=grid_spec,
        out_shape=jax.ShapeDtypeStruct((s, B_WIDTH), MXU_DTYPE),
        compiler_params=_params(("parallel", "arbitrary")),
        name="forgetting_attention",
    )(lo, qt, ka, vt, proj)


GLA_GROUP = 4


def _gla_level_matrix():
    c = GLA_CHUNK
    mats = []
    for b in GLA_LEVELS:
        m = np.zeros((c, c), np.float32)
        for i in range(c):
            pivot = (i // (2 * b)) * 2 * b + b - 1
            if i > pivot:
                m[i, pivot + 1:i + 1] = 1.0
            else:
                m[i, i + 1:pivot + 1] = 1.0
        mats.append(m)
    mats.append(np.tril(np.ones((c, c), np.float32)))
    return np.concatenate(mats, axis=0)


def _gla_body(q_ref, k_ref, v_ref, g_ref, z_ref, gu_ref, gb_ref, og_ref, lvl_ref, o_ref,
              st_scr, *, tile):
    c = GLA_CHUNK
    nlev = len(GLA_LEVELS)
    nch = tile // c
    grp = GLA_GROUP * c

    @pl.when(pl.program_id(1) == 0)
    def _():
        st_scr[...] = jnp.zeros_like(st_scr)

    gate_in = g_ref[:, 0:GATE_RANK]
    gate_hi = gate_in.astype(MXU_DTYPE)
    gate_lo = (gate_in - gate_hi.astype(F32)).astype(MXU_DTYPE)
    gate_w = gu_ref[0].astype(MXU_DTYPE)
    logit = (jnp.dot(gate_hi, gate_w, preferred_element_type=F32)
             + jnp.dot(gate_lo, gate_w, preferred_element_type=F32) + gb_ref[0])
    la = _log_sigmoid(logit) * (1.0 / GATE_TEMP)

    la_cat = jnp.concatenate([la[ci * c:(ci + 1) * c] for ci in range(nch)], axis=1)
    la_hi = la_cat.astype(MXU_DTYPE)
    la_lo = (la_cat - la_hi.astype(F32)).astype(MXU_DTYPE)
    f = jnp.exp(jnp.dot(lvl_ref[0:nlev * c, :], la_hi, preferred_element_type=F32))
    tri = lvl_ref[nlev * c:(nlev + 1) * c, :]
    bc = (jnp.dot(tri, la_hi, preferred_element_type=F32)
          + jnp.dot(tri, la_lo, preferred_element_type=F32))

    row = lax.broadcasted_iota(jnp.int32, (grp, grp), 0)
    col = lax.broadcasted_iota(jnp.int32, (grp, grp), 1)
    diff = (row ^ col) & (c - 1)
    level = jnp.full((grp, grp), nlev, jnp.int32)
    for li, b in reversed(list(enumerate(GLA_LEVELS))):
        level = jnp.where(diff >= b, li, level)
    level = jnp.where(((row & -c) == (col & -c)) & (col <= row), level, -1)
    out_gain = og_ref[...]

    state = st_scr[...]
    for g0 in range(0, nch, GLA_GROUP):
        rows = slice(g0 * c, g0 * c + grp)
        q = q_ref[0, rows, :].astype(F32)
        k = k_ref[0, rows, :].astype(F32)
        v = jnp.concatenate([v_ref[0, rows, :], v_ref[1, rows, :]], axis=-1)
        z = jnp.concatenate([z_ref[0, rows, :], z_ref[1, rows, :]], axis=-1)
        a = lax.dot_general(q.astype(MXU_DTYPE), k.astype(MXU_DTYPE), _NT,
                            preferred_element_type=F32)
        attn = jnp.where(level == nlev, a, 0.0)
        for li in range(nlev):
            fl = jnp.concatenate([f[li * c:(li + 1) * c, ch * LANES:(ch + 1) * LANES]
                                  for ch in range(g0, g0 + GLA_GROUP)], axis=0)
            a = lax.dot_general((q * fl).astype(MXU_DTYPE), (k * fl).astype(MXU_DTYPE), _NT,
                                preferred_element_type=F32)
            attn = jnp.where(level == li, a, attn)
        o_intra = jnp.dot(attn.astype(MXU_DTYPE), v, preferred_element_type=F32)

        for ci in range(GLA_GROUP):
            ch = g0 + ci
            r = slice(ci * c, (ci + 1) * c)
            bc_c = bc[:, ch * LANES:(ch + 1) * LANES]
            b_last = bc_c[c - 1:c, :]
            q_dec = (q[r] * jnp.exp(bc_c)).astype(MXU_DTYPE)
            o = o_intra[r] + jnp.dot(q_dec, state.astype(MXU_DTYPE), preferred_element_type=F32)
            k_dec = (k[r] * jnp.exp(b_last - bc_c)).astype(MXU_DTYPE)
            decay = jnp.broadcast_to(jnp.exp(b_last), (LANES, LANES)).T
            state = (state * jnp.concatenate([decay, decay], axis=1)
                     + lax.dot_general(k_dec, v[r], _TN, preferred_element_type=F32))
            ms = jnp.sum(o * o, axis=-1, keepdims=True) * (1.0 / C_DV)
            y = o * lax.rsqrt(ms + RMS_EPS) * out_gain * z[r].astype(F32)
            o_ref[pl.ds(ch * c, c), :] = y.astype(o_ref.dtype)
    st_scr[...] = state


def _gated_linear_attention(proj, gate, gate_up, gate_bias, out_gain, tile=512):
    s = proj.shape[1]
    lvl = jnp.asarray(_gla_level_matrix(), MXU_DTYPE)
    return pl.pallas_call(
        functools.partial(_gla_body, tile=tile),
        grid=(C_HEADS, s // tile),
        in_specs=[
            pl.BlockSpec((1, tile, LANES), lambda h, t: (CQ + h, t, 0)),
            pl.BlockSpec((1, tile, LANES), lambda h, t: (CK + h, t, 0)),
            pl.BlockSpec((2, tile, LANES), lambda h, t: (CV // 2 + h, t, 0)),
            pl.BlockSpec((tile, LANES), lambda h, t: (t, 0)),
            pl.BlockSpec((2, tile, LANES), lambda h, t: (CZ // 2 + h, t, 0)),
            pl.BlockSpec((1, GATE_RANK, C_DK_PAD), lambda h, t: (h, 0, 0)),
            pl.BlockSpec((1, 1, C_DK_PAD), lambda h, t: (h, 0, 0)),
            pl.BlockSpec((1, C_DV_PAD), lambda h, t: (0, 0)),
            pl.BlockSpec(lvl.shape, lambda h, t: (0, 0)),
        ],
        out_specs=pl.BlockSpec((tile, C_DV_PAD), lambda h, t: (t, h)),
        out_shape=jax.ShapeDtypeStruct((s, C_HEADS * C_DV_PAD), MXU_DTYPE),
        scratch_shapes=[pltpu.VMEM((C_DK_PAD, C_DV_PAD), F32)],
        compiler_params=_params(("parallel", "arbitrary")),
        name="gated_linear_attention",
    )(proj, proj, proj, gate, proj, gate_up, gate_bias, out_gain, lvl)


def _out_body(x_ref, a_ref, b_ref, c_ref, wab_ref, wc_ref, o_ref):
    wa = wab_ref[0, 0:A_WIDTH, :].astype(MXU_DTYPE)
    wb = wab_ref[0, A_WIDTH:A_WIDTH + B_WIDTH, :].astype(MXU_DTYPE)
    acc = jnp.dot(a_ref[...], wa, preferred_element_type=F32)
    acc = acc + jnp.dot(b_ref[...], wb, preferred_element_type=F32)
    acc = acc + jnp.dot(c_ref[...], wc_ref[0], preferred_element_type=F32)
    o_ref[...] = x_ref[...] + acc


def _out_proj(x, ma, mb, mc, w_out, wo_c, layer, tm=1024, tn=512):
    s, d = x.shape
    row_blk = lambda m: pl.BlockSpec((tm, m.shape[1]), lambda i, j: (i, 0))
    return pl.pallas_call(
        _out_body,
        grid=(s // tm, d // tn),
        in_specs=[pl.BlockSpec((tm, tn), lambda i, j: (i, j)),
                  row_blk(ma), row_blk(mb), row_blk(mc),
                  pl.BlockSpec((1, A_WIDTH + B_WIDTH, tn), lambda i, j: (layer, 0, j)),
                  pl.BlockSpec((1, mc.shape[1], tn), lambda i, j: (layer, 0, j))],
        out_specs=pl.BlockSpec((tm, tn), lambda i, j: (i, j)),
        out_shape=jax.ShapeDtypeStruct((s, d), F32),
        compiler_params=_params(("parallel", "arbitrary")),
        name="out_proj",
    )(x, ma, mb, mc, w_out, wo_c)


def _pad_heads(w, heads, width, padded):
    lead = w.shape[:-1]
    w = w.reshape(*lead, heads, width)
    w = jnp.pad(w, [(0, 0)] * len(lead) + [(0, 0), (0, padded - width)])
    return w.reshape(*lead, heads * padded)


def _prepare(w_in, w_out, a_q_gain, a_k_gain, b_q_gain, b_k_gain):
    layers, d, _ = w_in.shape
    w_t = jnp.swapaxes(w_in, 1, 2)
    ck, cv = C_HEADS * C_DK, C_HEADS * C_DV
    ab = 4 * (A_WIDTH + B_WIDTH)
    bounds = np.cumsum([ab, B_HEADS, ck, ck, cv, cv, GATE_RANK])
    bf, cq, ckk, cvv, cz, cr = [w_t[:, lo:hi] for lo, hi in zip(bounds[:-1], bounds[1:])]

    def pad_rows(w, width, padded):
        w = w.reshape(layers, C_HEADS, width, d)
        w = jnp.pad(w, ((0, 0), (0, 0), (0, padded - width), (0, 0)))
        return w.reshape(layers, C_HEADS * padded, d)

    pad_k = lambda w: pad_rows(w, C_DK, C_DK_PAD)
    pad_v = lambda w: pad_rows(w, C_DV, C_DV_PAD)
    tail = jnp.zeros((layers, MXU_COLS - GATE_RANK - B_HEADS, d), F32)
    w_c_t = jnp.concatenate([pad_k(cq), pad_k(ckk), pad_v(cvv), pad_v(cz), cr, bf, tail], axis=1)

    q_scale = HEAD_DIM ** -0.5 * LOG2E
    tile_heads = lambda g, n: jnp.tile(g, (1, n))
    ones = lambda n: jnp.ones((layers, n), F32)
    aux_ab = jnp.concatenate([
        tile_heads(a_q_gain * q_scale, A_HEADS), tile_heads(a_k_gain, A_HEADS), ones(2 * A_WIDTH),
        tile_heads(b_q_gain * q_scale, B_HEADS), tile_heads(b_k_gain, B_HEADS), ones(2 * B_WIDTH)],
        axis=-1)[:, None, :]
    aux_c = jnp.concatenate([
        jnp.full((layers, C_HEADS * C_DK_PAD), C_DK ** -0.5, F32),
        ones(C_HEADS * (C_DK_PAD + 2 * C_DV_PAD) + MXU_COLS)], axis=-1)[:, None, :]

    wo_c = w_out[:, A_WIDTH + B_WIDTH:].reshape(layers, C_HEADS, C_DV, d)
    wo_c = jnp.pad(wo_c, ((0, 0), (0, 0), (0, C_DV_PAD - C_DV), (0, 0)))
    wo_c = wo_c.reshape(layers, C_HEADS * C_DV_PAD, d).astype(MXU_DTYPE)
    return w_t, w_c_t, aux_ab, aux_c, wo_c


def _layer(x, layer, norm_g, w_in, w_c, aux_ab, aux_c, w_out, wo_c, fox_bias, a_shift, fox_shift,
           gla_gate_up, gla_gate_bias, gla_out_gain):
    h = _rmsnorm(x, norm_g, layer)
    proj_ab, v4, v16 = _in_proj(h, w_in, aux_ab, layer, AB_KINDS, "in_proj_ab")
    proj_c, gate = _in_proj(h, w_c, aux_c, layer, C_KINDS, "in_proj_c")
    c = _fox_cumsum(gate[:, GATE_RANK:GATE_RANK + B_HEADS].T, fox_bias)
    mixed_a = _dilated_mixture(proj_ab, (v4, v16), a_shift)
    mixed_b = _forgetting_attention(proj_ab, c, fox_shift)
    gate_up = _pad_heads(gla_gate_up, C_HEADS, C_DK, C_DK_PAD)
    gate_up = gate_up.reshape(GATE_RANK, C_HEADS, C_DK_PAD).transpose(1, 0, 2)
    gate_bias = _pad_heads(gla_gate_bias, C_HEADS, C_DK, C_DK_PAD).reshape(C_HEADS, 1, C_DK_PAD)
    out_gain = jnp.pad(gla_out_gain, (0, C_DV_PAD - C_DV)).reshape(1, C_DV_PAD)
    mixed_c = _gated_linear_attention(proj_c, gate, gate_up, gate_bias, out_gain)
    return _out_proj(x, mixed_a, mixed_b, mixed_c, w_out, wo_c, layer)


@jax.jit
def kernel(x, norm_g, w_in, a_q_gain, a_k_gain, b_q_gain, b_k_gain, fox_bias, gla_gate_up,
           gla_gate_bias, gla_out_gain, w_out):
    bsz, s, d = x.shape
    assert bsz == 1, "batch size 1 only"
    w_t, w_c, aux_ab, aux_c, wo_c = _prepare(w_in, w_out, a_q_gain, a_k_gain, b_q_gain, b_k_gain)
    norm_g = norm_g[:, None, :]
    bound = lambda gq, gk: (HEAD_DIM ** 0.5 * LOG2E * jnp.max(jnp.abs(gq), axis=-1)
                            * jnp.max(jnp.abs(gk), axis=-1))
    a_shift, fox_shift = bound(a_q_gain, a_k_gain), bound(b_q_gain, b_k_gain)
    y = x.reshape(s, d)
    for layer in range(w_in.shape[0]):
        y = _layer(y, layer, norm_g, w_t, w_c, aux_ab, aux_c, w_out, wo_c, fox_bias[layer],
                   a_shift[layer], fox_shift[layer], gla_gate_up[layer], gla_gate_bias[layer],
                   gla_out_gain[layer])
    return y.reshape(bsz, s, d)
```

```python
import functools

import numpy as np
import jax
import jax.numpy as jnp
from jax import lax
from jax.experimental import pallas as pl
from jax.experimental.pallas import tpu as pltpu

F32 = jnp.float32
MXU_DTYPE = jnp.bfloat16

LANES = 128
MXU_COLS = 256
HEAD_DIM = 64
A_HEADS, B_HEADS, C_HEADS = 12, 8, 4
A_CHUNKS, B_CHUNKS = A_HEADS // 2, B_HEADS // 2
C_DK, C_DV = 96, 192
C_DK_PAD, C_DV_PAD = 128, 256
GATE_RANK = 16
GATE_TEMP = 16.0
RMS_EPS = 1e-6
LOG2E = 1.4426950408889634
DILATIONS = (1, 4, 16)
BAND = 128
A_TILE = 2048
GLA_CHUNK = 64
GLA_LEVELS = (32, 16, 8, 4, 2, 1)
VMEM_LIMIT = 56 * 1024 * 1024

AQ, AK, AV, AZ = 0, 6, 12, 18
BQ, BK, BV, BZ = 24, 28, 32, 36
CQ, CK, CV, CZ = 0, 4, 8, 16
A_WIDTH, B_WIDTH = A_HEADS * HEAD_DIM, B_HEADS * HEAD_DIM

_NT = (((1,), (1,)), ((), ()))
_TN = (((0,), (0,)), ((), ()))


def _params(sem):
    return pltpu.CompilerParams(dimension_semantics=sem, vmem_limit_bytes=VMEM_LIMIT)


def _log_sigmoid(x):
    return jnp.minimum(x, 0.0) - jnp.log1p(jnp.exp(-jnp.abs(x)))


AB_KINDS = (("headnorm+views",) * (2 * A_CHUNKS) + ("scale+views",) * A_CHUNKS
            + ("silu",) * A_CHUNKS
            + ("headnorm",) * (2 * B_CHUNKS) + ("scale",) * B_CHUNKS + ("silu",) * B_CHUNKS)[::2]
C_KINDS = (("scale",) * (4 * C_HEADS) + ("silu",) * (2 * C_HEADS))[::2] + ("gate",)


def _kind_ranges(kinds, kind):
    runs, start = [], None
    for t, k in enumerate(kinds + (None,)):
        if k == kind and start is None:
            start = t
        elif k != kind and start is not None:
            runs.append((start, t))
            start = None
    return runs


def _view_tiles(kinds):
    n = sum(k.endswith("+views") for k in kinds)
    assert all(k.endswith("+views") for k in kinds[:n])
    return n


def _rmsnorm_body(x_ref, g_ref, o_ref):
    x = x_ref[...]
    ms = jnp.mean(x * x, axis=-1, keepdims=True)
    o_ref[...] = (x * lax.rsqrt(ms + RMS_EPS) * g_ref[0]).astype(o_ref.dtype)


def _rmsnorm(x, norm_g, layer, tm=512):
    s, d = x.shape
    return pl.pallas_call(
        _rmsnorm_body,
        grid=(s // tm,),
        in_specs=[pl.BlockSpec((tm, d), lambda i: (i, 0)),
                  pl.BlockSpec((1, 1, d), lambda i: (layer, 0, 0))],
        out_specs=pl.BlockSpec((tm, d), lambda i: (i, 0)),
        out_shape=jax.ShapeDtypeStruct((s, d), MXU_DTYPE),
        compiler_params=_params(("parallel",)),
        name="rmsnorm",
    )(x, norm_g)


def _proj_body(h_ref, w_ref, aux_ref, *rest, kinds):
    view_tiles = _view_tiles(kinds)
    n_tiles = len(kinds)
    o_ref, rest = rest[0], rest[1:]
    gate_ref = None
    if kinds[-1] == "gate":
        gate_ref, rest = rest[0], rest[1:]
    if view_tiles:
        view_refs, rest = rest[:len(DILATIONS) - 1], rest[len(DILATIONS) - 1:]
        n_stage = MXU_COLS // LANES
        stage, rest = rest[-n_stage:], rest[:-n_stage]
    acc_scr = rest[0:2]
    j = pl.program_id(1)
    tm = h_ref.shape[0]

    def matmul(dst, rows):
        dst[rows, :] = lax.dot_general(h_ref[rows, :], w_ref[0].astype(MXU_DTYPE), _NT,
                                       preferred_element_type=F32)

    def epilogue(src, kind, rows):
        r0, nr = rows.start, rows.stop - rows.start
        if kind == "gate":
            gate_ref[rows, :] = src[rows, 0:LANES]
            return
        base = kind.split("+")[0]
        for c in range(MXU_COLS // LANES):
            cols = slice(c * LANES, (c + 1) * LANES)
            y = src[rows, cols]
            if base == "headnorm":
                first = lax.broadcasted_iota(jnp.int32, y.shape, 1) < HEAD_DIM
                y2 = y * y
                s0 = jnp.sum(jnp.where(first, y2, 0.0), axis=-1, keepdims=True)
                s1 = jnp.sum(jnp.where(first, 0.0, y2), axis=-1, keepdims=True)
                ms = jnp.where(first, s0, s1) * (1.0 / HEAD_DIM)
                y = y * lax.rsqrt(ms + RMS_EPS) * aux_ref[0, :, cols]
            elif base == "scale":
                y = y * aux_ref[0, :, cols]
            elif base == "silu":
                y = y * jax.nn.sigmoid(y)
            o_ref[c, rows, :] = y.astype(o_ref.dtype)
            if not kind.endswith("+views"):
                continue
            stage[c][rows, :] = y
            for view, dil in zip(view_refs, DILATIONS[1:]):
                for r in range(dil):
                    picked = stage[c][pl.ds(r0 + r, nr // dil, stride=dil), :]
                    view[c, r0 // dil:(r0 + nr) // dil, r * LANES:(r + 1) * LANES] = (
                        picked.astype(view.dtype))

    halves = tuple(slice(r, r + tm // 4) for r in range(0, tm, tm // 4))

    @pl.when(j == 0)
    def _():
        for rows in halves:
            matmul(acc_scr[0], rows)

    for parity in range(2):
        for kind in sorted(set(kinds[:-1])):
            in_kind = functools.reduce(
                jnp.logical_or, [(j - 1 >= lo) & (j - 1 < min(hi, n_tiles - 1))
                                 for lo, hi in _kind_ranges(kinds, kind) if lo < n_tiles - 1])

            @pl.when((j >= 1) & (j < n_tiles) & (j % 2 == parity) & in_kind)
            def _(parity=parity, kind=kind):
                for rows in halves:
                    epilogue(acc_scr[1 - parity], kind, rows)
                    matmul(acc_scr[parity], rows)

    @pl.when(j == n_tiles)
    def _():
        for rows in halves:
            epilogue(acc_scr[(n_tiles - 1) % 2], kinds[-1], rows)


def _in_proj(h, w_t, aux, layer, kinds, name, tm=2048):
    s, d = h.shape
    n_tiles = len(kinds)
    view_tiles = _view_tiles(kinds)
    chunks = MXU_COLS // LANES
    has_gate = kinds[-1] == "gate"
    out_tiles = n_tiles - 1 if has_gate else n_tiles
    prev = lambda j, n: jnp.clip(j - 1, 0, n - 1)
    out_shape = [jax.ShapeDtypeStruct((out_tiles * chunks, s, LANES), MXU_DTYPE)]
    out_specs = [pl.BlockSpec((chunks, tm, LANES), lambda i, j: (prev(j, out_tiles), i, 0))]
    scratch = [pltpu.VMEM((tm, MXU_COLS), F32), pltpu.VMEM((tm, MXU_COLS), F32)]
    if has_gate:
        out_shape.append(jax.ShapeDtypeStruct((s, LANES), F32))
        out_specs.append(pl.BlockSpec((tm, LANES), lambda i, j: (i, 0)))
    if view_tiles:
        for dil in DILATIONS[1:]:
            out_shape.append(
                jax.ShapeDtypeStruct((view_tiles * chunks, s // dil, dil * LANES), MXU_DTYPE))
            out_specs.append(pl.BlockSpec((chunks, tm // dil, dil * LANES),
                                          lambda i, j: (prev(j, view_tiles), i, 0)))
        scratch += [pltpu.VMEM((tm, LANES), F32)] * chunks
    return pl.pallas_call(
        functools.partial(_proj_body, kinds=kinds),
        grid=(s // tm, n_tiles + 1),
        in_specs=[pl.BlockSpec((tm, d), lambda i, j: (i, 0)),
                  pl.BlockSpec((1, MXU_COLS, d),
                               lambda i, j: (layer, jnp.minimum(j, n_tiles - 1), 0)),
                  pl.BlockSpec((1, 1, MXU_COLS), lambda i, j: (layer, 0, prev(j, n_tiles)))],
        out_specs=out_specs,
        out_shape=out_shape,
        scratch_shapes=scratch,
        compiler_params=_params(("parallel", "arbitrary")),
        name=name,
    )(h, w_t, aux)


def _fox_cumsum_body(x_ref, b_ref, o_ref):
    x = _log_sigmoid(x_ref[...] + b_ref[...])
    idx = lax.broadcasted_iota(jnp.int32, x.shape, 1)
    shift = 1
    while shift < x.shape[1]:
        x = x + jnp.where(idx >= shift, pltpu.roll(x, shift, axis=1), 0.0)
        shift *= 2
    o_ref[...] = x * LOG2E


def _fox_cumsum(logit_t, bias):
    nh, s = logit_t.shape
    return pl.pallas_call(
        _fox_cumsum_body,
        out_shape=jax.ShapeDtypeStruct((nh, s), F32),
        compiler_params=pltpu.CompilerParams(vmem_limit_bytes=VMEM_LIMIT),
        name="fox_cumsum",
    )(logit_t, bias.reshape(nh, 1))


A_GROUP = 8


A_DEN_MIN, A_DEN_MAX = 2.0 ** -90, 2.0 ** 100


def _band_attention(blocks, shift):
    first = lax.broadcasted_iota(jnp.int32, (BAND, LANES), 1) < HEAD_DIM
    first_kv = lax.broadcasted_iota(jnp.int32, (2 * BAND, LANES), 1) < HEAD_DIM
    chains = [(b, h) for b in range(len(blocks)) for h in range(2)]
    logits = []
    for b, h in chains:
        q, k2 = blocks[b][0], blocks[b][1]
        zero = jnp.zeros_like(q)
        qm = jnp.where(first, q, zero) if h == 0 else jnp.where(first, zero, q)
        logits.append(lax.dot_general(qm, k2, _NT, preferred_element_type=F32))
    logits = [s + blocks[b][3] for (b, h), s in zip(chains, logits)]
    if shift is None:
        maxes = [jnp.max(s, axis=-1, keepdims=True) for s in logits]
        probs = [jnp.exp2(s - m).astype(MXU_DTYPE) for s, m in zip(logits, maxes)]
    else:
        maxes = [shift] * len(chains)
        probs = [jnp.exp2(s).astype(MXU_DTYPE) for s in logits]
    res = []
    for (b, h), p in zip(chains, probs):
        v2 = blocks[b][2]
        one = jnp.ones_like(v2)
        vh = jnp.where(first_kv, v2, one) if h == 0 else jnp.where(first_kv, one, v2)
        res.append(jnp.dot(p, vh, preferred_element_type=F32))
    results = []
    for b in range(len(blocks)):
        r0, r1 = res[2 * b], res[2 * b + 1]
        num = jnp.where(first, r0, r1)
        den = pltpu.roll(jnp.where(first, r1, r0), HEAD_DIM, axis=1)
        m = maxes[2 * b] if shift is not None else jnp.where(first, maxes[2 * b], maxes[2 * b + 1])
        results.append((num / den, m + jnp.log2(den), den))
    return results


def _dilated_body(*refs):
    ins, z_ref, u_ref, o_ref, scr = refs[:15], refs[15], refs[16], refs[17], refs[18:]
    kv_scr, o_scr, l_scr, den_scr = scr[:6], scr[6], scr[7], scr[8]
    n = pl.program_id(1)
    for bi, d in enumerate(DILATIONS):
        kc, kp, vc, vp = ins[5 * bi + 1:5 * bi + 5]
        kf, vf = kv_scr[2 * bi:2 * bi + 2]
        rows = A_TILE // d
        for cur, prev, full in ((kc, kp, kf), (vc, vp, vf)):
            full[0:BAND, :] = prev[0, rows - BAND:rows, :]
            full[BAND:BAND + rows, :] = cur[0]

    row = lax.broadcasted_iota(jnp.int32, (BAND, 2 * BAND), 0)
    col = lax.broadcasted_iota(jnp.int32, (BAND, 2 * BAND), 1)
    band = (col >= row) & (col <= row + BAND)
    shift = u_ref[0:1, 0:1]

    def attend(shifted):
        offset = shift if shifted else 0.0
        bias_band = jnp.where(band, 0.0, -jnp.inf) - offset
        bias_first = jnp.where(band & (col >= BAND), 0.0, -jnp.inf) - offset
        for bi, d in enumerate(DILATIONS):
            q_ref = ins[5 * bi]
            kf, vf = kv_scr[2 * bi:2 * bi + 2]
            nb = A_TILE // d // BAND

            def group(members, q_ref=q_ref, kf=kf, vf=vf, d=d, nb=nb, bi=bi):
                blocks = []
                for j, r, maybe_first in members:
                    cols = slice(r * LANES, (r + 1) * LANES)
                    row0 = j * BAND if isinstance(j, int) else pl.multiple_of(j * BAND, BAND)
                    bias = bias_band
                    if maybe_first:
                        bias = jnp.where(n * nb + j >= 1, bias_band, bias_first)
                    blocks.append((q_ref[0, pl.ds(row0, BAND), cols],
                                   kf[pl.ds(row0, 2 * BAND), cols],
                                   vf[pl.ds(row0, 2 * BAND), cols], bias))
                results = _band_attention(blocks, shift if shifted else None)
                for (j, r, _), (o, lse, den) in zip(members, results):
                    dst = pl.ds(j * BAND * d + r, BAND, stride=d)
                    o_scr[bi, dst, :] = o
                    l_scr[bi, dst, :] = lse
                    if shifted:
                        den_scr[0] = jnp.minimum(den_scr[0], den)
                        den_scr[1] = jnp.maximum(den_scr[1], den)

            if nb >= A_GROUP:
                for r in range(d):
                    def body(g, carry, r=r, group=group):
                        j0 = pl.multiple_of(g * A_GROUP, A_GROUP)
                        group([(j0 + i, r, i == 0) for i in range(A_GROUP)])
                        return carry
                    lax.fori_loop(0, nb // A_GROUP, body, 0)
            else:
                for r0 in range(0, d, A_GROUP // nb):
                    group([(j, r0 + i, j == 0) for i in range(A_GROUP // nb) for j in range(nb)])

    den_scr[0] = jnp.full((BAND, LANES), jnp.inf, F32)
    den_scr[1] = jnp.zeros((BAND, LANES), F32)
    attend(True)
    in_range = (jnp.min(den_scr[0]) >= A_DEN_MIN) & (jnp.max(den_scr[1]) <= A_DEN_MAX)
    pl.when(jnp.logical_not(in_range))(functools.partial(attend, False))
    lse = l_scr[...]
    w = jnp.exp2(lse - jnp.max(lse, axis=0, keepdims=True))
    mixed = jnp.sum(w * o_scr[...], axis=0) / jnp.sum(w, axis=0)
    o_ref[...] = (mixed * z_ref[0].astype(F32)).astype(o_ref.dtype)


def _dilated_mixture(proj, views, shift):
    s = proj.shape[1]
    operands, in_specs, scratch = [], [], []
    for d, view in zip(DILATIONS, (proj,) + tuple(views)):
        rows, width = A_TILE // d, d * LANES
        blk = (1, rows, width)
        operands += [view] * 5
        in_specs += [
            pl.BlockSpec(blk, lambda c, n: (AQ + c, n, 0)),
            pl.BlockSpec(blk, lambda c, n: (AK + c, n, 0)),
            pl.BlockSpec(blk, lambda c, n: (AK + c, jnp.maximum(n - 1, 0), 0)),
            pl.BlockSpec(blk, lambda c, n: (AV + c, n, 0)),
            pl.BlockSpec(blk, lambda c, n: (AV + c, jnp.maximum(n - 1, 0), 0)),
        ]
        scratch += [pltpu.VMEM((rows + BAND, width), MXU_DTYPE)] * 2
    scratch += [pltpu.VMEM((len(DILATIONS), A_TILE, LANES), F32)] * 2
    scratch.append(pltpu.VMEM((2, BAND, LANES), F32))
    operands += [proj, jnp.full((8, LANES), shift, F32)]
    in_specs += [pl.BlockSpec((1, A_TILE, LANES), lambda c, n: (AZ + c, n, 0)),
                 pl.BlockSpec((8, LANES), lambda c, n: (0, 0))]
    return pl.pallas_call(
        _dilated_body,
        grid=(A_CHUNKS, s // A_TILE),
        in_specs=in_specs,
        out_specs=pl.BlockSpec((A_TILE, LANES), lambda c, n: (n, c)),
        out_shape=jax.ShapeDtypeStruct((s, A_WIDTH), MXU_DTYPE),
        scratch_shapes=scratch,
        compiler_params=_params(("parallel", "arbitrary")),
        name="dilated_mixture",
    )(*operands)


FOX_AUG = 3
FOX_VROWS = HEAD_DIM + 16
FOX_DEN_MIN, FOX_DEN_MAX = 2.0 ** -90, 2.0 ** 100
FOX_ZERO_EXP = 160.0


def _split3(c):
    hi = c.astype(jnp.bfloat16).astype(F32)
    r = c - hi
    mid = r.astype(jnp.bfloat16).astype(F32)
    return hi, mid, r - mid


def _fox_prep_body(q_ref, k_ref, v_ref, ccol_ref, crow_ref, qt_ref, ka_ref, vt_ref):
    tm = q_ref.shape[1]
    q_t = q_ref[0].astype(F32).T
    v_t = v_ref[0].astype(F32).T
    k = k_ref[0]
    sub = lax.broadcasted_iota(jnp.int32, (HEAD_DIM, tm), 0)
    lane = lax.broadcasted_iota(jnp.int32, (tm, LANES), 1)
    v_tail = (lax.broadcasted_iota(jnp.int32, (FOX_VROWS - HEAD_DIM, tm), 0) == 0).astype(F32)
    for h in range(2):
        hi, mid, lo = _split3(crow_ref[0, h:h + 1, :])
        aug_q = jnp.where(sub == 0, hi, jnp.where(sub == 1, mid, jnp.where(
            sub == 2, lo, jnp.where(sub < 2 * FOX_AUG, 1.0, 0.0))))
        q_h = q_t[h * HEAD_DIM:(h + 1) * HEAD_DIM]
        parts = [q_h, aug_q] if h == 0 else [aug_q, q_h]
        qt_ref[h] = jnp.concatenate(parts, axis=0).astype(qt_ref.dtype)
        hi, mid, lo = _split3(ccol_ref[0, :, h:h + 1])
        a0 = (1 - h) * HEAD_DIM
        aug_k = jnp.where(lane == a0 + FOX_AUG, -hi, jnp.where(lane == a0 + FOX_AUG + 1, -mid, jnp.where(
            lane == a0 + FOX_AUG + 2, -lo, jnp.where((lane >= a0) & (lane < a0 + FOX_AUG), 1.0, 0.0))))
        own = (lane < HEAD_DIM) if h == 0 else (lane >= HEAD_DIM)
        ka_ref[h] = jnp.where(own, k, aug_k.astype(k.dtype))
        vt_ref[h, 0] = jnp.concatenate([v_t[h * HEAD_DIM:(h + 1) * HEAD_DIM], v_tail],
                                       axis=0).astype(vt_ref.dtype)


def _fox_prep(proj, c, shift, tm):
    s = proj.shape[1]
    c_col = c.reshape(B_CHUNKS, 2, s).transpose(0, 2, 1)
    c_row = c.reshape(B_CHUNKS, 2, s) - shift
    return pl.pallas_call(
        _fox_prep_body,
        grid=(B_CHUNKS, s // tm),
        in_specs=[
            pl.BlockSpec((1, tm, LANES), lambda p, i: (BQ + p, i, 0)),
            pl.BlockSpec((1, tm, LANES), lambda p, i: (BK + p, i, 0)),
            pl.BlockSpec((1, tm, LANES), lambda p, i: (BV + p, i, 0)),
            pl.BlockSpec((1, tm, 2), lambda p, i: (p, i, 0)),
            pl.BlockSpec((1, 2, tm), lambda p, i: (p, 0, i)),
        ],
        out_specs=[
            pl.BlockSpec((2, LANES, tm), lambda p, i: (p, 0, i)),
            pl.BlockSpec((2, tm, LANES), lambda p, i: (p, i, 0)),
            pl.BlockSpec((2, 1, FOX_VROWS, tm), lambda p, i: (p, i, 0, 0)),
        ],
        out_shape=[
            jax.ShapeDtypeStruct((B_HEADS, LANES, s), MXU_DTYPE),
            jax.ShapeDtypeStruct((B_HEADS, s, LANES), MXU_DTYPE),
            jax.ShapeDtypeStruct((B_HEADS, s // tm, FOX_VROWS, tm), MXU_DTYPE),
        ],
        compiler_params=_params(("parallel", "parallel")),
        name="fox_prep",
    )(proj, proj, proj, c_col, c_row)


def _fox_body(lo_ref, qt_ref, ka_ref, vt_ref, z_ref, o_ref, s_scr, m_scr, acc_scr, *, blk):
    qi = pl.program_id(1)
    causal = (lax.broadcasted_iota(jnp.int32, (blk, blk), 0)
              <= lax.broadcasted_iota(jnp.int32, (blk, blk), 1))
    q_t = (qt_ref[0], qt_ref[1])

    def logits(kb, h):
        k0 = pl.multiple_of(kb * blk, blk)
        s_scr[h] = jnp.dot(ka_ref[h, pl.ds(k0, blk), :], q_t[h], preferred_element_type=F32)

    def accumulate_shifted(kb, h, diagonal=False):
        s = s_scr[h]
        if diagonal:
            s = jnp.where(causal, s, -jnp.inf)
        acc_scr[h] += jnp.dot(vt_ref[h, kb], jnp.exp2(s).astype(MXU_DTYPE),
                              preferred_element_type=F32)

    def accumulate_online(kb, h, diagonal=False):
        s = s_scr[h]
        if diagonal:
            s = jnp.where(causal, s, -jnp.inf)
        m = m_scr[h]
        m_new = jnp.maximum(m, jnp.max(s, axis=0, keepdims=True))
        p = jnp.exp2(s - m_new).astype(MXU_DTYPE)
        acc_scr[h] = (jnp.exp2(m - m_new) * acc_scr[h]
                      + jnp.dot(vt_ref[h, kb], p, preferred_element_type=F32))
        m_scr[h] = m_new

    def attend(accumulate, lo):
        acc_scr[...] = jnp.zeros(acc_scr.shape, F32)
        logits(lo, 0)

        def block(kb, carry):
            logits(kb, 1)
            accumulate(kb, 0)
            logits(kb + 1, 0)
            accumulate(kb, 1)
            return carry

        lax.fori_loop(lo, qi, block, 0)
        logits(qi, 1)
        accumulate(qi, 0, diagonal=True)
        accumulate(qi, 1, diagonal=True)

        outs =[acc_scr[h, 0:HEAD_DIM] / acc_scr[h, HEAD_DIM:HEAD_DIM + 1] for h in range(2)]
        o = jnp.concatenate(outs, axis=0).T
        o_ref[...] = (o * z_ref[0].astype(F32)).astype(o_ref.dtype)

    attend(accumulate_shifted, lo_ref[pl.program_id(0), qi])
    den = jnp.concatenate([acc_scr[h, HEAD_DIM:HEAD_DIM + 1] for h in range(2)], axis=0)
    in_range = (jnp.min(den) >= FOX_DEN_MIN) & (jnp.max(den) <= FOX_DEN_MAX)

    @pl.when(jnp.logical_not(in_range))
    def _():
        m_scr[...] = jnp.full(m_scr.shape, -jnp.inf, F32)
        attend(accumulate_online, 0)


def _fox_first_blocks(c, shift, blk):
    heads, s = c.shape
    nblk = s // blk
    c_first = c[:, ::blk]
    c_last = c[:, blk - 1::blk]
    dead = (c_first[:, :, None] - c_last[:, None, :]) <= -(FOX_ZERO_EXP + 0.02 * shift)
    block = jnp.arange(nblk)
    dead &= block[None, None, :] < block[None, :, None]
    lo = jnp.min(jnp.where(dead, nblk, block), axis=-1)
    return jnp.min(lo.reshape(heads // 2, 2, nblk), axis=1).astype(jnp.int32)


def _forgetting_attention(proj, c, shift, blk=512):
    s = proj.shape[1]
    qt, ka, vt = _fox_prep(proj, c, shift, blk)
    lo = _fox_first_blocks(c, shift, blk)
    grid_spec = pltpu.PrefetchScalarGridSpec(
        num_scalar_prefetch=1,
        grid=(B_CHUNKS, s // blk),
        in_specs=[
            pl.BlockSpec((2, LANES, blk), lambda p, i, lo: (p, 0, i)),
            pl.BlockSpec((2, s, LANES), lambda p, i, lo: (p, 0, 0)),
            pl.BlockSpec((2, s // blk, FOX_VROWS, blk), lambda p, i, lo: (p, 0, 0, 0)),
            pl.BlockSpec((1, blk, LANES), lambda p, i, lo: (BZ + p, i, 0)),
        ],
        out_specs=pl.BlockSpec((blk, LANES), lambda p, i, lo: (i, p)),
        scratch_shapes=[pltpu.VMEM((2, blk, blk), F32), pltpu.VMEM((2, 1, blk), F32),
                        pltpu.VMEM((2, FOX_VROWS, blk), F32)],
    )
    return pl.pallas_call(
        functools.partial(_fox_body, blk=blk),
        grid_spec=grid_spec,
        out_shape=jax.ShapeDtypeStruct((s, B_WIDTH), MXU_DTYPE),
        compiler_params=_params(("parallel", "arbitrary")),
        name="forgetting_attention",
    )(lo, qt, ka, vt, proj)


GLA_GROUP = 4


def _gla_level_matrix():
    c = GLA_CHUNK
    mats = []
    for b in GLA_LEVELS:
        m = np.zeros((c, c), np.float32)
        for i in range(c):
            pivot = (i // (2 * b)) * 2 * b + b - 1
            if i > pivot:
                m[i, pivot + 1:i + 1] = 1.0
            else:
                m[i, i + 1:pivot + 1] = 1.0
        mats.append(m)
    mats.append(np.tril(np.ones((c, c), np.float32)))
    return np.concatenate(mats, axis=0)


def _gla_body(q_ref, k_ref, v_ref, g_ref, z_ref, gu_ref, gb_ref, og_ref, lvl_ref, o_ref,
              st_scr, *, tile):
    c = GLA_CHUNK
    nlev = len(GLA_LEVELS)
    nch = tile // c
    grp = GLA_GROUP * c

    @pl.when(pl.program_id(1) == 0)
    def _():
        st_scr[...] = jnp.zeros_like(st_scr)

    gate_in = g_ref[:, 0:GATE_RANK]
    gate_hi = gate_in.astype(MXU_DTYPE)
    gate_lo = (gate_in - gate_hi.astype(F32)).astype(MXU_DTYPE)
    gate_w = gu_ref[0].astype(MXU_DTYPE)
    logit = (jnp.dot(gate_hi, gate_w, preferred_element_type=F32)
             + jnp.dot(gate_lo, gate_w, preferred_element_type=F32) + gb_ref[0])
    la = _log_sigmoid(logit) * (1.0 / GATE_TEMP)

    la_cat = jnp.concatenate([la[ci * c:(ci + 1) * c] for ci in range(nch)], axis=1)
    la_hi = la_cat.astype(MXU_DTYPE)
    la_lo = (la_cat - la_hi.astype(F32)).astype(MXU_DTYPE)
    f = jnp.exp(jnp.dot(lvl_ref[0:nlev * c, :], la_hi, preferred_element_type=F32))
    tri = lvl_ref[nlev * c:(nlev + 1) * c, :]
    bc = (jnp.dot(tri, la_hi, preferred_element_type=F32)
          + jnp.dot(tri, la_lo, preferred_element_type=F32))

    row = lax.broadcasted_iota(jnp.int32, (grp, grp), 0)
    col = lax.broadcasted_iota(jnp.int32, (grp, grp), 1)
    diff = (row ^ col) & (c - 1)
    level = jnp.full((grp, grp), nlev, jnp.int32)
    for li, b in reversed(list(enumerate(GLA_LEVELS))):
        level = jnp.where(diff >= b, li, level)
    level = jnp.where(((row & -c) == (col & -c)) & (col <= row), level, -1)
    out_gain = og_ref[...]

    state = st_scr[...]
    for g0 in range(0, nch, GLA_GROUP):
        rows = slice(g0 * c, g0 * c + grp)
        q = q_ref[0, rows, :].astype(F32)
        k = k_ref[0, rows, :].astype(F32)
        v = jnp.concatenate([v_ref[0, rows, :], v_ref[1, rows, :]], axis=-1)
        z = jnp.concatenate([z_ref[0, rows, :], z_ref[1, rows, :]], axis=-1)
        a = lax.dot_general(q.astype(MXU_DTYPE), k.astype(MXU_DTYPE), _NT,
                            preferred_element_type=F32)
        attn = jnp.where(level == nlev, a, 0.0)
        for li in range(nlev):
            fl = jnp.concatenate([f[li * c:(li + 1) * c, ch * LANES:(ch + 1) * LANES]
                                  for ch in range(g0, g0 + GLA_GROUP)], axis=0)
            a = lax.dot_general((q * fl).astype(MXU_DTYPE), (k * fl).astype(MXU_DTYPE), _NT,
                                preferred_element_type=F32)
            attn = jnp.where(level == li, a, attn)
        o_intra = jnp.dot(attn.astype(MXU_DTYPE), v, preferred_element_type=F32)

        for ci in range(GLA_GROUP):
            ch = g0 + ci
            r = slice(ci * c, (ci + 1) * c)
            bc_c = bc[:, ch * LANES:(ch + 1) * LANES]
            b_last = bc_c[c - 1:c, :]
            q_dec = (q[r] * jnp.exp(bc_c)).astype(MXU_DTYPE)
            o = o_intra[r] + jnp.dot(q_dec, state.astype(MXU_DTYPE), preferred_element_type=F32)
            k_dec = (k[r] * jnp.exp(b_last - bc_c)).astype(MXU_DTYPE)
            decay = jnp.broadcast_to(jnp.exp(b_last), (LANES, LANES)).T
            state = (state * jnp.concatenate([decay, decay], axis=1)
                     + lax.dot_general(k_dec, v[r], _TN, preferred_element_type=F32))
            ms = jnp.sum(o * o, axis=-1, keepdims=True) * (1.0 / C_DV)
            y = o * lax.rsqrt(ms + RMS_EPS) * out_gain * z[r].astype(F32)
            o_ref[pl.ds(ch * c, c), :] = y.astype(o_ref.dtype)
    st_scr[...] = state


def _gated_linear_attention(proj, gate, gate_up, gate_bias, out_gain, tile=512):
    s = proj.shape[1]
    lvl = jnp.asarray(_gla_level_matrix(), MXU_DTYPE)
    return pl.pallas_call(
        functools.partial(_gla_body, tile=tile),
        grid=(C_HEADS, s // tile),
        in_specs=[
            pl.BlockSpec((1, tile, LANES), lambda h, t: (CQ + h, t, 0)),
            pl.BlockSpec((1, tile, LANES), lambda h, t: (CK + h, t, 0)),
            pl.BlockSpec((2, tile, LANES), lambda h, t: (CV // 2 + h, t, 0)),
            pl.BlockSpec((tile, LANES), lambda h, t: (t, 0)),
            pl.BlockSpec((2, tile, LANES), lambda h, t: (CZ // 2 + h, t, 0)),
            pl.BlockSpec((1, GATE_RANK, C_DK_PAD), lambda h, t: (h, 0, 0)),
            pl.BlockSpec((1, 1, C_DK_PAD), lambda h, t: (h, 0, 0)),
            pl.BlockSpec((1, C_DV_PAD), lambda h, t: (0, 0)),
            pl.BlockSpec(lvl.shape, lambda h, t: (0, 0)),
        ],
        out_specs=pl.BlockSpec((tile, C_DV_PAD), lambda h, t: (t, h)),
        out_shape=jax.ShapeDtypeStruct((s, C_HEADS * C_DV_PAD), MXU_DTYPE),
        scratch_shapes=[pltpu.VMEM((C_DK_PAD, C_DV_PAD), F32)],
        compiler_params=_params(("parallel", "arbitrary")),
        name="gated_linear_attention",
    )(proj, proj, proj, gate, proj, gate_up, gate_bias, out_gain, lvl)


def _out_body(x_ref, a_ref, b_ref, c_ref, wab_ref, wc_ref, o_ref):
    wa = wab_ref[0, 0:A_WIDTH, :].astype(MXU_DTYPE)
    wb = wab_ref[0, A_WIDTH:A_WIDTH + B_WIDTH, :].astype(MXU_DTYPE)
    acc = jnp.dot(a_ref[...], wa, preferred_element_type=F32)
    acc = acc + jnp.dot(b_ref[...], wb, preferred_element_type=F32)
    acc = acc + jnp.dot(c_ref[...], wc_ref[0], preferred_element_type=F32)
    o_ref[...] = x_ref[...] + acc


def _out_proj(x, ma, mb, mc, w_out, wo_c, layer, tm=1024, tn=512):
    s, d = x.shape
    row_blk = lambda m: pl.BlockSpec((tm, m.shape[1]), lambda i, j: (i, 0))
    return pl.pallas_call(
        _out_body,
        grid=(s // tm, d // tn),
        in_specs=[pl.BlockSpec((tm, tn), lambda i, j: (i, j)),
                  row_blk(ma), row_blk(mb), row_blk(mc),
                  pl.BlockSpec((1, A_WIDTH + B_WIDTH, tn), lambda i, j: (layer, 0, j)),
                  pl.BlockSpec((1, mc.shape[1], tn), lambda i, j: (layer, 0, j))],
        out_specs=pl.BlockSpec((tm, tn), lambda i, j: (i, j)),
        out_shape=jax.ShapeDtypeStruct((s, d), F32),
        compiler_params=_params(("parallel", "arbitrary")),
        name="out_proj",
    )(x, ma, mb, mc, w_out, wo_c)


def _pad_heads(w, heads, width, padded):
    lead = w.shape[:-1]
    w = w.reshape(*lead, heads, width)
    w = jnp.pad(w, [(0, 0)] * len(lead) + [(0, 0), (0, padded - width)])
    return w.reshape(*lead, heads * padded)


def _prepare(w_in, w_out, a_q_gain, a_k_gain, b_q_gain, b_k_gain):
    layers, d, _ = w_in.shape
    w_t = jnp.swapaxes(w_in, 1, 2)
    ck, cv = C_HEADS * C_DK, C_HEADS * C_DV
    ab = 4 * (A_WIDTH + B_WIDTH)
    bounds = np.cumsum([ab, B_HEADS, ck, ck, cv, cv, GATE_RANK])
    bf, cq, ckk, cvv, cz, cr = [w_t[:, lo:hi] for lo, hi in zip(bounds[:-1], bounds[1:])]

    def pad_rows(w, width, padded):
        w = w.reshape(layers, C_HEADS, width, d)
        w = jnp.pad(w, ((0, 0), (0, 0), (0, padded - width), (0, 0)))
        return w.reshape(layers, C_HEADS * padded, d)

    pad_k = lambda w: pad_rows(w, C_DK, C_DK_PAD)
    pad_v = lambda w: pad_rows(w, C_DV, C_DV_PAD)
    tail = jnp.zeros((layers, MXU_COLS - GATE_RANK - B_HEADS, d), F32)
    w_c_t = jnp.concatenate([pad_k(cq), pad_k(ckk), pad_v(cvv), pad_v(cz), cr, bf, tail], axis=1)

    q_scale = HEAD_DIM ** -0.5 * LOG2E
    tile_heads = lambda g, n: jnp.tile(g, (1, n))
    ones = lambda n: jnp.ones((layers, n), F32)
    aux_ab = jnp.concatenate([
        tile_heads(a_q_gain * q_scale, A_HEADS), tile_heads(a_k_gain, A_HEADS), ones(2 * A_WIDTH),
        tile_heads(b_q_gain * q_scale, B_HEADS), tile_heads(b_k_gain, B_HEADS), ones(2 * B_WIDTH)],
        axis=-1)[:, None, :]
    aux_c = jnp.concatenate([
        jnp.full((layers, C_HEADS * C_DK_PAD), C_DK ** -0.5, F32),
        ones(C_HEADS * (C_DK_PAD + 2 * C_DV_PAD) + MXU_COLS)], axis=-1)[:, None, :]

    wo_c = w_out[:, A_WIDTH + B_WIDTH:].reshape(layers, C_HEADS, C_DV, d)
    wo_c = jnp.pad(wo_c, ((0, 0), (0, 0), (0, C_DV_PAD - C_DV), (0, 0)))
    wo_c = wo_c.reshape(layers, C_HEADS * C_DV_PAD, d).astype(MXU_DTYPE)
    return w_t, w_c_t, aux_ab, aux_c, wo_c


def _layer(x, layer, norm_g, w_in, w_c, aux_ab, aux_c, w_out, wo_c, fox_bias, a_shift, fox_shift,
           gla_gate_up, gla_gate_bias, gla_out_gain):
    h = _rmsnorm(x, norm_g, layer)
    proj_ab, v4, v16 = _in_proj(h, w_in, aux_ab, layer, AB_KINDS, "in_proj_ab")
    proj_c, gate = _in_proj(h, w_c, aux_c, layer, C_KINDS, "in_proj_c")
    c = _fox_cumsum(gate[:, GATE_RANK:GATE_RANK + B_HEADS].T, fox_bias)
    mixed_a = _dilated_mixture(proj_ab, (v4, v16), a_shift)
    mixed_b = _forgetting_attention(proj_ab, c, fox_shift)
    gate_up = _pad_heads(gla_gate_up, C_HEADS, C_DK, C_DK_PAD)
    gate_up = gate_up.reshape(GATE_RANK, C_HEADS, C_DK_PAD).transpose(1, 0, 2)
    gate_bias = _pad_heads(gla_gate_bias, C_HEADS, C_DK, C_DK_PAD).reshape(C_HEADS, 1, C_DK_PAD)
    out_gain = jnp.pad(gla_out_gain, (0, C_DV_PAD - C_DV)).reshape(1, C_DV_PAD)
    mixed_c = _gated_linear_attention(proj_c, gate, gate_up, gate_bias, out_gain)
    return _out_proj(x, mixed_a, mixed_b, mixed_c, w_out, wo_c, layer)


@jax.jit
def kernel(x, norm_g, w_in, a_q_gain, a_k_gain, b_q_gain, b_k_gain, fox_bias, gla_gate_up,
           gla_gate_bias, gla_out_gain, w_out):
    bsz, s, d = x.shape
    assert bsz == 1, "batch size 1 only"
    w_t, w_c, aux_ab, aux_c, wo_c = _prepare(w_in, w_out, a_q_gain, a_k_gain, b_q_gain, b_k_gain)
    norm_g = norm_g[:, None, :]
    bound = lambda gq, gk: (HEAD_DIM ** 0.5 * LOG2E * jnp.max(jnp.abs(gq), axis=-1)
                            * jnp.max(jnp.abs(gk), axis=-1))
    a_shift, fox_shift = bound(a_q_gain, a_k_gain), bound(b_q_gain, b_k_gain)
    y = x.reshape(s, d)
    for layer in range(w_in.shape[0]):
        y = _layer(y, layer, norm_g, w_t, w_c, aux_ab, aux_c, w_out, wo_c, fox_bias[layer],
                   a_shift[layer], fox_shift[layer], gla_gate_up[layer], gla_gate_bias[layer],
                   gla_out_gain[layer])
    return y.reshape(bsz, s, d)
```

```python
import functools

import numpy as np
import jax
import jax.numpy as jnp
from jax import lax
from jax.experimental import pallas as pl
from jax.experimental.pallas import tpu as pltpu

F32 = jnp.float32
MXU_DTYPE = jnp.bfloat16

LANES = 128
MXU_COLS = 256
HEAD_DIM = 64
A_HEADS, B_HEADS, C_HEADS = 12, 8, 4
A_CHUNKS, B_CHUNKS = A_HEADS // 2, B_HEADS // 2
C_DK, C_DV = 96, 192
C_DK_PAD, C_DV_PAD = 128, 256
GATE_RANK = 16
GATE_TEMP = 16.0
RMS_EPS = 1e-6
LOG2E = 1.4426950408889634
DILATIONS = (1, 4, 16)
BAND = 128
A_TILE = 2048
GLA_CHUNK = 64
GLA_LEVELS = (32, 16, 8, 4, 2, 1)
VMEM_LIMIT = 56 * 1024 * 1024

AQ, AK, AV, AZ = 0, 6, 12, 18
BQ, BK, BV, BZ = 24, 28, 32, 36
CQ, CK, CV, CZ = 0, 4, 8, 16
A_WIDTH, B_WIDTH = A_HEADS * HEAD_DIM, B_HEADS * HEAD_DIM

_NT = (((1,), (1,)), ((), ()))
_TN = (((0,), (0,)), ((), ()))


def _params(sem):
    return pltpu.CompilerParams(dimension_semantics=sem, vmem_limit_bytes=VMEM_LIMIT)


def _log_sigmoid(x):
    return jnp.minimum(x, 0.0) - jnp.log1p(jnp.exp(-jnp.abs(x)))


AB_KINDS = (("headnorm+views",) * (2 * A_CHUNKS) + ("scale+views",) * A_CHUNKS
            + ("silu",) * A_CHUNKS
            + ("headnorm",) * (2 * B_CHUNKS) + ("scale",) * B_CHUNKS + ("silu",) * B_CHUNKS)[::2]
C_KINDS = (("scale",) * (4 * C_HEADS) + ("silu",) * (2 * C_HEADS))[::2] + ("gate",)


def _kind_ranges(kinds, kind):
    runs, start = [], None
    for t, k in enumerate(kinds + (None,)):
        if k == kind and start is None:
            start = t
        elif k != kind and start is not None:
            runs.append((start, t))
            start = None
    return runs


def _view_tiles(kinds):
    n = sum(k.endswith("+views") for k in kinds)
    assert all(k.endswith("+views") for k in kinds[:n])
    return n


def _rmsnorm_body(x_ref, g_ref, o_ref):
    x = x_ref[...]
    ms = jnp.mean(x * x, axis=-1, keepdims=True)
    o_ref[...] = (x * lax.rsqrt(ms + RMS_EPS) * g_ref[0]).astype(o_ref.dtype)


def _rmsnorm(x, norm_g, layer, tm=512):
    s, d = x.shape
    return pl.pallas_call(
        _rmsnorm_body,
        grid=(s // tm,),
        in_specs=[pl.BlockSpec((tm, d), lambda i: (i, 0)),
                  pl.BlockSpec((1, 1, d), lambda i: (layer, 0, 0))],
        out_specs=pl.BlockSpec((tm, d), lambda i: (i, 0)),
        out_shape=jax.ShapeDtypeStruct((s, d), MXU_DTYPE),
        compiler_params=_params(("parallel",)),
        name="rmsnorm",
    )(x, norm_g)


def _proj_body(h_ref, w_ref, aux_ref, *rest, kinds):
    view_tiles = _view_tiles(kinds)
    n_tiles = len(kinds)
    o_ref, rest = rest[0], rest[1:]
    gate_ref = None
    if kinds[-1] == "gate":
        gate_ref, rest = rest[0], rest[1:]
    if view_tiles:
        view_refs, rest = rest[:len(DILATIONS) - 1], rest[len(DILATIONS) - 1:]
        n_stage = MXU_COLS // LANES
        stage, rest = rest[-n_stage:], rest[:-n_stage]
    acc_scr = rest[0:2]
    j = pl.program_id(1)
    tm = h_ref.shape[0]

    def matmul(dst, rows):
        dst[rows, :] = lax.dot_general(h_ref[rows, :], w_ref[0].astype(MXU_DTYPE), _NT,
                                       preferred_element_type=F32)

    def epilogue(src, kind, rows):
        r0, nr = rows.start, rows.stop - rows.start
        if kind == "gate":
            gate_ref[rows, :] = src[rows, 0:LANES]
            return
        base = kind.split("+")[0]
        for c in range(MXU_COLS // LANES):
            cols = slice(c * LANES, (c + 1) * LANES)
            y = src[rows, cols]
            if base == "headnorm":
                first = lax.broadcasted_iota(jnp.int32, y.shape, 1) < HEAD_DIM
                y2 = y * y
                s0 = jnp.sum(jnp.where(first, y2, 0.0), axis=-1, keepdims=True)
                s1 = jnp.sum(jnp.where(first, 0.0, y2), axis=-1, keepdims=True)
                ms = jnp.where(first, s0, s1) * (1.0 / HEAD_DIM)
                y = y * lax.rsqrt(ms + RMS_EPS) * aux_ref[0, :, cols]
            elif base == "scale":
                y = y * aux_ref[0, :, cols]
            elif base == "silu":
                y = y * jax.nn.sigmoid(y)
            o_ref[c, rows, :] = y.astype(o_ref.dtype)
            if not kind.endswith("+views"):
                continue
            stage[c][rows, :] = y
            for view, dil in zip(view_refs, DILATIONS[1:]):
                for r in range(dil):
                    picked = stage[c][pl.ds(r0 + r, nr // dil, stride=dil), :]
                    view[c, r0 // dil:(r0 + nr) // dil, r * LANES:(r + 1) * LANES] = (
                        picked.astype(view.dtype))

    halves = tuple(slice(r, r + tm // 4) for r in range(0, tm, tm // 4))

    @pl.when(j == 0)
    def _():
        for rows in halves:
            matmul(acc_scr[0], rows)

    for parity in range(2):
        for kind in sorted(set(kinds[:-1])):
            in_kind = functools.reduce(
                jnp.logical_or, [(j - 1 >= lo) & (j - 1 < min(hi, n_tiles - 1))
                                 for lo, hi in _kind_ranges(kinds, kind) if lo < n_tiles - 1])

            @pl.when((j >= 1) & (j < n_tiles) & (j % 2 == parity) & in_kind)
            def _(parity=parity, kind=kind):
                for rows in halves:
                    epilogue(acc_scr[1 - parity], kind, rows)
                    matmul(acc_scr[parity], rows)

    @pl.when(j == n_tiles)
    def _():
        for rows in halves:
            epilogue(acc_scr[(n_tiles - 1) % 2], kinds[-1], rows)


def _in_proj(h, w_t, aux, layer, kinds, name, tm=2048):
    s, d = h.shape
    n_tiles = len(kinds)
    view_tiles = _view_tiles(kinds)
    chunks = MXU_COLS // LANES
    has_gate = kinds[-1] == "gate"
    out_tiles = n_tiles - 1 if has_gate else n_tiles
    prev = lambda j, n: jnp.clip(j - 1, 0, n - 1)
    out_shape = [jax.ShapeDtypeStruct((out_tiles * chunks, s, LANES), MXU_DTYPE)]
    out_specs = [pl.BlockSpec((chunks, tm, LANES), lambda i, j: (prev(j, out_tiles), i, 0))]
    scratch = [pltpu.VMEM((tm, MXU_COLS), F32), pltpu.VMEM((tm, MXU_COLS), F32)]
    if has_gate:
        out_shape.append(jax.ShapeDtypeStruct((s, LANES), F32))
        out_specs.append(pl.BlockSpec((tm, LANES), lambda i, j: (i, 0)))
    if view_tiles:
        for dil in DILATIONS[1:]:
            out_shape.append(
                jax.ShapeDtypeStruct((view_tiles * chunks, s // dil, dil * LANES), MXU_DTYPE))
            out_specs.append(pl.BlockSpec((chunks, tm // dil, dil * LANES),
                                          lambda i, j: (prev(j, view_tiles), i, 0)))
        scratch += [pltpu.VMEM((tm, LANES), F32)] * chunks
    return pl.pallas_call(
        functools.partial(_proj_body, kinds=kinds),
        grid=(s // tm, n_tiles + 1),
        in_specs=[pl.BlockSpec((tm, d), lambda i, j: (i, 0)),
                  pl.BlockSpec((1, MXU_COLS, d),
                               lambda i, j: (layer, jnp.minimum(j, n_tiles - 1), 0)),
                  pl.BlockSpec((1, 1, MXU_COLS), lambda i, j: (layer, 0, prev(j, n_tiles)))],
        out_specs=out_specs,
        out_shape=out_shape,
        scratch_shapes=scratch,
        compiler_params=_params(("parallel", "arbitrary")),
        name=name,
    )(h, w_t, aux)


def _fox_cumsum_body(x_ref, b_ref, o_ref):
    x = _log_sigmoid(x_ref[...] + b_ref[...])
    idx = lax.broadcasted_iota(jnp.int32, x.shape, 1)
    shift = 1
    while shift < x.shape[1]:
        x = x + jnp.where(idx >= shift, pltpu.roll(x, shift, axis=1), 0.0)
        shift *= 2
    o_ref[...] = x * LOG2E


def _fox_cumsum(logit_t, bias):
    nh, s = logit_t.shape
    return pl.pallas_call(
        _fox_cumsum_body,
        out_shape=jax.ShapeDtypeStruct((nh, s), F32),
        compiler_params=pltpu.CompilerParams(vmem_limit_bytes=VMEM_LIMIT),
        name="fox_cumsum",
    )(logit_t, bias.reshape(nh, 1))


A_GROUP = 8


A_DEN_MIN, A_DEN_MAX = 2.0 ** -90, 2.0 ** 100


def _band_attention(blocks, shift):
    first = lax.broadcasted_iota(jnp.int32, (BAND, LANES), 1) < HEAD_DIM
    first_kv = lax.broadcasted_iota(jnp.int32, (2 * BAND, LANES), 1) < HEAD_DIM
    chains = [(b, h) for b in range(len(blocks)) for h in range(2)]
    logits = []
    for b, h in chains:
        q, k2 = blocks[b][0], blocks[b][1]
        zero = jnp.zeros_like(q)
        qm = jnp.where(first, q, zero) if h == 0 else jnp.where(first, zero, q)
        logits.append(lax.dot_general(qm, k2, _NT, preferred_element_type=F32))
    logits = [s + blocks[b][3] for (b, h), s in zip(chains, logits)]
    if shift is None:
        maxes = [jnp.max(s, axis=-1, keepdims=True) for s in logits]
        probs = [jnp.exp2(s - m).astype(MXU_DTYPE) for s, m in zip(logits, maxes)]
    else:
        maxes = [shift] * len(chains)
        probs = [jnp.exp2(s).astype(MXU_DTYPE) for s in logits]
    res = []
    for (b, h), p in zip(chains, probs):
        v2 = blocks[b][2]
        one = jnp.ones_like(v2)
        vh = jnp.where(first_kv, v2, one) if h == 0 else jnp.where(first_kv, one, v2)
        res.append(jnp.dot(p, vh, preferred_element_type=F32))
    results = []
    for b in range(len(blocks)):
        r0, r1 = res[2 * b], res[2 * b + 1]
        num = jnp.where(first, r0, r1)
        den = pltpu.roll(jnp.where(first, r1, r0), HEAD_DIM, axis=1)
        m = maxes[2 * b] if shift is not None else jnp.where(first, maxes[2 * b], maxes[2 * b + 1])
        results.append((num / den, m + jnp.log2(den), den))
    return results


def _dilated_body(*refs):
    ins, z_ref, u_ref, o_ref, scr = refs[:15], refs[15], refs[16], refs[17], refs[18:]
    kv_scr, o_scr, l_scr, den_scr = scr[:6], scr[6], scr[7], scr[8]
    n = pl.program_id(1)
    for bi, d in enumerate(DILATIONS):
        kc, kp, vc, vp = ins[5 * bi + 1:5 * bi + 5]
        kf, vf = kv_scr[2 * bi:2 * bi + 2]
        rows = A_TILE // d
        for cur, prev, full in ((kc, kp, kf), (vc, vp, vf)):
            full[0:BAND, :] = prev[0, rows - BAND:rows, :]
            full[BAND:BAND + rows, :] = cur[0]

    row = lax.broadcasted_iota(jnp.int32, (BAND, 2 * BAND), 0)
    col = lax.broadcasted_iota(jnp.int32, (BAND, 2 * BAND), 1)
    band = (col >= row) & (col <= row + BAND)
    shift = u_ref[0:1, 0:1]

    def attend(shifted):
        offset = shift if shifted else 0.0
        bias_band = jnp.where(band, 0.0, -jnp.inf) - offset
        bias_first = jnp.where(band & (col >= BAND), 0.0, -jnp.inf) - offset
        for bi, d in enumerate(DILATIONS):
            q_ref = ins[5 * bi]
            kf, vf = kv_scr[2 * bi:2 * bi + 2]
            nb = A_TILE // d // BAND

            def group(members, q_ref=q_ref, kf=kf, vf=vf, d=d, nb=nb, bi=bi):
                blocks = []
                for j, r, maybe_first in members:
                    cols = slice(r * LANES, (r + 1) * LANES)
                    row0 = j * BAND if isinstance(j, int) else pl.multiple_of(j * BAND, BAND)
                    bias = bias_band
                    if maybe_first:
                        bias = jnp.where(n * nb + j >= 1, bias_band, bias_first)
                    blocks.append((q_ref[0, pl.ds(row0, BAND), cols],
                                   kf[pl.ds(row0, 2 * BAND), cols],
                                   vf[pl.ds(row0, 2 * BAND), cols], bias))
                results = _band_attention(blocks, shift if shifted else None)
                for (j, r, _), (o, lse, den) in zip(members, results):
                    dst = pl.ds(j * BAND * d + r, BAND, stride=d)
                    o_scr[bi, dst, :] = o
                    l_scr[bi, dst, :] = lse
                    if shifted:
                        den_scr[0] = jnp.minimum(den_scr[0], den)
                        den_scr[1] = jnp.maximum(den_scr[1], den)

            if nb >= A_GROUP:
                for r in range(d):
                    def body(g, carry, r=r, group=group):
                        j0 = pl.multiple_of(g * A_GROUP, A_GROUP)
                        group([(j0 + i, r, i == 0) for i in range(A_GROUP)])
                        return carry
                    lax.fori_loop(0, nb // A_GROUP, body, 0)
            else:
                for r0 in range(0, d, A_GROUP // nb):
                    group([(j, r0 + i, j == 0) for i in range(A_GROUP // nb) for j in range(nb)])

    den_scr[0] = jnp.full((BAND, LANES), jnp.inf, F32)
    den_scr[1] = jnp.zeros((BAND, LANES), F32)
    attend(True)
    in_range = (jnp.min(den_scr[0]) >= A_DEN_MIN) & (jnp.max(den_scr[1]) <= A_DEN_MAX)
    pl.when(jnp.logical_not(in_range))(functools.partial(attend, False))
    lse = l_scr[...]
    w = jnp.exp2(lse - jnp.max(lse, axis=0, keepdims=True))
    mixed = jnp.sum(w * o_scr[...], axis=0) / jnp.sum(w, axis=0)
    o_ref[...] = (mixed * z_ref[0].astype(F32)).astype(o_ref.dtype)


def _dilated_mixture(proj, views, shift):
    s = proj.shape[1]
    operands, in_specs, scratch = [], [], []
    for d, view in zip(DILATIONS, (proj,) + tuple(views)):
        rows, width = A_TILE // d, d * LANES
        blk = (1, rows, width)
        operands += [view] * 5
        in_specs += [
            pl.BlockSpec(blk, lambda c, n: (AQ + c, n, 0)),
            pl.BlockSpec(blk, lambda c, n: (AK + c, n, 0)),
            pl.BlockSpec(blk, lambda c, n: (AK + c, jnp.maximum(n - 1, 0), 0)),
            pl.BlockSpec(blk, lambda c, n: (AV + c, n, 0)),
            pl.BlockSpec(blk, lambda c, n: (AV + c, jnp.maximum(n - 1, 0), 0)),
        ]
        scratch += [pltpu.VMEM((rows + BAND, width), MXU_DTYPE)] * 2
    scratch += [pltpu.VMEM((len(DILATIONS), A_TILE, LANES), F32)] * 2
    scratch.append(pltpu.VMEM((2, BAND, LANES), F32))
    operands += [proj, jnp.full((8, LANES), shift, F32)]
    in_specs += [pl.BlockSpec((1, A_TILE, LANES), lambda c, n: (AZ + c, n, 0)),
                 pl.BlockSpec((8, LANES), lambda c, n: (0, 0))]
    return pl.pallas_call(
        _dilated_body,
        grid=(A_CHUNKS, s // A_TILE),
        in_specs=in_specs,
        out_specs=pl.BlockSpec((A_TILE, LANES), lambda c, n: (n, c)),
        out_shape=jax.ShapeDtypeStruct((s, A_WIDTH), MXU_DTYPE),
        scratch_shapes=scratch,
        compiler_params=_params(("parallel", "arbitrary")),
        name="dilated_mixture",
    )(*operands)


FOX_AUG = 3
FOX_VROWS = HEAD_DIM + 16
FOX_DEN_MIN, FOX_DEN_MAX = 2.0 ** -90, 2.0 ** 100
FOX_ZERO_EXP = 160.0


def _split3(c):
    hi = c.astype(jnp.bfloat16).astype(F32)
    r = c - hi
    mid = r.astype(jnp.bfloat16).astype(F32)
    return hi, mid, r - mid


def _fox_prep_body(q_ref, k_ref, v_ref, ccol_ref, crow_ref, qt_ref, ka_ref, vt_ref):
    tm = q_ref.shape[1]
    q_t = q_ref[0].astype(F32).T
    v_t = v_ref[0].astype(F32).T
    k = k_ref[0]
    sub = lax.broadcasted_iota(jnp.int32, (HEAD_DIM, tm), 0)
    lane = lax.broadcasted_iota(jnp.int32, (tm, LANES), 1)
    v_tail = (lax.broadcasted_iota(jnp.int32, (FOX_VROWS - HEAD_DIM, tm), 0) == 0).astype(F32)
    for h in range(2):
        hi, mid, lo = _split3(crow_ref[0, h:h + 1, :])
        aug_q = jnp.where(sub == 0, hi, jnp.where(sub == 1, mid, jnp.where(
            sub == 2, lo, jnp.where(sub < 2 * FOX_AUG, 1.0, 0.0))))
        q_h = q_t[h * HEAD_DIM:(h + 1) * HEAD_DIM]
        parts = [q_h, aug_q] if h == 0 else [aug_q, q_h]
        qt_ref[h] = jnp.concatenate(parts, axis=0).astype(qt_ref.dtype)
        hi, mid, lo = _split3(ccol_ref[0, :, h:h + 1])
        a0 = (1 - h) * HEAD_DIM
        aug_k = jnp.where(lane == a0 + FOX_AUG, -hi, jnp.where(lane == a0 + FOX_AUG + 1, -mid, jnp.where(
            lane == a0 + FOX_AUG + 2, -lo, jnp.where((lane >= a0) & (lane < a0 + FOX_AUG), 1.0, 0.0))))
        own = (lane < HEAD_DIM) if h == 0 else (lane >= HEAD_DIM)
        ka_ref[h] = jnp.where(own, k, aug_k.astype(k.dtype))
        vt_ref[h, 0] = jnp.concatenate([v_t[h * HEAD_DIM:(h + 1) * HEAD_DIM], v_tail],
                                       axis=0).astype(vt_ref.dtype)


def _fox_prep(proj, c, shift, tm):
    s = proj.shape[1]
    c_col = c.reshape(B_CHUNKS, 2, s).transpose(0, 2, 1)
    c_row = c.reshape(B_CHUNKS, 2, s) - shift
    return pl.pallas_call(
        _fox_prep_body,
        grid=(B_CHUNKS, s // tm),
        in_specs=[
            pl.BlockSpec((1, tm, LANES), lambda p, i: (BQ + p, i, 0)),
            pl.BlockSpec((1, tm, LANES), lambda p, i: (BK + p, i, 0)),
            pl.BlockSpec((1, tm, LANES), lambda p, i: (BV + p, i, 0)),
            pl.BlockSpec((1, tm, 2), lambda p, i: (p, i, 0)),
            pl.BlockSpec((1, 2, tm), lambda p, i: (p, 0, i)),
        ],
        out_specs=[
            pl.BlockSpec((2, LANES, tm), lambda p, i: (p, 0, i)),
            pl.BlockSpec((2, tm, LANES), lambda p, i: (p, i, 0)),
            pl.BlockSpec((2, 1, FOX_VROWS, tm), lambda p, i: (p, i, 0, 0)),
        ],
        out_shape=[
            jax.ShapeDtypeStruct((B_HEADS, LANES, s), MXU_DTYPE),
            jax.ShapeDtypeStruct((B_HEADS, s, LANES), MXU_DTYPE),
            jax.ShapeDtypeStruct((B_HEADS, s // tm, FOX_VROWS, tm), MXU_DTYPE),
        ],
        compiler_params=_params(("parallel", "parallel")),
        name="fox_prep",
    )(proj, proj, proj, c_col, c_row)


def _fox_body(lo_ref, qt_ref, ka_ref, vt_ref, z_ref, o_ref, s_scr, m_scr, acc_scr, *, blk):
    qi = pl.program_id(1)
    causal = (lax.broadcasted_iota(jnp.int32, (blk, blk), 0)
              <= lax.broadcasted_iota(jnp.int32, (blk, blk), 1))
    q_t = (qt_ref[0], qt_ref[1])

    def logits(kb, h):
        k0 = pl.multiple_of(kb * blk, blk)
        s_scr[h] = jnp.dot(ka_ref[h, pl.ds(k0, blk), :], q_t[h], preferred_element_type=F32)

    def accumulate_shifted(kb, h, diagonal=False):
        s = s_scr[h]
        if diagonal:
            s = jnp.where(causal, s, -jnp.inf)
        acc_scr[h] += jnp.dot(vt_ref[h, kb], jnp.exp2(s).astype(MXU_DTYPE),
                              preferred_element_type=F32)

    def accumulate_online(kb, h, diagonal=False):
        s = s_scr[h]
        if diagonal:
            s = jnp.where(causal, s, -jnp.inf)
        m = m_scr[h]
        m_new = jnp.maximum(m, jnp.max(s, axis=0, keepdims=True))
        p = jnp.exp2(s - m_new).astype(MXU_DTYPE)
        acc_scr[h] = (jnp.exp2(m - m_new) * acc_scr[h]
                      + jnp.dot(vt_ref[h, kb], p, preferred_element_type=F32))
        m_scr[h] = m_new

    def attend(accumulate, lo):
        acc_scr[...] = jnp.zeros(acc_scr.shape, F32)
        logits(lo, 0)

        def block(kb, carry):
            logits(kb, 1)
            accumulate(kb, 0)
            logits(kb + 1, 0)
            accumulate(kb, 1)
            return carry

        lax.fori_loop(lo, qi, block, 0)
        logits(qi, 1)
        accumulate(qi, 0, diagonal=True)
        accumulate(qi, 1, diagonal=True)

        outs =[acc_scr[h, 0:HEAD_DIM] / acc_scr[h, HEAD_DIM:HEAD_DIM + 1] for h in range(2)]
        o = jnp.concatenate(outs, axis=0).T
        o_ref[...] = (o * z_ref[0].astype(F32)).astype(o_ref.dtype)

    attend(accumulate_shifted, lo_ref[pl.program_id(0), qi])
    den = jnp.concatenate([acc_scr[h, HEAD_DIM:HEAD_DIM + 1] for h in range(2)], axis=0)
    in_range = (jnp.min(den) >= FOX_DEN_MIN) & (jnp.max(den) <= FOX_DEN_MAX)

    @pl.when(jnp.logical_not(in_range))
    def _():
        m_scr[...] = jnp.full(m_scr.shape, -jnp.inf, F32)
        attend(accumulate_online, 0)


def _fox_first_blocks(c, shift, blk):
    heads, s = c.shape
    nblk = s // blk
    c_first = c[:, ::blk]
    c_last = c[:, blk - 1::blk]
    dead = (c_first[:, :, None] - c_last[:, None, :]) <= -(FOX_ZERO_EXP + 0.02 * shift)
    block = jnp.arange(nblk)
    dead &= block[None, None, :] < block[None, :, None]
    lo = jnp.min(jnp.where(dead, nblk, block), axis=-1)
    return jnp.min(lo.reshape(heads // 2, 2, nblk), axis=1).astype(jnp.int32)


def _forgetting_attention(proj, c, shift, blk=512):
    s = proj.shape[1]
    qt, ka, vt = _fox_prep(proj, c, shift, blk)
    lo = _fox_first_blocks(c, shift, blk)
    grid_spec = pltpu.PrefetchScalarGridSpec(
        num_scalar_prefetch=1,
        grid=(B_CHUNKS, s // blk),
        in_specs=[
            pl.BlockSpec((2, LANES, blk), lambda p, i, lo: (p, 0, i)),
            pl.BlockSpec((2, s, LANES), lambda p, i, lo: (p, 0, 0)),
            pl.BlockSpec((2, s // blk, FOX_VROWS, blk), lambda p, i, lo: (p, 0, 0, 0)),
            pl.BlockSpec((1, blk, LANES), lambda p, i, lo: (BZ + p, i, 0)),
        ],
        out_specs=pl.BlockSpec((blk, LANES), lambda p, i, lo: (i, p)),
        scratch_shapes=[pltpu.VMEM((2, blk, blk), F32), pltpu.VMEM((2, 1, blk), F32),
                        pltpu.VMEM((2, FOX_VROWS, blk), F32)],
    )
    return pl.pallas_call(
        functools.partial(_fox_body, blk=blk),
        grid_spec=grid_spec,
        out_shape=jax.ShapeDtypeStruct((s, B_WIDTH), MXU_DTYPE),
        compiler_params=_params(("parallel", "arbitrary")),
        name="forgetting_attention",
    )(lo, qt, ka, vt, proj)


GLA_GROUP = 4


def _gla_level_matrix():
    c = GLA_CHUNK
    mats = []
    for b in GLA_LEVELS:
        m = np.zeros((c, c), np.float32)
        for i in range(c):
            pivot = (i // (2 * b)) * 2 * b + b - 1
            if i > pivot:
                m[i, pivot + 1:i + 1] = 1.0
            else:
                m[i, i + 1:pivot + 1] = 1.0
        mats.append(m)
    mats.append(np.tril(np.ones((c, c), np.float32)))
    return np.concatenate(mats, axis=0)


def _gla_body(q_ref, k_ref, v_ref, g_ref, z_ref, gu_ref, gb_ref, og_ref, lvl_ref, o_ref,
              st_scr, *, tile):
    c = GLA_CHUNK
    nlev = len(GLA_LEVELS)
    nch = tile // c
    grp = GLA_GROUP * c

    @pl.when(pl.program_id(1) == 0)
    def _():
        st_scr[...] = jnp.zeros_like(st_scr)

    gate_in = g_ref[:, 0:GATE_RANK]
    gate_hi = gate_in.astype(MXU_DTYPE)
    gate_lo = (gate_in - gate_hi.astype(F32)).astype(MXU_DTYPE)
    gate_w = gu_ref[0].astype(MXU_DTYPE)
    logit = (jnp.dot(gate_hi, gate_w, preferred_element_type=F32)
             + jnp.dot(gate_lo, gate_w, preferred_element_type=F32) + gb_ref[0])
    la = _log_sigmoid(logit) * (1.0 / GATE_TEMP)

    la_cat = jnp.concatenate([la[ci * c:(ci + 1) * c] for ci in range(nch)], axis=1)
    la_hi = la_cat.astype(MXU_DTYPE)
    la_lo = (la_cat - la_hi.astype(F32)).astype(MXU_DTYPE)
    f = jnp.exp(jnp.dot(lvl_ref[0:nlev * c, :], la_hi, preferred_element_type=F32))
    tri = lvl_ref[nlev * c:(nlev + 1) * c, :]
    bc = (jnp.dot(tri, la_hi, preferred_element_type=F32)
          + jnp.dot(tri, la_lo, preferred_element_type=F32))

    row = lax.broadcasted_iota(jnp.int32, (grp, grp), 0)
    col = lax.broadcasted_iota(jnp.int32, (grp, grp), 1)
    diff = (row ^ col) & (c - 1)
    level = jnp.full((grp, grp), nlev, jnp.int32)
    for li, b in reversed(list(enumerate(GLA_LEVELS))):
        level = jnp.where(diff >= b, li, level)
    level = jnp.where(((row & -c) == (col & -c)) & (col <= row), level, -1)
    out_gain = og_ref[...]

    state = st_scr[...]
    for g0 in range(0, nch, GLA_GROUP):
        rows = slice(g0 * c, g0 * c + grp)
        q = q_ref[0, rows, :].astype(F32)
        k = k_ref[0, rows, :].astype(F32)
        v = jnp.concatenate([v_ref[0, rows, :], v_ref[1, rows, :]], axis=-1)
        z = jnp.concatenate([z_ref[0, rows, :], z_ref[1, rows, :]], axis=-1)
        a = lax.dot_general(q.astype(MXU_DTYPE), k.astype(MXU_DTYPE), _NT,
                            preferred_element_type=F32)
        attn = jnp.where(level == nlev, a, 0.0)
        for li in range(nlev):
            fl = jnp.concatenate([f[li * c:(li + 1) * c, ch * LANES:(ch + 1) * LANES]
                                  for ch in range(g0, g0 + GLA_GROUP)], axis=0)
            a = lax.dot_general((q * fl).astype(MXU_DTYPE), (k * fl).astype(MXU_DTYPE), _NT,
                                preferred_element_type=F32)
            attn = jnp.where(level == li, a, attn)
        o_intra = jnp.dot(attn.astype(MXU_DTYPE), v, preferred_element_type=F32)

        chunk_rows = [slice(ci * c, (ci + 1) * c) for ci in range(GLA_GROUP)]
        bcs = [bc[:, ch * LANES:(ch + 1) * LANES] for ch in range(g0, g0 + GLA_GROUP)]
        q_decs, decays, updates = [], [], []
        for r, bc_c in zip(chunk_rows, bcs):
            b_last = bc_c[c - 1:c, :]
            q_decs.append((q[r] * jnp.exp(bc_c)).astype(MXU_DTYPE))
            k_dec = (k[r] * jnp.exp(b_last - bc_c)).astype(MXU_DTYPE)
            decay = jnp.broadcast_to(jnp.exp(b_last), (LANES, LANES)).T
            decays.append(jnp.concatenate([decay, decay], axis=1))
            updates.append(lax.dot_general(k_dec, v[r], _TN, preferred_element_type=F32))
        states = []
        for decay, update in zip(decays, updates):
            states.append(state.astype(MXU_DTYPE))
            state = state * decay + update
        for ci, (r, q_dec, before) in enumerate(zip(chunk_rows, q_decs, states)):
            o = o_intra[r] + jnp.dot(q_dec, before, preferred_element_type=F32)
            ms = jnp.sum(o * o, axis=-1, keepdims=True) * (1.0 / C_DV)
            y = o * lax.rsqrt(ms + RMS_EPS) * out_gain * z[r].astype(F32)
            o_ref[pl.ds((g0 + ci) * c, c), :] = y.astype(o_ref.dtype)
    st_scr[...] = state


def _gated_linear_attention(proj, gate, gate_up, gate_bias, out_gain, tile=512):
    s = proj.shape[1]
    lvl = jnp.asarray(_gla_level_matrix(), MXU_DTYPE)
    return pl.pallas_call(
        functools.partial(_gla_body, tile=tile),
        grid=(C_HEADS, s // tile),
        in_specs=[
            pl.BlockSpec((1, tile, LANES), lambda h, t: (CQ + h, t, 0)),
            pl.BlockSpec((1, tile, LANES), lambda h, t: (CK + h, t, 0)),
            pl.BlockSpec((2, tile, LANES), lambda h, t: (CV // 2 + h, t, 0)),
            pl.BlockSpec((tile, LANES), lambda h, t: (t, 0)),
            pl.BlockSpec((2, tile, LANES), lambda h, t: (CZ // 2 + h, t, 0)),
            pl.BlockSpec((1, GATE_RANK, C_DK_PAD), lambda h, t: (h, 0, 0)),
            pl.BlockSpec((1, 1, C_DK_PAD), lambda h, t: (h, 0, 0)),
            pl.BlockSpec((1, C_DV_PAD), lambda h, t: (0, 0)),
            pl.BlockSpec(lvl.shape, lambda h, t: (0, 0)),
        ],
        out_specs=pl.BlockSpec((tile, C_DV_PAD), lambda h, t: (t, h)),
        out_shape=jax.ShapeDtypeStruct((s, C_HEADS * C_DV_PAD), MXU_DTYPE),
        scratch_shapes=[pltpu.VMEM((C_DK_PAD, C_DV_PAD), F32)],
        compiler_params=_params(("parallel", "arbitrary")),
        name="gated_linear_attention",
    )(proj, proj, proj, gate, proj, gate_up, gate_bias, out_gain, lvl)


OUT_COLS = 512


def _out_body(x_ref, a_ref, b_ref, c_ref, wab_ref, wc_ref, g_ref, o_ref, h_ref):
    d = o_ref.shape[1]
    ssq = jnp.zeros((o_ref.shape[0], 1), F32)
    for c0 in range(0, d, OUT_COLS):
        cols = slice(c0, c0 + OUT_COLS)
        acc = jnp.dot(a_ref[...], wab_ref[0, 0:A_WIDTH, cols], preferred_element_type=F32)
        acc = acc + jnp.dot(b_ref[...], wab_ref[0, A_WIDTH:A_WIDTH + B_WIDTH, cols],
                            preferred_element_type=F32)
        acc = acc + jnp.dot(c_ref[...], wc_ref[0, :, cols], preferred_element_type=F32)
        y = x_ref[:, cols] + acc
        o_ref[:, cols] = y
        ssq = ssq + jnp.sum(y * y, axis=-1, keepdims=True)
    scale = lax.rsqrt(ssq * (1.0 / d) + RMS_EPS)
    h_ref[...] = (o_ref[...] * scale * g_ref[0]).astype(h_ref.dtype)


def _out_proj(x, ma, mb, mc, wo_ab, wo_c, norm_g, layer, tm=512):
    s, d = x.shape
    layers = norm_g.shape[0]
    row_blk = lambda m: pl.BlockSpec((tm, m.shape[1]), lambda i: (i, 0))
    return pl.pallas_call(
        _out_body,
        grid=(s // tm,),
        in_specs=[row_blk(x), row_blk(ma), row_blk(mb), row_blk(mc),
                  pl.BlockSpec((1,) + wo_ab.shape[1:], lambda i: (layer, 0, 0)),
                  pl.BlockSpec((1,) + wo_c.shape[1:], lambda i: (layer, 0, 0)),
                  pl.BlockSpec((1, 1, d), lambda i: (min(layer + 1, layers - 1), 0, 0))],
        out_specs=[row_blk(x), row_blk(x)],
        out_shape=[jax.ShapeDtypeStruct((s, d), F32), jax.ShapeDtypeStruct((s, d), MXU_DTYPE)],
        compiler_params=_params(("parallel",)),
        name="out_proj",
    )(x, ma, mb, mc, wo_ab, wo_c, norm_g)


def _pad_heads(w, heads, width, padded):
    lead = w.shape[:-1]
    w = w.reshape(*lead, heads, width)
    w = jnp.pad(w, [(0, 0)] * len(lead) + [(0, 0), (0, padded - width)])
    return w.reshape(*lead, heads * padded)


def _prepare(w_in, w_out, a_q_gain, a_k_gain, b_q_gain, b_k_gain):
    layers, d, _ = w_in.shape
    w_t = jnp.swapaxes(w_in, 1, 2)
    ck, cv = C_HEADS * C_DK, C_HEADS * C_DV
    ab = 4 * (A_WIDTH + B_WIDTH)
    bounds = np.cumsum([ab, B_HEADS, ck, ck, cv, cv, GATE_RANK])
    bf, cq, ckk, cvv, cz, cr = [w_t[:, lo:hi] for lo, hi in zip(bounds[:-1], bounds[1:])]

    def pad_rows(w, width, padded):
        w = w.reshape(layers, C_HEADS, width, d)
        w = jnp.pad(w, ((0, 0), (0, 0), (0, padded - width), (0, 0)))
        return w.reshape(layers, C_HEADS * padded, d)

    pad_k = lambda w: pad_rows(w, C_DK, C_DK_PAD)
    pad_v = lambda w: pad_rows(w, C_DV, C_DV_PAD)
    tail = jnp.zeros((layers, MXU_COLS - GATE_RANK - B_HEADS, d), F32)
    w_c_t = jnp.concatenate([pad_k(cq), pad_k(ckk), pad_v(cvv), pad_v(cz), cr, bf, tail], axis=1)

    q_scale = HEAD_DIM ** -0.5 * LOG2E
    tile_heads = lambda g, n: jnp.tile(g, (1, n))
    ones = lambda n: jnp.ones((layers, n), F32)
    aux_ab = jnp.concatenate([
        tile_heads(a_q_gain * q_scale, A_HEADS), tile_heads(a_k_gain, A_HEADS), ones(2 * A_WIDTH),
        tile_heads(b_q_gain * q_scale, B_HEADS), tile_heads(b_k_gain, B_HEADS), ones(2 * B_WIDTH)],
        axis=-1)[:, None, :]
    aux_c = jnp.concatenate([
        jnp.full((layers, C_HEADS * C_DK_PAD), C_DK ** -0.5, F32),
        ones(C_HEADS * (C_DK_PAD + 2 * C_DV_PAD) + MXU_COLS)], axis=-1)[:, None, :]

    wo_c = w_out[:, A_WIDTH + B_WIDTH:].reshape(layers, C_HEADS, C_DV, d)
    wo_c = jnp.pad(wo_c, ((0, 0), (0, 0), (0, C_DV_PAD - C_DV), (0, 0)))
    wo_c = wo_c.reshape(layers, C_HEADS * C_DV_PAD, d).astype(MXU_DTYPE)
    wo_ab = w_out[:, :A_WIDTH + B_WIDTH].astype(MXU_DTYPE)
    return w_t, w_c_t, aux_ab, aux_c, wo_ab, wo_c


def _layer(x, h, layer, norm_g, w_in, w_c, aux_ab, aux_c, wo_ab, wo_c, fox_bias, a_shift,
           fox_shift, gla_gate_up, gla_gate_bias, gla_out_gain):
    proj_ab, v4, v16 = _in_proj(h, w_in, aux_ab, layer, AB_KINDS, "in_proj_ab")
    proj_c, gate = _in_proj(h, w_c, aux_c, layer, C_KINDS, "in_proj_c")
    c = _fox_cumsum(gate[:, GATE_RANK:GATE_RANK + B_HEADS].T, fox_bias)
    mixed_a = _dilated_mixture(proj_ab, (v4, v16), a_shift)
    mixed_b = _forgetting_attention(proj_ab, c, fox_shift)
    gate_up = _pad_heads(gla_gate_up, C_HEADS, C_DK, C_DK_PAD)
    gate_up = gate_up.reshape(GATE_RANK, C_HEADS, C_DK_PAD).transpose(1, 0, 2)
    gate_bias = _pad_heads(gla_gate_bias, C_HEADS, C_DK, C_DK_PAD).reshape(C_HEADS, 1, C_DK_PAD)
    out_gain = jnp.pad(gla_out_gain, (0, C_DV_PAD - C_DV)).reshape(1, C_DV_PAD)
    mixed_c = _gated_linear_attention(proj_c, gate, gate_up, gate_bias, out_gain)
    return _out_proj(x, mixed_a, mixed_b, mixed_c, wo_ab, wo_c, norm_g, layer)


@jax.jit
def kernel(x, norm_g, w_in, a_q_gain, a_k_gain, b_q_gain, b_k_gain, fox_bias, gla_gate_up,
           gla_gate_bias, gla_out_gain, w_out):
    bsz, s, d = x.shape
    assert bsz == 1, "batch size 1 only"
    w_t, w_c, aux_ab, aux_c, wo_ab, wo_c = _prepare(w_in, w_out, a_q_gain, a_k_gain, b_q_gain,
                                                    b_k_gain)
    norm_g = norm_g[:, None, :]
    bound = lambda gq, gk: (HEAD_DIM ** 0.5 * LOG2E * jnp.max(jnp.abs(gq), axis=-1)
                            * jnp.max(jnp.abs(gk), axis=-1))
    a_shift, fox_shift = bound(a_q_gain, a_k_gain), bound(b_q_gain, b_k_gain)
    y = x.reshape(s, d)
    h = _rmsnorm(y, norm_g, 0)
    for layer in range(w_in.shape[0]):
        y, h = _layer(y, h, layer, norm_g, w_t, w_c, aux_ab, aux_c, wo_ab, wo_c, fox_bias[layer],
                      a_shift[layer], fox_shift[layer], gla_gate_up[layer], gla_gate_bias[layer],
                      gla_out_gain[layer])
    return y.reshape(bsz, s, d)
```

```python
import functools

import numpy as np
import jax
import jax.numpy as jnp
from jax import lax
from jax.experimental import pallas as pl
from jax.experimental.pallas import tpu as pltpu

F32 = jnp.float32
MXU_DTYPE = jnp.bfloat16

LANES = 128
MXU_COLS = 256
HEAD_DIM = 64
A_HEADS, B_HEADS, C_HEADS = 12, 8, 4
A_CHUNKS, B_CHUNKS = A_HEADS // 2, B_HEADS // 2
C_DK, C_DV = 96, 192
C_DK_PAD, C_DV_PAD = 128, 256
GATE_RANK = 16
GATE_TEMP = 16.0
RMS_EPS = 1e-6
LOG2E = 1.4426950408889634
DILATIONS = (1, 4, 16)
BAND = 128
A_TILE = 2048
GLA_CHUNK = 64
GLA_LEVELS = (32, 16, 8, 4, 2, 1)
VMEM_LIMIT = 56 * 1024 * 1024

AQ, AK, AV, AZ = 0, 6, 12, 18
BQ, BK, BV, BZ = 24, 28, 32, 36
CQ, CK, CV, CZ = 0, 4, 8, 16
A_WIDTH, B_WIDTH = A_HEADS * HEAD_DIM, B_HEADS * HEAD_DIM

_NT = (((1,), (1,)), ((), ()))
_TN = (((0,), (0,)), ((), ()))


def _params(sem):
    return pltpu.CompilerParams(dimension_semantics=sem, vmem_limit_bytes=VMEM_LIMIT)


def _log_sigmoid(x):
    return jnp.minimum(x, 0.0) - jnp.log1p(jnp.exp(-jnp.abs(x)))


AB_KINDS = (("headnorm+views",) * (2 * A_CHUNKS) + ("scale+views",) * A_CHUNKS
            + ("silu",) * A_CHUNKS
            + ("headnorm",) * (2 * B_CHUNKS) + ("scale",) * B_CHUNKS + ("silu",) * B_CHUNKS)[::2]
C_KINDS = (("scale",) * (4 * C_HEADS) + ("silu",) * (2 * C_HEADS))[::2] + ("gate",)


def _kind_ranges(kinds, kind):
    runs, start = [], None
    for t, k in enumerate(kinds + (None,)):
        if k == kind and start is None:
            start = t
        elif k != kind and start is not None:
            runs.append((start, t))
            start = None
    return runs


def _view_tiles(kinds):
    n = sum(k.endswith("+views") for k in kinds)
    assert all(k.endswith("+views") for k in kinds[:n])
    return n


def _rmsnorm_body(x_ref, g_ref, o_ref):
    x = x_ref[...]
    ms = jnp.mean(x * x, axis=-1, keepdims=True)
    o_ref[...] = (x * lax.rsqrt(ms + RMS_EPS) * g_ref[0]).astype(o_ref.dtype)


def _rmsnorm(x, norm_g, layer, tm=512):
    s, d = x.shape
    return pl.pallas_call(
        _rmsnorm_body,
        grid=(s // tm,),
        in_specs=[pl.BlockSpec((tm, d), lambda i: (i, 0)),
                  pl.BlockSpec((1, 1, d), lambda i: (layer, 0, 0))],
        out_specs=pl.BlockSpec((tm, d), lambda i: (i, 0)),
        out_shape=jax.ShapeDtypeStruct((s, d), MXU_DTYPE),
        compiler_params=_params(("parallel",)),
        name="rmsnorm",
    )(x, norm_g)


def _proj_body(h_ref, w_ref, aux_ref, *rest, kinds):
    view_tiles = _view_tiles(kinds)
    n_tiles = len(kinds)
    o_ref, rest = rest[0], rest[1:]
    gate_ref = None
    if kinds[-1] == "gate":
        gate_ref, rest = rest[0], rest[1:]
    if view_tiles:
        view_refs, rest = rest[:len(DILATIONS) - 1], rest[len(DILATIONS) - 1:]
        n_stage = MXU_COLS // LANES
        stage, rest = rest[-n_stage:], rest[:-n_stage]
    acc_scr = rest[0:2]
    j = pl.program_id(1)
    tm = h_ref.shape[0]

    def matmul(dst, rows):
        dst[rows, :] = lax.dot_general(h_ref[rows, :], w_ref[0].astype(MXU_DTYPE), _NT,
                                       preferred_element_type=F32)

    def epilogue(src, kind, rows):
        r0, nr = rows.start, rows.stop - rows.start
        if kind == "gate":
            gate_ref[rows, :] = src[rows, 0:LANES]
            return
        base = kind.split("+")[0]
        for c in range(MXU_COLS // LANES):
            cols = slice(c * LANES, (c + 1) * LANES)
            y = src[rows, cols]
            if base == "headnorm":
                first = lax.broadcasted_iota(jnp.int32, y.shape, 1) < HEAD_DIM
                y2 = y * y
                s0 = jnp.sum(jnp.where(first, y2, 0.0), axis=-1, keepdims=True)
                s1 = jnp.sum(jnp.where(first, 0.0, y2), axis=-1, keepdims=True)
                ms = jnp.where(first, s0, s1) * (1.0 / HEAD_DIM)
                y = y * lax.rsqrt(ms + RMS_EPS) * aux_ref[0, :, cols]
            elif base == "scale":
                y = y * aux_ref[0, :, cols]
            elif base == "silu":
                y = y * jax.nn.sigmoid(y)
            o_ref[c, rows, :] = y.astype(o_ref.dtype)
            if not kind.endswith("+views"):
                continue
            stage[c][rows, :] = y
            for view, dil in zip(view_refs, DILATIONS[1:]):
                for r in range(dil):
                    picked = stage[c][pl.ds(r0 + r, nr // dil, stride=dil), :]
                    view[c, r0 // dil:(r0 + nr) // dil, r * LANES:(r + 1) * LANES] = (
                        picked.astype(view.dtype))

    halves = tuple(slice(r, r + tm // 4) for r in range(0, tm, tm // 4))

    @pl.when(j == 0)
    def _():
        for rows in halves:
            matmul(acc_scr[0], rows)

    for parity in range(2):
        for kind in sorted(set(kinds[:-1])):
            in_kind = functools.reduce(
                jnp.logical_or, [(j - 1 >= lo) & (j - 1 < min(hi, n_tiles - 1))
                                 for lo, hi in _kind_ranges(kinds, kind) if lo < n_tiles - 1])

            @pl.when((j >= 1) & (j < n_tiles) & (j % 2 == parity) & in_kind)
            def _(parity=parity, kind=kind):
                for rows in halves:
                    epilogue(acc_scr[1 - parity], kind, rows)
                    matmul(acc_scr[parity], rows)

    @pl.when(j == n_tiles)
    def _():
        for rows in halves:
            epilogue(acc_scr[(n_tiles - 1) % 2], kinds[-1], rows)


def _in_proj(h, w_t, aux, layer, kinds, name, tm=2048):
    s, d = h.shape
    n_tiles = len(kinds)
    view_tiles = _view_tiles(kinds)
    chunks = MXU_COLS // LANES
    has_gate = kinds[-1] == "gate"
    out_tiles = n_tiles - 1 if has_gate else n_tiles
    prev = lambda j, n: jnp.clip(j - 1, 0, n - 1)
    out_shape = [jax.ShapeDtypeStruct((out_tiles * chunks, s, LANES), MXU_DTYPE)]
    out_specs = [pl.BlockSpec((chunks, tm, LANES), lambda i, j: (prev(j, out_tiles), i, 0))]
    scratch = [pltpu.VMEM((tm, MXU_COLS), F32), pltpu.VMEM((tm, MXU_COLS), F32)]
    if has_gate:
        out_shape.append(jax.ShapeDtypeStruct((s, LANES), F32))
        out_specs.append(pl.BlockSpec((tm, LANES), lambda i, j: (i, 0)))
    if view_tiles:
        for dil in DILATIONS[1:]:
            out_shape.append(
                jax.ShapeDtypeStruct((view_tiles * chunks, s // dil, dil * LANES), MXU_DTYPE))
            out_specs.append(pl.BlockSpec((chunks, tm // dil, dil * LANES),
                                          lambda i, j: (prev(j, view_tiles), i, 0)))
        scratch += [pltpu.VMEM((tm, LANES), F32)] * chunks
    return pl.pallas_call(
        functools.partial(_proj_body, kinds=kinds),
        grid=(s // tm, n_tiles + 1),
        in_specs=[pl.BlockSpec((tm, d), lambda i, j: (i, 0)),
                  pl.BlockSpec((1, MXU_COLS, d),
                               lambda i, j: (layer, jnp.minimum(j, n_tiles - 1), 0)),
                  pl.BlockSpec((1, 1, MXU_COLS), lambda i, j: (layer, 0, prev(j, n_tiles)))],
        out_specs=out_specs,
        out_shape=out_shape,
        scratch_shapes=scratch,
        compiler_params=_params(("parallel", "arbitrary")),
        name=name,
    )(h, w_t, aux)


def _fox_cumsum_body(x_ref, b_ref, o_ref):
    x = _log_sigmoid(x_ref[...] + b_ref[...])
    idx = lax.broadcasted_iota(jnp.int32, x.shape, 1)
    shift = 1
    while shift < x.shape[1]:
        x = x + jnp.where(idx >= shift, pltpu.roll(x, shift, axis=1), 0.0)
        shift *= 2
    o_ref[...] = x * LOG2E


def _fox_cumsum(logit_t, bias):
    nh, s = logit_t.shape
    return pl.pallas_call(
        _fox_cumsum_body,
        out_shape=jax.ShapeDtypeStruct((nh, s), F32),
        compiler_params=pltpu.CompilerParams(vmem_limit_bytes=VMEM_LIMIT),
        name="fox_cumsum",
    )(logit_t, bias.reshape(nh, 1))


A_GROUP = 8


A_DEN_MIN, A_DEN_MAX = 2.0 ** -90, 2.0 ** 100


def _band_attention(blocks, shift):
    first = lax.broadcasted_iota(jnp.int32, (BAND, LANES), 1) < HEAD_DIM
    first_kv = lax.broadcasted_iota(jnp.int32, (2 * BAND, LANES), 1) < HEAD_DIM
    chains = [(b, h) for b in range(len(blocks)) for h in range(2)]
    logits = []
    for b, h in chains:
        q, k2 = blocks[b][0], blocks[b][1]
        zero = jnp.zeros_like(q)
        qm = jnp.where(first, q, zero) if h == 0 else jnp.where(first, zero, q)
        logits.append(lax.dot_general(qm, k2, _NT, preferred_element_type=F32))
    logits = [s + blocks[b][3] for (b, h), s in zip(chains, logits)]
    if shift is None:
        maxes = [jnp.max(s, axis=-1, keepdims=True) for s in logits]
        probs = [jnp.exp2(s - m).astype(MXU_DTYPE) for s, m in zip(logits, maxes)]
    else:
        maxes = [shift] * len(chains)
        probs = [jnp.exp2(s).astype(MXU_DTYPE) for s in logits]
    res = []
    for (b, h), p in zip(chains, probs):
        v2 = blocks[b][2]
        one = jnp.ones_like(v2)
        vh = jnp.where(first_kv, v2, one) if h == 0 else jnp.where(first_kv, one, v2)
        res.append(jnp.dot(p, vh, preferred_element_type=F32))
    results = []
    for b in range(len(blocks)):
        r0, r1 = res[2 * b], res[2 * b + 1]
        num = jnp.where(first, r0, r1)
        den = pltpu.roll(jnp.where(first, r1, r0), HEAD_DIM, axis=1)
        m = maxes[2 * b] if shift is not None else jnp.where(first, maxes[2 * b], maxes[2 * b + 1])
        results.append((num / den, m + jnp.log2(den), den))
    return results


def _dilated_body(*refs):
    ins, z_ref, u_ref, o_ref, scr = refs[:15], refs[15], refs[16], refs[17], refs[18:]
    kv_scr, o_scr, l_scr, den_scr = scr[:6], scr[6], scr[7], scr[8]
    n = pl.program_id(1)
    for bi, d in enumerate(DILATIONS):
        kc, kp, vc, vp = ins[5 * bi + 1:5 * bi + 5]
        kf, vf = kv_scr[2 * bi:2 * bi + 2]
        rows = A_TILE // d
        for cur, prev, full in ((kc, kp, kf), (vc, vp, vf)):
            full[0:BAND, :] = prev[0, rows - BAND:rows, :]
            full[BAND:BAND + rows, :] = cur[0]

    row = lax.broadcasted_iota(jnp.int32, (BAND, 2 * BAND), 0)
    col = lax.broadcasted_iota(jnp.int32, (BAND, 2 * BAND), 1)
    band = (col >= row) & (col <= row + BAND)
    shift = u_ref[0:1, 0:1]

    def attend(shifted):
        offset = shift if shifted else 0.0
        bias_band = jnp.where(band, 0.0, -jnp.inf) - offset
        bias_first = jnp.where(band & (col >= BAND), 0.0, -jnp.inf) - offset
        for bi, d in enumerate(DILATIONS):
            q_ref = ins[5 * bi]
            kf, vf = kv_scr[2 * bi:2 * bi + 2]
            nb = A_TILE // d // BAND

            def group(members, q_ref=q_ref, kf=kf, vf=vf, d=d, nb=nb, bi=bi):
                blocks = []
                for j, r, maybe_first in members:
                    cols = slice(r * LANES, (r + 1) * LANES)
                    row0 = j * BAND if isinstance(j, int) else pl.multiple_of(j * BAND, BAND)
                    bias = bias_band
                    if maybe_first:
                        bias = jnp.where(n * nb + j >= 1, bias_band, bias_first)
                    blocks.append((q_ref[0, pl.ds(row0, BAND), cols],
                                   kf[pl.ds(row0, 2 * BAND), cols],
                                   vf[pl.ds(row0, 2 * BAND), cols], bias))
                results = _band_attention(blocks, shift if shifted else None)
                for (j, r, _), (o, lse, den) in zip(members, results):
                    dst = pl.ds(j * BAND * d + r, BAND, stride=d)
                    o_scr[bi, dst, :] = o
                    l_scr[bi, dst, :] = lse
                    if shifted:
                        den_scr[0] = jnp.minimum(den_scr[0], den)
                        den_scr[1] = jnp.maximum(den_scr[1], den)

            if nb >= A_GROUP:
                for r in range(d):
                    def body(g, carry, r=r, group=group):
                        j0 = pl.multiple_of(g * A_GROUP, A_GROUP)
                        group([(j0 + i, r, i == 0) for i in range(A_GROUP)])
                        return carry
                    lax.fori_loop(0, nb // A_GROUP, body, 0)
            else:
                for r0 in range(0, d, A_GROUP // nb):
                    group([(j, r0 + i, j == 0) for i in range(A_GROUP // nb) for j in range(nb)])

    den_scr[0] = jnp.full((BAND, LANES), jnp.inf, F32)
    den_scr[1] = jnp.zeros((BAND, LANES), F32)
    attend(True)
    in_range = (jnp.min(den_scr[0]) >= A_DEN_MIN) & (jnp.max(den_scr[1]) <= A_DEN_MAX)
    pl.when(jnp.logical_not(in_range))(functools.partial(attend, False))
    lse = l_scr[...]
    w = jnp.exp2(lse - jnp.max(lse, axis=0, keepdims=True))
    mixed = jnp.sum(w * o_scr[...], axis=0) / jnp.sum(w, axis=0)
    o_ref[...] = (mixed * z_ref[0].astype(F32)).astype(o_ref.dtype)


def _dilated_mixture(proj, views, shift):
    s = proj.shape[1]
    operands, in_specs, scratch = [], [], []
    for d, view in zip(DILATIONS, (proj,) + tuple(views)):
        rows, width = A_TILE // d, d * LANES
        blk = (1, rows, width)
        operands += [view] * 5
        in_specs += [
            pl.BlockSpec(blk, lambda c, n: (AQ + c, n, 0)),
            pl.BlockSpec(blk, lambda c, n: (AK + c, n, 0)),
            pl.BlockSpec(blk, lambda c, n: (AK + c, jnp.maximum(n - 1, 0), 0)),
            pl.BlockSpec(blk, lambda c, n: (AV + c, n, 0)),
            pl.BlockSpec(blk, lambda c, n: (AV + c, jnp.maximum(n - 1, 0), 0)),
        ]
        scratch += [pltpu.VMEM((rows + BAND, width), MXU_DTYPE)] * 2
    scratch += [pltpu.VMEM((len(DILATIONS), A_TILE, LANES), F32)] * 2
    scratch.append(pltpu.VMEM((2, BAND, LANES), F32))
    operands += [proj, jnp.full((8, LANES), shift, F32)]
    in_specs += [pl.BlockSpec((1, A_TILE, LANES), lambda c, n: (AZ + c, n, 0)),
                 pl.BlockSpec((8, LANES), lambda c, n: (0, 0))]
    return pl.pallas_call(
        _dilated_body,
        grid=(A_CHUNKS, s // A_TILE),
        in_specs=in_specs,
        out_specs=pl.BlockSpec((A_TILE, LANES), lambda c, n: (n, c)),
        out_shape=jax.ShapeDtypeStruct((s, A_WIDTH), MXU_DTYPE),
        scratch_shapes=scratch,
        compiler_params=_params(("parallel", "arbitrary")),
        name="dilated_mixture",
    )(*operands)


FOX_AUG = 3
FOX_VROWS = HEAD_DIM + 16
FOX_DEN_MIN, FOX_DEN_MAX = 2.0 ** -90, 2.0 ** 100
FOX_ZERO_EXP = 160.0


def _split3(c):
    hi = c.astype(jnp.bfloat16).astype(F32)
    r = c - hi
    mid = r.astype(jnp.bfloat16).astype(F32)
    return hi, mid, r - mid


def _fox_prep_body(q_ref, k_ref, v_ref, ccol_ref, crow_ref, qt_ref, ka_ref, vt_ref):
    tm = q_ref.shape[1]
    q_t = q_ref[0].astype(F32).T
    v_t = v_ref[0].astype(F32).T
    k = k_ref[0]
    sub = lax.broadcasted_iota(jnp.int32, (HEAD_DIM, tm), 0)
    lane = lax.broadcasted_iota(jnp.int32, (tm, LANES), 1)
    v_tail = (lax.broadcasted_iota(jnp.int32, (FOX_VROWS - HEAD_DIM, tm), 0) == 0).astype(F32)
    for h in range(2):
        hi, mid, lo = _split3(crow_ref[0, h:h + 1, :])
        aug_q = jnp.where(sub == 0, hi, jnp.where(sub == 1, mid, jnp.where(
            sub == 2, lo, jnp.where(sub < 2 * FOX_AUG, 1.0, 0.0))))
        q_h = q_t[h * HEAD_DIM:(h + 1) * HEAD_DIM]
        parts = [q_h, aug_q] if h == 0 else [aug_q, q_h]
        qt_ref[h] = jnp.concatenate(parts, axis=0).astype(qt_ref.dtype)
        hi, mid, lo = _split3(ccol_ref[0, :, h:h + 1])
        a0 = (1 - h) * HEAD_DIM
        aug_k = jnp.where(lane == a0 + FOX_AUG, -hi, jnp.where(lane == a0 + FOX_AUG + 1, -mid, jnp.where(
            lane == a0 + FOX_AUG + 2, -lo, jnp.where((lane >= a0) & (lane < a0 + FOX_AUG), 1.0, 0.0))))
        own = (lane < HEAD_DIM) if h == 0 else (lane >= HEAD_DIM)
        ka_ref[h] = jnp.where(own, k, aug_k.astype(k.dtype))
        v_h = jnp.concatenate([v_t[h * HEAD_DIM:(h + 1) * HEAD_DIM], v_tail],
                              axis=0).astype(vt_ref.dtype)
        blk = vt_ref.shape[3]
        for b in range(tm // blk):
            vt_ref[h, b] = v_h[:, b * blk:(b + 1) * blk]


def _fox_prep(proj, c, shift, blk, tm=1024):
    s = proj.shape[1]
    c_col = c.reshape(B_CHUNKS, 2, s).transpose(0, 2, 1)
    c_row = c.reshape(B_CHUNKS, 2, s) - shift
    return pl.pallas_call(
        _fox_prep_body,
        grid=(B_CHUNKS, s // tm),
        in_specs=[
            pl.BlockSpec((1, tm, LANES), lambda p, i: (BQ + p, i, 0)),
            pl.BlockSpec((1, tm, LANES), lambda p, i: (BK + p, i, 0)),
            pl.BlockSpec((1, tm, LANES), lambda p, i: (BV + p, i, 0)),
            pl.BlockSpec((1, tm, 2), lambda p, i: (p, i, 0)),
            pl.BlockSpec((1, 2, tm), lambda p, i: (p, 0, i)),
        ],
        out_specs=[
            pl.BlockSpec((2, LANES, tm), lambda p, i: (p, 0, i)),
            pl.BlockSpec((2, tm, LANES), lambda p, i: (p, i, 0)),
            pl.BlockSpec((2, tm // blk, FOX_VROWS, blk), lambda p, i: (p, i, 0, 0)),
        ],
        out_shape=[
            jax.ShapeDtypeStruct((B_HEADS, LANES, s), MXU_DTYPE),
            jax.ShapeDtypeStruct((B_HEADS, s, LANES), MXU_DTYPE),
            jax.ShapeDtypeStruct((B_HEADS, s // blk, FOX_VROWS, blk), MXU_DTYPE),
        ],
        compiler_params=_params(("parallel", "parallel")),
        name="fox_prep",
    )(proj, proj, proj, c_col, c_row)


def _fox_body(lo_ref, qt_ref, ka_ref, vt_ref, z_ref, o_ref, s_scr, m_scr, acc_scr, *, blk):
    qi = pl.program_id(1)
    causal = (lax.broadcasted_iota(jnp.int32, (blk, blk), 0)
              <= lax.broadcasted_iota(jnp.int32, (blk, blk), 1))
    q_t = (qt_ref[0], qt_ref[1])

    def logits(kb, h):
        k0 = pl.multiple_of(kb * blk, blk)
        s_scr[h] = jnp.dot(ka_ref[h, pl.ds(k0, blk), :], q_t[h], preferred_element_type=F32)

    def accumulate_shifted(kb, h, diagonal=False):
        s = s_scr[h]
        if diagonal:
            s = jnp.where(causal, s, -jnp.inf)
        acc_scr[h] += jnp.dot(vt_ref[h, kb], jnp.exp2(s).astype(MXU_DTYPE),
                              preferred_element_type=F32)

    def accumulate_online(kb, h, diagonal=False):
        s = s_scr[h]
        if diagonal:
            s = jnp.where(causal, s, -jnp.inf)
        m = m_scr[h]
        m_new = jnp.maximum(m, jnp.max(s, axis=0, keepdims=True))
        p = jnp.exp2(s - m_new).astype(MXU_DTYPE)
        acc_scr[h] = (jnp.exp2(m - m_new) * acc_scr[h]
                      + jnp.dot(vt_ref[h, kb], p, preferred_element_type=F32))
        m_scr[h] = m_new

    def attend(accumulate, lo):
        acc_scr[...] = jnp.zeros(acc_scr.shape, F32)
        logits(lo, 0)

        def block(kb, carry):
            logits(kb, 1)
            accumulate(kb, 0)
            logits(kb + 1, 0)
            accumulate(kb, 1)
            return carry

        lax.fori_loop(lo, qi, block, 0)
        logits(qi, 1)
        accumulate(qi, 0, diagonal=True)
        accumulate(qi, 1, diagonal=True)

        outs =[acc_scr[h, 0:HEAD_DIM] / acc_scr[h, HEAD_DIM:HEAD_DIM + 1] for h in range(2)]
        o = jnp.concatenate(outs, axis=0).T
        o_ref[...] = (o * z_ref[0].astype(F32)).astype(o_ref.dtype)

    attend(accumulate_shifted, lo_ref[pl.program_id(0), qi])
    den = jnp.concatenate([acc_scr[h, HEAD_DIM:HEAD_DIM + 1] for h in range(2)], axis=0)
    in_range = (jnp.min(den) >= FOX_DEN_MIN) & (jnp.max(den) <= FOX_DEN_MAX)

    @pl.when(jnp.logical_not(in_range))
    def _():
        m_scr[...] = jnp.full(m_scr.shape, -jnp.inf, F32)
        attend(accumulate_online, 0)


def _fox_first_blocks(c, shift, blk):
    heads, s = c.shape
    nblk = s // blk
    c_first = c[:, ::blk]
    c_last = c[:, blk - 1::blk]
    dead = (c_first[:, :, None] - c_last[:, None, :]) <= -(FOX_ZERO_EXP + 0.02 * shift)
    block = jnp.arange(nblk)
    dead &= block[None, None, :] < block[None, :, None]
    lo = jnp.min(jnp.where(dead, nblk, block), axis=-1)
    return jnp.min(lo.reshape(heads // 2, 2, nblk), axis=1).astype(jnp.int32)


def _forgetting_attention(proj, c, shift, blk=512):
    s = proj.shape[1]
    qt, ka, vt = _fox_prep(proj, c, shift, blk)
    lo = _fox_first_blocks(c, shift, blk)
    grid_spec = pltpu.PrefetchScalarGridSpec(
        num_scalar_prefetch=1,
        grid=(B_CHUNKS, s // blk),
        in_specs=[
            pl.BlockSpec((2, LANES, blk), lambda p, i, lo: (p, 0, i)),
            pl.BlockSpec((2, s, LANES), lambda p, i, lo: (p, 0, 0)),
            pl.BlockSpec((2, s // blk, FOX_VROWS, blk), lambda p, i, lo: (p, 0, 0, 0)),
            pl.BlockSpec((1, blk, LANES), lambda p, i, lo: (BZ + p, i, 0)),
        ],
        out_specs=pl.BlockSpec((blk, LANES), lambda p, i, lo: (i, p)),
        scratch_shapes=[pltpu.VMEM((2, blk, blk), F32), pltpu.VMEM((2, 1, blk), F32),
                        pltpu.VMEM((2, FOX_VROWS, blk), F32)],
    )
    return pl.pallas_call(
        functools.partial(_fox_body, blk=blk),
        grid_spec=grid_spec,
        out_shape=jax.ShapeDtypeStruct((s, B_WIDTH), MXU_DTYPE),
        compiler_params=_params(("parallel", "arbitrary")),
        name="forgetting_attention",
    )(lo, qt, ka, vt, proj)


GLA_GROUP = 4


def _gla_level_matrix():
    c = GLA_CHUNK
    mats = []
    for b in GLA_LEVELS:
        m = np.zeros((c, c), np.float32)
        for i in range(c):
            pivot = (i // (2 * b)) * 2 * b + b - 1
            if i > pivot:
                m[i, pivot + 1:i + 1] = 1.0
            else:
                m[i, i + 1:pivot + 1] = 1.0
        mats.append(m)
    mats.append(np.tril(np.ones((c, c), np.float32)))
    return np.concatenate(mats, axis=0)


def _gla_body(q_ref, k_ref, v_ref, g_ref, z_ref, gu_ref, gb_ref, og_ref, lvl_ref, o_ref,
              st_scr, *, tile):
    c = GLA_CHUNK
    nlev = len(GLA_LEVELS)
    nch = tile // c
    grp = GLA_GROUP * c

    @pl.when(pl.program_id(1) == 0)
    def _():
        st_scr[...] = jnp.zeros_like(st_scr)

    gate_in = g_ref[:, 0:GATE_RANK]
    gate_hi = gate_in.astype(MXU_DTYPE)
    gate_lo = (gate_in - gate_hi.astype(F32)).astype(MXU_DTYPE)
    gate_w = gu_ref[0].astype(MXU_DTYPE)
    logit = (jnp.dot(gate_hi, gate_w, preferred_element_type=F32)
             + jnp.dot(gate_lo, gate_w, preferred_element_type=F32) + gb_ref[0])
    la = _log_sigmoid(logit) * (1.0 / GATE_TEMP)

    la_cat = jnp.concatenate([la[ci * c:(ci + 1) * c] for ci in range(nch)], axis=1)
    la_hi = la_cat.astype(MXU_DTYPE)
    la_lo = (la_cat - la_hi.astype(F32)).astype(MXU_DTYPE)
    f = jnp.exp(jnp.dot(lvl_ref[0:nlev * c, :], la_hi, preferred_element_type=F32))
    tri = lvl_ref[nlev * c:(nlev + 1) * c, :]
    bc = (jnp.dot(tri, la_hi, preferred_element_type=F32)
          + jnp.dot(tri, la_lo, preferred_element_type=F32))

    row = lax.broadcasted_iota(jnp.int32, (grp, grp), 0)
    col = lax.broadcasted_iota(jnp.int32, (grp, grp), 1)
    diff = (row ^ col) & (c - 1)
    level = jnp.full((grp, grp), nlev, jnp.int32)
    for li, b in reversed(list(enumerate(GLA_LEVELS))):
        level = jnp.where(diff >= b, li, level)
    level = jnp.where(((row & -c) == (col & -c)) & (col <= row), level, -1)
    out_gain = og_ref[...]

    state = st_scr[...]
    for g0 in range(0, nch, GLA_GROUP):
        rows = slice(g0 * c, g0 * c + grp)
        q = q_ref[0, rows, :].astype(F32)
        k = k_ref[0, rows, :].astype(F32)
        v = jnp.concatenate([v_ref[0, rows, :], v_ref[1, rows, :]], axis=-1)
        z = jnp.concatenate([z_ref[0, rows, :], z_ref[1, rows, :]], axis=-1)
        attn = jnp.where(level == nlev, jnp.sum(q * k, axis=-1, keepdims=True), 0.0)
        for li in range(nlev):
            fl = jnp.concatenate([f[li * c:(li + 1) * c, ch * LANES:(ch + 1) * LANES]
                                  for ch in range(g0, g0 + GLA_GROUP)], axis=0)
            a = lax.dot_general((q * fl).astype(MXU_DTYPE), (k * fl).astype(MXU_DTYPE), _NT,
                                preferred_element_type=F32)
            attn = jnp.where(level == li, a, attn)
        o_intra = jnp.dot(attn.astype(MXU_DTYPE), v, preferred_element_type=F32)

        chunk_rows = [slice(ci * c, (ci + 1) * c) for ci in range(GLA_GROUP)]
        bcs = [bc[:, ch * LANES:(ch + 1) * LANES] for ch in range(g0, g0 + GLA_GROUP)]
        q_decs, decays, updates = [], [], []
        for r, bc_c in zip(chunk_rows, bcs):
            b_last = bc_c[c - 1:c, :]
            q_decs.append((q[r] * jnp.exp(bc_c)).astype(MXU_DTYPE))
            k_dec = (k[r] * jnp.exp(b_last - bc_c)).astype(MXU_DTYPE)
            decay = jnp.broadcast_to(jnp.exp(b_last), (LANES, LANES)).T
            decays.append(jnp.concatenate([decay, decay], axis=1))
            updates.append(lax.dot_general(k_dec, v[r], _TN, preferred_element_type=F32))
        states = []
        for decay, update in zip(decays, updates):
            states.append(state.astype(MXU_DTYPE))
            state = state * decay + update
        for ci, (r, q_dec, before) in enumerate(zip(chunk_rows, q_decs, states)):
            o = o_intra[r] + jnp.dot(q_dec, before, preferred_element_type=F32)
            ms = jnp.sum(o * o, axis=-1, keepdims=True) * (1.0 / C_DV)
            y = o * lax.rsqrt(ms + RMS_EPS) * out_gain * z[r].astype(F32)
            o_ref[pl.ds((g0 + ci) * c, c), :] = y.astype(o_ref.dtype)
    st_scr[...] = state


def _gated_linear_attention(proj, gate, gate_up, gate_bias, out_gain, tile=512):
    s = proj.shape[1]
    lvl = jnp.asarray(_gla_level_matrix(), MXU_DTYPE)
    return pl.pallas_call(
        functools.partial(_gla_body, tile=tile),
        grid=(C_HEADS, s // tile),
        in_specs=[
            pl.BlockSpec((1, tile, LANES), lambda h, t: (CQ + h, t, 0)),
            pl.BlockSpec((1, tile, LANES), lambda h, t: (CK + h, t, 0)),
            pl.BlockSpec((2, tile, LANES), lambda h, t: (CV // 2 + h, t, 0)),
            pl.BlockSpec((tile, LANES), lambda h, t: (t, 0)),
            pl.BlockSpec((2, tile, LANES), lambda h, t: (CZ // 2 + h, t, 0)),
            pl.BlockSpec((1, GATE_RANK, C_DK_PAD), lambda h, t: (h, 0, 0)),
            pl.BlockSpec((1, 1, C_DK_PAD), lambda h, t: (h, 0, 0)),
            pl.BlockSpec((1, C_DV_PAD), lambda h, t: (0, 0)),
            pl.BlockSpec(lvl.shape, lambda h, t: (0, 0)),
        ],
        out_specs=pl.BlockSpec((tile, C_DV_PAD), lambda h, t: (t, h)),
        out_shape=jax.ShapeDtypeStruct((s, C_HEADS * C_DV_PAD), MXU_DTYPE),
        scratch_shapes=[pltpu.VMEM((C_DK_PAD, C_DV_PAD), F32)],
        compiler_params=_params(("parallel", "arbitrary")),
        name="gated_linear_attention",
    )(proj, proj, proj, gate, proj, gate_up, gate_bias, out_gain, lvl)


OUT_COLS = 512


def _out_body(x_ref, a_ref, b_ref, c_ref, wab_ref, wc_ref, g_ref, o_ref, h_ref):
    d = o_ref.shape[1]
    ssq = jnp.zeros((o_ref.shape[0], 1), F32)
    for c0 in range(0, d, OUT_COLS):
        cols = slice(c0, c0 + OUT_COLS)
        acc = jnp.dot(a_ref[...], wab_ref[0, 0:A_WIDTH, cols], preferred_element_type=F32)
        acc = acc + jnp.dot(b_ref[...], wab_ref[0, A_WIDTH:A_WIDTH + B_WIDTH, cols],
                            preferred_element_type=F32)
        acc = acc + jnp.dot(c_ref[...], wc_ref[0, :, cols], preferred_element_type=F32)
        y = x_ref[:, cols] + acc
        o_ref[:, cols] = y
        ssq = ssq + jnp.sum(y * y, axis=-1, keepdims=True)
    scale = lax.rsqrt(ssq * (1.0 / d) + RMS_EPS)
    h_ref[...] = (o_ref[...] * scale * g_ref[0]).astype(h_ref.dtype)


def _out_proj(x, ma, mb, mc, wo_ab, wo_c, norm_g, layer, tm=512):
    s, d = x.shape
    layers = norm_g.shape[0]
    row_blk = lambda m: pl.BlockSpec((tm, m.shape[1]), lambda i: (i, 0))
    return pl.pallas_call(
        _out_body,
        grid=(s // tm,),
        in_specs=[row_blk(x), row_blk(ma), row_blk(mb), row_blk(mc),
                  pl.BlockSpec((1,) + wo_ab.shape[1:], lambda i: (layer, 0, 0)),
                  pl.BlockSpec((1,) + wo_c.shape[1:], lambda i: (layer, 0, 0)),
                  pl.BlockSpec((1, 1, d), lambda i: (min(layer + 1, layers - 1), 0, 0))],
        out_specs=[row_blk(x), row_blk(x)],
        out_shape=[jax.ShapeDtypeStruct((s, d), F32), jax.ShapeDtypeStruct((s, d), MXU_DTYPE)],
        compiler_params=_params(("parallel",)),
        name="out_proj",
    )(x, ma, mb, mc, wo_ab, wo_c, norm_g)


def _pad_heads(w, heads, width, padded):
    lead = w.shape[:-1]
    w = w.reshape(*lead, heads, width)
    w = jnp.pad(w, [(0, 0)] * len(lead) + [(0, 0), (0, padded - width)])
    return w.reshape(*lead, heads * padded)


def _prepare(w_in, w_out, a_q_gain, a_k_gain, b_q_gain, b_k_gain):
    layers, d, _ = w_in.shape
    w_t = jnp.swapaxes(w_in, 1, 2)
    ck, cv = C_HEADS * C_DK, C_HEADS * C_DV
    ab = 4 * (A_WIDTH + B_WIDTH)
    bounds = np.cumsum([ab, B_HEADS, ck, ck, cv, cv, GATE_RANK])
    bf, cq, ckk, cvv, cz, cr = [w_t[:, lo:hi] for lo, hi in zip(bounds[:-1], bounds[1:])]

    def pad_rows(w, width, padded):
        w = w.reshape(layers, C_HEADS, width, d)
        w = jnp.pad(w, ((0, 0), (0, 0), (0, padded - width), (0, 0)))
        return w.reshape(layers, C_HEADS * padded, d)

    pad_k = lambda w: pad_rows(w, C_DK, C_DK_PAD)
    pad_v = lambda w: pad_rows(w, C_DV, C_DV_PAD)
    tail = jnp.zeros((layers, MXU_COLS - GATE_RANK - B_HEADS, d), F32)
    w_c_t = jnp.concatenate([pad_k(cq), pad_k(ckk), pad_v(cvv), pad_v(cz), cr, bf, tail], axis=1)

    q_scale = HEAD_DIM ** -0.5 * LOG2E
    tile_heads = lambda g, n: jnp.tile(g, (1, n))
    ones = lambda n: jnp.ones((layers, n), F32)
    aux_ab = jnp.concatenate([
        tile_heads(a_q_gain * q_scale, A_HEADS), tile_heads(a_k_gain, A_HEADS), ones(2 * A_WIDTH),
        tile_heads(b_q_gain * q_scale, B_HEADS), tile_heads(b_k_gain, B_HEADS), ones(2 * B_WIDTH)],
        axis=-1)[:, None, :]
    aux_c = jnp.concatenate([
        jnp.full((layers, C_HEADS * C_DK_PAD), C_DK ** -0.5, F32),
        ones(C_HEADS * (C_DK_PAD + 2 * C_DV_PAD) + MXU_COLS)], axis=-1)[:, None, :]

    wo_c = w_out[:, A_WIDTH + B_WIDTH:].reshape(layers, C_HEADS, C_DV, d)
    wo_c = jnp.pad(wo_c, ((0, 0), (0, 0), (0, C_DV_PAD - C_DV), (0, 0)))
    wo_c = wo_c.reshape(layers, C_HEADS * C_DV_PAD, d).astype(MXU_DTYPE)
    wo_ab = w_out[:, :A_WIDTH + B_WIDTH].astype(MXU_DTYPE)
    return w_t, w_c_t, aux_ab, aux_c, wo_ab, wo_c


def _layer(x, h, layer, norm_g, w_t, w_c, aux_ab, aux_c, wo_ab, wo_c, fox_bias, a_shift,
           fox_shift, gla_gate_up, gla_gate_bias, gla_out_gain):
    proj_ab, v4, v16 = _in_proj(h, w_t, aux_ab, layer, AB_KINDS, "in_proj_ab")
    proj_c, gate = _in_proj(h, w_c, aux_c, layer, C_KINDS, "in_proj_c")
    c = _fox_cumsum(gate[:, GATE_RANK:GATE_RANK + B_HEADS].T, fox_bias)
    mixed_a = _dilated_mixture(proj_ab, (v4, v16), a_shift)
    mixed_b = _forgetting_attention(proj_ab, c, fox_shift)
    gate_up = _pad_heads(gla_gate_up, C_HEADS, C_DK, C_DK_PAD)
    gate_up = gate_up.reshape(GATE_RANK, C_HEADS, C_DK_PAD).transpose(1, 0, 2)
    gate_bias = _pad_heads(gla_gate_bias, C_HEADS, C_DK, C_DK_PAD).reshape(C_HEADS, 1, C_DK_PAD)
    out_gain = jnp.pad(gla_out_gain, (0, C_DV_PAD - C_DV)).reshape(1, C_DV_PAD)
    mixed_c = _gated_linear_attention(proj_c, gate, gate_up, gate_bias, out_gain)
    return _out_proj(x, mixed_a, mixed_b, mixed_c, wo_ab, wo_c, norm_g, layer)


@jax.jit
def kernel(x, norm_g, w_in, a_q_gain, a_k_gain, b_q_gain, b_k_gain, fox_bias, gla_gate_up,
           gla_gate_bias, gla_out_gain, w_out):
    bsz, s, d = x.shape
    assert bsz == 1, "batch size 1 only"
    w_t, w_c, aux_ab, aux_c, wo_ab, wo_c = _prepare(w_in, w_out, a_q_gain, a_k_gain, b_q_gain,
                                                    b_k_gain)
    norm_g = norm_g[:, None, :]
    bound = lambda gq, gk: (HEAD_DIM ** 0.5 * LOG2E * jnp.max(jnp.abs(gq), axis=-1)
                            * jnp.max(jnp.abs(gk), axis=-1))
    a_shift, fox_shift = bound(a_q_gain, a_k_gain), bound(b_q_gain, b_k_gain)
    y = x.reshape(s, d)
    h = _rmsnorm(y, norm_g, 0)
    for layer in range(w_in.shape[0]):
        y, h = _layer(y, h, layer, norm_g, w_t, w_c, aux_ab, aux_c, wo_ab, wo_c, fox_bias[layer],
                      a_shift[layer], fox_shift[layer], gla_gate_up[layer], gla_gate_bias[layer],
                      gla_out_gain[layer])
    return y.reshape(bsz, s, d)
```

```python
import functools

import numpy as np
import jax
import jax.numpy as jnp
from jax import lax
from jax.experimental import pallas as pl
from jax.experimental.pallas import tpu as pltpu

F32 = jnp.float32
MXU_DTYPE = jnp.bfloat16

LANES = 128
MXU_COLS = 256
HEAD_DIM = 64
A_HEADS, B_HEADS, C_HEADS = 12, 8, 4
A_CHUNKS, B_CHUNKS = A_HEADS // 2, B_HEADS // 2
C_DK, C_DV = 96, 192
C_DK_PAD, C_DV_PAD = 128, 256
GATE_RANK = 16
GATE_TEMP = 16.0
RMS_EPS = 1e-6
LOG2E = 1.4426950408889634
DILATIONS = (1, 4, 16)
BAND = 128
A_TILE = 2048
GLA_CHUNK = 64
GLA_LEVELS = (32, 16, 8, 4, 2, 1)
VMEM_LIMIT = 56 * 1024 * 1024

AQ, AK, AV, AZ = 0, 6, 12, 18
BQ, BK, BV, BZ = 24, 28, 32, 36
CQ, CK, CV, CZ = 0, 4, 8, 16
A_WIDTH, B_WIDTH = A_HEADS * HEAD_DIM, B_HEADS * HEAD_DIM

_NT = (((1,), (1,)), ((), ()))
_TN = (((0,), (0,)), ((), ()))


def _params(sem):
    return pltpu.CompilerParams(dimension_semantics=sem, vmem_limit_bytes=VMEM_LIMIT)


def _log_sigmoid(x):
    return jnp.minimum(x, 0.0) - jnp.log1p(jnp.exp(-jnp.abs(x)))


AB_KINDS = (("headnorm+views",) * (2 * A_CHUNKS) + ("scale+views",) * A_CHUNKS
            + ("silu",) * A_CHUNKS
            + ("headnorm",) * (2 * B_CHUNKS) + ("scale",) * B_CHUNKS + ("silu",) * B_CHUNKS)[::2]
C_KINDS = (("scale",) * (4 * C_HEADS) + ("silu",) * (2 * C_HEADS))[::2] + ("gate",)


def _kind_ranges(kinds, kind):
    runs, start = [], None
    for t, k in enumerate(kinds + (None,)):
        if k == kind and start is None:
            start = t
        elif k != kind and start is not None:
            runs.append((start, t))
            start = None
    return runs


def _view_tiles(kinds):
    n = sum(k.endswith("+views") for k in kinds)
    assert all(k.endswith("+views") for k in kinds[:n])
    return n


def _rmsnorm_body(x_ref, g_ref, o_ref):
    x = x_ref[...]
    ms = jnp.mean(x * x, axis=-1, keepdims=True)
    o_ref[...] = (x * lax.rsqrt(ms + RMS_EPS) * g_ref[0]).astype(o_ref.dtype)


def _rmsnorm(x, norm_g, layer, tm=512):
    s, d = x.shape
    return pl.pallas_call(
        _rmsnorm_body,
        grid=(s // tm,),
        in_specs=[pl.BlockSpec((tm, d), lambda i: (i, 0)),
                  pl.BlockSpec((1, 1, d), lambda i: (layer, 0, 0))],
        out_specs=pl.BlockSpec((tm, d), lambda i: (i, 0)),
        out_shape=jax.ShapeDtypeStruct((s, d), MXU_DTYPE),
        compiler_params=_params(("parallel",)),
        name="rmsnorm",
    )(x, norm_g)


def _proj_body(h_ref, w_ref, aux_ref, *rest, kinds):
    view_tiles = _view_tiles(kinds)
    n_tiles = len(kinds)
    o_ref, rest = rest[0], rest[1:]
    gate_ref = None
    if kinds[-1] == "gate":
        gate_ref, rest = rest[0], rest[1:]
    if view_tiles:
        view_refs, rest = rest[:len(DILATIONS) - 1], rest[len(DILATIONS) - 1:]
        n_stage = MXU_COLS // LANES
        stage, rest = rest[-n_stage:], rest[:-n_stage]
    acc_scr = rest[0:2]
    j = pl.program_id(1)
    tm = h_ref.shape[0]

    def matmul(dst, rows):
        dst[rows, :] = lax.dot_general(h_ref[rows, :], w_ref[0].astype(MXU_DTYPE), _NT,
                                       preferred_element_type=F32)

    def epilogue(src, kind, rows):
        r0, nr = rows.start, rows.stop - rows.start
        if kind == "gate":
            gate_ref[rows, :] = src[rows, 0:LANES]
            return
        base = kind.split("+")[0]
        for c in range(MXU_COLS // LANES):
            cols = slice(c * LANES, (c + 1) * LANES)
            y = src[rows, cols]
            if base == "headnorm":
                first = lax.broadcasted_iota(jnp.int32, y.shape, 1) < HEAD_DIM
                y2 = y * y
                s0 = jnp.sum(jnp.where(first, y2, 0.0), axis=-1, keepdims=True)
                s1 = jnp.sum(jnp.where(first, 0.0, y2), axis=-1, keepdims=True)
                ms = jnp.where(first, s0, s1) * (1.0 / HEAD_DIM)
                y = y * lax.rsqrt(ms + RMS_EPS) * aux_ref[0, :, cols]
            elif base == "scale":
                y = y * aux_ref[0, :, cols]
            elif base == "silu":
                y = y * jax.nn.sigmoid(y)
            o_ref[c, rows, :] = y.astype(o_ref.dtype)
            if not kind.endswith("+views"):
                continue
            stage[c][rows, :] = y
            for view, dil in zip(view_refs, DILATIONS[1:]):
                for r in range(dil):
                    picked = stage[c][pl.ds(r0 + r, nr // dil, stride=dil), :]
                    view[c, r0 // dil:(r0 + nr) // dil, r * LANES:(r + 1) * LANES] = (
                        picked.astype(view.dtype))

    halves = tuple(slice(r, r + tm // 4) for r in range(0, tm, tm // 4))

    @pl.when(j == 0)
    def _():
        for rows in halves:
            matmul(acc_scr[0], rows)

    for parity in range(2):
        for kind in sorted(set(kinds[:-1])):
            in_kind = functools.reduce(
                jnp.logical_or, [(j - 1 >= lo) & (j - 1 < min(hi, n_tiles - 1))
                                 for lo, hi in _kind_ranges(kinds, kind) if lo < n_tiles - 1])

            @pl.when((j >= 1) & (j < n_tiles) & (j % 2 == parity) & in_kind)
            def _(parity=parity, kind=kind):
                for rows in halves:
                    epilogue(acc_scr[1 - parity], kind, rows)
                    matmul(acc_scr[parity], rows)

    @pl.when(j == n_tiles)
    def _():
        for rows in halves:
            epilogue(acc_scr[(n_tiles - 1) % 2], kinds[-1], rows)


def _in_proj(h, w_t, aux, layer, kinds, name, tm=2048):
    s, d = h.shape
    n_tiles = len(kinds)
    view_tiles = _view_tiles(kinds)
    chunks = MXU_COLS // LANES
    has_gate = kinds[-1] == "gate"
    out_tiles = n_tiles - 1 if has_gate else n_tiles
    prev = lambda j, n: jnp.clip(j - 1, 0, n - 1)
    out_shape = [jax.ShapeDtypeStruct((out_tiles * chunks, s, LANES), MXU_DTYPE)]
    out_specs = [pl.BlockSpec((chunks, tm, LANES), lambda i, j: (prev(j, out_tiles), i, 0))]
    scratch = [pltpu.VMEM((tm, MXU_COLS), F32), pltpu.VMEM((tm, MXU_COLS), F32)]
    if has_gate:
        out_shape.append(jax.ShapeDtypeStruct((s, LANES), F32))
        out_specs.append(pl.BlockSpec((tm, LANES), lambda i, j: (i, 0)))
    if view_tiles:
        for dil in DILATIONS[1:]:
            out_shape.append(
                jax.ShapeDtypeStruct((view_tiles * chunks, s // dil, dil * LANES), MXU_DTYPE))
            out_specs.append(pl.BlockSpec((chunks, tm // dil, dil * LANES),
                                          lambda i, j: (prev(j, view_tiles), i, 0)))
        scratch += [pltpu.VMEM((tm, LANES), F32)] * chunks
    return pl.pallas_call(
        functools.partial(_proj_body, kinds=kinds),
        grid=(s // tm, n_tiles + 1),
        in_specs=[pl.BlockSpec((tm, d), lambda i, j: (i, 0)),
                  pl.BlockSpec((1, MXU_COLS, d),
                               lambda i, j: (layer, jnp.minimum(j, n_tiles - 1), 0)),
                  pl.BlockSpec((1, 1, MXU_COLS), lambda i, j: (layer, 0, prev(j, n_tiles)))],
        out_specs=out_specs,
        out_shape=out_shape,
        scratch_shapes=scratch,
        compiler_params=_params(("parallel", "arbitrary")),
        name=name,
    )(h, w_t, aux)


def _fox_cumsum_body(x_ref, b_ref, o_ref):
    x = _log_sigmoid(x_ref[...] + b_ref[...])
    idx = lax.broadcasted_iota(jnp.int32, x.shape, 1)
    shift = 1
    while shift < x.shape[1]:
        x = x + jnp.where(idx >= shift, pltpu.roll(x, shift, axis=1), 0.0)
        shift *= 2
    o_ref[...] = x * LOG2E


def _fox_cumsum(logit_t, bias):
    nh, s = logit_t.shape
    return pl.pallas_call(
        _fox_cumsum_body,
        out_shape=jax.ShapeDtypeStruct((nh, s), F32),
        compiler_params=pltpu.CompilerParams(vmem_limit_bytes=VMEM_LIMIT),
        name="fox_cumsum",
    )(logit_t, bias.reshape(nh, 1))


A_GROUP = 16


A_DEN_MIN, A_DEN_MAX = 2.0 ** -90, 2.0 ** 100


def _band_attention(blocks, shift):
    first = lax.broadcasted_iota(jnp.int32, (BAND, LANES), 1) < HEAD_DIM
    first_kv = lax.broadcasted_iota(jnp.int32, (2 * BAND, LANES), 1) < HEAD_DIM
    chains = [(b, h) for b in range(len(blocks)) for h in range(2)]
    logits = []
    for b, h in chains:
        q, k2 = blocks[b][0], blocks[b][1]
        zero = jnp.zeros_like(q)
        qm = jnp.where(first, q, zero) if h == 0 else jnp.where(first, zero, q)
        logits.append(lax.dot_general(qm, k2, _NT, preferred_element_type=F32))
    logits = [s + blocks[b][3] for (b, h), s in zip(chains, logits)]
    if shift is None:
        maxes = [jnp.max(s, axis=-1, keepdims=True) for s in logits]
        probs = [jnp.exp2(s - m).astype(MXU_DTYPE) for s, m in zip(logits, maxes)]
    else:
        maxes = [shift] * len(chains)
        probs = [jnp.exp2(s).astype(MXU_DTYPE) for s in logits]
    res = []
    for (b, h), p in zip(chains, probs):
        v2 = blocks[b][2]
        one = jnp.ones_like(v2)
        vh = jnp.where(first_kv, v2, one) if h == 0 else jnp.where(first_kv, one, v2)
        res.append(jnp.dot(p, vh, preferred_element_type=F32))
    results = []
    for b in range(len(blocks)):
        r0, r1 = res[2 * b], res[2 * b + 1]
        num = jnp.where(first, r0, r1)
        den = pltpu.roll(jnp.where(first, r1, r0), HEAD_DIM, axis=1)
        m = maxes[2 * b] if shift is not None else jnp.where(first, maxes[2 * b], maxes[2 * b + 1])
        results.append((num / den, m + jnp.log2(den), den))
    return results


def _dilated_body(*refs):
    ins, z_ref, u_ref, o_ref, scr = refs[:15], refs[15], refs[16], refs[17], refs[18:]
    kv_scr, o_scr, l_scr, den_scr = scr[:6], scr[6], scr[7], scr[8]
    n = pl.program_id(1)
    for bi, d in enumerate(DILATIONS):
        kc, kp, vc, vp = ins[5 * bi + 1:5 * bi + 5]
        kf, vf = kv_scr[2 * bi:2 * bi + 2]
        rows = A_TILE // d
        for cur, prev, full in ((kc, kp, kf), (vc, vp, vf)):
            full[0:BAND, :] = prev[0, rows - BAND:rows, :]
            full[BAND:BAND + rows, :] = cur[0]

    row = lax.broadcasted_iota(jnp.int32, (BAND, 2 * BAND), 0)
    col = lax.broadcasted_iota(jnp.int32, (BAND, 2 * BAND), 1)
    band = (col >= row) & (col <= row + BAND)
    shift = u_ref[0:1, 0:1]

    def attend(shifted):
        offset = shift if shifted else 0.0
        bias_band = jnp.where(band, 0.0, -jnp.inf) - offset
        bias_first = jnp.where(band & (col >= BAND), 0.0, -jnp.inf) - offset
        for bi, d in enumerate(DILATIONS):
            q_ref = ins[5 * bi]
            kf, vf = kv_scr[2 * bi:2 * bi + 2]
            nb = A_TILE // d // BAND

            def group(members, q_ref=q_ref, kf=kf, vf=vf, d=d, nb=nb, bi=bi):
                blocks = []
                for j, r, maybe_first in members:
                    cols = slice(r * LANES, (r + 1) * LANES)
                    row0 = j * BAND if isinstance(j, int) else pl.multiple_of(j * BAND, BAND)
                    bias = bias_band
                    if maybe_first:
                        bias = jnp.where(n * nb + j >= 1, bias_band, bias_first)
                    blocks.append((q_ref[0, pl.ds(row0, BAND), cols],
                                   kf[pl.ds(row0, 2 * BAND), cols],
                                   vf[pl.ds(row0, 2 * BAND), cols], bias))
                results = _band_attention(blocks, shift if shifted else None)
                for (j, r, _), (o, lse, den) in zip(members, results):
                    dst = pl.ds(j * BAND * d + r, BAND, stride=d)
                    o_scr[bi, dst, :] = o
                    l_scr[bi, dst, :] = lse
                    if shifted:
                        den_scr[0] = jnp.minimum(den_scr[0], den)
                        den_scr[1] = jnp.maximum(den_scr[1], den)

            if nb >= A_GROUP:
                for r in range(d):
                    def body(g, carry, r=r, group=group):
                        j0 = pl.multiple_of(g * A_GROUP, A_GROUP)
                        group([(j0 + i, r, i == 0) for i in range(A_GROUP)])
                        return carry
                    lax.fori_loop(0, nb // A_GROUP, body, 0)
            else:
                for r0 in range(0, d, A_GROUP // nb):
                    group([(j, r0 + i, j == 0) for i in range(A_GROUP // nb) for j in range(nb)])

    den_scr[0] = jnp.full((BAND, LANES), jnp.inf, F32)
    den_scr[1] = jnp.zeros((BAND, LANES), F32)
    attend(True)
    in_range = (jnp.min(den_scr[0]) >= A_DEN_MIN) & (jnp.max(den_scr[1]) <= A_DEN_MAX)
    pl.when(jnp.logical_not(in_range))(functools.partial(attend, False))
    lse = l_scr[...]
    w = jnp.exp2(lse - jnp.max(lse, axis=0, keepdims=True))
    mixed = jnp.sum(w * o_scr[...], axis=0) / jnp.sum(w, axis=0)
    o_ref[...] = (mixed * z_ref[0].astype(F32)).astype(o_ref.dtype)


def _dilated_mixture(proj, views, shift):
    s = proj.shape[1]
    operands, in_specs, scratch = [], [], []
    for d, view in zip(DILATIONS, (proj,) + tuple(views)):
        rows, width = A_TILE // d, d * LANES
        blk = (1, rows, width)
        operands += [view] * 5
        in_specs += [
            pl.BlockSpec(blk, lambda c, n: (AQ + c, n, 0)),
            pl.BlockSpec(blk, lambda c, n: (AK + c, n, 0)),
            pl.BlockSpec(blk, lambda c, n: (AK + c, jnp.maximum(n - 1, 0), 0)),
            pl.BlockSpec(blk, lambda c, n: (AV + c, n, 0)),
            pl.BlockSpec(blk, lambda c, n: (AV + c, jnp.maximum(n - 1, 0), 0)),
        ]
        scratch += [pltpu.VMEM((rows + BAND, width), MXU_DTYPE)] * 2
    scratch += [pltpu.VMEM((len(DILATIONS), A_TILE, LANES), F32)] * 2
    scratch.append(pltpu.VMEM((2, BAND, LANES), F32))
    operands += [proj, jnp.full((8, LANES), shift, F32)]
    in_specs += [pl.BlockSpec((1, A_TILE, LANES), lambda c, n: (AZ + c, n, 0)),
                 pl.BlockSpec((8, LANES), lambda c, n: (0, 0))]
    return pl.pallas_call(
        _dilated_body,
        grid=(A_CHUNKS, s // A_TILE),
        in_specs=in_specs,
        out_specs=pl.BlockSpec((A_TILE, LANES), lambda c, n: (n, c)),
        out_shape=jax.ShapeDtypeStruct((s, A_WIDTH), MXU_DTYPE),
        scratch_shapes=scratch,
        compiler_params=_params(("parallel", "arbitrary")),
        name="dilated_mixture",
    )(*operands)


FOX_AUG = 3
FOX_VROWS = HEAD_DIM + 16
FOX_DEN_MIN, FOX_DEN_MAX = 2.0 ** -90, 2.0 ** 100
FOX_ZERO_EXP = 160.0


def _split3(c):
    hi = c.astype(jnp.bfloat16).astype(F32)
    r = c - hi
    mid = r.astype(jnp.bfloat16).astype(F32)
    return hi, mid, r - mid


def _fox_prep_body(q_ref, k_ref, v_ref, ccol_ref, crow_ref, qt_ref, ka_ref, vt_ref):
    tm = q_ref.shape[1]
    q_t = q_ref[0].astype(F32).T
    v_t = v_ref[0].astype(F32).T
    k = k_ref[0]
    sub = lax.broadcasted_iota(jnp.int32, (HEAD_DIM, tm), 0)
    lane = lax.broadcasted_iota(jnp.int32, (tm, LANES), 1)
    v_tail = (lax.broadcasted_iota(jnp.int32, (FOX_VROWS - HEAD_DIM, tm), 0) == 0).astype(F32)
    for h in range(2):
        hi, mid, lo = _split3(crow_ref[0, h:h + 1, :])
        aug_q = jnp.where(sub == 0, hi, jnp.where(sub == 1, mid, jnp.where(
            sub == 2, lo, jnp.where(sub < 2 * FOX_AUG, 1.0, 0.0))))
        q_h = q_t[h * HEAD_DIM:(h + 1) * HEAD_DIM]
        parts = [q_h, aug_q] if h == 0 else [aug_q, q_h]
        qt_ref[h] = jnp.concatenate(parts, axis=0).astype(qt_ref.dtype)
        hi, mid, lo = _split3(ccol_ref[0, :, h:h + 1])
        a0 = (1 - h) * HEAD_DIM
        aug_k = jnp.where(lane == a0 + FOX_AUG, -hi, jnp.where(lane == a0 + FOX_AUG + 1, -mid, jnp.where(
            lane == a0 + FOX_AUG + 2, -lo, jnp.where((lane >= a0) & (lane < a0 + FOX_AUG), 1.0, 0.0))))
        own = (lane < HEAD_DIM) if h == 0 else (lane >= HEAD_DIM)
        ka_ref[h] = jnp.where(own, k, aug_k.astype(k.dtype))
        v_h = jnp.concatenate([v_t[h * HEAD_DIM:(h + 1) * HEAD_DIM], v_tail],
                              axis=0).astype(vt_ref.dtype)
        blk = vt_ref.shape[3]
        for b in range(tm // blk):
            vt_ref[h, b] = v_h[:, b * blk:(b + 1) * blk]


def _fox_prep(proj, c, shift, blk, tm=2048):
    s = proj.shape[1]
    c_col = c.reshape(B_CHUNKS, 2, s).transpose(0, 2, 1)
    c_row = c.reshape(B_CHUNKS, 2, s) - shift
    return pl.pallas_call(
        _fox_prep_body,
        grid=(B_CHUNKS, s // tm),
        in_specs=[
            pl.BlockSpec((1, tm, LANES), lambda p, i: (BQ + p, i, 0)),
            pl.BlockSpec((1, tm, LANES), lambda p, i: (BK + p, i, 0)),
            pl.BlockSpec((1, tm, LANES), lambda p, i: (BV + p, i, 0)),
            pl.BlockSpec((1, tm, 2), lambda p, i: (p, i, 0)),
            pl.BlockSpec((1, 2, tm), lambda p, i: (p, 0, i)),
        ],
        out_specs=[
            pl.BlockSpec((2, LANES, tm), lambda p, i: (p, 0, i)),
            pl.BlockSpec((2, tm, LANES), lambda p, i: (p, i, 0)),
            pl.BlockSpec((2, tm // blk, FOX_VROWS, blk), lambda p, i: (p, i, 0, 0)),
        ],
        out_shape=[
            jax.ShapeDtypeStruct((B_HEADS, LANES, s), MXU_DTYPE),
            jax.ShapeDtypeStruct((B_HEADS, s, LANES), MXU_DTYPE),
            jax.ShapeDtypeStruct((B_HEADS, s // blk, FOX_VROWS, blk), MXU_DTYPE),
        ],
        compiler_params=_params(("parallel", "parallel")),
        name="fox_prep",
    )(proj, proj, proj, c_col, c_row)


def _fox_body(lo_ref, qt_ref, ka_ref, vt_ref, z_ref, o_ref, s_scr, m_scr, acc_scr, *, blk):
    qi = pl.program_id(1)
    causal = (lax.broadcasted_iota(jnp.int32, (blk, blk), 0)
              <= lax.broadcasted_iota(jnp.int32, (blk, blk), 1))
    q_t = (qt_ref[0], qt_ref[1])

    def logits(kb, h):
        k0 = pl.multiple_of(kb * blk, blk)
        s_scr[h] = jnp.dot(ka_ref[h, pl.ds(k0, blk), :], q_t[h], preferred_element_type=F32)

    def accumulate_shifted(kb, h, diagonal=False):
        s = s_scr[h]
        if diagonal:
            s = jnp.where(causal, s, -jnp.inf)
        acc_scr[h] += jnp.dot(vt_ref[h, kb], jnp.exp2(s).astype(MXU_DTYPE),
                              preferred_element_type=F32)

    def accumulate_online(kb, h, diagonal=False):
        s = s_scr[h]
        if diagonal:
            s = jnp.where(causal, s, -jnp.inf)
        m = m_scr[h]
        m_new = jnp.maximum(m, jnp.max(s, axis=0, keepdims=True))
        p = jnp.exp2(s - m_new).astype(MXU_DTYPE)
        acc_scr[h] = (jnp.exp2(m - m_new) * acc_scr[h]
                      + jnp.dot(vt_ref[h, kb], p, preferred_element_type=F32))
        m_scr[h] = m_new

    def attend(accumulate, lo):
        acc_scr[...] = jnp.zeros(acc_scr.shape, F32)
        logits(lo, 0)

        def block(kb, carry):
            logits(kb, 1)
            accumulate(kb, 0)
            logits(kb + 1, 0)
            accumulate(kb, 1)
            return carry

        lax.fori_loop(lo, qi, block, 0)
        logits(qi, 1)
        accumulate(qi, 0, diagonal=True)
        accumulate(qi, 1, diagonal=True)

        outs =[acc_scr[h, 0:HEAD_DIM] / acc_scr[h, HEAD_DIM:HEAD_DIM + 1] for h in range(2)]
        o = jnp.concatenate(outs, axis=0).T
        o_ref[...] = (o * z_ref[0].astype(F32)).astype(o_ref.dtype)

    attend(accumulate_shifted, lo_ref[pl.program_id(0), qi])
    den = jnp.concatenate([acc_scr[h, HEAD_DIM:HEAD_DIM + 1] for h in range(2)], axis=0)
    in_range = (jnp.min(den) >= FOX_DEN_MIN) & (jnp.max(den) <= FOX_DEN_MAX)

    @pl.when(jnp.logical_not(in_range))
    def _():
        m_scr[...] = jnp.full(m_scr.shape, -jnp.inf, F32)
        attend(accumulate_online, 0)


def _fox_first_blocks(c, shift, blk):
    heads, s = c.shape
    nblk = s // blk
    c_first = c[:, ::blk]
    c_last = c[:, blk - 1::blk]
    dead = (c_first[:, :, None] - c_last[:, None, :]) <= -(FOX_ZERO_EXP + 0.02 * shift)
    block = jnp.arange(nblk)
    dead &= block[None, None, :] < block[None, :, None]
    lo = jnp.min(jnp.where(dead, nblk, block), axis=-1)
    return jnp.min(lo.reshape(heads // 2, 2, nblk), axis=1).astype(jnp.int32)


def _forgetting_attention(proj, c, shift, blk=512):
    s = proj.shape[1]
    qt, ka, vt = _fox_prep(proj, c, shift, blk)
    lo = _fox_first_blocks(c, shift, blk)
    grid_spec = pltpu.PrefetchScalarGridSpec(
        num_scalar_prefetch=1,
        grid=(B_CHUNKS, s // blk),
        in_specs=[
            pl.BlockSpec((2, LANES, blk), lambda p, i, lo: (p, 0, i)),
            pl.BlockSpec((2, s, LANES), lambda p, i, lo: (p, 0, 0)),
            pl.BlockSpec((2, s // blk, FOX_VROWS, blk), lambda p, i, lo: (p, 0, 0, 0)),
            pl.BlockSpec((1, blk, LANES), lambda p, i, lo: (BZ + p, i, 0)),
        ],
        out_specs=pl.BlockSpec((blk, LANES), lambda p, i, lo: (i, p)),
        scratch_shapes=[pltpu.VMEM((2, blk, blk), F32), pltpu.VMEM((2, 1, blk), F32),
                        pltpu.VMEM((2, FOX_VROWS, blk), F32)],
    )
    return pl.pallas_call(
        functools.partial(_fox_body, blk=blk),
        grid_spec=grid_spec,
        out_shape=jax.ShapeDtypeStruct((s, B_WIDTH), MXU_DTYPE),
        compiler_params=_params(("parallel", "arbitrary")),
        name="forgetting_attention",
    )(lo, qt, ka, vt, proj)


GLA_GROUP = 4


def _gla_level_matrix():
    c = GLA_CHUNK
    mats = []
    for b in GLA_LEVELS:
        m = np.zeros((c, c), np.float32)
        for i in range(c):
            pivot = (i // (2 * b)) * 2 * b + b - 1
            if i > pivot:
                m[i, pivot + 1:i + 1] = 1.0
            else:
                m[i, i + 1:pivot + 1] = 1.0
        mats.append(m)
    mats.append(np.tril(np.ones((c, c), np.float32)))
    return np.concatenate(mats, axis=0)


def _gla_body(q_ref, k_ref, v_ref, g_ref, z_ref, gu_ref, gb_ref, og_ref, lvl_ref, o_ref,
              st_scr, *, tile):
    c = GLA_CHUNK
    nlev = len(GLA_LEVELS)
    nch = tile // c
    grp = GLA_GROUP * c

    @pl.when(pl.program_id(1) == 0)
    def _():
        st_scr[...] = jnp.zeros_like(st_scr)

    gate_in = g_ref[:, 0:GATE_RANK]
    gate_hi = gate_in.astype(MXU_DTYPE)
    gate_lo = (gate_in - gate_hi.astype(F32)).astype(MXU_DTYPE)
    gate_w = gu_ref[0].astype(MXU_DTYPE)
    logit = (jnp.dot(gate_hi, gate_w, preferred_element_type=F32)
             + jnp.dot(gate_lo, gate_w, preferred_element_type=F32) + gb_ref[0])
    la = _log_sigmoid(logit) * (1.0 / GATE_TEMP)

    la_cat = jnp.concatenate([la[ci * c:(ci + 1) * c] for ci in range(nch)], axis=1)
    la_hi = la_cat.astype(MXU_DTYPE)
    la_lo = (la_cat - la_hi.astype(F32)).astype(MXU_DTYPE)
    f = jnp.exp(jnp.dot(lvl_ref[0:nlev * c, :], la_hi, preferred_element_type=F32))
    tri = lvl_ref[nlev * c:(nlev + 1) * c, :]
    bc = (jnp.dot(tri, la_hi, preferred_element_type=F32)
          + jnp.dot(tri, la_lo, preferred_element_type=F32))

    row = lax.broadcasted_iota(jnp.int32, (grp, grp), 0)
    col = lax.broadcasted_iota(jnp.int32, (grp, grp), 1)
    diff = (row ^ col) & (c - 1)
    level = jnp.full((grp, grp), nlev, jnp.int32)
    for li, b in reversed(list(enumerate(GLA_LEVELS))):
        level = jnp.where(diff >= b, li, level)
    level = jnp.where(((row & -c) == (col & -c)) & (col <= row), level, -1)
    out_gain = og_ref[...]

    state = st_scr[...]
    for g0 in range(0, nch, GLA_GROUP):
        rows = slice(g0 * c, g0 * c + grp)
        q = q_ref[0, rows, :].astype(F32)
        k = k_ref[0, rows, :].astype(F32)
        v = jnp.concatenate([v_ref[0, rows, :], v_ref[1, rows, :]], axis=-1)
        z = jnp.concatenate([z_ref[0, rows, :], z_ref[1, rows, :]], axis=-1)
        attn = jnp.where(level == nlev, jnp.sum(q * k, axis=-1, keepdims=True), 0.0)
        for li in range(nlev):
            fl = jnp.concatenate([f[li * c:(li + 1) * c, ch * LANES:(ch + 1) * LANES]
                                  for ch in range(g0, g0 + GLA_GROUP)], axis=0)
            a = lax.dot_general((q * fl).astype(MXU_DTYPE), (k * fl).astype(MXU_DTYPE), _NT,
                                preferred_element_type=F32)
            attn = jnp.where(level == li, a, attn)
        o_intra = jnp.dot(attn.astype(MXU_DTYPE), v, preferred_element_type=F32)

        chunk_rows = [slice(ci * c, (ci + 1) * c) for ci in range(GLA_GROUP)]
        bcs = [bc[:, ch * LANES:(ch + 1) * LANES] for ch in range(g0, g0 + GLA_GROUP)]
        q_decs, decays, updates = [], [], []
        for r, bc_c in zip(chunk_rows, bcs):
            b_last = bc_c[c - 1:c, :]
            q_decs.append((q[r] * jnp.exp(bc_c)).astype(MXU_DTYPE))
            k_dec = (k[r] * jnp.exp(b_last - bc_c)).astype(MXU_DTYPE)
            decay = jnp.broadcast_to(jnp.exp(b_last), (LANES, LANES)).T
            decays.append(jnp.concatenate([decay, decay], axis=1))
            updates.append(lax.dot_general(k_dec, v[r], _TN, preferred_element_type=F32))
        states = []
        for decay, update in zip(decays, updates):
            states.append(state.astype(MXU_DTYPE))
            state = state * decay + update
        for ci, (r, q_dec, before) in enumerate(zip(chunk_rows, q_decs, states)):
            o = o_intra[r] + jnp.dot(q_dec, before, preferred_element_type=F32)
            ms = jnp.sum(o * o, axis=-1, keepdims=True) * (1.0 / C_DV)
            y = o * lax.rsqrt(ms + RMS_EPS) * out_gain * z[r].astype(F32)
            o_ref[pl.ds((g0 + ci) * c, c), :] = y.astype(o_ref.dtype)
    st_scr[...] = state


def _gated_linear_attention(proj, gate, gate_up, gate_bias, out_gain, tile=1024):
    s = proj.shape[1]
    lvl = jnp.asarray(_gla_level_matrix(), MXU_DTYPE)
    return pl.pallas_call(
        functools.partial(_gla_body, tile=tile),
        grid=(C_HEADS, s // tile),
        in_specs=[
            pl.BlockSpec((1, tile, LANES), lambda h, t: (CQ + h, t, 0)),
            pl.BlockSpec((1, tile, LANES), lambda h, t: (CK + h, t, 0)),
            pl.BlockSpec((2, tile, LANES), lambda h, t: (CV // 2 + h, t, 0)),
            pl.BlockSpec((tile, LANES), lambda h, t: (t, 0)),
            pl.BlockSpec((2, tile, LANES), lambda h, t: (CZ // 2 + h, t, 0)),
            pl.BlockSpec((1, GATE_RANK, C_DK_PAD), lambda h, t: (h, 0, 0)),
            pl.BlockSpec((1, 1, C_DK_PAD), lambda h, t: (h, 0, 0)),
            pl.BlockSpec((1, C_DV_PAD), lambda h, t: (0, 0)),
            pl.BlockSpec(lvl.shape, lambda h, t: (0, 0)),
        ],
        out_specs=pl.BlockSpec((tile, C_DV_PAD), lambda h, t: (t, h)),
        out_shape=jax.ShapeDtypeStruct((s, C_HEADS * C_DV_PAD), MXU_DTYPE),
        scratch_shapes=[pltpu.VMEM((C_DK_PAD, C_DV_PAD), F32)],
        compiler_params=_params(("parallel", "arbitrary")),
        name="gated_linear_attention",
    )(proj, proj, proj, gate, proj, gate_up, gate_bias, out_gain, lvl)


OUT_COLS = 512


def _out_body(x_ref, a_ref, b_ref, c_ref, wab_ref, wc_ref, g_ref, o_ref, h_ref):
    d = o_ref.shape[1]
    ssq = jnp.zeros((o_ref.shape[0], 1), F32)
    for c0 in range(0, d, OUT_COLS):
        cols = slice(c0, c0 + OUT_COLS)
        acc = jnp.dot(a_ref[...], wab_ref[0, 0:A_WIDTH, cols], preferred_element_type=F32)
        acc = acc + jnp.dot(b_ref[...], wab_ref[0, A_WIDTH:A_WIDTH + B_WIDTH, cols],
                            preferred_element_type=F32)
        acc = acc + jnp.dot(c_ref[...], wc_ref[0, :, cols], preferred_element_type=F32)
        y = x_ref[:, cols] + acc
        o_ref[:, cols] = y
        ssq = ssq + jnp.sum(y * y, axis=-1, keepdims=True)
    scale = lax.rsqrt(ssq * (1.0 / d) + RMS_EPS)
    h_ref[...] = (o_ref[...] * scale * g_ref[0]).astype(h_ref.dtype)


def _out_proj(x, ma, mb, mc, wo_ab, wo_c, norm_g, layer, tm=512):
    s, d = x.shape
    layers = norm_g.shape[0]
    row_blk = lambda m: pl.BlockSpec((tm, m.shape[1]), lambda i: (i, 0))
    return pl.pallas_call(
        _out_body,
        grid=(s // tm,),
        in_specs=[row_blk(x), row_blk(ma), row_blk(mb), row_blk(mc),
                  pl.BlockSpec((1,) + wo_ab.shape[1:], lambda i: (layer, 0, 0)),
                  pl.BlockSpec((1,) + wo_c.shape[1:], lambda i: (layer, 0, 0)),
                  pl.BlockSpec((1, 1, d), lambda i: (min(layer + 1, layers - 1), 0, 0))],
        out_specs=[row_blk(x), row_blk(x)],
        out_shape=[jax.ShapeDtypeStruct((s, d), F32), jax.ShapeDtypeStruct((s, d), MXU_DTYPE)],
        compiler_params=_params(("parallel",)),
        name="out_proj",
    )(x, ma, mb, mc, wo_ab, wo_c, norm_g)


def _pad_heads(w, heads, width, padded):
    lead = w.shape[:-1]
    w = w.reshape(*lead, heads, width)
    w = jnp.pad(w, [(0, 0)] * len(lead) + [(0, 0), (0, padded - width)])
    return w.reshape(*lead, heads * padded)


def _prepare(w_in, w_out, a_q_gain, a_k_gain, b_q_gain, b_k_gain):
    layers, d, _ = w_in.shape
    w_t = jnp.swapaxes(w_in, 1, 2)
    ck, cv = C_HEADS * C_DK, C_HEADS * C_DV
    ab = 4 * (A_WIDTH + B_WIDTH)
    bounds = np.cumsum([ab, B_HEADS, ck, ck, cv, cv, GATE_RANK])
    bf, cq, ckk, cvv, cz, cr = [w_t[:, lo:hi] for lo, hi in zip(bounds[:-1], bounds[1:])]

    def pad_rows(w, width, padded):
        w = w.reshape(layers, C_HEADS, width, d)
        w = jnp.pad(w, ((0, 0), (0, 0), (0, padded - width), (0, 0)))
        return w.reshape(layers, C_HEADS * padded, d)

    pad_k = lambda w: pad_rows(w, C_DK, C_DK_PAD)
    pad_v = lambda w: pad_rows(w, C_DV, C_DV_PAD)
    tail = jnp.zeros((layers, MXU_COLS - GATE_RANK - B_HEADS, d), F32)
    w_c_t = jnp.concatenate([pad_k(cq), pad_k(ckk), pad_v(cvv), pad_v(cz), cr, bf, tail], axis=1)

    q_scale = HEAD_DIM ** -0.5 * LOG2E
    tile_heads = lambda g, n: jnp.tile(g, (1, n))
    ones = lambda n: jnp.ones((layers, n), F32)
    aux_ab = jnp.concatenate([
        tile_heads(a_q_gain * q_scale, A_HEADS), tile_heads(a_k_gain, A_HEADS), ones(2 * A_WIDTH),
        tile_heads(b_q_gain * q_scale, B_HEADS), tile_heads(b_k_gain, B_HEADS), ones(2 * B_WIDTH)],
        axis=-1)[:, None, :]
    aux_c = jnp.concatenate([
        jnp.full((layers, C_HEADS * C_DK_PAD), C_DK ** -0.5, F32),
        ones(C_HEADS * (C_DK_PAD + 2 * C_DV_PAD) + MXU_COLS)], axis=-1)[:, None, :]

    wo_c = w_out[:, A_WIDTH + B_WIDTH:].reshape(layers, C_HEADS, C_DV, d)
    wo_c = jnp.pad(wo_c, ((0, 0), (0, 0), (0, C_DV_PAD - C_DV), (0, 0)))
    wo_c = wo_c.reshape(layers, C_HEADS * C_DV_PAD, d).astype(MXU_DTYPE)
    wo_ab = w_out[:, :A_WIDTH + B_WIDTH].astype(MXU_DTYPE)
    return w_t, w_c_t, aux_ab, aux_c, wo_ab, wo_c


def _layer(x, h, layer, norm_g, w_t, w_c, aux_ab, aux_c, wo_ab, wo_c, fox_bias, a_shift,
           fox_shift, gla_gate_up, gla_gate_bias, gla_out_gain):
    proj_ab, v4, v16 = _in_proj(h, w_t, aux_ab, layer, AB_KINDS, "in_proj_ab")
    proj_c, gate = _in_proj(h, w_c, aux_c, layer, C_KINDS, "in_proj_c")
    c = _fox_cumsum(gate[:, GATE_RANK:GATE_RANK + B_HEADS].T, fox_bias)
    mixed_a = _dilated_mixture(proj_ab, (v4, v16), a_shift)
    mixed_b = _forgetting_attention(proj_ab, c, fox_shift)
    gate_up = _pad_heads(gla_gate_up, C_HEADS, C_DK, C_DK_PAD)
    gate_up = gate_up.reshape(GATE_RANK, C_HEADS, C_DK_PAD).transpose(1, 0, 2)
    gate_bias = _pad_heads(gla_gate_bias, C_HEADS, C_DK, C_DK_PAD).reshape(C_HEADS, 1, C_DK_PAD)
    out_gain = jnp.pad(gla_out_gain, (0, C_DV_PAD - C_DV)).reshape(1, C_DV_PAD)
    mixed_c = _gated_linear_attention(proj_c, gate, gate_up, gate_bias, out_gain)
    return _out_proj(x, mixed_a, mixed_b, mixed_c, wo_ab, wo_c, norm_g, layer)


@jax.jit
def kernel(x, norm_g, w_in, a_q_gain, a_k_gain, b_q_gain, b_k_gain, fox_bias, gla_gate_up,
           gla_gate_bias, gla_out_gain, w_out):
    bsz, s, d = x.shape
    assert bsz == 1, "batch size 1 only"
    w_t, w_c, aux_ab, aux_c, wo_ab, wo_c = _prepare(w_in, w_out, a_q_gain, a_k_gain, b_q_gain,
                                                    b_k_gain)
    norm_g = norm_g[:, None, :]
    bound = lambda gq, gk: (HEAD_DIM ** 0.5 * LOG2E * jnp.max(jnp.abs(gq), axis=-1)
                            * jnp.max(jnp.abs(gk), axis=-1))
    a_shift, fox_shift = bound(a_q_gain, a_k_gain), bound(b_q_gain, b_k_gain)
    y = x.reshape(s, d)
    h = _rmsnorm(y, norm_g, 0)
    for layer in range(w_in.shape[0]):
        y, h = _layer(y, h, layer, norm_g, w_t, w_c, aux_ab, aux_c, wo_ab, wo_c, fox_bias[layer],
                      a_shift[layer], fox_shift[layer], gla_gate_up[layer], gla_gate_bias[layer],
                      gla_out_gain[layer])
    return y.reshape(bsz, s, d)
```

```python
import functools

import numpy as np
import jax
import jax.numpy as jnp
from jax import lax
from jax.experimental import pallas as pl
from jax.experimental.pallas import tpu as pltpu

F32 = jnp.float32
MXU_DTYPE = jnp.bfloat16

LANES = 128
MXU_COLS = 256
HEAD_DIM = 64
A_HEADS, B_HEADS, C_HEADS = 12, 8, 4
A_CHUNKS, B_CHUNKS = A_HEADS // 2, B_HEADS // 2
C_DK, C_DV = 96, 192
C_DK_PAD, C_DV_PAD = 128, 256
GATE_RANK = 16
GATE_TEMP = 16.0
RMS_EPS = 1e-6
LOG2E = 1.4426950408889634
DILATIONS = (1, 4, 16)
BAND = 128
A_TILE = 2048
GLA_CHUNK = 64
GLA_LEVELS = (32, 16, 8, 4, 2, 1)
VMEM_LIMIT = 56 * 1024 * 1024

AQ, AK, AV, AZ = 0, 6, 12, 18
BQ, BK, BV, BZ = 24, 28, 32, 36
CQ, CK, CV, CZ = 0, 4, 8, 16
A_WIDTH, B_WIDTH = A_HEADS * HEAD_DIM, B_HEADS * HEAD_DIM

_NT = (((1,), (1,)), ((), ()))
_TN = (((0,), (0,)), ((), ()))


def _params(sem):
    return pltpu.CompilerParams(dimension_semantics=sem, vmem_limit_bytes=VMEM_LIMIT)


def _log_sigmoid(x):
    return jnp.minimum(x, 0.0) - jnp.log1p(jnp.exp(-jnp.abs(x)))


AB_KINDS = (("headnorm+views",) * (2 * A_CHUNKS) + ("scale+views",) * A_CHUNKS
            + ("silu",) * A_CHUNKS
            + ("headnorm",) * (2 * B_CHUNKS) + ("scale",) * B_CHUNKS + ("silu",) * B_CHUNKS)[::2]
C_KINDS = (("scale",) * (4 * C_HEADS) + ("silu",) * (2 * C_HEADS))[::2] + ("gate",)


def _kind_ranges(kinds, kind):
    runs, start = [], None
    for t, k in enumerate(kinds + (None,)):
        if k == kind and start is None:
            start = t
        elif k != kind and start is not None:
            runs.append((start, t))
            start = None
    return runs


def _view_tiles(kinds):
    n = sum(k.endswith("+views") for k in kinds)
    assert all(k.endswith("+views") for k in kinds[:n])
    return n


def _rmsnorm_body(x_ref, g_ref, o_ref):
    x = x_ref[...]
    ms = jnp.mean(x * x, axis=-1, keepdims=True)
    o_ref[...] = (x * lax.rsqrt(ms + RMS_EPS) * g_ref[0]).astype(o_ref.dtype)


def _rmsnorm(x, norm_g, layer, tm=512):
    s, d = x.shape
    return pl.pallas_call(
        _rmsnorm_body,
        grid=(s // tm,),
        in_specs=[pl.BlockSpec((tm, d), lambda i: (i, 0)),
                  pl.BlockSpec((1, 1, d), lambda i: (layer, 0, 0))],
        out_specs=pl.BlockSpec((tm, d), lambda i: (i, 0)),
        out_shape=jax.ShapeDtypeStruct((s, d), MXU_DTYPE),
        compiler_params=_params(("parallel",)),
        name="rmsnorm",
    )(x, norm_g)


def _proj_body(h_ref, w_ref, aux_ref, *rest, kinds):
    view_tiles = _view_tiles(kinds)
    n_tiles = len(kinds)
    o_ref, rest = rest[0], rest[1:]
    gate_ref = None
    if kinds[-1] == "gate":
        gate_ref, rest = rest[0], rest[1:]
    if view_tiles:
        view_refs, rest = rest[:len(DILATIONS) - 1], rest[len(DILATIONS) - 1:]
        n_stage = MXU_COLS // LANES
        stage, rest = rest[-n_stage:], rest[:-n_stage]
    acc_scr = rest[0:2]
    j = pl.program_id(1)
    tm = h_ref.shape[0]

    def matmul(dst, rows):
        dst[rows, :] = lax.dot_general(h_ref[rows, :], w_ref[0].astype(MXU_DTYPE), _NT,
                                       preferred_element_type=F32)

    def epilogue(src, kind, rows):
        r0, nr = rows.start, rows.stop - rows.start
        if kind == "gate":
            gate_ref[rows, :] = src[rows, 0:LANES]
            return
        base = kind.split("+")[0]
        for c in range(MXU_COLS // LANES):
            cols = slice(c * LANES, (c + 1) * LANES)
            y = src[rows, cols]
            if base == "headnorm":
                first = lax.broadcasted_iota(jnp.int32, y.shape, 1) < HEAD_DIM
                y2 = y * y
                s0 = jnp.sum(jnp.where(first, y2, 0.0), axis=-1, keepdims=True)
                s1 = jnp.sum(jnp.where(first, 0.0, y2), axis=-1, keepdims=True)
                ms = jnp.where(first, s0, s1) * (1.0 / HEAD_DIM)
                y = y * lax.rsqrt(ms + RMS_EPS) * aux_ref[0, :, cols]
            elif base == "scale":
                y = y * aux_ref[0, :, cols]
            elif base == "silu":
                y = y * jax.nn.sigmoid(y)
            o_ref[c, rows, :] = y.astype(o_ref.dtype)
            if not kind.endswith("+views"):
                continue
            stage[c][rows, :] = y
            for view, dil in zip(view_refs, DILATIONS[1:]):
                for r in range(dil):
                    picked = stage[c][pl.ds(r0 + r, nr // dil, stride=dil), :]
                    view[c, r0 // dil:(r0 + nr) // dil, r * LANES:(r + 1) * LANES] = (
                        picked.astype(view.dtype))

    halves = tuple(slice(r, r + tm // 4) for r in range(0, tm, tm // 4))

    @pl.when(j == 0)
    def _():
        for rows in halves:
            matmul(acc_scr[0], rows)

    for parity in range(2):
        for kind in sorted(set(kinds[:-1])):
            in_kind = functools.reduce(
                jnp.logical_or, [(j - 1 >= lo) & (j - 1 < min(hi, n_tiles - 1))
                                 for lo, hi in _kind_ranges(kinds, kind) if lo < n_tiles - 1])

            @pl.when((j >= 1) & (j < n_tiles) & (j % 2 == parity) & in_kind)
            def _(parity=parity, kind=kind):
                for rows in halves:
                    epilogue(acc_scr[1 - parity], kind, rows)
                    matmul(acc_scr[parity], rows)

    @pl.when(j == n_tiles)
    def _():
        for rows in halves:
            epilogue(acc_scr[(n_tiles - 1) % 2], kinds[-1], rows)


def _in_proj(h, w_t, aux, layer, kinds, name, tm=2048):
    s, d = h.shape
    n_tiles = len(kinds)
    view_tiles = _view_tiles(kinds)
    chunks = MXU_COLS // LANES
    has_gate = kinds[-1] == "gate"
    out_tiles = n_tiles - 1 if has_gate else n_tiles
    prev = lambda j, n: jnp.clip(j - 1, 0, n - 1)
    out_shape = [jax.ShapeDtypeStruct((out_tiles * chunks, s, LANES), MXU_DTYPE)]
    out_specs = [pl.BlockSpec((chunks, tm, LANES), lambda i, j: (prev(j, out_tiles), i, 0))]
    scratch = [pltpu.VMEM((tm, MXU_COLS), F32), pltpu.VMEM((tm, MXU_COLS), F32)]
    if has_gate:
        out_shape.append(jax.ShapeDtypeStruct((s, LANES), F32))
        out_specs.append(pl.BlockSpec((tm, LANES), lambda i, j: (i, 0)))
    if view_tiles:
        for dil in DILATIONS[1:]:
            out_shape.append(
                jax.ShapeDtypeStruct((view_tiles * chunks, s // dil, dil * LANES), MXU_DTYPE))
            out_specs.append(pl.BlockSpec((chunks, tm // dil, dil * LANES),
                                          lambda i, j: (prev(j, view_tiles), i, 0)))
        scratch += [pltpu.VMEM((tm, LANES), F32)] * chunks
    return pl.pallas_call(
        functools.partial(_proj_body, kinds=kinds),
        grid=(s // tm, n_tiles + 1),
        in_specs=[pl.BlockSpec((tm, d), lambda i, j: (i, 0)),
                  pl.BlockSpec((1, MXU_COLS, d),
                               lambda i, j: (layer, jnp.minimum(j, n_tiles - 1), 0)),
                  pl.BlockSpec((1, 1, MXU_COLS), lambda i, j: (layer, 0, prev(j, n_tiles)))],
        out_specs=out_specs,
        out_shape=out_shape,
        scratch_shapes=scratch,
        compiler_params=_params(("parallel", "arbitrary")),
        name=name,
    )(h, w_t, aux)


def _fox_cumsum_body(x_ref, b_ref, o_ref):
    x = _log_sigmoid(x_ref[...] + b_ref[...])
    idx = lax.broadcasted_iota(jnp.int32, x.shape, 1)
    shift = 1
    while shift < x.shape[1]:
        x = x + jnp.where(idx >= shift, pltpu.roll(x, shift, axis=1), 0.0)
        shift *= 2
    o_ref[...] = x * LOG2E


def _fox_cumsum(logit_t, bias):
    nh, s = logit_t.shape
    return pl.pallas_call(
        _fox_cumsum_body,
        out_shape=jax.ShapeDtypeStruct((nh, s), F32),
        compiler_params=pltpu.CompilerParams(vmem_limit_bytes=VMEM_LIMIT),
        name="fox_cumsum",
    )(logit_t, bias.reshape(nh, 1))


A_GROUP = 16


A_DEN_MIN, A_DEN_MAX = 2.0 ** -90, 2.0 ** 100


def _band_attention(blocks, shift):
    first = lax.broadcasted_iota(jnp.int32, (BAND, LANES), 1) < HEAD_DIM
    first_kv = lax.broadcasted_iota(jnp.int32, (2 * BAND, LANES), 1) < HEAD_DIM
    chains = [(b, h) for b in range(len(blocks)) for h in range(2)]
    logits = []
    for b, h in chains:
        q, k2 = blocks[b][0], blocks[b][1]
        zero = jnp.zeros_like(q)
        qm = jnp.where(first, q, zero) if h == 0 else jnp.where(first, zero, q)
        logits.append(lax.dot_general(qm, k2, _NT, preferred_element_type=F32))
    logits = [s + blocks[b][3] for (b, h), s in zip(chains, logits)]
    if shift is None:
        maxes = [jnp.max(s, axis=-1, keepdims=True) for s in logits]
        probs = [jnp.exp2(s - m).astype(MXU_DTYPE) for s, m in zip(logits, maxes)]
    else:
        maxes = [shift] * len(chains)
        probs = [jnp.exp2(s).astype(MXU_DTYPE) for s in logits]
    res = []
    for (b, h), p in zip(chains, probs):
        v2 = blocks[b][2]
        one = jnp.ones_like(v2)
        vh = jnp.where(first_kv, v2, one) if h == 0 else jnp.where(first_kv, one, v2)
        res.append(jnp.dot(p, vh, preferred_element_type=F32))
    results = []
    for b in range(len(blocks)):
        r0, r1 = res[2 * b], res[2 * b + 1]
        num = jnp.where(first, r0, r1)
        den = pltpu.roll(jnp.where(first, r1, r0), HEAD_DIM, axis=1)
        m = maxes[2 * b] if shift is not None else jnp.where(first, maxes[2 * b], maxes[2 * b + 1])
        results.append((num / den, m + jnp.log2(den), den))
    return results


def _dilated_body(*refs):
    ins, z_ref, u_ref, o_ref, scr = refs[:15], refs[15], refs[16], refs[17], refs[18:]
    kv_scr, o_scr, l_scr, den_scr = scr[:6], scr[6], scr[7], scr[8]
    n = pl.program_id(1)
    for bi, d in enumerate(DILATIONS):
        kc, kp, vc, vp = ins[5 * bi + 1:5 * bi + 5]
        kf, vf = kv_scr[2 * bi:2 * bi + 2]
        rows = A_TILE // d
        for cur, prev, full in ((kc, kp, kf), (vc, vp, vf)):
            full[0:BAND, :] = prev[0, rows - BAND:rows, :]
            full[BAND:BAND + rows, :] = cur[0]

    row = lax.broadcasted_iota(jnp.int32, (BAND, 2 * BAND), 0)
    col = lax.broadcasted_iota(jnp.int32, (BAND, 2 * BAND), 1)
    band = (col >= row) & (col <= row + BAND)
    shift = u_ref[0:1, 0:1]

    def attend(shifted):
        offset = shift if shifted else 0.0
        bias_band = jnp.where(band, 0.0, -jnp.inf) - offset
        bias_first = jnp.where(band & (col >= BAND), 0.0, -jnp.inf) - offset
        for bi, d in enumerate(DILATIONS):
            q_ref = ins[5 * bi]
            kf, vf = kv_scr[2 * bi:2 * bi + 2]
            nb = A_TILE // d // BAND

            def group(members, q_ref=q_ref, kf=kf, vf=vf, d=d, nb=nb, bi=bi):
                blocks = []
                for j, r, maybe_first in members:
                    cols = slice(r * LANES, (r + 1) * LANES)
                    row0 = j * BAND if isinstance(j, int) else pl.multiple_of(j * BAND, BAND)
                    bias = bias_band
                    if maybe_first:
                        bias = jnp.where(n * nb + j >= 1, bias_band, bias_first)
                    blocks.append((q_ref[0, pl.ds(row0, BAND), cols],
                                   kf[pl.ds(row0, 2 * BAND), cols],
                                   vf[pl.ds(row0, 2 * BAND), cols], bias))
                results = _band_attention(blocks, shift if shifted else None)
                for (j, r, _), (o, lse, den) in zip(members, results):
                    dst = pl.ds(j * BAND * d + r, BAND, stride=d)
                    o_scr[bi, dst, :] = o
                    l_scr[bi, dst, :] = lse
                    if shifted:
                        den_scr[0] = jnp.minimum(den_scr[0], den)
                        den_scr[1] = jnp.maximum(den_scr[1], den)

            if nb >= A_GROUP:
                for r in range(d):
                    def body(g, carry, r=r, group=group):
                        j0 = pl.multiple_of(g * A_GROUP, A_GROUP)
                        group([(j0 + i, r, i == 0) for i in range(A_GROUP)])
                        return carry
                    lax.fori_loop(0, nb // A_GROUP, body, 0)
            else:
                for r0 in range(0, d, A_GROUP // nb):
                    group([(j, r0 + i, j == 0) for i in range(A_GROUP // nb) for j in range(nb)])

    den_scr[0] = jnp.full((BAND, LANES), jnp.inf, F32)
    den_scr[1] = jnp.zeros((BAND, LANES), F32)
    attend(True)
    in_range = (jnp.min(den_scr[0]) >= A_DEN_MIN) & (jnp.max(den_scr[1]) <= A_DEN_MAX)
    pl.when(jnp.logical_not(in_range))(functools.partial(attend, False))
    lse = l_scr[...]
    w = jnp.exp2(lse - jnp.max(lse, axis=0, keepdims=True))
    mixed = jnp.sum(w * o_scr[...], axis=0) / jnp.sum(w, axis=0)
    o_ref[...] = (mixed * z_ref[0].astype(F32)).astype(o_ref.dtype)


def _dilated_mixture(proj, views, shift):
    s = proj.shape[1]
    operands, in_specs, scratch = [], [], []
    for d, view in zip(DILATIONS, (proj,) + tuple(views)):
        rows, width = A_TILE // d, d * LANES
        blk = (1, rows, width)
        operands += [view] * 5
        in_specs += [
            pl.BlockSpec(blk, lambda c, n: (AQ + c, n, 0)),
            pl.BlockSpec(blk, lambda c, n: (AK + c, n, 0)),
            pl.BlockSpec(blk, lambda c, n: (AK + c, jnp.maximum(n - 1, 0), 0)),
            pl.BlockSpec(blk, lambda c, n: (AV + c, n, 0)),
            pl.BlockSpec(blk, lambda c, n: (AV + c, jnp.maximum(n - 1, 0), 0)),
        ]
        scratch += [pltpu.VMEM((rows + BAND, width), MXU_DTYPE)] * 2
    scratch += [pltpu.VMEM((len(DILATIONS), A_TILE, LANES), F32)] * 2
    scratch.append(pltpu.VMEM((2, BAND, LANES), F32))
    operands += [proj, jnp.full((8, LANES), shift, F32)]
    in_specs += [pl.BlockSpec((1, A_TILE, LANES), lambda c, n: (AZ + c, n, 0)),
                 pl.BlockSpec((8, LANES), lambda c, n: (0, 0))]
    return pl.pallas_call(
        _dilated_body,
        grid=(A_CHUNKS, s // A_TILE),
        in_specs=in_specs,
        out_specs=pl.BlockSpec((A_TILE, LANES), lambda c, n: (n, c)),
        out_shape=jax.ShapeDtypeStruct((s, A_WIDTH), MXU_DTYPE),
        scratch_shapes=scratch,
        compiler_params=_params(("parallel", "arbitrary")),
        name="dilated_mixture",
    )(*operands)


FOX_AUG = 3
FOX_VROWS = HEAD_DIM + 16
FOX_DEN_MIN, FOX_DEN_MAX = 2.0 ** -90, 2.0 ** 100
FOX_ZERO_EXP = 160.0


def _split3(c):
    hi = c.astype(jnp.bfloat16).astype(F32)
    r = c - hi
    mid = r.astype(jnp.bfloat16).astype(F32)
    return hi, mid, r - mid


def _fox_prep_body(q_ref, k_ref, v_ref, ccol_ref, crow_ref, qt_ref, ka_ref, vt_ref):
    tm = q_ref.shape[1]
    q_t = q_ref[0].astype(F32).T
    v_t = v_ref[0].astype(F32).T
    k = k_ref[0]
    sub = lax.broadcasted_iota(jnp.int32, (HEAD_DIM, tm), 0)
    lane = lax.broadcasted_iota(jnp.int32, (tm, LANES), 1)
    v_tail = (lax.broadcasted_iota(jnp.int32, (FOX_VROWS - HEAD_DIM, tm), 0) == 0).astype(F32)
    for h in range(2):
        hi, mid, lo = _split3(crow_ref[0, h:h + 1, :])
        aug_q = jnp.where(sub == 0, hi, jnp.where(sub == 1, mid, jnp.where(
            sub == 2, lo, jnp.where(sub < 2 * FOX_AUG, 1.0, 0.0))))
        q_h = q_t[h * HEAD_DIM:(h + 1) * HEAD_DIM]
        parts = [q_h, aug_q] if h == 0 else [aug_q, q_h]
        qt_ref[h] = jnp.concatenate(parts, axis=0).astype(qt_ref.dtype)
        hi, mid, lo = _split3(ccol_ref[0, :, h:h + 1])
        a0 = (1 - h) * HEAD_DIM
        aug_k = jnp.where(lane == a0 + FOX_AUG, -hi, jnp.where(lane == a0 + FOX_AUG + 1, -mid, jnp.where(
            lane == a0 + FOX_AUG + 2, -lo, jnp.where((lane >= a0) & (lane < a0 + FOX_AUG), 1.0, 0.0))))
        own = (lane < HEAD_DIM) if h == 0 else (lane >= HEAD_DIM)
        ka_ref[h] = jnp.where(own, k, aug_k.astype(k.dtype))
        v_h = jnp.concatenate([v_t[h * HEAD_DIM:(h + 1) * HEAD_DIM], v_tail],
                              axis=0).astype(vt_ref.dtype)
        blk = vt_ref.shape[3]
        for b in range(tm // blk):
            vt_ref[h, b] = v_h[:, b * blk:(b + 1) * blk]


def _fox_prep(proj, c, shift, blk, tm=2048):
    s = proj.shape[1]
    c_col = c.reshape(B_CHUNKS, 2, s).transpose(0, 2, 1)
    c_row = c.reshape(B_CHUNKS, 2, s) - shift
    return pl.pallas_call(
        _fox_prep_body,
        grid=(B_CHUNKS, s // tm),
        in_specs=[
            pl.BlockSpec((1, tm, LANES), lambda p, i: (BQ + p, i, 0)),
            pl.BlockSpec((1, tm, LANES), lambda p, i: (BK + p, i, 0)),
            pl.BlockSpec((1, tm, LANES), lambda p, i: (BV + p, i, 0)),
            pl.BlockSpec((1, tm, 2), lambda p, i: (p, i, 0)),
            pl.BlockSpec((1, 2, tm), lambda p, i: (p, 0, i)),
        ],
        out_specs=[
            pl.BlockSpec((2, LANES, tm), lambda p, i: (p, 0, i)),
            pl.BlockSpec((2, tm, LANES), lambda p, i: (p, i, 0)),
            pl.BlockSpec((2, tm // blk, FOX_VROWS, blk), lambda p, i: (p, i, 0, 0)),
        ],
        out_shape=[
            jax.ShapeDtypeStruct((B_HEADS, LANES, s), MXU_DTYPE),
            jax.ShapeDtypeStruct((B_HEADS, s, LANES), MXU_DTYPE),
            jax.ShapeDtypeStruct((B_HEADS, s // blk, FOX_VROWS, blk), MXU_DTYPE),
        ],
        compiler_params=_params(("parallel", "parallel")),
        name="fox_prep",
    )(proj, proj, proj, c_col, c_row)


def _fox_body(lo_ref, qt_ref, ka_ref, vt_ref, z_ref, o_ref, s_scr, m_scr, acc_scr, *, blk):
    qi = pl.program_id(1)
    qblk = qt_ref.shape[2]
    ratio = qblk // blk
    key = lax.broadcasted_iota(jnp.int32, (blk, qblk), 0)
    query = lax.broadcasted_iota(jnp.int32, (blk, qblk), 1)
    causal = [key + r * blk <= query for r in range(ratio)]
    first_own = qi * ratio
    q_t = (qt_ref[0], qt_ref[1])

    def logits(kb, h):
        k0 = pl.multiple_of(kb * blk, blk)
        s_scr[h] = jnp.dot(ka_ref[h, pl.ds(k0, blk), :], q_t[h], preferred_element_type=F32)

    def accumulate_shifted(kb, h, mask=None):
        s = s_scr[h]
        if mask is not None:
            s = jnp.where(mask, s, -jnp.inf)
        acc_scr[h] += jnp.dot(vt_ref[h, kb], jnp.exp2(s).astype(MXU_DTYPE),
                              preferred_element_type=F32)

    def accumulate_online(kb, h, mask=None):
        s = s_scr[h]
        if mask is not None:
            s = jnp.where(mask, s, -jnp.inf)
        m = m_scr[h]
        m_new = jnp.maximum(m, jnp.max(s, axis=0, keepdims=True))
        p = jnp.exp2(s - m_new).astype(MXU_DTYPE)
        acc_scr[h] = (jnp.exp2(m - m_new) * acc_scr[h]
                      + jnp.dot(vt_ref[h, kb], p, preferred_element_type=F32))
        m_scr[h] = m_new

    def attend(accumulate, lo):
        acc_scr[...] = jnp.zeros(acc_scr.shape, F32)
        logits(lo, 0)

        def block(kb, carry):
            logits(kb, 1)
            accumulate(kb, 0)
            logits(kb + 1, 0)
            accumulate(kb, 1)
            return carry

        lax.fori_loop(lo, first_own, block, 0)
        for r in range(ratio):
            logits(first_own + r, 1)
            accumulate(first_own + r, 0, causal[r])
            if r + 1 < ratio:
                logits(first_own + r + 1, 0)
            accumulate(first_own + r, 1, causal[r])

        outs = [acc_scr[h, 0:HEAD_DIM] / acc_scr[h, HEAD_DIM:HEAD_DIM + 1] for h in range(2)]
        o = jnp.concatenate(outs, axis=0).T
        o_ref[...] = (o * z_ref[0].astype(F32)).astype(o_ref.dtype)

    attend(accumulate_shifted, lo_ref[pl.program_id(0), qi])
    den = jnp.concatenate([acc_scr[h, HEAD_DIM:HEAD_DIM + 1] for h in range(2)], axis=0)
    in_range = (jnp.min(den) >= FOX_DEN_MIN) & (jnp.max(den) <= FOX_DEN_MAX)

    @pl.when(jnp.logical_not(in_range))
    def _():
        m_scr[...] = jnp.full(m_scr.shape, -jnp.inf, F32)
        attend(accumulate_online, 0)


def _fox_first_blocks(c, shift, blk, qblk):
    heads, s = c.shape
    nblk, ntile = s // blk, s // qblk
    c_first = c[:, ::qblk]
    c_last = c[:, blk - 1::blk]
    dead = (c_first[:, :, None] - c_last[:, None, :]) <= -(FOX_ZERO_EXP + 0.02 * shift)
    block = jnp.arange(nblk)
    first_own = jnp.arange(ntile) * (qblk // blk)
    dead &= block[None, None, :] < first_own[None, :, None]
    lo = jnp.min(jnp.where(dead, nblk, block), axis=-1)
    return jnp.min(lo.reshape(heads // 2, 2, ntile), axis=1).astype(jnp.int32)


def _forgetting_attention(proj, c, shift, blk=512, qblk=1024):
    s = proj.shape[1]
    qt, ka, vt = _fox_prep(proj, c, shift, blk)
    lo = _fox_first_blocks(c, shift, blk, qblk)
    grid_spec = pltpu.PrefetchScalarGridSpec(
        num_scalar_prefetch=1,
        grid=(B_CHUNKS, s // qblk),
        in_specs=[
            pl.BlockSpec((2, LANES, qblk), lambda p, i, lo: (p, 0, i)),
            pl.BlockSpec((2, s, LANES), lambda p, i, lo: (p, 0, 0)),
            pl.BlockSpec((2, s // blk, FOX_VROWS, blk), lambda p, i, lo: (p, 0, 0, 0)),
            pl.BlockSpec((1, qblk, LANES), lambda p, i, lo: (BZ + p, i, 0)),
        ],
        out_specs=pl.BlockSpec((qblk, LANES), lambda p, i, lo: (i, p)),
        scratch_shapes=[pltpu.VMEM((2, blk, qblk), F32), pltpu.VMEM((2, 1, qblk), F32),
                        pltpu.VMEM((2, FOX_VROWS, qblk), F32)],
    )
    return pl.pallas_call(
        functools.partial(_fox_body, blk=blk),
        grid_spec=grid_spec,
        out_shape=jax.ShapeDtypeStruct((s, B_WIDTH), MXU_DTYPE),
        compiler_params=_params(("parallel", "arbitrary")),
        name="forgetting_attention",
    )(lo, qt, ka, vt, proj)


GLA_GROUP = 4


def _gla_level_matrix():
    c = GLA_CHUNK
    mats = []
    for b in GLA_LEVELS:
        m = np.zeros((c, c), np.float32)
        for i in range(c):
            pivot = (i // (2 * b)) * 2 * b + b - 1
            if i > pivot:
                m[i, pivot + 1:i + 1] = 1.0
            else:
                m[i, i + 1:pivot + 1] = 1.0
        mats.append(m)
    mats.append(np.tril(np.ones((c, c), np.float32)))
    return np.concatenate(mats, axis=0)


def _gla_body(q_ref, k_ref, v_ref, g_ref, z_ref, gu_ref, gb_ref, og_ref, lvl_ref, o_ref,
              st_scr, *, tile):
    c = GLA_CHUNK
    nlev = len(GLA_LEVELS)
    nch = tile // c
    grp = GLA_GROUP * c

    @pl.when(pl.program_id(1) == 0)
    def _():
        st_scr[...] = jnp.zeros_like(st_scr)

    gate_in = g_ref[:, 0:GATE_RANK]
    gate_hi = gate_in.astype(MXU_DTYPE)
    gate_lo = (gate_in - gate_hi.astype(F32)).astype(MXU_DTYPE)
    gate_w = gu_ref[0].astype(MXU_DTYPE)
    logit = (jnp.dot(gate_hi, gate_w, preferred_element_type=F32)
             + jnp.dot(gate_lo, gate_w, preferred_element_type=F32) + gb_ref[0])
    la = _log_sigmoid(logit) * (1.0 / GATE_TEMP)

    la_cat = jnp.concatenate([la[ci * c:(ci + 1) * c] for ci in range(nch)], axis=1)
    la_hi = la_cat.astype(MXU_DTYPE)
    la_lo = (la_cat - la_hi.astype(F32)).astype(MXU_DTYPE)
    f = jnp.exp(jnp.dot(lvl_ref[0:nlev * c, :], la_hi, preferred_element_type=F32))
    tri = lvl_ref[nlev * c:(nlev + 1) * c, :]
    bc = (jnp.dot(tri, la_hi, preferred_element_type=F32)
          + jnp.dot(tri, la_lo, preferred_element_type=F32))

    row = lax.broadcasted_iota(jnp.int32, (grp, grp), 0)
    col = lax.broadcasted_iota(jnp.int32, (grp, grp), 1)
    diff = (row ^ col) & (c - 1)
    level = jnp.full((grp, grp), nlev, jnp.int32)
    for li, b in reversed(list(enumerate(GLA_LEVELS))):
        level = jnp.where(diff >= b, li, level)
    level = jnp.where(((row & -c) == (col & -c)) & (col <= row), level, -1)
    out_gain = og_ref[...]

    state = st_scr[...]
    for g0 in range(0, nch, GLA_GROUP):
        rows = slice(g0 * c, g0 * c + grp)
        q = q_ref[0, rows, :].astype(F32)
        k = k_ref[0, rows, :].astype(F32)
        v = jnp.concatenate([v_ref[0, rows, :], v_ref[1, rows, :]], axis=-1)
        z = jnp.concatenate([z_ref[0, rows, :], z_ref[1, rows, :]], axis=-1)
        attn = jnp.where(level == nlev, jnp.sum(q * k, axis=-1, keepdims=True), 0.0)
        for li in range(nlev):
            fl = jnp.concatenate([f[li * c:(li + 1) * c, ch * LANES:(ch + 1) * LANES]
                                  for ch in range(g0, g0 + GLA_GROUP)], axis=0)
            a = lax.dot_general((q * fl).astype(MXU_DTYPE), (k * fl).astype(MXU_DTYPE), _NT,
                                preferred_element_type=F32)
            attn = jnp.where(level == li, a, attn)
        o_intra = jnp.dot(attn.astype(MXU_DTYPE), v, preferred_element_type=F32)

        chunk_rows = [slice(ci * c, (ci + 1) * c) for ci in range(GLA_GROUP)]
        bcs = [bc[:, ch * LANES:(ch + 1) * LANES] for ch in range(g0, g0 + GLA_GROUP)]
        q_decs, decays, updates = [], [], []
        for r, bc_c in zip(chunk_rows, bcs):
            b_last = bc_c[c - 1:c, :]
            q_decs.append((q[r] * jnp.exp(bc_c)).astype(MXU_DTYPE))
            k_dec = (k[r] * jnp.exp(b_last - bc_c)).astype(MXU_DTYPE)
            decay = jnp.broadcast_to(jnp.exp(b_last), (LANES, LANES)).T
            decays.append(jnp.concatenate([decay, decay], axis=1))
            updates.append(lax.dot_general(k_dec, v[r], _TN, preferred_element_type=F32))
        states = []
        for decay, update in zip(decays, updates):
            states.append(state.astype(MXU_DTYPE))
            state = state * decay + update
        for ci, (r, q_dec, before) in enumerate(zip(chunk_rows, q_decs, states)):
            o = o_intra[r] + jnp.dot(q_dec, before, preferred_element_type=F32)
            ms = jnp.sum(o * o, axis=-1, keepdims=True) * (1.0 / C_DV)
            y = o * lax.rsqrt(ms + RMS_EPS) * out_gain * z[r].astype(F32)
            o_ref[pl.ds((g0 + ci) * c, c), :] = y.astype(o_ref.dtype)
    st_scr[...] = state


def _gated_linear_attention(proj, gate, gate_up, gate_bias, out_gain, tile=1024):
    s = proj.shape[1]
    lvl = jnp.asarray(_gla_level_matrix(), MXU_DTYPE)
    return pl.pallas_call(
        functools.partial(_gla_body, tile=tile),
        grid=(C_HEADS, s // tile),
        in_specs=[
            pl.BlockSpec((1, tile, LANES), lambda h, t: (CQ + h, t, 0)),
            pl.BlockSpec((1, tile, LANES), lambda h, t: (CK + h, t, 0)),
            pl.BlockSpec((2, tile, LANES), lambda h, t: (CV // 2 + h, t, 0)),
            pl.BlockSpec((tile, LANES), lambda h, t: (t, 0)),
            pl.BlockSpec((2, tile, LANES), lambda h, t: (CZ // 2 + h, t, 0)),
            pl.BlockSpec((1, GATE_RANK, C_DK_PAD), lambda h, t: (h, 0, 0)),
            pl.BlockSpec((1, 1, C_DK_PAD), lambda h, t: (h, 0, 0)),
            pl.BlockSpec((1, C_DV_PAD), lambda h, t: (0, 0)),
            pl.BlockSpec(lvl.shape, lambda h, t: (0, 0)),
        ],
        out_specs=pl.BlockSpec((tile, C_DV_PAD), lambda h, t: (t, h)),
        out_shape=jax.ShapeDtypeStruct((s, C_HEADS * C_DV_PAD), MXU_DTYPE),
        scratch_shapes=[pltpu.VMEM((C_DK_PAD, C_DV_PAD), F32)],
        compiler_params=_params(("parallel", "arbitrary")),
        name="gated_linear_attention",
    )(proj, proj, proj, gate, proj, gate_up, gate_bias, out_gain, lvl)


OUT_COLS = 512


def _out_body(x_ref, a_ref, b_ref, c_ref, wab_ref, wc_ref, g_ref, o_ref, h_ref):
    d = o_ref.shape[1]
    ssq = jnp.zeros((o_ref.shape[0], 1), F32)
    for c0 in range(0, d, OUT_COLS):
        cols = slice(c0, c0 + OUT_COLS)
        acc = jnp.dot(a_ref[...], wab_ref[0, 0:A_WIDTH, cols], preferred_element_type=F32)
        acc = acc + jnp.dot(b_ref[...], wab_ref[0, A_WIDTH:A_WIDTH + B_WIDTH, cols],
                            preferred_element_type=F32)
        acc = acc + jnp.dot(c_ref[...], wc_ref[0, :, cols], preferred_element_type=F32)
        y = x_ref[:, cols] + acc
        o_ref[:, cols] = y
        ssq = ssq + jnp.sum(y * y, axis=-1, keepdims=True)
    scale = lax.rsqrt(ssq * (1.0 / d) + RMS_EPS)
    h_ref[...] = (o_ref[...] * scale * g_ref[0]).astype(h_ref.dtype)


def _out_proj(x, ma, mb, mc, wo_ab, wo_c, norm_g, layer, tm=512):
    s, d = x.shape
    layers = norm_g.shape[0]
    row_blk = lambda m: pl.BlockSpec((tm, m.shape[1]), lambda i: (i, 0))
    return pl.pallas_call(
        _out_body,
        grid=(s // tm,),
        in_specs=[row_blk(x), row_blk(ma), row_blk(mb), row_blk(mc),
                  pl.BlockSpec((1,) + wo_ab.shape[1:], lambda i: (layer, 0, 0)),
                  pl.BlockSpec((1,) + wo_c.shape[1:], lambda i: (layer, 0, 0)),
                  pl.BlockSpec((1, 1, d), lambda i: (min(layer + 1, layers - 1), 0, 0))],
        out_specs=[row_blk(x), row_blk(x)],
        out_shape=[jax.ShapeDtypeStruct((s, d), F32), jax.ShapeDtypeStruct((s, d), MXU_DTYPE)],
        compiler_params=_params(("parallel",)),
        name="out_proj",
    )(x, ma, mb, mc, wo_ab, wo_c, norm_g)


def _pad_heads(w, heads, width, padded):
    lead = w.shape[:-1]
    w = w.reshape(*lead, heads, width)
    w = jnp.pad(w, [(0, 0)] * len(lead) + [(0, 0), (0, padded - width)])
    return w.reshape(*lead, heads * padded)


def _prepare(w_in, w_out, a_q_gain, a_k_gain, b_q_gain, b_k_gain):
    layers, d, _ = w_in.shape
    w_t = jnp.swapaxes(w_in, 1, 2)
    ck, cv = C_HEADS * C_DK, C_HEADS * C_DV
    ab = 4 * (A_WIDTH + B_WIDTH)
    bounds = np.cumsum([ab, B_HEADS, ck, ck, cv, cv, GATE_RANK])
    bf, cq, ckk, cvv, cz, cr = [w_t[:, lo:hi] for lo, hi in zip(bounds[:-1], bounds[1:])]

    def pad_rows(w, width, padded):
        w = w.reshape(layers, C_HEADS, width, d)
        w = jnp.pad(w, ((0, 0), (0, 0), (0, padded - width), (0, 0)))
        return w.reshape(layers, C_HEADS * padded, d)

    pad_k = lambda w: pad_rows(w, C_DK, C_DK_PAD)
    pad_v = lambda w: pad_rows(w, C_DV, C_DV_PAD)
    tail = jnp.zeros((layers, MXU_COLS - GATE_RANK - B_HEADS, d), F32)
    w_c_t = jnp.concatenate([pad_k(cq), pad_k(ckk), pad_v(cvv), pad_v(cz), cr, bf, tail], axis=1)

    q_scale = HEAD_DIM ** -0.5 * LOG2E
    tile_heads = lambda g, n: jnp.tile(g, (1, n))
    ones = lambda n: jnp.ones((layers, n), F32)
    aux_ab = jnp.concatenate([
        tile_heads(a_q_gain * q_scale, A_HEADS), tile_heads(a_k_gain, A_HEADS), ones(2 * A_WIDTH),
        tile_heads(b_q_gain * q_scale, B_HEADS), tile_heads(b_k_gain, B_HEADS), ones(2 * B_WIDTH)],
        axis=-1)[:, None, :]
    aux_c = jnp.concatenate([
        jnp.full((layers, C_HEADS * C_DK_PAD), C_DK ** -0.5, F32),
        ones(C_HEADS * (C_DK_PAD + 2 * C_DV_PAD) + MXU_COLS)], axis=-1)[:, None, :]

    wo_c = w_out[:, A_WIDTH + B_WIDTH:].reshape(layers, C_HEADS, C_DV, d)
    wo_c = jnp.pad(wo_c, ((0, 0), (0, 0), (0, C_DV_PAD - C_DV), (0, 0)))
    wo_c = wo_c.reshape(layers, C_HEADS * C_DV_PAD, d).astype(MXU_DTYPE)
    wo_ab = w_out[:, :A_WIDTH + B_WIDTH].astype(MXU_DTYPE)
    return w_t, w_c_t, aux_ab, aux_c, wo_ab, wo_c


def _layer(x, h, layer, norm_g, w_t, w_c, aux_ab, aux_c, wo_ab, wo_c, fox_bias, a_shift,
           fox_shift, gla_gate_up, gla_gate_bias, gla_out_gain):
    proj_ab, v4, v16 = _in_proj(h, w_t, aux_ab, layer, AB_KINDS, "in_proj_ab")
    proj_c, gate = _in_proj(h, w_c, aux_c, layer, C_KINDS, "in_proj_c")
    c = _fox_cumsum(gate[:, GATE_RANK:GATE_RANK + B_HEADS].T, fox_bias)
    mixed_a = _dilated_mixture(proj_ab, (v4, v16), a_shift)
    mixed_b = _forgetting_attention(proj_ab, c, fox_shift)
    gate_up = _pad_heads(gla_gate_up, C_HEADS, C_DK, C_DK_PAD)
    gate_up = gate_up.reshape(GATE_RANK, C_HEADS, C_DK_PAD).transpose(1, 0, 2)
    gate_bias = _pad_heads(gla_gate_bias, C_HEADS, C_DK, C_DK_PAD).reshape(C_HEADS, 1, C_DK_PAD)
    out_gain = jnp.pad(gla_out_gain, (0, C_DV_PAD - C_DV)).reshape(1, C_DV_PAD)
    mixed_c = _gated_linear_attention(proj_c, gate, gate_up, gate_bias, out_gain)
    return _out_proj(x, mixed_a, mixed_b, mixed_c, wo_ab, wo_c, norm_g, layer)


@jax.jit
def kernel(x, norm_g, w_in, a_q_gain, a_k_gain, b_q_gain, b_k_gain, fox_bias, gla_gate_up,
           gla_gate_bias, gla_out_gain, w_out):
    bsz, s, d = x.shape
    assert bsz == 1, "batch size 1 only"
    w_t, w_c, aux_ab, aux_c, wo_ab, wo_c = _prepare(w_in, w_out, a_q_gain, a_k_gain, b_q_gain,
                                                    b_k_gain)
    norm_g = norm_g[:, None, :]
    bound = lambda gq, gk: (HEAD_DIM ** 0.5 * LOG2E * jnp.max(jnp.abs(gq), axis=-1)
                            * jnp.max(jnp.abs(gk), axis=-1))
    a_shift, fox_shift = bound(a_q_gain, a_k_gain), bound(b_q_gain, b_k_gain)
    y = x.reshape(s, d)
    h = _rmsnorm(y, norm_g, 0)
    for layer in range(w_in.shape[0]):
        y, h = _layer(y, h, layer, norm_g, w_t, w_c, aux_ab, aux_c, wo_ab, wo_c, fox_bias[layer],
                      a_shift[layer], fox_shift[layer], gla_gate_up[layer], gla_gate_bias[layer],
                      gla_out_gain[layer])
    return y.reshape(bsz, s, d)
```

```python
import functools

import numpy as np
import jax
import jax.numpy as jnp
from jax import lax
from jax.experimental import pallas as pl
from jax.experimental.pallas import tpu as pltpu

F32 = jnp.float32
MXU_DTYPE = jnp.bfloat16

LANES = 128
MXU_COLS = 256
HEAD_DIM = 64
A_HEADS, B_HEADS, C_HEADS = 12, 8, 4
A_CHUNKS, B_CHUNKS = A_HEADS // 2, B_HEADS // 2
C_DK, C_DV = 96, 192
C_DK_PAD, C_DV_PAD = 128, 256
GATE_RANK = 16
GATE_TEMP = 16.0
RMS_EPS = 1e-6
LOG2E = 1.4426950408889634
DILATIONS = (1, 4, 16)
BAND = 128
A_TILE = 2048
GLA_CHUNK = 64
GLA_LEVELS = (32, 16, 8, 4, 2, 1)
VMEM_LIMIT = 56 * 1024 * 1024

AQ, AK, AV, AZ = 0, 6, 12, 18
BQ, BK, BV, BZ = 24, 28, 32, 36
CQ, CK, CV, CZ = 0, 4, 8, 16
A_WIDTH, B_WIDTH = A_HEADS * HEAD_DIM, B_HEADS * HEAD_DIM

_NT = (((1,), (1,)), ((), ()))
_TN = (((0,), (0,)), ((), ()))


def _params(sem):
    return pltpu.CompilerParams(dimension_semantics=sem, vmem_limit_bytes=VMEM_LIMIT)


def _log_sigmoid(x):
    return jnp.minimum(x, 0.0) - jnp.log1p(jnp.exp(-jnp.abs(x)))


AB_KINDS = (("headnorm+views",) * (2 * A_CHUNKS) + ("scale+views",) * A_CHUNKS
            + ("silu",) * A_CHUNKS
            + ("headnorm",) * (2 * B_CHUNKS) + ("scale",) * B_CHUNKS + ("silu",) * B_CHUNKS)[::2]
C_KINDS = (("scale",) * (4 * C_HEADS) + ("silu",) * (2 * C_HEADS))[::2] + ("gate",)


def _kind_ranges(kinds, kind):
    runs, start = [], None
    for t, k in enumerate(kinds + (None,)):
        if k == kind and start is None:
            start = t
        elif k != kind and start is not None:
            runs.append((start, t))
            start = None
    return runs


def _view_tiles(kinds):
    n = sum(k.endswith("+views") for k in kinds)
    assert all(k.endswith("+views") for k in kinds[:n])
    return n


def _rmsnorm_body(x_ref, g_ref, o_ref):
    x = x_ref[...]
    ms = jnp.mean(x * x, axis=-1, keepdims=True)
    o_ref[...] = (x * lax.rsqrt(ms + RMS_EPS) * g_ref[0]).astype(o_ref.dtype)


def _rmsnorm(x, norm_g, layer, tm=512):
    s, d = x.shape
    return pl.pallas_call(
        _rmsnorm_body,
        grid=(s // tm,),
        in_specs=[pl.BlockSpec((tm, d), lambda i: (i, 0)),
                  pl.BlockSpec((1, 1, d), lambda i: (layer, 0, 0))],
        out_specs=pl.BlockSpec((tm, d), lambda i: (i, 0)),
        out_shape=jax.ShapeDtypeStruct((s, d), MXU_DTYPE),
        compiler_params=_params(("parallel",)),
        name="rmsnorm",
    )(x, norm_g)


def _proj_body(h_ref, w_ref, aux_ref, *rest, kinds):
    view_tiles = _view_tiles(kinds)
    n_tiles = len(kinds)
    o_ref, rest = rest[0], rest[1:]
    gate_ref = None
    if kinds[-1] == "gate":
        gate_ref, rest = rest[0], rest[1:]
    if view_tiles:
        view_refs, rest = rest[:len(DILATIONS) - 1], rest[len(DILATIONS) - 1:]
        n_stage = MXU_COLS // LANES
        stage, rest = rest[-n_stage:], rest[:-n_stage]
    acc_scr = rest[0:2]
    j = pl.program_id(1)
    tm = h_ref.shape[0]

    def matmul(dst, rows):
        dst[rows, :] = lax.dot_general(h_ref[rows, :], w_ref[0].astype(MXU_DTYPE), _NT,
                                       preferred_element_type=F32)

    def epilogue(src, kind, rows):
        r0, nr = rows.start, rows.stop - rows.start
        if kind == "gate":
            gate_ref[rows, :] = src[rows, 0:LANES]
            return
        base = kind.split("+")[0]
        for c in range(MXU_COLS // LANES):
            cols = slice(c * LANES, (c + 1) * LANES)
            y = src[rows, cols]
            if base == "headnorm":
                first = lax.broadcasted_iota(jnp.int32, y.shape, 1) < HEAD_DIM
                y2 = y * y
                s0 = jnp.sum(jnp.where(first, y2, 0.0), axis=-1, keepdims=True)
                s1 = jnp.sum(jnp.where(first, 0.0, y2), axis=-1, keepdims=True)
                ms = jnp.where(first, s0, s1) * (1.0 / HEAD_DIM)
                y = y * lax.rsqrt(ms + RMS_EPS) * aux_ref[0, :, cols]
            elif base == "scale":
                y = y * aux_ref[0, :, cols]
            elif base == "silu":
                y = y * jax.nn.sigmoid(y)
            o_ref[c, rows, :] = y.astype(o_ref.dtype)
            if not kind.endswith("+views"):
                continue
            stage[c][rows, :] = y
            for view, dil in zip(view_refs, DILATIONS[1:]):
                for r in range(dil):
                    picked = stage[c][pl.ds(r0 + r, nr // dil, stride=dil), :]
                    view[c, r0 // dil:(r0 + nr) // dil, r * LANES:(r + 1) * LANES] = (
                        picked.astype(view.dtype))

    quarters = tuple(slice(r, r + tm // 4) for r in range(0, tm, tm // 4))

    @pl.when(j == 0)
    def _():
        for rows in quarters:
            matmul(acc_scr[0], rows)

    for parity in range(2):
        for kind in sorted(set(kinds[:-1])):
            in_kind = functools.reduce(
                jnp.logical_or, [(j - 1 >= lo) & (j - 1 < min(hi, n_tiles - 1))
                                 for lo, hi in _kind_ranges(kinds, kind) if lo < n_tiles - 1])

            @pl.when((j >= 1) & (j < n_tiles) & (j % 2 == parity) & in_kind)
            def _(parity=parity, kind=kind):
                for rows in quarters:
                    epilogue(acc_scr[1 - parity], kind, rows)
                    matmul(acc_scr[parity], rows)

    @pl.when(j == n_tiles)
    def _():
        for rows in quarters:
            epilogue(acc_scr[(n_tiles - 1) % 2], kinds[-1], rows)


def _in_proj(h, w_t, aux, layer, kinds, name, tm=2048):
    s, d = h.shape
    n_tiles = len(kinds)
    view_tiles = _view_tiles(kinds)
    chunks = MXU_COLS // LANES
    has_gate = kinds[-1] == "gate"
    out_tiles = n_tiles - 1 if has_gate else n_tiles
    prev = lambda j, n: jnp.clip(j - 1, 0, n - 1)
    out_shape = [jax.ShapeDtypeStruct((out_tiles * chunks, s, LANES), MXU_DTYPE)]
    out_specs = [pl.BlockSpec((chunks, tm, LANES), lambda i, j: (prev(j, out_tiles), i, 0))]
    scratch = [pltpu.VMEM((tm, MXU_COLS), F32), pltpu.VMEM((tm, MXU_COLS), F32)]
    if has_gate:
        out_shape.append(jax.ShapeDtypeStruct((s, LANES), F32))
        out_specs.append(pl.BlockSpec((tm, LANES), lambda i, j: (i, 0)))
    if view_tiles:
        for dil in DILATIONS[1:]:
            out_shape.append(
                jax.ShapeDtypeStruct((view_tiles * chunks, s // dil, dil * LANES), MXU_DTYPE))
            out_specs.append(pl.BlockSpec((chunks, tm // dil, dil * LANES),
                                          lambda i, j: (prev(j, view_tiles), i, 0)))
        scratch += [pltpu.VMEM((tm, LANES), F32)] * chunks
    return pl.pallas_call(
        functools.partial(_proj_body, kinds=kinds),
        grid=(s // tm, n_tiles + 1),
        in_specs=[pl.BlockSpec((tm, d), lambda i, j: (i, 0)),
                  pl.BlockSpec((1, MXU_COLS, d),
                               lambda i, j: (layer, jnp.minimum(j, n_tiles - 1), 0)),
                  pl.BlockSpec((1, 1, MXU_COLS), lambda i, j: (layer, 0, prev(j, n_tiles)))],
        out_specs=out_specs,
        out_shape=out_shape,
        scratch_shapes=scratch,
        compiler_params=_params(("parallel", "arbitrary")),
        name=name,
    )(h, w_t, aux)


def _fox_cumsum_body(x_ref, b_ref, o_ref):
    x = _log_sigmoid(x_ref[...] + b_ref[...])
    idx = lax.broadcasted_iota(jnp.int32, x.shape, 1)
    shift = 1
    while shift < x.shape[1]:
        x = x + jnp.where(idx >= shift, pltpu.roll(x, shift, axis=1), 0.0)
        shift *= 2
    o_ref[...] = x * LOG2E


def _fox_cumsum(logit_t, bias):
    nh, s = logit_t.shape
    return pl.pallas_call(
        _fox_cumsum_body,
        out_shape=jax.ShapeDtypeStruct((nh, s), F32),
        compiler_params=pltpu.CompilerParams(vmem_limit_bytes=VMEM_LIMIT),
        name="fox_cumsum",
    )(logit_t, bias.reshape(nh, 1))


A_GROUP = 16


A_DEN_MIN, A_DEN_MAX = 2.0 ** -90, 2.0 ** 100


def _band_attention(blocks, shift):
    first = lax.broadcasted_iota(jnp.int32, (BAND, LANES), 1) < HEAD_DIM
    first_kv = lax.broadcasted_iota(jnp.int32, (2 * BAND, LANES), 1) < HEAD_DIM
    chains = [(b, h) for b in range(len(blocks)) for h in range(2)]
    logits = []
    for b, h in chains:
        q, k2 = blocks[b][0], blocks[b][1]
        zero = jnp.zeros_like(q)
        qm = jnp.where(first, q, zero) if h == 0 else jnp.where(first, zero, q)
        logits.append(lax.dot_general(qm, k2, _NT, preferred_element_type=F32))
    logits = [s + blocks[b][3] for (b, h), s in zip(chains, logits)]
    if shift is None:
        maxes = [jnp.max(s, axis=-1, keepdims=True) for s in logits]
        probs = [jnp.exp2(s - m).astype(MXU_DTYPE) for s, m in zip(logits, maxes)]
    else:
        maxes = [shift] * len(chains)
        probs = [jnp.exp2(s).astype(MXU_DTYPE) for s in logits]
    res = []
    for (b, h), p in zip(chains, probs):
        v2 = blocks[b][2]
        one = jnp.ones_like(v2)
        vh = jnp.where(first_kv, v2, one) if h == 0 else jnp.where(first_kv, one, v2)
        res.append(jnp.dot(p, vh, preferred_element_type=F32))
    results = []
    for b in range(len(blocks)):
        r0, r1 = res[2 * b], res[2 * b + 1]
        num = jnp.where(first, r0, r1)
        den = pltpu.roll(jnp.where(first, r1, r0), HEAD_DIM, axis=1)
        m = maxes[2 * b] if shift is not None else jnp.where(first, maxes[2 * b], maxes[2 * b + 1])
        results.append((num / den, m + jnp.log2(den), den))
    return results


def _dilated_body(*refs):
    ins, z_ref, u_ref, o_ref, scr = refs[:15], refs[15], refs[16], refs[17], refs[18:]
    kv_scr, o_scr, l_scr, den_scr = scr[:6], scr[6], scr[7], scr[8]
    n = pl.program_id(1)
    for bi, d in enumerate(DILATIONS):
        kc, kp, vc, vp = ins[5 * bi + 1:5 * bi + 5]
        kf, vf = kv_scr[2 * bi:2 * bi + 2]
        rows = A_TILE // d
        for cur, prev, full in ((kc, kp, kf), (vc, vp, vf)):
            full[0:BAND, :] = prev[0, rows - BAND:rows, :]
            full[BAND:BAND + rows, :] = cur[0]

    row = lax.broadcasted_iota(jnp.int32, (BAND, 2 * BAND), 0)
    col = lax.broadcasted_iota(jnp.int32, (BAND, 2 * BAND), 1)
    band = (col >= row) & (col <= row + BAND)
    shift = u_ref[0:1, 0:1]

    def attend(shifted):
        offset = shift if shifted else 0.0
        bias_band = jnp.where(band, 0.0, -jnp.inf) - offset
        bias_first = jnp.where(band & (col >= BAND), 0.0, -jnp.inf) - offset
        for bi, d in enumerate(DILATIONS):
            q_ref = ins[5 * bi]
            kf, vf = kv_scr[2 * bi:2 * bi + 2]
            nb = A_TILE // d // BAND

            def group(members, q_ref=q_ref, kf=kf, vf=vf, d=d, nb=nb, bi=bi):
                blocks = []
                for j, r, maybe_first in members:
                    cols = slice(r * LANES, (r + 1) * LANES)
                    row0 = j * BAND if isinstance(j, int) else pl.multiple_of(j * BAND, BAND)
                    bias = bias_band
                    if maybe_first:
                        bias = jnp.where(n * nb + j >= 1, bias_band, bias_first)
                    blocks.append((q_ref[0, pl.ds(row0, BAND), cols],
                                   kf[pl.ds(row0, 2 * BAND), cols],
                                   vf[pl.ds(row0, 2 * BAND), cols], bias))
                results = _band_attention(blocks, shift if shifted else None)
                for (j, r, _), (o, lse, den) in zip(members, results):
                    dst = pl.ds(j * BAND * d + r, BAND, stride=d)
                    o_scr[bi, dst, :] = o
                    l_scr[bi, dst, :] = lse
                    if shifted:
                        den_scr[0] = jnp.minimum(den_scr[0], den)
                        den_scr[1] = jnp.maximum(den_scr[1], den)

            if nb >= A_GROUP:
                for r in range(d):
                    def body(g, carry, r=r, group=group):
                        j0 = pl.multiple_of(g * A_GROUP, A_GROUP)
                        group([(j0 + i, r, i == 0) for i in range(A_GROUP)])
                        return carry
                    lax.fori_loop(0, nb // A_GROUP, body, 0)
            else:
                for r0 in range(0, d, A_GROUP // nb):
                    group([(j, r0 + i, j == 0) for i in range(A_GROUP // nb) for j in range(nb)])

    den_scr[0] = jnp.full((BAND, LANES), jnp.inf, F32)
    den_scr[1] = jnp.zeros((BAND, LANES), F32)
    attend(True)
    in_range = (jnp.min(den_scr[0]) >= A_DEN_MIN) & (jnp.max(den_scr[1]) <= A_DEN_MAX)
    pl.when(jnp.logical_not(in_range))(functools.partial(attend, False))
    lse = l_scr[...]
    w = jnp.exp2(lse - jnp.max(lse, axis=0, keepdims=True))
    mixed = jnp.sum(w * o_scr[...], axis=0) / jnp.sum(w, axis=0)
    o_ref[...] = (mixed * z_ref[0].astype(F32)).astype(o_ref.dtype)


def _dilated_mixture(proj, views, shift):
    s = proj.shape[1]
    operands, in_specs, scratch = [], [], []
    for d, view in zip(DILATIONS, (proj,) + tuple(views)):
        rows, width = A_TILE // d, d * LANES
        blk = (1, rows, width)
        operands += [view] * 5
        in_specs += [
            pl.BlockSpec(blk, lambda c, n: (AQ + c, n, 0)),
            pl.BlockSpec(blk, lambda c, n: (AK + c, n, 0)),
            pl.BlockSpec(blk, lambda c, n: (AK + c, jnp.maximum(n - 1, 0), 0)),
            pl.BlockSpec(blk, lambda c, n: (AV + c, n, 0)),
            pl.BlockSpec(blk, lambda c, n: (AV + c, jnp.maximum(n - 1, 0), 0)),
        ]
        scratch += [pltpu.VMEM((rows + BAND, width), MXU_DTYPE)] * 2
    scratch += [pltpu.VMEM((len(DILATIONS), A_TILE, LANES), F32)] * 2
    scratch.append(pltpu.VMEM((2, BAND, LANES), F32))
    operands += [proj, jnp.full((8, LANES), shift, F32)]
    in_specs += [pl.BlockSpec((1, A_TILE, LANES), lambda c, n: (AZ + c, n, 0)),
                 pl.BlockSpec((8, LANES), lambda c, n: (0, 0))]
    return pl.pallas_call(
        _dilated_body,
        grid=(A_CHUNKS, s // A_TILE),
        in_specs=in_specs,
        out_specs=pl.BlockSpec((A_TILE, LANES), lambda c, n: (n, c)),
        out_shape=jax.ShapeDtypeStruct((s, A_WIDTH), MXU_DTYPE),
        scratch_shapes=scratch,
        compiler_params=_params(("parallel", "arbitrary")),
        name="dilated_mixture",
    )(*operands)


FOX_AUG = 3
FOX_VROWS = HEAD_DIM + 16
FOX_DEN_MIN, FOX_DEN_MAX = 2.0 ** -90, 2.0 ** 100
FOX_ZERO_EXP = 160.0


def _split3(c):
    hi = c.astype(jnp.bfloat16).astype(F32)
    r = c - hi
    mid = r.astype(jnp.bfloat16).astype(F32)
    return hi, mid, r - mid


def _fox_prep_body(q_ref, k_ref, v_ref, ccol_ref, crow_ref, qt_ref, ka_ref, vt_ref):
    tm = q_ref.shape[1]
    q_t = q_ref[0].astype(F32).T
    v_t = v_ref[0].astype(F32).T
    k = k_ref[0]
    sub = lax.broadcasted_iota(jnp.int32, (HEAD_DIM, tm), 0)
    lane = lax.broadcasted_iota(jnp.int32, (tm, LANES), 1)
    v_tail = (lax.broadcasted_iota(jnp.int32, (FOX_VROWS - HEAD_DIM, tm), 0) == 0).astype(F32)
    for h in range(2):
        hi, mid, lo = _split3(crow_ref[0, h:h + 1, :])
        aug_q = jnp.where(sub == 0, hi, jnp.where(sub == 1, mid, jnp.where(
            sub == 2, lo, jnp.where(sub < 2 * FOX_AUG, 1.0, 0.0))))
        q_h = q_t[h * HEAD_DIM:(h + 1) * HEAD_DIM]
        parts = [q_h, aug_q] if h == 0 else [aug_q, q_h]
        qt_ref[h] = jnp.concatenate(parts, axis=0).astype(qt_ref.dtype)
        hi, mid, lo = _split3(ccol_ref[0, :, h:h + 1])
        a0 = (1 - h) * HEAD_DIM
        aug_k = jnp.where(lane == a0 + FOX_AUG, -hi, jnp.where(lane == a0 + FOX_AUG + 1, -mid, jnp.where(
            lane == a0 + FOX_AUG + 2, -lo, jnp.where((lane >= a0) & (lane < a0 + FOX_AUG), 1.0, 0.0))))
        own = (lane < HEAD_DIM) if h == 0 else (lane >= HEAD_DIM)
        ka_ref[h] = jnp.where(own, k, aug_k.astype(k.dtype))
        v_h = jnp.concatenate([v_t[h * HEAD_DIM:(h + 1) * HEAD_DIM], v_tail],
                              axis=0).astype(vt_ref.dtype)
        blk = vt_ref.shape[3]
        for b in range(tm // blk):
            vt_ref[h, b] = v_h[:, b * blk:(b + 1) * blk]


def _fox_prep(proj, c, shift, blk, tm=2048):
    s = proj.shape[1]
    c_col = c.reshape(B_CHUNKS, 2, s).transpose(0, 2, 1)
    c_row = c.reshape(B_CHUNKS, 2, s) - shift
    return pl.pallas_call(
        _fox_prep_body,
        grid=(B_CHUNKS, s // tm),
        in_specs=[
            pl.BlockSpec((1, tm, LANES), lambda p, i: (BQ + p, i, 0)),
            pl.BlockSpec((1, tm, LANES), lambda p, i: (BK + p, i, 0)),
            pl.BlockSpec((1, tm, LANES), lambda p, i: (BV + p, i, 0)),
            pl.BlockSpec((1, tm, 2), lambda p, i: (p, i, 0)),
            pl.BlockSpec((1, 2, tm), lambda p, i: (p, 0, i)),
        ],
        out_specs=[
            pl.BlockSpec((2, LANES, tm), lambda p, i: (p, 0, i)),
            pl.BlockSpec((2, tm, LANES), lambda p, i: (p, i, 0)),
            pl.BlockSpec((2, tm // blk, FOX_VROWS, blk), lambda p, i: (p, i, 0, 0)),
        ],
        out_shape=[
            jax.ShapeDtypeStruct((B_HEADS, LANES, s), MXU_DTYPE),
            jax.ShapeDtypeStruct((B_HEADS, s, LANES), MXU_DTYPE),
            jax.ShapeDtypeStruct((B_HEADS, s // blk, FOX_VROWS, blk), MXU_DTYPE),
        ],
        compiler_params=_params(("parallel", "parallel")),
        name="fox_prep",
    )(proj, proj, proj, c_col, c_row)


def _fox_body(lo_ref, qt_ref, ka_ref, vt_ref, z_ref, o_ref, s_scr, m_scr, acc_scr, *, blk):
    qi = pl.program_id(1)
    qblk = qt_ref.shape[2]
    ratio = qblk // blk
    key = lax.broadcasted_iota(jnp.int32, (blk, qblk), 0)
    query = lax.broadcasted_iota(jnp.int32, (blk, qblk), 1)
    causal = [key + r * blk <= query for r in range(ratio)]
    first_own = qi * ratio
    q_t = (qt_ref[0], qt_ref[1])

    def logits(kb, h):
        k0 = pl.multiple_of(kb * blk, blk)
        s_scr[h] = jnp.dot(ka_ref[h, pl.ds(k0, blk), :], q_t[h], preferred_element_type=F32)

    def accumulate_shifted(kb, h, mask=None):
        s = s_scr[h]
        if mask is not None:
            s = jnp.where(mask, s, -jnp.inf)
        acc_scr[h] += jnp.dot(vt_ref[h, kb], jnp.exp2(s).astype(MXU_DTYPE),
                              preferred_element_type=F32)

    def accumulate_online(kb, h, mask=None):
        s = s_scr[h]
        if mask is not None:
            s = jnp.where(mask, s, -jnp.inf)
        m = m_scr[h]
        m_new = jnp.maximum(m, jnp.max(s, axis=0, keepdims=True))
        p = jnp.exp2(s - m_new).astype(MXU_DTYPE)
        acc_scr[h] = (jnp.exp2(m - m_new) * acc_scr[h]
                      + jnp.dot(vt_ref[h, kb], p, preferred_element_type=F32))
        m_scr[h] = m_new

    def attend(accumulate, lo):
        acc_scr[...] = jnp.zeros(acc_scr.shape, F32)
        logits(lo, 0)

        def block(kb, carry):
            logits(kb, 1)
            accumulate(kb, 0)
            logits(kb + 1, 0)
            accumulate(kb, 1)
            return carry

        lax.fori_loop(lo, first_own, block, 0)
        for r in range(ratio):
            logits(first_own + r, 1)
            accumulate(first_own + r, 0, causal[r])
            if r + 1 < ratio:
                logits(first_own + r + 1, 0)
            accumulate(first_own + r, 1, causal[r])

        outs = [acc_scr[h, 0:HEAD_DIM] / acc_scr[h, HEAD_DIM:HEAD_DIM + 1] for h in range(2)]
        o = jnp.concatenate(outs, axis=0).T
        o_ref[...] = (o * z_ref[0].astype(F32)).astype(o_ref.dtype)

    attend(accumulate_shifted, lo_ref[pl.program_id(0), qi])
    den = jnp.concatenate([acc_scr[h, HEAD_DIM:HEAD_DIM + 1] for h in range(2)], axis=0)
    in_range = (jnp.min(den) >= FOX_DEN_MIN) & (jnp.max(den) <= FOX_DEN_MAX)

    @pl.when(jnp.logical_not(in_range))
    def _():
        m_scr[...] = jnp.full(m_scr.shape, -jnp.inf, F32)
        attend(accumulate_online, 0)


def _fox_first_blocks(c, shift, blk, qblk):
    heads, s = c.shape
    nblk, ntile = s // blk, s // qblk
    c_first = c[:, ::qblk]
    c_last = c[:, blk - 1::blk]
    dead = (c_first[:, :, None] - c_last[:, None, :]) <= -(FOX_ZERO_EXP + 0.02 * shift)
    block = jnp.arange(nblk)
    first_own = jnp.arange(ntile) * (qblk // blk)
    dead &= block[None, None, :] < first_own[None, :, None]
    lo = jnp.min(jnp.where(dead, nblk, block), axis=-1)
    return jnp.min(lo.reshape(heads // 2, 2, ntile), axis=1).astype(jnp.int32)


def _forgetting_attention(proj, c, shift, blk=512, qblk=1024):
    s = proj.shape[1]
    qt, ka, vt = _fox_prep(proj, c, shift, blk)
    lo = _fox_first_blocks(c, shift, blk, qblk)
    grid_spec = pltpu.PrefetchScalarGridSpec(
        num_scalar_prefetch=1,
        grid=(B_CHUNKS, s // qblk),
        in_specs=[
            pl.BlockSpec((2, LANES, qblk), lambda p, i, lo: (p, 0, i)),
            pl.BlockSpec((2, s, LANES), lambda p, i, lo: (p, 0, 0)),
            pl.BlockSpec((2, s // blk, FOX_VROWS, blk), lambda p, i, lo: (p, 0, 0, 0)),
            pl.BlockSpec((1, qblk, LANES), lambda p, i, lo: (BZ + p, i, 0)),
        ],
        out_specs=pl.BlockSpec((qblk, LANES), lambda p, i, lo: (i, p)),
        scratch_shapes=[pltpu.VMEM((2, blk, qblk), F32), pltpu.VMEM((2, 1, qblk), F32),
                        pltpu.VMEM((2, FOX_VROWS, qblk), F32)],
    )
    return pl.pallas_call(
        functools.partial(_fox_body, blk=blk),
        grid_spec=grid_spec,
        out_shape=jax.ShapeDtypeStruct((s, B_WIDTH), MXU_DTYPE),
        compiler_params=_params(("parallel", "arbitrary")),
        name="forgetting_attention",
    )(lo, qt, ka, vt, proj)


GLA_GROUP = 4


def _gla_level_matrix():
    c = GLA_CHUNK
    mats = []
    for b in GLA_LEVELS:
        m = np.zeros((c, c), np.float32)
        for i in range(c):
            pivot = (i // (2 * b)) * 2 * b + b - 1
            if i > pivot:
                m[i, pivot + 1:i + 1] = 1.0
            else:
                m[i, i + 1:pivot + 1] = 1.0
        mats.append(m)
    mats.append(np.tril(np.ones((c, c), np.float32)))
    return np.concatenate(mats, axis=0)


def _gla_body(q_ref, k_ref, v_ref, g_ref, z_ref, gu_ref, gb_ref, og_ref, lvl_ref, o_ref,
              st_scr, *, tile):
    c = GLA_CHUNK
    nlev = len(GLA_LEVELS)
    nch = tile // c
    grp = GLA_GROUP * c

    @pl.when(pl.program_id(1) == 0)
    def _():
        st_scr[...] = jnp.zeros_like(st_scr)

    gate_in = g_ref[:, 0:GATE_RANK]
    gate_hi = gate_in.astype(MXU_DTYPE)
    gate_lo = (gate_in - gate_hi.astype(F32)).astype(MXU_DTYPE)
    gate_w = gu_ref[0].astype(MXU_DTYPE)
    logit = (jnp.dot(gate_hi, gate_w, preferred_element_type=F32)
             + jnp.dot(gate_lo, gate_w, preferred_element_type=F32) + gb_ref[0])
    la = _log_sigmoid(logit) * (1.0 / GATE_TEMP)

    la_cat = jnp.concatenate([la[ci * c:(ci + 1) * c] for ci in range(nch)], axis=1)
    la_hi = la_cat.astype(MXU_DTYPE)
    la_lo = (la_cat - la_hi.astype(F32)).astype(MXU_DTYPE)
    f = jnp.exp(jnp.dot(lvl_ref[0:nlev * c, :], la_hi, preferred_element_type=F32))
    tri = lvl_ref[nlev * c:(nlev + 1) * c, :]
    bc = (jnp.dot(tri, la_hi, preferred_element_type=F32)
          + jnp.dot(tri, la_lo, preferred_element_type=F32))

    row = lax.broadcasted_iota(jnp.int32, (grp, grp), 0)
    col = lax.broadcasted_iota(jnp.int32, (grp, grp), 1)
    diff = (row ^ col) & (c - 1)
    level = jnp.full((grp, grp), nlev, jnp.int32)
    for li, b in reversed(list(enumerate(GLA_LEVELS))):
        level = jnp.where(diff >= b, li, level)
    level = jnp.where(((row & -c) == (col & -c)) & (col <= row), level, -1)
    out_gain = og_ref[...]

    state = st_scr[...]
    for g0 in range(0, nch, GLA_GROUP):
        rows = slice(g0 * c, g0 * c + grp)
        q = q_ref[0, rows, :].astype(F32)
        k = k_ref[0, rows, :].astype(F32)
        v = jnp.concatenate([v_ref[0, rows, :], v_ref[1, rows, :]], axis=-1)
        z = jnp.concatenate([z_ref[0, rows, :], z_ref[1, rows, :]], axis=-1)
        attn = jnp.where(level == nlev, jnp.sum(q * k, axis=-1, keepdims=True), 0.0)
        for li in range(nlev):
            fl = jnp.concatenate([f[li * c:(li + 1) * c, ch * LANES:(ch + 1) * LANES]
                                  for ch in range(g0, g0 + GLA_GROUP)], axis=0)
            a = lax.dot_general((q * fl).astype(MXU_DTYPE), (k * fl).astype(MXU_DTYPE), _NT,
                                preferred_element_type=F32)
            attn = jnp.where(level == li, a, attn)
        o_intra = jnp.dot(attn.astype(MXU_DTYPE), v, preferred_element_type=F32)

        chunk_rows = [slice(ci * c, (ci + 1) * c) for ci in range(GLA_GROUP)]
        bcs = [bc[:, ch * LANES:(ch + 1) * LANES] for ch in range(g0, g0 + GLA_GROUP)]
        q_decs, decays, updates = [], [], []
        for r, bc_c in zip(chunk_rows, bcs):
            b_last = bc_c[c - 1:c, :]
            q_decs.append((q[r] * jnp.exp(bc_c)).astype(MXU_DTYPE))
            k_dec = (k[r] * jnp.exp(b_last - bc_c)).astype(MXU_DTYPE)
            decay = jnp.broadcast_to(jnp.exp(b_last), (LANES, LANES)).T
            decays.append(jnp.concatenate([decay, decay], axis=1))
            updates.append(lax.dot_general(k_dec, v[r], _TN, preferred_element_type=F32))
        states = []
        for decay, update in zip(decays, updates):
            states.append(state.astype(MXU_DTYPE))
            state = state * decay + update
        for ci, (r, q_dec, before) in enumerate(zip(chunk_rows, q_decs, states)):
            o = o_intra[r] + jnp.dot(q_dec, before, preferred_element_type=F32)
            ms = jnp.sum(o * o, axis=-1, keepdims=True) * (1.0 / C_DV)
            y = o * lax.rsqrt(ms + RMS_EPS) * out_gain * z[r].astype(F32)
            o_ref[pl.ds((g0 + ci) * c, c), :] = y.astype(o_ref.dtype)
    st_scr[...] = state


def _gated_linear_attention(proj, gate, gate_up, gate_bias, out_gain, tile=1024):
    s = proj.shape[1]
    lvl = jnp.asarray(_gla_level_matrix(), MXU_DTYPE)
    return pl.pallas_call(
        functools.partial(_gla_body, tile=tile),
        grid=(C_HEADS, s // tile),
        in_specs=[
            pl.BlockSpec((1, tile, LANES), lambda h, t: (CQ + h, t, 0)),
            pl.BlockSpec((1, tile, LANES), lambda h, t: (CK + h, t, 0)),
            pl.BlockSpec((2, tile, LANES), lambda h, t: (CV // 2 + h, t, 0)),
            pl.BlockSpec((tile, LANES), lambda h, t: (t, 0)),
            pl.BlockSpec((2, tile, LANES), lambda h, t: (CZ // 2 + h, t, 0)),
            pl.BlockSpec((1, GATE_RANK, C_DK_PAD), lambda h, t: (h, 0, 0)),
            pl.BlockSpec((1, 1, C_DK_PAD), lambda h, t: (h, 0, 0)),
            pl.BlockSpec((1, C_DV_PAD), lambda h, t: (0, 0)),
            pl.BlockSpec(lvl.shape, lambda h, t: (0, 0)),
        ],
        out_specs=pl.BlockSpec((tile, C_DV_PAD), lambda h, t: (t, h)),
        out_shape=jax.ShapeDtypeStruct((s, C_HEADS * C_DV_PAD), MXU_DTYPE),
        scratch_shapes=[pltpu.VMEM((C_DK_PAD, C_DV_PAD), F32)],
        compiler_params=_params(("parallel", "arbitrary")),
        name="gated_linear_attention",
    )(proj, proj, proj, gate, proj, gate_up, gate_bias, out_gain, lvl)


OUT_COLS = 512


def _out_body(x_ref, a_ref, b_ref, c_ref, wab_ref, wc_ref, g_ref, o_ref, h_ref):
    d = o_ref.shape[1]
    ssq = jnp.zeros((o_ref.shape[0], 1), F32)
    for c0 in range(0, d, OUT_COLS):
        cols = slice(c0, c0 + OUT_COLS)
        acc = jnp.dot(a_ref[...], wab_ref[0, 0:A_WIDTH, cols], preferred_element_type=F32)
        acc = acc + jnp.dot(b_ref[...], wab_ref[0, A_WIDTH:A_WIDTH + B_WIDTH, cols],
                            preferred_element_type=F32)
        acc = acc + jnp.dot(c_ref[...], wc_ref[0, :, cols], preferred_element_type=F32)
        y = x_ref[:, cols] + acc
        o_ref[:, cols] = y
        ssq = ssq + jnp.sum(y * y, axis=-1, keepdims=True)
    scale = lax.rsqrt(ssq * (1.0 / d) + RMS_EPS)
    h_ref[...] = (o_ref[...] * scale * g_ref[0]).astype(h_ref.dtype)


def _out_proj(x, ma, mb, mc, wo_ab, wo_c, norm_g, layer, tm=512):
    s, d = x.shape
    layers = norm_g.shape[0]
    row_blk = lambda m: pl.BlockSpec((tm, m.shape[1]), lambda i: (i, 0))
    return pl.pallas_call(
        _out_body,
        grid=(s // tm,),
        in_specs=[row_blk(x), row_blk(ma), row_blk(mb), row_blk(mc),
                  pl.BlockSpec((1, A_WIDTH + B_WIDTH, d), lambda i: (layer, 0, 0)),
                  pl.BlockSpec((1,) + wo_c.shape[1:], lambda i: (layer, 0, 0)),
                  pl.BlockSpec((1, 1, d), lambda i: (min(layer + 1, layers - 1), 0, 0))],
        out_specs=[row_blk(x), row_blk(x)],
        out_shape=[jax.ShapeDtypeStruct((s, d), F32), jax.ShapeDtypeStruct((s, d), MXU_DTYPE)],
        compiler_params=_params(("parallel",)),
        name="out_proj",
    )(x, ma, mb, mc, wo_ab, wo_c, norm_g)


def _pad_heads(w, heads, width, padded):
    lead = w.shape[:-1]
    w = w.reshape(*lead, heads, width)
    w = jnp.pad(w, [(0, 0)] * len(lead) + [(0, 0), (0, padded - width)])
    return w.reshape(*lead, heads * padded)


def _prepare(w_in, w_out, a_q_gain, a_k_gain, b_q_gain, b_k_gain):
    layers, d, _ = w_in.shape
    w_t = jnp.swapaxes(w_in, 1, 2)
    ck, cv = C_HEADS * C_DK, C_HEADS * C_DV
    ab = 4 * (A_WIDTH + B_WIDTH)
    bounds = np.cumsum([ab, B_HEADS, ck, ck, cv, cv, GATE_RANK])
    bf, cq, ckk, cvv, cz, cr = [w_t[:, lo:hi] for lo, hi in zip(bounds[:-1], bounds[1:])]

    def pad_rows(w, width, padded):
        w = w.reshape(layers, C_HEADS, width, d)
        w = jnp.pad(w, ((0, 0), (0, 0), (0, padded - width), (0, 0)))
        return w.reshape(layers, C_HEADS * padded, d)

    pad_k = lambda w: pad_rows(w, C_DK, C_DK_PAD)
    pad_v = lambda w: pad_rows(w, C_DV, C_DV_PAD)
    tail = jnp.zeros((layers, MXU_COLS - GATE_RANK - B_HEADS, d), F32)
    w_c_t = jnp.concatenate([pad_k(cq), pad_k(ckk), pad_v(cvv), pad_v(cz), cr, bf, tail], axis=1)

    q_scale = HEAD_DIM ** -0.5 * LOG2E
    tile_heads = lambda g, n: jnp.tile(g, (1, n))
    ones = lambda n: jnp.ones((layers, n), F32)
    aux_ab = jnp.concatenate([
        tile_heads(a_q_gain * q_scale, A_HEADS), tile_heads(a_k_gain, A_HEADS), ones(2 * A_WIDTH),
        tile_heads(b_q_gain * q_scale, B_HEADS), tile_heads(b_k_gain, B_HEADS), ones(2 * B_WIDTH)],
        axis=-1)[:, None, :]
    aux_c = jnp.concatenate([
        jnp.full((layers, C_HEADS * C_DK_PAD), C_DK ** -0.5, F32),
        ones(C_HEADS * (C_DK_PAD + 2 * C_DV_PAD) + MXU_COLS)], axis=-1)[:, None, :]

    wo = w_out.astype(MXU_DTYPE)
    wo_c = wo[:, A_WIDTH + B_WIDTH:].reshape(layers, C_HEADS, C_DV, d)
    wo_c = jnp.pad(wo_c, ((0, 0), (0, 0), (0, C_DV_PAD - C_DV), (0, 0)))
    wo_c = wo_c.reshape(layers, C_HEADS * C_DV_PAD, d)
    return w_t, w_c_t, aux_ab, aux_c, wo, wo_c


def _layer(x, h, layer, norm_g, w_t, w_c, aux_ab, aux_c, wo_ab, wo_c, fox_bias, a_shift,
           fox_shift, gla_gate_up, gla_gate_bias, gla_out_gain):
    proj_ab, v4, v16 = _in_proj(h, w_t, aux_ab, layer, AB_KINDS, "in_proj_ab")
    proj_c, gate = _in_proj(h, w_c, aux_c, layer, C_KINDS, "in_proj_c")
    c = _fox_cumsum(gate[:, GATE_RANK:GATE_RANK + B_HEADS].T, fox_bias)
    mixed_a = _dilated_mixture(proj_ab, (v4, v16), a_shift)
    mixed_b = _forgetting_attention(proj_ab, c, fox_shift)
    gate_up = _pad_heads(gla_gate_up, C_HEADS, C_DK, C_DK_PAD)
    gate_up = gate_up.reshape(GATE_RANK, C_HEADS, C_DK_PAD).transpose(1, 0, 2)
    gate_bias = _pad_heads(gla_gate_bias, C_HEADS, C_DK, C_DK_PAD).reshape(C_HEADS, 1, C_DK_PAD)
    out_gain = jnp.pad(gla_out_gain, (0, C_DV_PAD - C_DV)).reshape(1, C_DV_PAD)
    mixed_c = _gated_linear_attention(proj_c, gate, gate_up, gate_bias, out_gain)
    return _out_proj(x, mixed_a, mixed_b, mixed_c, wo_ab, wo_c, norm_g, layer)


@jax.jit
def kernel(x, norm_g, w_in, a_q_gain, a_k_gain, b_q_gain, b_k_gain, fox_bias, gla_gate_up,
           gla_gate_bias, gla_out_gain, w_out):
    bsz, s, d = x.shape
    assert bsz == 1, "batch size 1 only"
    w_t, w_c, aux_ab, aux_c, wo_ab, wo_c = _prepare(w_in, w_out, a_q_gain, a_k_gain, b_q_gain,
                                                    b_k_gain)
    norm_g = norm_g[:, None, :]
    bound = lambda gq, gk: (HEAD_DIM ** 0.5 * LOG2E * jnp.max(jnp.abs(gq), axis=-1)
                            * jnp.max(jnp.abs(gk), axis=-1))
    a_shift, fox_shift = bound(a_q_gain, a_k_gain), bound(b_q_gain, b_k_gain)
    y = x.reshape(s, d)
    h = _rmsnorm(y, norm_g, 0)
    for layer in range(w_in.shape[0]):
        y, h = _layer(y, h, layer, norm_g, w_t, w_c, aux_ab, aux_c, wo_ab, wo_c, fox_bias[layer],
                      a_shift[layer], fox_shift[layer], gla_gate_up[layer], gla_gate_bias[layer],
                      gla_out_gain[layer])
    return y.reshape(bsz, s, d)
```

```python
import functools

import numpy as np
import jax
import jax.numpy as jnp
from jax import lax
from jax.experimental import pallas as pl
from jax.experimental.pallas import tpu as pltpu

F32 = jnp.float32
MXU_DTYPE = jnp.bfloat16

LANES = 128
MXU_COLS = 256
HEAD_DIM = 64
A_HEADS, B_HEADS, C_HEADS = 12, 8, 4
A_CHUNKS, B_CHUNKS = A_HEADS // 2, B_HEADS // 2
C_DK, C_DV = 96, 192
C_DK_PAD, C_DV_PAD = 128, 256
GATE_RANK = 16
GATE_TEMP = 16.0
RMS_EPS = 1e-6
LOG2E = 1.4426950408889634
DILATIONS = (1, 4, 16)
BAND = 128
A_TILE = 2048
GLA_CHUNK = 64
GLA_LEVELS = (32, 16, 8, 4, 2, 1)
VMEM_LIMIT = 56 * 1024 * 1024

AQ, AK, AV, AZ = 0, 6, 12, 18
BQ, BK, BV, BZ = 24, 28, 32, 36
CQ, CK, CV, CZ = 0, 4, 8, 16
A_WIDTH, B_WIDTH = A_HEADS * HEAD_DIM, B_HEADS * HEAD_DIM

_NT = (((1,), (1,)), ((), ()))
_TN = (((0,), (0,)), ((), ()))


def _params(sem):
    return pltpu.CompilerParams(dimension_semantics=sem, vmem_limit_bytes=VMEM_LIMIT)


def _log_sigmoid(x):
    return jnp.minimum(x, 0.0) - jnp.log1p(jnp.exp(-jnp.abs(x)))


AB_KINDS = (("headnorm+views",) * (2 * A_CHUNKS) + ("scale+views",) * A_CHUNKS
            + ("silu",) * A_CHUNKS
            + ("headnorm",) * (2 * B_CHUNKS) + ("scale",) * B_CHUNKS + ("silu",) * B_CHUNKS)[::2]
C_KINDS = (("scale",) * (4 * C_HEADS) + ("silu",) * (2 * C_HEADS))[::2] + ("gate",)


def _kind_ranges(kinds, kind):
    runs, start = [], None
    for t, k in enumerate(kinds + (None,)):
        if k == kind and start is None:
            start = t
        elif k != kind and start is not None:
            runs.append((start, t))
            start = None
    return runs


def _view_tiles(kinds):
    n = sum(k.endswith("+views") for k in kinds)
    assert all(k.endswith("+views") for k in kinds[:n])
    return n


def _rmsnorm_body(x_ref, g_ref, o_ref):
    x = x_ref[...]
    ms = jnp.mean(x * x, axis=-1, keepdims=True)
    o_ref[...] = (x * lax.rsqrt(ms + RMS_EPS) * g_ref[0]).astype(o_ref.dtype)


def _rmsnorm(x, norm_g, layer, tm=512):
    s, d = x.shape
    return pl.pallas_call(
        _rmsnorm_body,
        grid=(s // tm,),
        in_specs=[pl.BlockSpec((tm, d), lambda i: (i, 0)),
                  pl.BlockSpec((1, 1, d), lambda i: (layer, 0, 0))],
        out_specs=pl.BlockSpec((tm, d), lambda i: (i, 0)),
        out_shape=jax.ShapeDtypeStruct((s, d), MXU_DTYPE),
        compiler_params=_params(("parallel",)),
        name="rmsnorm",
    )(x, norm_g)


def _proj_body(h_ref, w_ref, aux_ref, *rest, kinds):
    view_tiles = _view_tiles(kinds)
    n_tiles = len(kinds)
    o_ref, rest = rest[0], rest[1:]
    gate_ref = None
    if kinds[-1] == "gate":
        gate_ref, rest = rest[0], rest[1:]
    if view_tiles:
        view_refs, rest = rest[:len(DILATIONS) - 1], rest[len(DILATIONS) - 1:]
        n_stage = MXU_COLS // LANES
        stage, rest = rest[-n_stage:], rest[:-n_stage]
    acc_scr = rest[0:2]
    j = pl.program_id(1)
    tm = h_ref.shape[0]

    def matmul(dst, rows):
        dst[rows, :] = lax.dot_general(h_ref[rows, :], w_ref[0].astype(MXU_DTYPE), _NT,
                                       preferred_element_type=F32)

    def epilogue(src, kind, rows):
        r0, nr = rows.start, rows.stop - rows.start
        if kind == "gate":
            gate_ref[rows, :] = src[rows, 0:LANES]
            return
        base = kind.split("+")[0]
        for c in range(MXU_COLS // LANES):
            cols = slice(c * LANES, (c + 1) * LANES)
            y = src[rows, cols]
            if base == "headnorm":
                first = lax.broadcasted_iota(jnp.int32, y.shape, 1) < HEAD_DIM
                y2 = y * y
                s0 = jnp.sum(jnp.where(first, y2, 0.0), axis=-1, keepdims=True)
                s1 = jnp.sum(jnp.where(first, 0.0, y2), axis=-1, keepdims=True)
                ms = jnp.where(first, s0, s1) * (1.0 / HEAD_DIM)
                y = y * lax.rsqrt(ms + RMS_EPS) * aux_ref[0, :, cols]
            elif base == "scale":
                y = y * aux_ref[0, :, cols]
            elif base == "silu":
                y = y * jax.nn.sigmoid(y)
            o_ref[c, rows, :] = y.astype(o_ref.dtype)
            if not kind.endswith("+views"):
                continue
            stage[c][rows, :] = y
            for view, dil in zip(view_refs, DILATIONS[1:]):
                for r in range(dil):
                    picked = stage[c][pl.ds(r0 + r, nr // dil, stride=dil), :]
                    view[c, r0 // dil:(r0 + nr) // dil, r * LANES:(r + 1) * LANES] = (
                        picked.astype(view.dtype))

    quarters = tuple(slice(r, r + tm // 4) for r in range(0, tm, tm // 4))

    @pl.when(j == 0)
    def _():
        for rows in quarters:
            matmul(acc_scr[0], rows)

    for parity in range(2):
        for kind in sorted(set(kinds[:-1])):
            in_kind = functools.reduce(
                jnp.logical_or, [(j - 1 >= lo) & (j - 1 < min(hi, n_tiles - 1))
                                 for lo, hi in _kind_ranges(kinds, kind) if lo < n_tiles - 1])

            @pl.when((j >= 1) & (j < n_tiles) & (j % 2 == parity) & in_kind)
            def _(parity=parity, kind=kind):
                for rows in quarters:
                    epilogue(acc_scr[1 - parity], kind, rows)
                    matmul(acc_scr[parity], rows)

    @pl.when(j == n_tiles)
    def _():
        for rows in quarters:
            epilogue(acc_scr[(n_tiles - 1) % 2], kinds[-1], rows)


def _in_proj(h, w_t, aux, layer, kinds, name, tm=2048):
    s, d = h.shape
    n_tiles = len(kinds)
    view_tiles = _view_tiles(kinds)
    chunks = MXU_COLS // LANES
    has_gate = kinds[-1] == "gate"
    out_tiles = n_tiles - 1 if has_gate else n_tiles
    prev = lambda j, n: jnp.clip(j - 1, 0, n - 1)
    out_shape = [jax.ShapeDtypeStruct((out_tiles * chunks, s, LANES), MXU_DTYPE)]
    out_specs = [pl.BlockSpec((chunks, tm, LANES), lambda i, j: (prev(j, out_tiles), i, 0))]
    scratch = [pltpu.VMEM((tm, MXU_COLS), F32), pltpu.VMEM((tm, MXU_COLS), F32)]
    if has_gate:
        out_shape.append(jax.ShapeDtypeStruct((s, LANES), F32))
        out_specs.append(pl.BlockSpec((tm, LANES), lambda i, j: (i, 0)))
    if view_tiles:
        for dil in DILATIONS[1:]:
            out_shape.append(
                jax.ShapeDtypeStruct((view_tiles * chunks, s // dil, dil * LANES), MXU_DTYPE))
            out_specs.append(pl.BlockSpec((chunks, tm // dil, dil * LANES),
                                          lambda i, j: (prev(j, view_tiles), i, 0)))
        scratch += [pltpu.VMEM((tm, LANES), F32)] * chunks
    return pl.pallas_call(
        functools.partial(_proj_body, kinds=kinds),
        grid=(s // tm, n_tiles + 1),
        in_specs=[pl.BlockSpec((tm, d), lambda i, j: (i, 0)),
                  pl.BlockSpec((1, MXU_COLS, d),
                               lambda i, j: (layer, jnp.minimum(j, n_tiles - 1), 0)),
                  pl.BlockSpec((1, 1, MXU_COLS), lambda i, j: (layer, 0, prev(j, n_tiles)))],
        out_specs=out_specs,
        out_shape=out_shape,
        scratch_shapes=scratch,
        compiler_params=_params(("parallel", "arbitrary")),
        name=name,
    )(h, w_t, aux)


def _fox_cumsum_body(x_ref, b_ref, o_ref):
    x = _log_sigmoid(x_ref[...] + b_ref[...])
    idx = lax.broadcasted_iota(jnp.int32, x.shape, 1)
    shift = 1
    while shift < x.shape[1]:
        x = x + jnp.where(idx >= shift, pltpu.roll(x, shift, axis=1), 0.0)
        shift *= 2
    o_ref[...] = x * LOG2E


def _fox_cumsum(logit_t, bias):
    nh, s = logit_t.shape
    return pl.pallas_call(
        _fox_cumsum_body,
        out_shape=jax.ShapeDtypeStruct((nh, s), F32),
        compiler_params=pltpu.CompilerParams(vmem_limit_bytes=VMEM_LIMIT),
        name="fox_cumsum",
    )(logit_t, bias.reshape(nh, 1))


A_GROUP = 16


A_DEN_MIN, A_DEN_MAX = 2.0 ** -90, 2.0 ** 100


def _band_attention(blocks, shift):
    first = lax.broadcasted_iota(jnp.int32, (BAND, LANES), 1) < HEAD_DIM
    first_kv = lax.broadcasted_iota(jnp.int32, (2 * BAND, LANES), 1) < HEAD_DIM
    chains = [(b, h) for b in range(len(blocks)) for h in range(2)]
    logits = []
    for b, h in chains:
        q, k2 = blocks[b][0], blocks[b][1]
        zero = jnp.zeros_like(q)
        qm = jnp.where(first, q, zero) if h == 0 else jnp.where(first, zero, q)
        logits.append(lax.dot_general(qm, k2, _NT, preferred_element_type=F32))
    logits = [s + blocks[b][3] for (b, h), s in zip(chains, logits)]
    if shift is None:
        maxes = [jnp.max(s, axis=-1, keepdims=True) for s in logits]
        probs = [jnp.exp2(s - m).astype(MXU_DTYPE) for s, m in zip(logits, maxes)]
    else:
        maxes = [shift] * len(chains)
        probs = [jnp.exp2(s).astype(MXU_DTYPE) for s in logits]
    res = []
    for (b, h), p in zip(chains, probs):
        v2 = blocks[b][2]
        one = jnp.ones_like(v2)
        vh = jnp.where(first_kv, v2, one) if h == 0 else jnp.where(first_kv, one, v2)
        res.append(jnp.dot(p, vh, preferred_element_type=F32))
    results = []
    for b in range(len(blocks)):
        r0, r1 = res[2 * b], res[2 * b + 1]
        num = jnp.where(first, r0, r1)
        den = pltpu.roll(jnp.where(first, r1, r0), HEAD_DIM, axis=1)
        m = maxes[2 * b] if shift is not None else jnp.where(first, maxes[2 * b], maxes[2 * b + 1])
        results.append((num / den, m + jnp.log2(den), den))
    return results


def _dilated_body(*refs):
    ins, z_ref, u_ref, o_ref, scr = refs[:15], refs[15], refs[16], refs[17], refs[18:]
    kv_scr, o_scr, l_scr, den_scr = scr[:6], scr[6], scr[7], scr[8]
    n = pl.program_id(1)
    for bi, d in enumerate(DILATIONS):
        kc, kp, vc, vp = ins[5 * bi + 1:5 * bi + 5]
        kf, vf = kv_scr[2 * bi:2 * bi + 2]
        rows = A_TILE // d
        for cur, prev, full in ((kc, kp, kf), (vc, vp, vf)):
            full[0:BAND, :] = prev[0, rows - BAND:rows, :]
            full[BAND:BAND + rows, :] = cur[0]

    row = lax.broadcasted_iota(jnp.int32, (BAND, 2 * BAND), 0)
    col = lax.broadcasted_iota(jnp.int32, (BAND, 2 * BAND), 1)
    band = (col >= row) & (col <= row + BAND)
    shift = u_ref[0:1, 0:1]

    def attend(shifted):
        offset = shift if shifted else 0.0
        bias_band = jnp.where(band, 0.0, -jnp.inf) - offset
        bias_first = jnp.where(band & (col >= BAND), 0.0, -jnp.inf) - offset
        for bi, d in enumerate(DILATIONS):
            q_ref = ins[5 * bi]
            kf, vf = kv_scr[2 * bi:2 * bi + 2]
            nb = A_TILE // d // BAND

            def group(members, q_ref=q_ref, kf=kf, vf=vf, d=d, nb=nb, bi=bi):
                blocks = []
                for j, r, maybe_first in members:
                    cols = slice(r * LANES, (r + 1) * LANES)
                    row0 = j * BAND if isinstance(j, int) else pl.multiple_of(j * BAND, BAND)
                    bias = bias_band
                    if maybe_first:
                        bias = jnp.where(n * nb + j >= 1, bias_band, bias_first)
                    blocks.append((q_ref[0, pl.ds(row0, BAND), cols],
                                   kf[pl.ds(row0, 2 * BAND), cols],
                                   vf[pl.ds(row0, 2 * BAND), cols], bias))
                results = _band_attention(blocks, shift if shifted else None)
                for (j, r, _), (o, lse, den) in zip(members, results):
                    dst = pl.ds(j * BAND * d + r, BAND, stride=d)
                    o_scr[bi, dst, :] = o
                    l_scr[bi, dst, :] = lse
                    if shifted:
                        den_scr[0] = jnp.minimum(den_scr[0], den)
                        den_scr[1] = jnp.maximum(den_scr[1], den)

            if nb >= A_GROUP:
                for r in range(d):
                    def body(g, carry, r=r, group=group):
                        j0 = pl.multiple_of(g * A_GROUP, A_GROUP)
                        group([(j0 + i, r, i == 0) for i in range(A_GROUP)])
                        return carry
                    lax.fori_loop(0, nb // A_GROUP, body, 0)
            else:
                for r0 in range(0, d, A_GROUP // nb):
                    group([(j, r0 + i, j == 0) for i in range(A_GROUP // nb) for j in range(nb)])

    den_scr[0] = jnp.full((BAND, LANES), jnp.inf, F32)
    den_scr[1] = jnp.zeros((BAND, LANES), F32)
    attend(True)
    in_range = (jnp.min(den_scr[0]) >= A_DEN_MIN) & (jnp.max(den_scr[1]) <= A_DEN_MAX)
    pl.when(jnp.logical_not(in_range))(functools.partial(attend, False))
    lse = l_scr[...]
    w = jnp.exp2(lse - jnp.max(lse, axis=0, keepdims=True))
    mixed = jnp.sum(w * o_scr[...], axis=0) / jnp.sum(w, axis=0)
    o_ref[...] = (mixed * z_ref[0].astype(F32)).astype(o_ref.dtype)


def _dilated_mixture(proj, views, shift):
    s = proj.shape[1]
    operands, in_specs, scratch = [], [], []
    for d, view in zip(DILATIONS, (proj,) + tuple(views)):
        rows, width = A_TILE // d, d * LANES
        blk = (1, rows, width)
        operands += [view] * 5
        in_specs += [
            pl.BlockSpec(blk, lambda c, n: (AQ + c, n, 0)),
            pl.BlockSpec(blk, lambda c, n: (AK + c, n, 0)),
            pl.BlockSpec(blk, lambda c, n: (AK + c, jnp.maximum(n - 1, 0), 0)),
            pl.BlockSpec(blk, lambda c, n: (AV + c, n, 0)),
            pl.BlockSpec(blk, lambda c, n: (AV + c, jnp.maximum(n - 1, 0), 0)),
        ]
        scratch += [pltpu.VMEM((rows + BAND, width), MXU_DTYPE)] * 2
    scratch += [pltpu.VMEM((len(DILATIONS), A_TILE, LANES), F32)] * 2
    scratch.append(pltpu.VMEM((2, BAND, LANES), F32))
    operands += [proj, jnp.full((8, LANES), shift, F32)]
    in_specs += [pl.BlockSpec((1, A_TILE, LANES), lambda c, n: (AZ + c, n, 0)),
                 pl.BlockSpec((8, LANES), lambda c, n: (0, 0))]
    return pl.pallas_call(
        _dilated_body,
        grid=(A_CHUNKS, s // A_TILE),
        in_specs=in_specs,
        out_specs=pl.BlockSpec((A_TILE, LANES), lambda c, n: (n, c)),
        out_shape=jax.ShapeDtypeStruct((s, A_WIDTH), MXU_DTYPE),
        scratch_shapes=scratch,
        compiler_params=_params(("parallel", "arbitrary")),
        name="dilated_mixture",
    )(*operands)


FOX_AUG = 3
FOX_VROWS = HEAD_DIM + 16
FOX_DEN_MIN, FOX_DEN_MAX = 2.0 ** -90, 2.0 ** 100
FOX_ZERO_EXP = 160.0


def _split3(c):
    hi = c.astype(jnp.bfloat16).astype(F32)
    r = c - hi
    mid = r.astype(jnp.bfloat16).astype(F32)
    return hi, mid, r - mid


def _fox_prep_body(q_ref, k_ref, v_ref, ccol_ref, crow_ref, qt_ref, ka_ref, vt_ref):
    tm = q_ref.shape[1]
    q_t = q_ref[0].astype(F32).T
    v_t = v_ref[0].astype(F32).T
    k = k_ref[0]
    sub = lax.broadcasted_iota(jnp.int32, (HEAD_DIM, tm), 0)
    lane = lax.broadcasted_iota(jnp.int32, (tm, LANES), 1)
    v_tail = (lax.broadcasted_iota(jnp.int32, (FOX_VROWS - HEAD_DIM, tm), 0) == 0).astype(F32)
    for h in range(2):
        hi, mid, lo = _split3(crow_ref[0, h:h + 1, :])
        aug_q = jnp.where(sub == 0, hi, jnp.where(sub == 1, mid, jnp.where(
            sub == 2, lo, jnp.where(sub < 2 * FOX_AUG, 1.0, 0.0))))
        q_h = q_t[h * HEAD_DIM:(h + 1) * HEAD_DIM]
        parts = [q_h, aug_q] if h == 0 else [aug_q, q_h]
        qt_ref[h] = jnp.concatenate(parts, axis=0).astype(qt_ref.dtype)
        hi, mid, lo = _split3(ccol_ref[0, :, h:h + 1])
        a0 = (1 - h) * HEAD_DIM
        aug_k = jnp.where(lane == a0 + FOX_AUG, -hi, jnp.where(lane == a0 + FOX_AUG + 1, -mid, jnp.where(
            lane == a0 + FOX_AUG + 2, -lo, jnp.where((lane >= a0) & (lane < a0 + FOX_AUG), 1.0, 0.0))))
        own = (lane < HEAD_DIM) if h == 0 else (lane >= HEAD_DIM)
        ka_ref[h] = jnp.where(own, k, aug_k.astype(k.dtype))
        v_h = jnp.concatenate([v_t[h * HEAD_DIM:(h + 1) * HEAD_DIM], v_tail],
                              axis=0).astype(vt_ref.dtype)
        blk = vt_ref.shape[3]
        for b in range(tm // blk):
            vt_ref[h, b] = v_h[:, b * blk:(b + 1) * blk]


def _fox_prep(proj, c, shift, blk, tm=2048):
    s = proj.shape[1]
    c_col = c.reshape(B_CHUNKS, 2, s).transpose(0, 2, 1)
    c_row = c.reshape(B_CHUNKS, 2, s) - shift
    return pl.pallas_call(
        _fox_prep_body,
        grid=(B_CHUNKS, s // tm),
        in_specs=[
            pl.BlockSpec((1, tm, LANES), lambda p, i: (BQ + p, i, 0)),
            pl.BlockSpec((1, tm, LANES), lambda p, i: (BK + p, i, 0)),
            pl.BlockSpec((1, tm, LANES), lambda p, i: (BV + p, i, 0)),
            pl.BlockSpec((1, tm, 2), lambda p, i: (p, i, 0)),
            pl.BlockSpec((1, 2, tm), lambda p, i: (p, 0, i)),
        ],
        out_specs=[
            pl.BlockSpec((2, LANES, tm), lambda p, i: (p, 0, i)),
            pl.BlockSpec((2, tm, LANES), lambda p, i: (p, i, 0)),
            pl.BlockSpec((2, tm // blk, FOX_VROWS, blk), lambda p, i: (p, i, 0, 0)),
        ],
        out_shape=[
            jax.ShapeDtypeStruct((B_HEADS, LANES, s), MXU_DTYPE),
            jax.ShapeDtypeStruct((B_HEADS, s, LANES), MXU_DTYPE),
            jax.ShapeDtypeStruct((B_HEADS, s // blk, FOX_VROWS, blk), MXU_DTYPE),
        ],
        compiler_params=_params(("parallel", "parallel")),
        name="fox_prep",
    )(proj, proj, proj, c_col, c_row)


def _fox_body(lo_ref, qt_ref, ka_ref, vt_ref, z_ref, o_ref, s_scr, m_scr, acc_scr, *, blk):
    qi = pl.program_id(1)
    qblk = qt_ref.shape[2]
    ratio = qblk // blk
    key = lax.broadcasted_iota(jnp.int32, (blk, qblk), 0)
    query = lax.broadcasted_iota(jnp.int32, (blk, qblk), 1)
    causal = [key + r * blk <= query for r in range(ratio)]
    first_own = qi * ratio
    q_t = (qt_ref[0], qt_ref[1])

    def logits(kb, h, q0=0):
        k0 = pl.multiple_of(kb * blk, blk)
        s_scr[h, :, q0:] = jnp.dot(ka_ref[h, pl.ds(k0, blk), :], q_t[h][:, q0:],
                                   preferred_element_type=F32)

    def accumulate_shifted(kb, h, mask=None, q0=0):
        s = s_scr[h, :, q0:]
        if mask is not None:
            s = jnp.where(mask[:, q0:], s, -jnp.inf)
        acc_scr[h, :, q0:] += jnp.dot(vt_ref[h, kb], jnp.exp2(s).astype(MXU_DTYPE),
                                      preferred_element_type=F32)

    def accumulate_online(kb, h, mask=None, q0=0):
        s = s_scr[h, :, q0:]
        if mask is not None:
            s = jnp.where(mask[:, q0:], s, -jnp.inf)
        m = m_scr[h, :, q0:]
        m_new = jnp.maximum(m, jnp.max(s, axis=0, keepdims=True))
        p = jnp.exp2(s - m_new).astype(MXU_DTYPE)
        acc_scr[h, :, q0:] = (jnp.exp2(m - m_new) * acc_scr[h, :, q0:]
                              + jnp.dot(vt_ref[h, kb], p, preferred_element_type=F32))
        m_scr[h, :, q0:] = m_new

    def attend(accumulate, lo):
        acc_scr[...] = jnp.zeros(acc_scr.shape, F32)
        logits(lo, 0)

        def block(kb, carry):
            logits(kb, 1)
            accumulate(kb, 0)
            logits(kb + 1, 0)
            accumulate(kb, 1)
            return carry

        lax.fori_loop(lo, first_own, block, 0)
        for r in range(ratio):
            logits(first_own + r, 1, r * blk)
            accumulate(first_own + r, 0, causal[r], r * blk)
            if r + 1 < ratio:
                logits(first_own + r + 1, 0, (r + 1) * blk)
            accumulate(first_own + r, 1, causal[r], r * blk)

        outs = [acc_scr[h, 0:HEAD_DIM] / acc_scr[h, HEAD_DIM:HEAD_DIM + 1] for h in range(2)]
        o = jnp.concatenate(outs, axis=0).T
        o_ref[...] = (o * z_ref[0].astype(F32)).astype(o_ref.dtype)

    attend(accumulate_shifted, lo_ref[pl.program_id(0), qi])
    den = jnp.concatenate([acc_scr[h, HEAD_DIM:HEAD_DIM + 1] for h in range(2)], axis=0)
    in_range = (jnp.min(den) >= FOX_DEN_MIN) & (jnp.max(den) <= FOX_DEN_MAX)

    @pl.when(jnp.logical_not(in_range))
    def _():
        m_scr[...] = jnp.full(m_scr.shape, -jnp.inf, F32)
        attend(accumulate_online, 0)


def _fox_first_blocks(c, shift, blk, qblk):
    heads, s = c.shape
    nblk, ntile = s // blk, s // qblk
    c_first = c[:, ::qblk]
    c_last = c[:, blk - 1::blk]
    dead = (c_first[:, :, None] - c_last[:, None, :]) <= -(FOX_ZERO_EXP + 0.02 * shift)
    block = jnp.arange(nblk)
    first_own = jnp.arange(ntile) * (qblk // blk)
    dead &= block[None, None, :] < first_own[None, :, None]
    lo = jnp.min(jnp.where(dead, nblk, block), axis=-1)
    return jnp.min(lo.reshape(heads // 2, 2, ntile), axis=1).astype(jnp.int32)


def _forgetting_attention(proj, c, shift, blk=512, qblk=1024):
    s = proj.shape[1]
    qt, ka, vt = _fox_prep(proj, c, shift, blk)
    lo = _fox_first_blocks(c, shift, blk, qblk)
    grid_spec = pltpu.PrefetchScalarGridSpec(
        num_scalar_prefetch=1,
        grid=(B_CHUNKS, s // qblk),
        in_specs=[
            pl.BlockSpec((2, LANES, qblk), lambda p, i, lo: (p, 0, i)),
            pl.BlockSpec((2, s, LANES), lambda p, i, lo: (p, 0, 0)),
            pl.BlockSpec((2, s // blk, FOX_VROWS, blk), lambda p, i, lo: (p, 0, 0, 0)),
            pl.BlockSpec((1, qblk, LANES), lambda p, i, lo: (BZ + p, i, 0)),
        ],
        out_specs=pl.BlockSpec((qblk, LANES), lambda p, i, lo: (i, p)),
        scratch_shapes=[pltpu.VMEM((2, blk, qblk), F32), pltpu.VMEM((2, 1, qblk), F32),
                        pltpu.VMEM((2, FOX_VROWS, qblk), F32)],
    )
    return pl.pallas_call(
        functools.partial(_fox_body, blk=blk),
        grid_spec=grid_spec,
        out_shape=jax.ShapeDtypeStruct((s, B_WIDTH), MXU_DTYPE),
        compiler_params=_params(("parallel", "arbitrary")),
        name="forgetting_attention",
    )(lo, qt, ka, vt, proj)


GLA_GROUP = 4


def _gla_level_matrix():
    c = GLA_CHUNK
    mats = []
    for b in GLA_LEVELS:
        m = np.zeros((c, c), np.float32)
        for i in range(c):
            pivot = (i // (2 * b)) * 2 * b + b - 1
            if i > pivot:
                m[i, pivot + 1:i + 1] = 1.0
            else:
                m[i, i + 1:pivot + 1] = 1.0
        mats.append(m)
    mats.append(np.tril(np.ones((c, c), np.float32)))
    return np.concatenate(mats, axis=0)


def _gla_body(q_ref, k_ref, v_ref, g_ref, z_ref, gu_ref, gb_ref, og_ref, lvl_ref, o_ref,
              st_scr, *, tile):
    c = GLA_CHUNK
    nlev = len(GLA_LEVELS)
    nch = tile // c
    grp = GLA_GROUP * c

    @pl.when(pl.program_id(1) == 0)
    def _():
        st_scr[...] = jnp.zeros_like(st_scr)

    gate_in = g_ref[:, 0:GATE_RANK]
    gate_hi = gate_in.astype(MXU_DTYPE)
    gate_lo = (gate_in - gate_hi.astype(F32)).astype(MXU_DTYPE)
    gate_w = gu_ref[0].astype(MXU_DTYPE)
    logit = (jnp.dot(gate_hi, gate_w, preferred_element_type=F32)
             + jnp.dot(gate_lo, gate_w, preferred_element_type=F32) + gb_ref[0])
    la = _log_sigmoid(logit) * (1.0 / GATE_TEMP)

    la_cat = jnp.concatenate([la[ci * c:(ci + 1) * c] for ci in range(nch)], axis=1)
    la_hi = la_cat.astype(MXU_DTYPE)
    la_lo = (la_cat - la_hi.astype(F32)).astype(MXU_DTYPE)
    f = jnp.exp(jnp.dot(lvl_ref[0:nlev * c, :], la_hi, preferred_element_type=F32))
    tri = lvl_ref[nlev * c:(nlev + 1) * c, :]
    bc = (jnp.dot(tri, la_hi, preferred_element_type=F32)
          + jnp.dot(tri, la_lo, preferred_element_type=F32))

    row = lax.broadcasted_iota(jnp.int32, (grp, grp), 0)
    col = lax.broadcasted_iota(jnp.int32, (grp, grp), 1)
    diff = (row ^ col) & (c - 1)
    level = jnp.full((grp, grp), nlev, jnp.int32)
    for li, b in reversed(list(enumerate(GLA_LEVELS))):
        level = jnp.where(diff >= b, li, level)
    level = jnp.where(((row & -c) == (col & -c)) & (col <= row), level, -1)
    out_gain = og_ref[...]

    state = st_scr[...]
    for g0 in range(0, nch, GLA_GROUP):
        rows = slice(g0 * c, g0 * c + grp)
        q = q_ref[0, rows, :].astype(F32)
        k = k_ref[0, rows, :].astype(F32)
        v = jnp.concatenate([v_ref[0, rows, :], v_ref[1, rows, :]], axis=-1)
        z = jnp.concatenate([z_ref[0, rows, :], z_ref[1, rows, :]], axis=-1)
        attn = jnp.where(level == nlev, jnp.sum(q * k, axis=-1, keepdims=True), 0.0)
        for li in range(nlev):
            fl = jnp.concatenate([f[li * c:(li + 1) * c, ch * LANES:(ch + 1) * LANES]
                                  for ch in range(g0, g0 + GLA_GROUP)], axis=0)
            a = lax.dot_general((q * fl).astype(MXU_DTYPE), (k * fl).astype(MXU_DTYPE), _NT,
                                preferred_element_type=F32)
            attn = jnp.where(level == li, a, attn)
        o_intra = jnp.dot(attn.astype(MXU_DTYPE), v, preferred_element_type=F32)

        chunk_rows = [slice(ci * c, (ci + 1) * c) for ci in range(GLA_GROUP)]
        bcs = [bc[:, ch * LANES:(ch + 1) * LANES] for ch in range(g0, g0 + GLA_GROUP)]
        q_decs, decays, updates = [], [], []
        for r, bc_c in zip(chunk_rows, bcs):
            b_last = bc_c[c - 1:c, :]
            q_decs.append((q[r] * jnp.exp(bc_c)).astype(MXU_DTYPE))
            k_dec = (k[r] * jnp.exp(b_last - bc_c)).astype(MXU_DTYPE)
            decay = jnp.broadcast_to(jnp.exp(b_last), (LANES, LANES)).T
            decays.append(jnp.concatenate([decay, decay], axis=1))
            updates.append(lax.dot_general(k_dec, v[r], _TN, preferred_element_type=F32))
        states = []
        for decay, update in zip(decays, updates):
            states.append(state.astype(MXU_DTYPE))
            state = state * decay + update
        for ci, (r, q_dec, before) in enumerate(zip(chunk_rows, q_decs, states)):
            o = o_intra[r] + jnp.dot(q_dec, before, preferred_element_type=F32)
            ms = jnp.sum(o * o, axis=-1, keepdims=True) * (1.0 / C_DV)
            y = o * lax.rsqrt(ms + RMS_EPS) * out_gain * z[r].astype(F32)
            o_ref[pl.ds((g0 + ci) * c, c), :] = y.astype(o_ref.dtype)
    st_scr[...] = state


def _gated_linear_attention(proj, gate, gate_up, gate_bias, out_gain, tile=1024):
    s = proj.shape[1]
    lvl = jnp.asarray(_gla_level_matrix(), MXU_DTYPE)
    return pl.pallas_call(
        functools.partial(_gla_body, tile=tile),
        grid=(C_HEADS, s // tile),
        in_specs=[
            pl.BlockSpec((1, tile, LANES), lambda h, t: (CQ + h, t, 0)),
            pl.BlockSpec((1, tile, LANES), lambda h, t: (CK + h, t, 0)),
            pl.BlockSpec((2, tile, LANES), lambda h, t: (CV // 2 + h, t, 0)),
            pl.BlockSpec((tile, LANES), lambda h, t: (t, 0)),
            pl.BlockSpec((2, tile, LANES), lambda h, t: (CZ // 2 + h, t, 0)),
            pl.BlockSpec((1, GATE_RANK, C_DK_PAD), lambda h, t: (h, 0, 0)),
            pl.BlockSpec((1, 1, C_DK_PAD), lambda h, t: (h, 0, 0)),
            pl.BlockSpec((1, C_DV_PAD), lambda h, t: (0, 0)),
            pl.BlockSpec(lvl.shape, lambda h, t: (0, 0)),
        ],
        out_specs=pl.BlockSpec((tile, C_DV_PAD), lambda h, t: (t, h)),
        out_shape=jax.ShapeDtypeStruct((s, C_HEADS * C_DV_PAD), MXU_DTYPE),
        scratch_shapes=[pltpu.VMEM((C_DK_PAD, C_DV_PAD), F32)],
        compiler_params=_params(("parallel", "arbitrary")),
        name="gated_linear_attention",
    )(proj, proj, proj, gate, proj, gate_up, gate_bias, out_gain, lvl)


OUT_COLS = 512


def _out_body(x_ref, a_ref, b_ref, c_ref, wab_ref, wc_ref, g_ref, o_ref, h_ref):
    d = o_ref.shape[1]
    ssq = jnp.zeros((o_ref.shape[0], 1), F32)
    for c0 in range(0, d, OUT_COLS):
        cols = slice(c0, c0 + OUT_COLS)
        acc = jnp.dot(a_ref[...], wab_ref[0, 0:A_WIDTH, cols], preferred_element_type=F32)
        acc = acc + jnp.dot(b_ref[...], wab_ref[0, A_WIDTH:A_WIDTH + B_WIDTH, cols],
                            preferred_element_type=F32)
        acc = acc + jnp.dot(c_ref[...], wc_ref[0, :, cols], preferred_element_type=F32)
        y = x_ref[:, cols] + acc
        o_ref[:, cols] = y
        ssq = ssq + jnp.sum(y * y, axis=-1, keepdims=True)
    scale = lax.rsqrt(ssq * (1.0 / d) + RMS_EPS)
    h_ref[...] = (o_ref[...] * scale * g_ref[0]).astype(h_ref.dtype)


def _out_proj(x, ma, mb, mc, wo_ab, wo_c, norm_g, layer, tm=512):
    s, d = x.shape
    layers = norm_g.shape[0]
    row_blk = lambda m: pl.BlockSpec((tm, m.shape[1]), lambda i: (i, 0))
    return pl.pallas_call(
        _out_body,
        grid=(s // tm,),
        in_specs=[row_blk(x), row_blk(ma), row_blk(mb), row_blk(mc),
                  pl.BlockSpec((1, A_WIDTH + B_WIDTH, d), lambda i: (layer, 0, 0)),
                  pl.BlockSpec((1,) + wo_c.shape[1:], lambda i: (layer, 0, 0)),
                  pl.BlockSpec((1, 1, d), lambda i: (min(layer + 1, layers - 1), 0, 0))],
        out_specs=[row_blk(x), row_blk(x)],
        out_shape=[jax.ShapeDtypeStruct((s, d), F32), jax.ShapeDtypeStruct((s, d), MXU_DTYPE)],
        compiler_params=_params(("parallel",)),
        name="out_proj",
    )(x, ma, mb, mc, wo_ab, wo_c, norm_g)


def _pad_heads(w, heads, width, padded):
    lead = w.shape[:-1]
    w = w.reshape(*lead, heads, width)
    w = jnp.pad(w, [(0, 0)] * len(lead) + [(0, 0), (0, padded - width)])
    return w.reshape(*lead, heads * padded)


def _prepare(w_in, w_out, a_q_gain, a_k_gain, b_q_gain, b_k_gain):
    layers, d, _ = w_in.shape
    w_t = jnp.swapaxes(w_in, 1, 2)
    ck, cv = C_HEADS * C_DK, C_HEADS * C_DV
    ab = 4 * (A_WIDTH + B_WIDTH)
    bounds = np.cumsum([ab, B_HEADS, ck, ck, cv, cv, GATE_RANK])
    bf, cq, ckk, cvv, cz, cr = [w_t[:, lo:hi] for lo, hi in zip(bounds[:-1], bounds[1:])]

    def pad_rows(w, width, padded):
        w = w.reshape(layers, C_HEADS, width, d)
        w = jnp.pad(w, ((0, 0), (0, 0), (0, padded - width), (0, 0)))
        return w.reshape(layers, C_HEADS * padded, d)

    pad_k = lambda w: pad_rows(w, C_DK, C_DK_PAD)
    pad_v = lambda w: pad_rows(w, C_DV, C_DV_PAD)
    tail = jnp.zeros((layers, MXU_COLS - GATE_RANK - B_HEADS, d), F32)
    w_c_t = jnp.concatenate([pad_k(cq), pad_k(ckk), pad_v(cvv), pad_v(cz), cr, bf, tail], axis=1)

    q_scale = HEAD_DIM ** -0.5 * LOG2E
    tile_heads = lambda g, n: jnp.tile(g, (1, n))
    ones = lambda n: jnp.ones((layers, n), F32)
    aux_ab = jnp.concatenate([
        tile_heads(a_q_gain * q_scale, A_HEADS), tile_heads(a_k_gain, A_HEADS), ones(2 * A_WIDTH),
        tile_heads(b_q_gain * q_scale, B_HEADS), tile_heads(b_k_gain, B_HEADS), ones(2 * B_WIDTH)],
        axis=-1)[:, None, :]
    aux_c = jnp.concatenate([
        jnp.full((layers, C_HEADS * C_DK_PAD), C_DK ** -0.5, F32),
        ones(C_HEADS * (C_DK_PAD + 2 * C_DV_PAD) + MXU_COLS)], axis=-1)[:, None, :]

    wo = w_out.astype(MXU_DTYPE)
    wo_c = wo[:, A_WIDTH + B_WIDTH:].reshape(layers, C_HEADS, C_DV, d)
    wo_c = jnp.pad(wo_c, ((0, 0), (0, 0), (0, C_DV_PAD - C_DV), (0, 0)))
    wo_c = wo_c.reshape(layers, C_HEADS * C_DV_PAD, d)
    return w_t, w_c_t, aux_ab, aux_c, wo, wo_c


def _layer(x, h, layer, norm_g, w_t, w_c, aux_ab, aux_c, wo_ab, wo_c, fox_bias, a_shift,
           fox_shift, gla_gate_up, gla_gate_bias, gla_out_gain):
    proj_ab, v4, v16 = _in_proj(h, w_t, aux_ab, layer, AB_KINDS, "in_proj_ab")
    proj_c, gate = _in_proj(h, w_c, aux_c, layer, C_KINDS, "in_proj_c")
    c = _fox_cumsum(gate[:, GATE_RANK:GATE_RANK + B_HEADS].T, fox_bias)
    mixed_a = _dilated_mixture(proj_ab, (v4, v16), a_shift)
    mixed_b = _forgetting_attention(proj_ab, c, fox_shift)
    gate_up = _pad_heads(gla_gate_up, C_HEADS, C_DK, C_DK_PAD)
    gate_up = gate_up.reshape(GATE_RANK, C_HEADS, C_DK_PAD).transpose(1, 0, 2)
    gate_bias = _pad_heads(gla_gate_bias, C_HEADS, C_DK, C_DK_PAD).reshape(C_HEADS, 1, C_DK_PAD)
    out_gain = jnp.pad(gla_out_gain, (0, C_DV_PAD - C_DV)).reshape(1, C_DV_PAD)
    mixed_c = _gated_linear_attention(proj_c, gate, gate_up, gate_bias, out_gain)
    return _out_proj(x, mixed_a, mixed_b, mixed_c, wo_ab, wo_c, norm_g, layer)


@jax.jit
def kernel(x, norm_g, w_in, a_q_gain, a_k_gain, b_q_gain, b_k_gain, fox_bias, gla_gate_up,
           gla_gate_bias, gla_out_gain, w_out):
    bsz, s, d = x.shape
    assert bsz == 1, "batch size 1 only"
    w_t, w_c, aux_ab, aux_c, wo_ab, wo_c = _prepare(w_in, w_out, a_q_gain, a_k_gain, b_q_gain,
                                                    b_k_gain)
    norm_g = norm_g[:, None, :]
    bound = lambda gq, gk: (HEAD_DIM ** 0.5 * LOG2E * jnp.max(jnp.abs(gq), axis=-1)
                            * jnp.max(jnp.abs(gk), axis=-1))
    a_shift, fox_shift = bound(a_q_gain, a_k_gain), bound(b_q_gain, b_k_gain)
    y = x.reshape(s, d)
    h = _rmsnorm(y, norm_g, 0)
    for layer in range(w_in.shape[0]):
        y, h = _layer(y, h, layer, norm_g, w_t, w_c, aux_ab, aux_c, wo_ab, wo_c, fox_bias[layer],
                      a_shift[layer], fox_shift[layer], gla_gate_up[layer], gla_gate_bias[layer],
                      gla_out_gain[layer])
    return y.reshape(bsz, s, d)
```
